```python
import math
import jax, jax.numpy as jnp
from jax import lax
import numpy as np

D_MODEL = 1024
BATCH = 8
SEQ = 4096
DEPTH = 2
DEC_BATCH = 8
DEC_SEQ = 32
PAST_LEN = 1024

CHUNK = 64
N_EVEN = (DEPTH + 1) // 2
N_ODD = DEPTH // 2
EPS = 1e-6
NEG_INF = -1e30
A_WIDTH = D_MODEL // 2
A_CONV = 3
B_HEADS = 8
B_KV_HEADS = 2
B_HEAD_DIM = 64
IDX_HEADS = 8
IDX_DIM = 32
TOPK_MAX = 256
Q_BLOCK = 128
REL_BUCKETS = 32
REL_MAX_DIST = 128
RNN_WIDTH = D_MODEL
RNN_BLOCKS = 8
RNN_BLOCK = RNN_WIDTH // RNN_BLOCKS
C_CONV = 4
RG_C = 8.0
MEM_LEN = 256
X_HEADS = 4
X_HEAD_DIM = D_MODEL // X_HEADS
D_FF = 2816
F_CONV = 3

EVEN_SIZES = (A_WIDTH, A_WIDTH, A_WIDTH, B_HEADS * B_HEAD_DIM, B_KV_HEADS * B_HEAD_DIM,
              B_KV_HEADS * B_HEAD_DIM, IDX_HEADS * IDX_DIM, IDX_DIM, IDX_HEADS)
EVEN_IN = 3 * A_WIDTH + B_HEADS * B_HEAD_DIM + 2 * B_KV_HEADS * B_HEAD_DIM + IDX_HEADS * IDX_DIM + IDX_DIM + IDX_HEADS
EVEN_OUT = A_WIDTH + B_HEADS * B_HEAD_DIM

kernel_name = "hybrid_streaming_conv_dsa_rglru_step"


def _rmsnorm(x, g):
    xf = x.astype(jnp.float32)
    y = xf * lax.rsqrt(jnp.mean(xf * xf, axis=-1, keepdims=True) + EPS)
    return (y * g.astype(jnp.float32)).astype(x.dtype)


def _split(z, sizes):
    idx, acc = [], 0
    for s in sizes[:-1]:
        acc += s
        idx.append(acc)
    return jnp.split(z, idx, axis=-1)


def _causal_conv(x, hist, w, b=None):
    width = w.shape[0]
    t = x.shape[1]
    xp = jnp.concatenate([hist.astype(x.dtype), x], axis=1)
    y = xp[:, width - 1:] * w[width - 1]
    for i in range(width - 1):
        y = y + xp[:, i:i + t] * w[i]
    if b is not None:
        y = y + b
    return y, xp[:, -(width - 1):]


def _rel_bucket(rel):
    half = REL_BUCKETS // 2
    max_exact = half // 2
    n = -rel
    ret = jnp.where(n < 0, half, 0)
    n = jnp.abs(n)
    nf = jnp.maximum(n, 1).astype(jnp.float32)
    large = max_exact + (jnp.log(nf / max_exact) / math.log(REL_MAX_DIST / max_exact)
                         * (half - max_exact)).astype(jnp.int32)
    large = jnp.minimum(large, half - 1)
    return ret + jnp.where(n < max_exact, n, large)


def _dsa_block(q, qi, wi, q_pos, k, v, ki, k_pos, rel_table, n_top):
    bsz, t = q.shape[0], q.shape[1]
    groups = B_HEADS // B_KV_HEADS
    q_chunk = q_pos // CHUNK
    visible = (k_pos[None, :] // CHUNK) <= q_chunk[:, None]
    relu_dots = jax.nn.relu(jnp.einsum("bthd,bld->btlh", qi, ki))
    score = jnp.einsum("btlh,bth->btl", relu_dots, wi).astype(jnp.float32)
    score = jnp.where(visible[None], score, -jnp.inf)
    _, sel = lax.top_k(score, n_top)
    sel_pos = k_pos[sel]
    sel_ok = (sel_pos // CHUNK) <= q_chunk[None, :, None]
    take = jax.vmap(lambda rows, idx: rows[idx])
    k_sel = take(k, sel)
    v_sel = take(v, sel)
    qg = q.reshape(bsz, t, B_KV_HEADS, groups, B_HEAD_DIM)
    logits = jnp.einsum("btngd,btknd->btngk", qg, k_sel).astype(jnp.float32) * (B_HEAD_DIM ** -0.5)
    bias = rel_table[_rel_bucket(sel_pos - q_pos[None, :, None])].astype(jnp.float32)
    bias = bias.reshape(bsz, t, n_top, B_KV_HEADS, groups).transpose(0, 1, 3, 4, 2)
    logits = jnp.where(sel_ok[:, :, None, None, :], logits + bias, NEG_INF)
    probs = jax.nn.softmax(logits, axis=-1).astype(v.dtype)
    out = jnp.einsum("btngk,btknd->btngd", probs, v_sel)
    return out.reshape(bsz, t, B_HEADS * B_HEAD_DIM)


def _dsa(q, qi, wi, q_pos, k, v, ki, k_pos, rel_table, n_top):
    bsz, t = q.shape[0], q.shape[1]
    if t > Q_BLOCK and t % Q_BLOCK == 0:
        nb = t // Q_BLOCK

        def blocks(a):
            return jnp.moveaxis(a.reshape(bsz, nb, Q_BLOCK, *a.shape[2:]), 1, 0)

        def one(args):
            qb, qib, wib, pb = args
            return _dsa_block(qb, qib, wib, pb, k, v, ki, k_pos, rel_table, n_top)

        out = lax.map(one, (blocks(q), blocks(qi), blocks(wi), q_pos.reshape(nb, Q_BLOCK)))
        return jnp.moveaxis(out, 0, 1).reshape(bsz, t, B_HEADS * B_HEAD_DIM)
    return _dsa_block(q, qi, wi, q_pos, k, v, ki, k_pos, rel_table, n_top)


def _rglru(xr, h0, q_pos, w_a, b_a, w_i, b_i, lam):
    bsz, t, _ = xr.shape
    xb = xr.reshape(bsz, t, RNN_BLOCKS, RNN_BLOCK)
    r = jax.nn.sigmoid((jnp.einsum("btni,nij->btnj", xb, w_a).reshape(bsz, t, RNN_WIDTH) + b_a).astype(jnp.float32))
    ig = jax.nn.sigmoid((jnp.einsum("btni,nij->btnj", xb, w_i).reshape(bsz, t, RNN_WIDTH) + b_i).astype(jnp.float32))
    log_a = -RG_C * r * jax.nn.softplus(-lam.astype(jnp.float32))
    a = jnp.exp(log_a)
    mult = jnp.sqrt(-jnp.expm1(2.0 * log_a))
    mult = jnp.where((q_pos == 0)[None, :, None], 1.0, mult)
    b = mult * ig * xr.astype(jnp.float32)

    def combine(c1, c2):
        a1, b1 = c1
        a2, b2 = c2
        return a1 * a2, a2 * b1 + b2

    a_cum, b_cum = lax.associative_scan(combine, (a, b), axis=1)
    h = a_cum * h0.astype(jnp.float32)[:, None] + b_cum
    return h.astype(xr.dtype), h[:, -1].astype(xr.dtype)


def _mem_kv(mem, g, w_k, k_norm, w_v):
    bsz, m, _ = mem.shape
    hm = _rmsnorm(mem, g)
    k = _rmsnorm((hm @ w_k).reshape(bsz, m, X_HEADS, X_HEAD_DIM), k_norm)
    v = (hm @ w_v).reshape(bsz, m, X_HEADS, X_HEAD_DIM)
    return k, v


def _mem_attn(h, mk, mv, w_q, q_norm, w_o):
    bsz, t, _ = h.shape
    q = _rmsnorm((h @ w_q).reshape(bsz, t, X_HEADS, X_HEAD_DIM), q_norm)
    logits = jnp.einsum("bthd,bmhd->bhtm", q, mk.astype(q.dtype)).astype(jnp.float32) * (X_HEAD_DIM ** -0.5)
    probs = jax.nn.softmax(logits, axis=-1).astype(h.dtype)
    o = jnp.einsum("bhtm,bmhd->bthd", probs, mv.astype(h.dtype)).reshape(bsz, t, X_HEADS * X_HEAD_DIM)
    return o @ w_o


def _trunk(x, st, mem_k, mem_v, p):
    bsz, t, _ = x.shape
    past = st["b_k"].shape[2]
    q_pos = past + jnp.arange(t, dtype=jnp.int32)
    k_pos = jnp.arange(past + t, dtype=jnp.int32)
    n_top = min(TOPK_MAX, (past + t) // 4)
    out = {"b_k": [], "b_v": [], "b_kidx": [], "a_conv": [], "c_conv": [], "c_h": [], "f_conv": []}
    for l in range(DEPTH):
        if l % 2 == 0:
            e = l // 2
            z = _rmsnorm(x, p["g_mix"][l]) @ p["w_in_even"][e]
            a_b, a_c, a_x, b_q, b_k, b_v, i_q, i_k, i_w = _split(z, EVEN_SIZES)
            conv_out, a_hist = _causal_conv(a_c * a_x, st["a_conv"][e], p["a_conv_w"][e])
            y_a = a_b * conv_out
            q = _rmsnorm(b_q.reshape(bsz, t, B_HEADS, B_HEAD_DIM), p["b_q_norm"][e])
            k_new = _rmsnorm(b_k.reshape(bsz, t, B_KV_HEADS, B_HEAD_DIM), p["b_k_norm"][e])
            v_new = b_v.reshape(bsz, t, B_KV_HEADS, B_HEAD_DIM)
            k_all = jnp.concatenate([st["b_k"][e].astype(x.dtype), k_new], axis=1)
            v_all = jnp.concatenate([st["b_v"][e].astype(x.dtype), v_new], axis=1)
            ki_all = jnp.concatenate([st["b_kidx"][e].astype(x.dtype), i_k], axis=1)
            qi = i_q.reshape(bsz, t, IDX_HEADS, IDX_DIM)
            wi = i_w * ((IDX_HEADS * IDX_DIM) ** -0.5)
            y_b = _dsa(q, qi, wi, q_pos, k_all, v_all, ki_all, k_pos, p["rel_table"], n_top)
            x = x + jnp.concatenate([y_a, y_b], axis=-1) @ p["w_out_even"][e]
            out["a_conv"].append(a_hist)
            out["b_k"].append(k_new)
            out["b_v"].append(v_new)
            out["b_kidx"].append(i_k)
        else:
            o = l // 2
            z = _rmsnorm(x, p["g_mix"][l]) @ p["w_in_odd"][o]
            gate, xr = jnp.split(z, 2, axis=-1)
            xr, c_hist = _causal_conv(xr, st["c_conv"][o], p["c_conv_w"][o], p["c_conv_b"][o])
            y_c, h_last = _rglru(xr, st["c_h"][o], q_pos, p["c_w_a"][o], p["c_b_a"][o],
                                 p["c_w_i"][o], p["c_b_i"][o], p["c_lambda"][o])
            x = x + (jax.nn.gelu(gate) * y_c) @ p["w_out_odd"][o]
            out["c_conv"].append(c_hist)
            out["c_h"].append(h_last)
        x = x + _mem_attn(_rmsnorm(x, p["g_x"][l]), mem_k[l], mem_v[l], p["w_xq"][l],
                          p["x_q_norm"][l], p["w_xo"][l])
        val, g = jnp.split(_rmsnorm(x, p["g_ffn"][l]) @ p["w_up"][l], 2, axis=-1)
        g, f_hist = _causal_conv(g, st["f_conv"][l], p["f_conv_w"][l], p["f_conv_b"][l])
        x = x + (jax.nn.gelu(g) * val) @ p["w_down"][l]
        out["f_conv"].append(f_hist)
    return x, {name: jnp.stack(v) for name, v in out.items()}


def setup_inputs(seed: int = 0) -> dict:
    key = jax.random.key(seed)
    ks = iter(jax.random.split(key, 64))
    f32 = jnp.float32

    def nrm(shape, scale=1.0):
        return jax.random.normal(next(ks), shape, f32) * scale

    def gain(shape):
        return 1.0 + 0.02 * jax.random.normal(next(ks), shape, f32)

    d = D_MODEL
    lam_a = jax.random.uniform(next(ks), (N_ODD, RNN_WIDTH), f32, 0.9, 0.999)
    return {
        "x_prompt": nrm((BATCH, SEQ, d)),
        "x_sample": nrm((DEC_BATCH, DEC_SEQ, d)),
        "cache_b_k": nrm((N_EVEN, DEC_BATCH, PAST_LEN, B_KV_HEADS, B_HEAD_DIM)),
        "cache_b_v": nrm((N_EVEN, DEC_BATCH, PAST_LEN, B_KV_HEADS, B_HEAD_DIM)),
        "cache_b_kidx": nrm((N_EVEN, DEC_BATCH, PAST_LEN, IDX_DIM)),
        "state_a_conv": nrm((N_EVEN, DEC_BATCH, A_CONV - 1, A_WIDTH)),
        "state_c_conv": nrm((N_ODD, DEC_BATCH, C_CONV - 1, RNN_WIDTH)),
        "state_c_h": nrm((N_ODD, DEC_BATCH, RNN_WIDTH), 0.5),
        "state_ffn_conv": nrm((DEPTH, DEC_BATCH, F_CONV - 1, D_FF)),
        "cache_mem_k": nrm((DEPTH, DEC_BATCH, MEM_LEN, X_HEADS, X_HEAD_DIM)),
        "cache_mem_v": nrm((DEPTH, DEC_BATCH, MEM_LEN, X_HEADS, X_HEAD_DIM)),
        "mem_prompt": nrm((BATCH, MEM_LEN, d)),
        "rel_table": nrm((REL_BUCKETS, B_HEADS), 0.5),
        "g_mix": gain((DEPTH, d)),
        "w_in_even": nrm((N_EVEN, d, EVEN_IN), d ** -0.5),
        "a_conv_w": nrm((N_EVEN, A_CONV, A_WIDTH), A_CONV ** -0.5),
        "b_q_norm": gain((N_EVEN, B_HEAD_DIM)),
        "b_k_norm": gain((N_EVEN, B_HEAD_DIM)),
        "w_out_even": nrm((N_EVEN, EVEN_OUT, d), EVEN_OUT ** -0.5),
        "w_in_odd": nrm((N_ODD, d, 2 * RNN_WIDTH), d ** -0.5),
        "c_conv_w": nrm((N_ODD, C_CONV, RNN_WIDTH), C_CONV ** -0.5),
        "c_conv_b": nrm((N_ODD, RNN_WIDTH), 0.01),
        "c_w_a": nrm((N_ODD, RNN_BLOCKS, RNN_BLOCK, RNN_BLOCK), RNN_BLOCK ** -0.5),
        "c_b_a": nrm((N_ODD, RNN_WIDTH), 0.01),
        "c_w_i": nrm((N_ODD, RNN_BLOCKS, RNN_BLOCK, RNN_BLOCK), RNN_BLOCK ** -0.5),
        "c_b_i": nrm((N_ODD, RNN_WIDTH), 0.01),
        "c_lambda": jnp.log(lam_a) - jnp.log1p(-lam_a),
        "w_out_odd": nrm((N_ODD, RNN_WIDTH, d), RNN_WIDTH ** -0.5),
        "g_mem": gain((DEPTH, d)),
        "g_x": gain((DEPTH, d)),
        "w_xq": nrm((DEPTH, d, X_HEADS * X_HEAD_DIM), d ** -0.5),
        "w_xk": nrm((DEPTH, d, X_HEADS * X_HEAD_DIM), d ** -0.5),
        "w_xv": nrm((DEPTH, d, X_HEADS * X_HEAD_DIM), d ** -0.5),
        "x_q_norm": gain((DEPTH, X_HEAD_DIM)),
        "x_k_norm": gain((DEPTH, X_HEAD_DIM)),
        "w_xo": nrm((DEPTH, X_HEADS * X_HEAD_DIM, d), (X_HEADS * X_HEAD_DIM) ** -0.5),
        "g_ffn": gain((DEPTH, d)),
        "w_up": nrm((DEPTH, d, 2 * D_FF), d ** -0.5),
        "f_conv_w": nrm((DEPTH, F_CONV, D_FF), F_CONV ** -0.5),
        "f_conv_b": nrm((DEPTH, D_FF), 0.01),
        "w_down": nrm((DEPTH, D_FF, d), D_FF ** -0.5),
    }


def reference(x_prompt, x_sample, cache_b_k, cache_b_v, cache_b_kidx, state_a_conv, state_c_conv,
              state_c_h, state_ffn_conv, cache_mem_k, cache_mem_v, mem_prompt, rel_table, g_mix,
              w_in_even, a_conv_w, b_q_norm, b_k_norm, w_out_even, w_in_odd, c_conv_w, c_conv_b,
              c_w_a, c_b_a, c_w_i, c_b_i, c_lambda, w_out_odd, g_mem, g_x, w_xq, w_xk, w_xv,
              x_q_norm, x_k_norm, w_xo, g_ffn, w_up, f_conv_w, f_conv_b, w_down):
    p = {"rel_table": rel_table, "g_mix": g_mix, "w_in_even": w_in_even, "a_conv_w": a_conv_w,
         "b_q_norm": b_q_norm, "b_k_norm": b_k_norm, "w_out_even": w_out_even,
         "w_in_odd": w_in_odd, "c_conv_w": c_conv_w, "c_conv_b": c_conv_b, "c_w_a": c_w_a,
         "c_b_a": c_b_a, "c_w_i": c_w_i, "c_b_i": c_b_i, "c_lambda": c_lambda,
         "w_out_odd": w_out_odd, "g_x": g_x, "w_xq": w_xq, "x_q_norm": x_q_norm, "w_xo": w_xo,
         "g_ffn": g_ffn, "w_up": w_up, "f_conv_w": f_conv_w, "f_conv_b": f_conv_b, "w_down": w_down}
    dt = x_prompt.dtype
    bp = x_prompt.shape[0]
    mk, mv = [], []
    for l in range(DEPTH):
        k_l, v_l = _mem_kv(mem_prompt, g_mem[l], w_xk[l], x_k_norm[l], w_xv[l])
        mk.append(k_l)
        mv.append(v_l)
    p_mem_k = jnp.stack(mk)
    p_mem_v = jnp.stack(mv)
    st_p = {"b_k": jnp.zeros((N_EVEN, bp, 0, B_KV_HEADS, B_HEAD_DIM), dt),
            "b_v": jnp.zeros((N_EVEN, bp, 0, B_KV_HEADS, B_HEAD_DIM), dt),
            "b_kidx": jnp.zeros((N_EVEN, bp, 0, IDX_DIM), dt),
            "a_conv": jnp.zeros((N_EVEN, bp, A_CONV - 1, A_WIDTH), dt),
            "c_conv": jnp.zeros((N_ODD, bp, C_CONV - 1, RNN_WIDTH), dt),
            "c_h": jnp.zeros((N_ODD, bp, RNN_WIDTH), dt),
            "f_conv": jnp.zeros((DEPTH, bp, F_CONV - 1, D_FF), dt)}
    y_prompt, new_p = _trunk(x_prompt, st_p, p_mem_k, p_mem_v, p)
    st_s = {"b_k": cache_b_k, "b_v": cache_b_v, "b_kidx": cache_b_kidx, "a_conv": state_a_conv,
            "c_conv": state_c_conv, "c_h": state_c_h, "f_conv": state_ffn_conv}
    y_sample, new_s = _trunk(x_sample, st_s, cache_mem_k, cache_mem_v, p)
    return (y_prompt, y_sample,
            new_p["b_k"], new_p["b_v"], new_p["b_kidx"], new_p["a_conv"], new_p["c_conv"],
            new_p["c_h"], new_p["f_conv"], p_mem_k, p_mem_v,
            new_s["b_k"], new_s["b_v"], new_s["b_kidx"], new_s["a_conv"], new_s["c_conv"],
            new_s["c_h"], new_s["f_conv"])
```

```python
import functools
import math

import jax
import jax.numpy as jnp
from jax import lax
from jax.experimental import pallas as pl
from jax.experimental.pallas import tpu as pltpu

F32 = jnp.float32
BF16 = jnp.bfloat16
I32 = jnp.int32

D_MODEL = 1024
DEPTH = 2
CHUNK = 64
EPS = 1e-6
NEG_INF = -1e30
A_WIDTH = 512
A_CONV = 3
B_HEADS = 8
B_KV_HEADS = 2
B_HEAD_DIM = 64
IDX_HEADS = 8
IDX_DIM = 32
TOPK_MAX = 256
REL_BUCKETS = 32
REL_MAX_DIST = 128
RNN_WIDTH = 1024
RNN_BLOCKS = 8
RNN_BLOCK = 128
C_CONV = 4
RG_C = 8.0
X_HEADS = 4
X_HEAD_DIM = 256
D_FF = 2816
F_CONV = 3
EVEN_IN = 2600

SUBLANES = 8
LANES = 128
MXU_DIM = 256
VMEM_LIMIT = 56 * 1024 * 1024

EVEN_IN_PAD = 2688
KEY_BLOCK = 256
INT_MIN = -2147483648


def _params(sem, vmem=VMEM_LIMIT):
    return pltpu.CompilerParams(dimension_semantics=sem, vmem_limit_bytes=vmem)


def _const_spec(shape):
    nd = len(shape)
    return pl.BlockSpec(shape, lambda *_: (0,) * nd, pipeline_mode=pl.Buffered(1))


def _rms(x, g):
    ms = jnp.mean(x * x, axis=-1, keepdims=True)
    return x * lax.rsqrt(ms + EPS) * g


def _head_rms(x, hd, gain):
    m, c = x.shape
    s = x * x
    parts = []
    if hd >= LANES:
        for h in range(c // hd):
            ms = jnp.mean(s[:, h * hd:(h + 1) * hd], axis=-1, keepdims=True)
            parts.append(x[:, h * hd:(h + 1) * hd] * lax.rsqrt(ms + EPS))
    else:
        lane = lax.broadcasted_iota(I32, (m, LANES), 1)
        for j in range(c // LANES):
            sj = s[:, j * LANES:(j + 1) * LANES]
            inv = jnp.zeros((m, LANES), F32)
            for k in range(LANES // hd):
                msk = (lane >= k * hd) & (lane < (k + 1) * hd)
                ms = jnp.sum(jnp.where(msk, sj, 0.0), axis=-1, keepdims=True) * (1.0 / hd)
                inv = jnp.where(msk, lax.rsqrt(ms + EPS), inv)
            parts.append(x[:, j * LANES:(j + 1) * LANES] * inv)
    y = parts[0] if len(parts) == 1 else jnp.concatenate(parts, axis=-1)
    return y * gain


def _dot(a, b):
    return jnp.dot(a, b, preferred_element_type=F32)


def _dot_t(a, b):
    return lax.dot_general(a, b, (((1,), (1,)), ((), ())), preferred_element_type=F32)


def _conv_taps(buf, cur, w_ref, width, tm):
    hist = (width - 1) * SUBLANES
    y = cur * w_ref[width - 1:width, :]
    for i in range(width - 1):
        y = y + buf[i * SUBLANES:i * SUBLANES + tm, :] * w_ref[i:i + 1, :]
    del hist
    return y


def _memkv_kernel(mem_ref, g_ref, wk_ref, kn_ref, wv_ref, k_ref, v_ref):
    hm = _rms(mem_ref[...], g_ref[0]).astype(BF16)
    k_ref[0] = _head_rms(_dot(hm, wk_ref[0]), X_HEAD_DIM, kn_ref[0])
    v_ref[0] = _dot(hm, wv_ref[0])


def _mem_kv(mem, g_mem, w_xk, x_k_norm, w_xv):
    rows = mem.shape[0]
    tm = min(512, rows)
    d = D_MODEL
    kn = jnp.tile(x_k_norm, (1, X_HEADS)).reshape(DEPTH, 1, d)
    out = jax.ShapeDtypeStruct((DEPTH, rows, d), F32)
    return pl.pallas_call(
        _memkv_kernel,
        grid=(DEPTH, rows // tm),
        in_specs=[
            pl.BlockSpec((tm, d), lambda l, i: (i, 0)),
            pl.BlockSpec((1, 1, d), lambda l, i: (l, 0, 0)),
            pl.BlockSpec((1, d, d), lambda l, i: (l, 0, 0)),
            pl.BlockSpec((1, 1, d), lambda l, i: (l, 0, 0)),
            pl.BlockSpec((1, d, d), lambda l, i: (l, 0, 0)),
        ],
        out_specs=[pl.BlockSpec((1, tm, d), lambda l, i: (l, i, 0))] * 2,
        out_shape=[out, out],
        compiler_params=_params(("arbitrary", "arbitrary")),
        name="mem_kv",
    )(mem, g_mem.reshape(DEPTH, 1, d), w_xk.astype(BF16), kn, w_xv.astype(BF16))


def _even_in_kernel(x_ref, g_ref, w_ref, cw_ref, qn_ref, kn_ref, hist_ref,
                    ya_ref, q_ref, k_ref, v_ref, iq_ref, ikw_ref, uh_ref, ubuf):
    tm = x_ref.shape[0]
    nh = (A_CONV - 1) * SUBLANES

    @pl.when(pl.program_id(0) == 0)
    def _():
        ubuf[0:nh, :] = hist_ref[...]

    xn = _rms(x_ref[...], g_ref[...]).astype(BF16)

    def proj(a, b):
        return _dot(xn, w_ref[:, a:b])

    zc = proj(A_WIDTH, 3 * A_WIDTH)
    u = zc[:, :A_WIDTH] * zc[:, A_WIDTH:]
    ubuf[nh:nh + tm, :] = u
    conv = _conv_taps(ubuf, u, cw_ref, A_CONV, tm)
    ya_ref[...] = (proj(0, A_WIDTH) * conv).astype(BF16)
    tail = ubuf[tm:tm + nh, :]
    ubuf[0:nh, :] = tail
    uh_ref[...] = tail

    o = 3 * A_WIDTH
    nq = B_HEADS * B_HEAD_DIM
    nkv = B_KV_HEADS * B_HEAD_DIM
    q_ref[...] = (_head_rms(proj(o, o + nq), B_HEAD_DIM, qn_ref[...]) * (B_HEAD_DIM ** -0.5)).astype(BF16)
    zkv = proj(o + nq, o + nq + 2 * nkv)
    k_ref[...] = _head_rms(zkv[:, :nkv], B_HEAD_DIM, kn_ref[...])
    v_ref[...] = zkv[:, nkv:]
    o2 = o + nq + 2 * nkv
    ni = IDX_HEADS * IDX_DIM
    zi = proj(o2, EVEN_IN_PAD)
    iq_ref[...] = zi[:, :ni].astype(BF16)
    ikw_ref[...] = zi[:, ni:]


def _even_in(x, g, w_pad, conv_w, qn, kn, hist, tm):
    rows = x.shape[0]
    d = D_MODEL
    nh = (A_CONV - 1) * SUBLANES
    nq = B_HEADS * B_HEAD_DIM
    nkv = B_KV_HEADS * B_HEAD_DIM
    ni = IDX_HEADS * IDX_DIM

    def row(c):
        return pl.BlockSpec((tm, c), lambda i: (i, 0))

    return pl.pallas_call(
        _even_in_kernel,
        grid=(rows // tm,),
        in_specs=[row(d), _const_spec((1, d)), _const_spec((d, EVEN_IN_PAD)), _const_spec((A_CONV, A_WIDTH)),
                  _const_spec((1, nq)), _const_spec((1, nkv)), _const_spec((nh, A_WIDTH))],
        out_specs=[row(A_WIDTH), row(nq), row(nkv), row(nkv), row(ni), row(LANES),
                   pl.BlockSpec((nh, A_WIDTH), lambda i: (0, 0))],
        out_shape=[jax.ShapeDtypeStruct((rows, A_WIDTH), BF16), jax.ShapeDtypeStruct((rows, nq), BF16),
                   jax.ShapeDtypeStruct((rows, nkv), F32), jax.ShapeDtypeStruct((rows, nkv), F32),
                   jax.ShapeDtypeStruct((rows, ni), BF16), jax.ShapeDtypeStruct((rows, LANES), F32),
                   jax.ShapeDtypeStruct((nh, A_WIDTH), F32)],
        scratch_shapes=[pltpu.VMEM((tm + nh, A_WIDTH), F32)],
        compiler_params=_params(("arbitrary",)),
        name="even_in",
    )(x, g, w_pad, conv_w, qn, kn, hist)


def _rel_bucket(rel):
    half = REL_BUCKETS // 2
    max_exact = half // 2
    n = -rel
    ret = jnp.where(n < 0, half, 0)
    n = jnp.abs(n)
    nf = jnp.maximum(n, 1).astype(F32)
    large = max_exact + (jnp.log(nf / max_exact) / math.log(REL_MAX_DIST / max_exact)
                         * (half - max_exact)).astype(I32)
    large = jnp.minimum(large, half - 1)
    return ret + jnp.where(n < max_exact, n, large)


def _bias_kernel(tab_ref, near_ref, far_ref):
    tk = near_ref.shape[-1]
    r = lax.broadcasted_iota(I32, (tk, tk), 0)
    c = lax.broadcasted_iota(I32, (tk, tk), 1)

    def lookup(bucket, h):
        def body(j, acc):
            return jnp.where(bucket == j, tab_ref[j, h], acc)
        return lax.fori_loop(0, REL_BUCKETS, body, jnp.zeros(bucket.shape, F32))

    for blk in range(2):
        bucket = _rel_bucket(c - r - blk * tk)
        for h in range(B_HEADS):
            near_ref[blk, h] = lookup(bucket, h)
    bucket = _rel_bucket(-REL_MAX_DIST - c[0:SUBLANES, :])
    for h in range(B_HEADS):
        far_ref[h] = lookup(bucket, h)


def _bias_tiles(rel_table, tk):
    return pl.pallas_call(
        _bias_kernel,
        in_specs=[pl.BlockSpec(memory_space=pltpu.SMEM)],
        out_shape=[jax.ShapeDtypeStruct((2, B_HEADS, tk, tk), F32),
                   jax.ShapeDtypeStruct((B_HEADS, SUBLANES, tk), F32)],
        name="rel_bias",
    )(rel_table)


def _dsa_kernel(q_ref, iq_ref, ikw_ref, k_ref, v_ref, ik_ref, near_ref, far_ref, o_ref,
                skey, m_s, l_s, acc_s, x_s, *, tq, tk, past, length, ntop, idx_bits):
    i = pl.program_id(1)
    q0 = past + i * tq
    nkb = (q0 + tq + tk - 1) // tk
    rows = lax.broadcasted_iota(I32, (tq, tk), 0)
    cols = lax.broadcasted_iota(I32, (tq, tk), 1)
    qchunk = jnp.right_shift(q0 + rows, 6)
    ni = IDX_HEADS * IDX_DIM
    iw = ikw_ref[:, IDX_DIM:IDX_DIM + IDX_HEADS] * (ni ** -0.5)
    iqs = [iq_ref[:, IDX_DIM * h:IDX_DIM * (h + 1)] for h in range(IDX_HEADS)]
    ws = [iw[:, h:h + 1] for h in range(IDX_HEADS)]

    def visible(kb):
        kpos = kb * tk + cols
        return (jnp.right_shift(kpos, 6) <= qchunk) & (kpos < length)

    def score_body(kb, carry):
        ks = pl.multiple_of(kb * tk, tk)
        ikb = ik_ref[0, pl.ds(ks, tk), :]
        s = jnp.zeros((tq, tk), F32)
        for h in range(IDX_HEADS):
            s = s + ws[h] * jnp.maximum(_dot_t(iqs[h], ikb), 0.0)
        s = jnp.where(s == 0.0, 0.0, s)
        s = jnp.where(visible(kb), s, -jnp.inf)
        bits = lax.bitcast_convert_type(s, I32)
        skey[kb] = jnp.where(bits < 0, bits ^ jnp.int32(0x7FFFFFFF), bits)
        return carry

    lax.fori_loop(0, nkb, score_body, 0)

    def count(pred):
        def body(kb, acc):
            return acc + jnp.where(pred(kb, skey[kb]), 1.0, 0.0)
        acc = lax.fori_loop(0, nkb, body, jnp.zeros((tq, tk), F32))
        return jnp.sum(acc, axis=1, keepdims=True)

    def bit_body(it, prefix):
        cand_u = prefix | jnp.left_shift(jnp.int32(1), 31 - it)
        cand = cand_u ^ jnp.int32(INT_MIN)
        cnt = count(lambda kb, key: key >= cand)
        return jnp.where(cnt >= ntop, cand_u, prefix)

    prefix = lax.fori_loop(0, 32, bit_body, jnp.zeros((tq, 1), I32))
    tau = prefix ^ jnp.int32(INT_MIN)

    cnt_gt = count(lambda kb, key: key > tau)
    cnt_ge = count(lambda kb, key: key >= tau)
    need = ntop - cnt_gt
    key_ninf = jnp.int32(0x7FFFFFFF) ^ jnp.int32(-8388608)
    flag = ((cnt_ge - cnt_gt) > need) & (tau != key_ninf)
    x_s[...] = jnp.full(x_s.shape, (1 << idx_bits) - 1, I32)

    @pl.when(jnp.max(jnp.where(flag, 1.0, 0.0)) > 0.0)
    def _():
        def xbit(it, xlim):
            cand = xlim | jnp.left_shift(jnp.int32(1), idx_bits - 1 - it)
            cnt = count(lambda kb, key: (key == tau) & ((kb * tk + cols) < cand))
            return jnp.where(cnt <= need, cand, xlim)
        xlim = lax.fori_loop(0, idx_bits, xbit, jnp.zeros((tq, 1), I32))
        x_s[...] = jnp.broadcast_to(xlim, x_s.shape)

    xlim = x_s[:, 0:1]

    m_s[...] = jnp.full(m_s.shape, NEG_INF, F32)
    l_s[...] = jnp.zeros(l_s.shape, F32)
    acc_s[...] = jnp.zeros(acc_s.shape, F32)
    groups = B_HEADS // B_KV_HEADS
    reps = tk // LANES

    def attend(kb, bias_of):
        ks = pl.multiple_of(kb * tk, tk)
        key = skey[kb]
        sel = (key > tau) | ((key == tau) & ((kb * tk + cols) < xlim))
        madd = jnp.where(sel & visible(kb), 0.0, NEG_INF)
        for n in range(B_KV_HEADS):
            kn = k_ref[0, n, pl.ds(ks, tk), :]
            vn = v_ref[0, n, pl.ds(ks, tk), :]
            for g in range(groups):
                h = n * groups + g
                qh = q_ref[:, B_HEAD_DIM * h:B_HEAD_DIM * (h + 1)]
                s = _dot_t(qh, kn) + (bias_of(h) + madd)
                m_prev = m_s[h]
                m_new = jnp.maximum(m_prev, jnp.max(s, axis=1, keepdims=True))
                alpha = jnp.exp(m_prev - m_new)
                p = jnp.exp(s - jnp.tile(m_new, (1, reps)))
                l_s[h] = alpha * l_s[h] + jnp.sum(p, axis=1, keepdims=True)
                acc_s[h] = acc_s[h] * alpha[:, :B_HEAD_DIM] + _dot(p.astype(BF16), vn)
                m_s[h] = m_new

    def far_body(kb, carry):
        attend(kb, lambda h: far_ref[h, 0:1, :])
        return carry

    lax.fori_loop(0, nkb - 2, far_body, 0)

    @pl.when(nkb >= 2)
    def _():
        attend(nkb - 2, lambda h: near_ref[1, h])

    attend(nkb - 1, lambda h: near_ref[0, h])

    outs = [acc_s[h] / l_s[h][:, :B_HEAD_DIM] for h in range(B_HEADS)]
    o_ref[...] = jnp.concatenate(outs, axis=-1).astype(BF16)


def _dsa(q, iq, ikw, k_att, v_att, ik_att, near, far, t_len, tq, past, length, ntop):
    tk = KEY_BLOCK
    nb = SUBLANES
    lp = k_att.shape[2]
    nq = B_HEADS * B_HEAD_DIM
    ni = IDX_HEADS * IDX_DIM
    assert past % tk == 0 and (tq == tk or t_len == tq) and tq <= tk and lp % tk == 0
    idx_bits = lp.bit_length()

    def view(a):
        return a.reshape(t_len, nb * a.shape[1])

    def qblk(c):
        return pl.BlockSpec((tq, c), lambda b, i: (i, b))

    kern = functools.partial(_dsa_kernel, tq=tq, tk=tk, past=past, length=length, ntop=ntop,
                             idx_bits=idx_bits)
    out = pl.pallas_call(
        kern,
        grid=(nb, t_len // tq),
        in_specs=[qblk(nq), qblk(ni), qblk(LANES),
                  pl.BlockSpec((1, B_KV_HEADS, lp, B_HEAD_DIM), lambda b, i: (b, 0, 0, 0)),
                  pl.BlockSpec((1, B_KV_HEADS, lp, B_HEAD_DIM), lambda b, i: (b, 0, 0, 0)),
                  pl.BlockSpec((1, lp, IDX_DIM), lambda b, i: (b, 0, 0)),
                  pl.BlockSpec((2, B_HEADS, tq, tk), lambda b, i: (0, 0, 0, 0)),
                  pl.BlockSpec((B_HEADS, SUBLANES, tk), lambda b, i: (0, 0, 0))],
        out_specs=qblk(nq),
        out_shape=jax.ShapeDtypeStruct((t_len, nb * nq), BF16),
        scratch_shapes=[pltpu.VMEM((lp // tk, tq, tk), I32),
                        pltpu.VMEM((B_HEADS, tq, LANES), F32),
                        pltpu.VMEM((B_HEADS, tq, LANES), F32),
                        pltpu.VMEM((B_HEADS, tq, B_HEAD_DIM), F32),
                        pltpu.VMEM((tq, LANES), I32)],
        compiler_params=_params(("arbitrary", "arbitrary")),
        name="dsa",
    )(view(q), view(iq), view(ikw), k_att, v_att, ik_att, near, far)
    return out.reshape(t_len * nb, nq)


def _xq_tail(x1, gx_ref, wxq_ref, qnx_ref):
    xn = _rms(x1, gx_ref[...]).astype(BF16)
    qx = _head_rms(_dot(xn, wxq_ref[...]), X_HEAD_DIM, qnx_ref[...])
    return (qx * (X_HEAD_DIM ** -0.5)).astype(BF16)


def _even_out_kernel(x_ref, ya_ref, yb_ref, wo_ref, gx_ref, wxq_ref, qnx_ref, x1_ref, qx_ref):
    x1 = x_ref[...] + _dot(ya_ref[...], wo_ref[0:A_WIDTH, :]) + _dot(yb_ref[...], wo_ref[A_WIDTH:, :])
    x1_ref[...] = x1
    qx_ref[...] = _xq_tail(x1, gx_ref, wxq_ref, qnx_ref)


def _even_out(x, ya, yb, w_out, gx, w_xq, qnx, tm):
    rows = x.shape[0]
    d = D_MODEL

    def row(c):
        return pl.BlockSpec((tm, c), lambda i: (i, 0))

    return pl.pallas_call(
        _even_out_kernel,
        grid=(rows // tm,),
        in_specs=[row(d), row(A_WIDTH), row(B_HEADS * B_HEAD_DIM), _const_spec((d, d)),
                  _const_spec((1, d)), _const_spec((d, d)), _const_spec((1, d))],
        out_specs=[row(d), row(d)],
        out_shape=[jax.ShapeDtypeStruct((rows, d), F32), jax.ShapeDtypeStruct((rows, d), BF16)],
        compiler_params=_params(("arbitrary",)),
        name="even_out",
    )(x, ya, yb, w_out, gx, w_xq, qnx)


def _xattn_kernel(q_ref, mk_ref, mv_ref, o_ref):
    for h in range(X_HEADS):
        sl = slice(h * X_HEAD_DIM, (h + 1) * X_HEAD_DIM)
        s = _dot_t(q_ref[:, sl], mk_ref[0, :, sl])
        p = jnp.exp(s - jnp.max(s, axis=1, keepdims=True))
        o = _dot(p.astype(BF16), mv_ref[0, :, sl]) / jnp.sum(p, axis=1, keepdims=True)
        o_ref[:, sl] = o.astype(BF16)


def _xattn(qx, mk, mv, t_len, tq):
    nb = SUBLANES
    d = D_MODEL
    m = mk.shape[1]
    out = pl.pallas_call(
        _xattn_kernel,
        grid=(nb, t_len // tq),
        in_specs=[pl.BlockSpec((tq, d), lambda b, i: (i, b)),
                  pl.BlockSpec((1, m, d), lambda b, i: (b, 0, 0)),
                  pl.BlockSpec((1, m, d), lambda b, i: (b, 0, 0))],
        out_specs=pl.BlockSpec((tq, d), lambda b, i: (i, b)),
        out_shape=jax.ShapeDtypeStruct((t_len, nb * d), BF16),
        compiler_params=_params(("arbitrary", "arbitrary")),
        name="mem_attn",
    )(qx.reshape(t_len, nb * d), mk, mv)
    return out.reshape(t_len * nb, d)


def _ffn_kernel(x_ref, o_ref, wxo_ref, g_ref, wup_ref, cw_ref, cb_ref, wdn_ref, hist_ref,
                y_ref, fh_ref, gbuf, *, chunk):
    tm = x_ref.shape[0]
    nh = (F_CONV - 1) * SUBLANES

    @pl.when(pl.program_id(0) == 0)
    def _():
        gbuf[0:nh, :] = hist_ref[...]

    x2 = x_ref[...] + _dot(o_ref[...], wxo_ref[...])
    xn = _rms(x2, g_ref[...]).astype(BF16)
    acc = jnp.zeros((tm, D_MODEL), F32)
    for c0 in range(0, D_FF, chunk):
        c1 = c0 + chunk
        val = _dot(xn, wup_ref[:, c0:c1])
        gate = _dot(xn, wup_ref[:, D_FF + c0:D_FF + c1])
        gbuf[nh:nh + tm, c0:c1] = gate
        conv = gate * cw_ref[F_CONV - 1:F_CONV, c0:c1]
        for i in range(F_CONV - 1):
            conv = conv + gbuf[i * SUBLANES:i * SUBLANES + tm, c0:c1] * cw_ref[i:i + 1, c0:c1]
        conv = conv + cb_ref[:, c0:c1]
        act = (jax.nn.gelu(conv) * val).astype(BF16)
        acc = acc + _dot(act, wdn_ref[c0:c1, :])
    tail = gbuf[tm:tm + nh, :]
    gbuf[0:nh, :] = tail
    fh_ref[...] = tail
    y_ref[...] = x2 + acc


def _ffn(x, o, w_xo, g, w_up, conv_w, conv_b, w_down, hist, tm):
    rows = x.shape[0]
    d = D_MODEL
    nh = (F_CONV - 1) * SUBLANES

    def row(c):
        return pl.BlockSpec((tm, c), lambda i: (i, 0))

    return pl.pallas_call(
        functools.partial(_ffn_kernel, chunk=MXU_DIM),
        grid=(rows // tm,),
        in_specs=[row(d), row(d), _const_spec((d, d)), _const_spec((1, d)), _const_spec((d, 2 * D_FF)),
                  _const_spec((F_CONV, D_FF)), _const_spec((1, D_FF)), _const_spec((D_FF, d)),
                  _const_spec((nh, D_FF))],
        out_specs=[row(d), pl.BlockSpec((nh, D_FF), lambda i: (0, 0))],
        out_shape=[jax.ShapeDtypeStruct((rows, d), F32), jax.ShapeDtypeStruct((nh, D_FF), F32)],
        scratch_shapes=[pltpu.VMEM((tm + nh, D_FF), F32)],
        compiler_params=_params(("arbitrary",)),
        name="ffn",
    )(x, o, w_xo, g, w_up, conv_w, conv_b, w_down, hist)


def _odd_kernel(x_ref, g_ref, win_ref, cw_ref, cb_ref, wa_ref, ba_ref, wi_ref, bi_ref, lam_ref,
                wo_ref, hist_ref, h0_ref, gx_ref, wxq_ref, qnx_ref,
                x1_ref, qx_ref, ch_ref, hl_ref, xbuf, a_s, b_s, h_s, *, stream_start):
    tm = x_ref.shape[0]
    nh = (C_CONV - 1) * SUBLANES
    first = pl.program_id(0) == 0

    @pl.when(first)
    def _():
        xbuf[0:nh, :] = hist_ref[...]
        h_s[...] = h0_ref[...]

    x = x_ref[...]
    xn = _rms(x, g_ref[...]).astype(BF16)
    xr_in = _dot(xn, win_ref[:, RNN_WIDTH:])
    xbuf[nh:nh + tm, :] = xr_in
    xr = _conv_taps(xbuf, xr_in, cw_ref, C_CONV, tm) + cb_ref[...]
    tail = xbuf[tm:tm + nh, :]
    xbuf[0:nh, :] = tail
    ch_ref[...] = tail

    xrb = xr.astype(BF16)
    lam = -lam_ref[...]
    sp = jnp.maximum(lam, 0.0) + jnp.log1p(jnp.exp(-jnp.abs(lam)))
    rows = lax.broadcasted_iota(I32, (tm, RNN_BLOCK), 0)
    for n in range(RNN_BLOCKS):
        sl = slice(n * RNN_BLOCK, (n + 1) * RNN_BLOCK)
        r = jax.nn.sigmoid(_dot(xrb[:, sl], wa_ref[n]) + ba_ref[:, sl])
        ig = jax.nn.sigmoid(_dot(xrb[:, sl], wi_ref[n]) + bi_ref[:, sl])
        log_a = -RG_C * r * sp[:, sl]
        a = jnp.exp(log_a)
        mult = jnp.sqrt(jnp.tanh(-log_a) * (1.0 + a * a))
        if stream_start:
            mult = jnp.where(first & (rows < SUBLANES), 1.0, mult)
        a_s[:, sl] = a
        b_s[:, sl] = mult * ig * xr[:, sl]

    def step(t, h):
        r0 = pl.multiple_of(t * SUBLANES, SUBLANES)
        h = a_s[pl.ds(r0, SUBLANES), :] * h + b_s[pl.ds(r0, SUBLANES), :]
        b_s[pl.ds(r0, SUBLANES), :] = h
        return h

    h = lax.fori_loop(0, tm // SUBLANES, step, h_s[...], unroll=8)
    h_s[...] = h
    hl_ref[...] = h

    gate = _dot(xn, win_ref[:, :RNN_WIDTH])
    act = (jax.nn.gelu(gate) * b_s[...]).astype(BF16)
    x1 = x + _dot(act, wo_ref[...])
    x1_ref[...] = x1
    qx_ref[...] = _xq_tail(x1, gx_ref, wxq_ref, qnx_ref)


def _odd(x, g, w_in, conv_w, conv_b, w_a, b_a, w_i, b_i, lam, w_out, hist, h0, gx, w_xq, qnx, tm,
         stream_start):
    rows = x.shape[0]
    d = D_MODEL
    r = RNN_WIDTH
    nh = (C_CONV - 1) * SUBLANES

    def row(c):
        return pl.BlockSpec((tm, c), lambda i: (i, 0))

    blk = (RNN_BLOCKS, RNN_BLOCK, RNN_BLOCK)
    return pl.pallas_call(
        functools.partial(_odd_kernel, stream_start=stream_start),
        grid=(rows // tm,),
        in_specs=[row(d), _const_spec((1, d)), _const_spec((d, 2 * r)), _const_spec((C_CONV, r)),
                  _const_spec((1, r)), _const_spec(blk), _const_spec((1, r)), _const_spec(blk),
                  _const_spec((1, r)), _const_spec((1, r)), _const_spec((r, d)), _const_spec((nh, r)),
                  _const_spec((SUBLANES, r)), _const_spec((1, d)), _const_spec((d, d)), _const_spec((1, d))],
        out_specs=[row(d), row(d), pl.BlockSpec((nh, r), lambda i: (0, 0)),
                   pl.BlockSpec((SUBLANES, r), lambda i: (0, 0))],
        out_shape=[jax.ShapeDtypeStruct((rows, d), F32), jax.ShapeDtypeStruct((rows, d), BF16),
                   jax.ShapeDtypeStruct((nh, r), F32), jax.ShapeDtypeStruct((SUBLANES, r), F32)],
        scratch_shapes=[pltpu.VMEM((tm + nh, r), F32), pltpu.VMEM((tm, r), F32), pltpu.VMEM((tm, r), F32),
                        pltpu.VMEM((SUBLANES, r), F32)],
        compiler_params=_params(("arbitrary",)),
        name="odd_mixer",
    )(x, g, w_in, conv_w, conv_b, w_a, b_a, w_i, b_i, lam, w_out, hist, h0, gx, w_xq, qnx)


def _to_tm(a):
    return jnp.transpose(a, (1, 0, 2)).reshape(a.shape[1] * a.shape[0], a.shape[2])


def _from_tm(a, w):
    return jnp.transpose(a.reshape(w, SUBLANES, a.shape[1]), (1, 0, 2))


def _trunk(x, st, mem_k, mem_v, p, bias, tm, tq_dsa, tq_x):
    nb, t_len, d = x.shape
    assert nb == SUBLANES
    rows = t_len * nb
    past = 0 if st is None else st["b_k"].shape[2]
    length = past + t_len
    ntop = min(TOPK_MAX, length // 4)
    near, far = bias
    xt = jnp.transpose(x, (1, 0, 2)).reshape(rows, d)
    out = {}

    def hist(name, l, width, c):
        if st is None:
            return jnp.zeros(((width - 1) * nb, c), F32)
        return _to_tm(st[name][l])

    for l in range(DEPTH):
        if l % 2 == 0:
            e = l // 2
            ya, q, k, v, iq, ikw, uh = _even_in(
                xt, p["g_mix"][l], p["w_in_even"][e], p["a_conv_w"][e], p["b_q_norm"][e], p["b_k_norm"][e],
                hist("a_conv", e, A_CONV, A_WIDTH), tm)
            k_new = _from_tm(k, t_len).reshape(nb, t_len, B_KV_HEADS, B_HEAD_DIM)
            v_new = _from_tm(v, t_len).reshape(nb, t_len, B_KV_HEADS, B_HEAD_DIM)
            ik_new = _from_tm(ikw[:, :IDX_DIM], t_len)
            k_all, v_all, ik_all = k_new, v_new, ik_new
            if st is not None:
                k_all = jnp.concatenate([st["b_k"][e], k_new], axis=1)
                v_all = jnp.concatenate([st["b_v"][e], v_new], axis=1)
                ik_all = jnp.concatenate([st["b_kidx"][e], ik_new], axis=1)
            lp = -(-length // KEY_BLOCK) * KEY_BLOCK
            padl = lp - length
            k_att = jnp.pad(jnp.transpose(k_all, (0, 2, 1, 3)).astype(BF16), ((0, 0), (0, 0), (0, padl), (0, 0)))
            v_att = jnp.pad(jnp.transpose(v_all, (0, 2, 1, 3)).astype(BF16), ((0, 0), (0, 0), (0, padl), (0, 0)))
            ik_att = jnp.pad(ik_all.astype(BF16), ((0, 0), (0, padl), (0, 0)))
            yb = _dsa(q, iq, ikw, k_att, v_att, ik_att, near, far, t_len, tq_dsa, past, length, ntop)
            x1, qx = _even_out(xt, ya, yb, p["w_out_even"][e], p["g_x"][l], p["w_xq"][l], p["x_q_norm"][l], tm)
            out.setdefault("a_conv", []).append(_from_tm(uh, A_CONV - 1))
            out.setdefault("b_k", []).append(k_new)
            out.setdefault("b_v", []).append(v_new)
            out.setdefault("b_kidx", []).append(ik_new)
        else:
            o = l // 2
            h0 = jnp.zeros((nb, RNN_WIDTH), F32) if st is None else st["c_h"][o]
            x1, qx, ch, hl = _odd(
                xt, p["g_mix"][l], p["w_in_odd"][o], p["c_conv_w"][o], p["c_conv_b"][o], p["c_w_a"][o],
                p["c_b_a"][o], p["c_w_i"][o], p["c_b_i"][o], p["c_lambda"][o], p["w_out_odd"][o],
                hist("c_conv", o, C_CONV, RNN_WIDTH), h0, p["g_x"][l], p["w_xq"][l], p["x_q_norm"][l], tm,
                stream_start=(past == 0))
            out.setdefault("c_conv", []).append(_from_tm(ch, C_CONV - 1))
            out.setdefault("c_h", []).append(hl)
        xo = _xattn(qx, mem_k[l], mem_v[l], t_len, tq_x)
        xt, fh = _ffn(x1, xo, p["w_xo"][l], p["g_ffn"][l], p["w_up"][l], p["f_conv_w"][l], p["f_conv_b"][l],
                      p["w_down"][l], hist("f_conv", l, F_CONV, D_FF), tm)
        out.setdefault("f_conv", []).append(_from_tm(fh, F_CONV - 1))
    y = jnp.transpose(xt.reshape(t_len, nb, d), (1, 0, 2))
    return y, {name: jnp.stack(v) for name, v in out.items()}


def kernel(x_prompt, x_sample, cache_b_k, cache_b_v, cache_b_kidx, state_a_conv, state_c_conv, state_c_h, state_ffn_conv, cache_mem_k, cache_mem_v, mem_prompt, rel_table, g_mix, w_in_even, a_conv_w, b_q_norm, b_k_norm, w_out_even, w_in_odd, c_conv_w, c_conv_b, c_w_a, c_b_a, c_w_i, c_b_i, c_lambda, w_out_odd, g_mem, g_x, w_xq, w_xk, w_xv, x_q_norm, x_k_norm, w_xo, g_ffn, w_up, f_conv_w, f_conv_b, w_down):
    d = D_MODEL
    n_even = w_in_even.shape[0]
    n_odd = w_in_odd.shape[0]
    bp, t_p, _ = x_prompt.shape
    m = mem_prompt.shape[1]

    def rowvec(a):
        return a.reshape(a.shape[0], 1, a.shape[-1])

    p = {
        "g_mix": rowvec(g_mix), "g_x": rowvec(g_x), "g_ffn": rowvec(g_ffn),
        "w_in_even": jnp.pad(w_in_even, ((0, 0), (0, 0), (0, EVEN_IN_PAD - EVEN_IN))).astype(BF16),
        "a_conv_w": a_conv_w,
        "b_q_norm": rowvec(jnp.tile(b_q_norm, (1, B_HEADS))),
        "b_k_norm": rowvec(jnp.tile(b_k_norm, (1, B_KV_HEADS))),
        "w_out_even": w_out_even.astype(BF16),
        "w_in_odd": w_in_odd.astype(BF16), "c_conv_w": c_conv_w, "c_conv_b": rowvec(c_conv_b),
        "c_w_a": c_w_a.astype(BF16), "c_b_a": rowvec(c_b_a), "c_w_i": c_w_i.astype(BF16), "c_b_i": rowvec(c_b_i),
        "c_lambda": rowvec(c_lambda), "w_out_odd": w_out_odd.astype(BF16),
        "w_xq": w_xq.astype(BF16), "x_q_norm": rowvec(jnp.tile(x_q_norm, (1, X_HEADS))),
        "w_xo": w_xo.astype(BF16), "w_up": w_up.astype(BF16), "f_conv_w": f_conv_w,
        "f_conv_b": rowvec(f_conv_b), "w_down": w_down.astype(BF16),
    }
    del n_even, n_odd
    bias = _bias_tiles(rel_table, KEY_BLOCK)

    mk, mv = _mem_kv(mem_prompt.reshape(bp * m, d), g_mem, w_xk, x_k_norm, w_xv)
    p_mem_k = mk.reshape(DEPTH, bp, m, X_HEADS, X_HEAD_DIM)
    p_mem_v = mv.reshape(DEPTH, bp, m, X_HEADS, X_HEAD_DIM)
    y_prompt, new_p = _trunk(x_prompt, None, mk.reshape(DEPTH, bp, m, d).astype(BF16),
                             mv.reshape(DEPTH, bp, m, d).astype(BF16), p, bias,
                             tm=512, tq_dsa=KEY_BLOCK, tq_x=512)

    bs, t_s, _ = x_sample.shape
    st_s = {"b_k": cache_b_k, "b_v": cache_b_v, "b_kidx": cache_b_kidx, "a_conv": state_a_conv,
            "c_conv": state_c_conv, "c_h": state_c_h, "f_conv": state_ffn_conv}
    ms = cache_mem_k.shape[2]
    y_sample, new_s = _trunk(x_sample, st_s, cache_mem_k.reshape(DEPTH, bs, ms, d).astype(BF16),
                             cache_mem_v.reshape(DEPTH, bs, ms, d).astype(BF16), p, bias,
                             tm=bs * t_s, tq_dsa=t_s, tq_x=t_s)
    return (y_prompt, y_sample,
            new_p["b_k"], new_p["b_v"], new_p["b_kidx"], new_p["a_conv"], new_p["c_conv"],
            new_p["c_h"], new_p["f_conv"], p_mem_k, p_mem_v,
            new_s["b_k"], new_s["b_v"], new_s["b_kidx"], new_s["a_conv"], new_s["c_conv"],
            new_s["c_h"], new_s["f_conv"])
```

```python
import functools
import math

import jax
import jax.numpy as jnp
from jax import lax
from jax.experimental import pallas as pl
from jax.experimental.pallas import tpu as pltpu

F32 = jnp.float32
BF16 = jnp.bfloat16
I32 = jnp.int32

D_MODEL = 1024
DEPTH = 2
CHUNK = 64
EPS = 1e-6
NEG_INF = -1e30
A_WIDTH = 512
A_CONV = 3
B_HEADS = 8
B_KV_HEADS = 2
B_HEAD_DIM = 64
IDX_HEADS = 8
IDX_DIM = 32
TOPK_MAX = 256
REL_BUCKETS = 32
REL_MAX_DIST = 128
RNN_WIDTH = 1024
RNN_BLOCKS = 8
RNN_BLOCK = 128
C_CONV = 4
RG_C = 8.0
X_HEADS = 4
X_HEAD_DIM = 256
D_FF = 2816
F_CONV = 3
EVEN_IN = 2600

SUBLANES = 8
LANES = 128
MXU_DIM = 256
VMEM_LIMIT = 56 * 1024 * 1024

EVEN_IN_PAD = 2688
KEY_BLOCK = 256
INT_MIN = -2147483648


def _params(sem, vmem=VMEM_LIMIT):
    return pltpu.CompilerParams(dimension_semantics=sem, vmem_limit_bytes=vmem)


def _const_spec(shape):
    nd = len(shape)
    return pl.BlockSpec(shape, lambda *_: (0,) * nd, pipeline_mode=pl.Buffered(1))


def _rms(x, g):
    ms = jnp.mean(x * x, axis=-1, keepdims=True)
    return x * lax.rsqrt(ms + EPS) * g


def _head_rms(x, hd, gain):
    m, c = x.shape
    s = x * x
    parts = []
    if hd >= LANES:
        for h in range(c // hd):
            ms = jnp.mean(s[:, h * hd:(h + 1) * hd], axis=-1, keepdims=True)
            parts.append(x[:, h * hd:(h + 1) * hd] * lax.rsqrt(ms + EPS))
    else:
        lane = lax.broadcasted_iota(I32, (m, LANES), 1)
        for j in range(c // LANES):
            sj = s[:, j * LANES:(j + 1) * LANES]
            inv = jnp.zeros((m, LANES), F32)
            for k in range(LANES // hd):
                msk = (lane >= k * hd) & (lane < (k + 1) * hd)
                ms = jnp.sum(jnp.where(msk, sj, 0.0), axis=-1, keepdims=True) * (1.0 / hd)
                inv = jnp.where(msk, lax.rsqrt(ms + EPS), inv)
            parts.append(x[:, j * LANES:(j + 1) * LANES] * inv)
    y = parts[0] if len(parts) == 1 else jnp.concatenate(parts, axis=-1)
    return y * gain


def _dot(a, b):
    return jnp.dot(a, b, preferred_element_type=F32)


def _dot_t(a, b):
    return lax.dot_general(a, b, (((1,), (1,)), ((), ())), preferred_element_type=F32)


def _conv_taps(buf, cur, w_ref, width, tm):
    hist = (width - 1) * SUBLANES
    y = cur * w_ref[width - 1:width, :]
    for i in range(width - 1):
        y = y + buf[i * SUBLANES:i * SUBLANES + tm, :] * w_ref[i:i + 1, :]
    del hist
    return y


def _memkv_kernel(mem_ref, g_ref, wk_ref, kn_ref, wv_ref, k_ref, v_ref):
    hm = _rms(mem_ref[...], g_ref[0]).astype(BF16)
    k_ref[0] = _head_rms(_dot(hm, wk_ref[0]), X_HEAD_DIM, kn_ref[0])
    v_ref[0] = _dot(hm, wv_ref[0])


def _mem_kv(mem, g_mem, w_xk, x_k_norm, w_xv):
    rows = mem.shape[0]
    tm = min(512, rows)
    d = D_MODEL
    kn = jnp.tile(x_k_norm, (1, X_HEADS)).reshape(DEPTH, 1, d)
    out = jax.ShapeDtypeStruct((DEPTH, rows, d), F32)
    return pl.pallas_call(
        _memkv_kernel,
        grid=(DEPTH, rows // tm),
        in_specs=[
            pl.BlockSpec((tm, d), lambda l, i: (i, 0)),
            pl.BlockSpec((1, 1, d), lambda l, i: (l, 0, 0)),
            pl.BlockSpec((1, d, d), lambda l, i: (l, 0, 0)),
            pl.BlockSpec((1, 1, d), lambda l, i: (l, 0, 0)),
            pl.BlockSpec((1, d, d), lambda l, i: (l, 0, 0)),
        ],
        out_specs=[pl.BlockSpec((1, tm, d), lambda l, i: (l, i, 0))] * 2,
        out_shape=[out, out],
        compiler_params=_params(("arbitrary", "arbitrary")),
        name="mem_kv",
    )(mem, g_mem.reshape(DEPTH, 1, d), w_xk.astype(BF16), kn, w_xv.astype(BF16))


def _even_in_kernel(x_ref, g_ref, w_ref, cw_ref, qn_ref, kn_ref, hist_ref,
                    ya_ref, q_ref, k_ref, v_ref, iq_ref, ikw_ref, uh_ref, ubuf):
    tm = x_ref.shape[0]
    nh = (A_CONV - 1) * SUBLANES

    @pl.when(pl.program_id(0) == 0)
    def _():
        ubuf[0:nh, :] = hist_ref[...]

    xn = _rms(x_ref[...], g_ref[...]).astype(BF16)

    def proj(a, b):
        return _dot(xn, w_ref[:, a:b])

    zc = proj(A_WIDTH, 3 * A_WIDTH)
    u = zc[:, :A_WIDTH] * zc[:, A_WIDTH:]
    ubuf[nh:nh + tm, :] = u
    conv = _conv_taps(ubuf, u, cw_ref, A_CONV, tm)
    ya_ref[...] = (proj(0, A_WIDTH) * conv).astype(BF16)
    tail = ubuf[tm:tm + nh, :]
    ubuf[0:nh, :] = tail
    uh_ref[...] = tail

    o = 3 * A_WIDTH
    nq = B_HEADS * B_HEAD_DIM
    nkv = B_KV_HEADS * B_HEAD_DIM
    q_ref[...] = (_head_rms(proj(o, o + nq), B_HEAD_DIM, qn_ref[...]) * (B_HEAD_DIM ** -0.5)).astype(BF16)
    zkv = proj(o + nq, o + nq + 2 * nkv)
    k_ref[...] = _head_rms(zkv[:, :nkv], B_HEAD_DIM, kn_ref[...])
    v_ref[...] = zkv[:, nkv:]
    o2 = o + nq + 2 * nkv
    ni = IDX_HEADS * IDX_DIM
    zi = proj(o2, EVEN_IN_PAD)
    iq_ref[...] = zi[:, :ni].astype(BF16)
    ikw_ref[...] = zi[:, ni:]


def _even_in(x, g, w_pad, conv_w, qn, kn, hist, tm):
    rows = x.shape[0]
    d = D_MODEL
    nh = (A_CONV - 1) * SUBLANES
    nq = B_HEADS * B_HEAD_DIM
    nkv = B_KV_HEADS * B_HEAD_DIM
    ni = IDX_HEADS * IDX_DIM

    def row(c):
        return pl.BlockSpec((tm, c), lambda i: (i, 0))

    return pl.pallas_call(
        _even_in_kernel,
        grid=(rows // tm,),
        in_specs=[row(d), _const_spec((1, d)), _const_spec((d, EVEN_IN_PAD)), _const_spec((A_CONV, A_WIDTH)),
                  _const_spec((1, nq)), _const_spec((1, nkv)), _const_spec((nh, A_WIDTH))],
        out_specs=[row(A_WIDTH), row(nq), row(nkv), row(nkv), row(ni), row(LANES),
                   pl.BlockSpec((nh, A_WIDTH), lambda i: (0, 0))],
        out_shape=[jax.ShapeDtypeStruct((rows, A_WIDTH), BF16), jax.ShapeDtypeStruct((rows, nq), BF16),
                   jax.ShapeDtypeStruct((rows, nkv), F32), jax.ShapeDtypeStruct((rows, nkv), F32),
                   jax.ShapeDtypeStruct((rows, ni), BF16), jax.ShapeDtypeStruct((rows, LANES), F32),
                   jax.ShapeDtypeStruct((nh, A_WIDTH), F32)],
        scratch_shapes=[pltpu.VMEM((tm + nh, A_WIDTH), F32)],
        compiler_params=_params(("arbitrary",)),
        name="even_in",
    )(x, g, w_pad, conv_w, qn, kn, hist)


def _rel_bucket(rel):
    half = REL_BUCKETS // 2
    max_exact = half // 2
    n = -rel
    ret = jnp.where(n < 0, half, 0)
    n = jnp.abs(n)
    nf = jnp.maximum(n, 1).astype(F32)
    large = max_exact + (jnp.log(nf / max_exact) / math.log(REL_MAX_DIST / max_exact)
                         * (half - max_exact)).astype(I32)
    large = jnp.minimum(large, half - 1)
    return ret + jnp.where(n < max_exact, n, large)


def _bias_kernel(tab_ref, near_ref, far_ref):
    tk = near_ref.shape[-1]
    r = lax.broadcasted_iota(I32, (tk, tk), 0)
    c = lax.broadcasted_iota(I32, (tk, tk), 1)

    def lookup(bucket, h):
        def body(j, acc):
            return jnp.where(bucket == j, tab_ref[j, h], acc)
        return lax.fori_loop(0, REL_BUCKETS, body, jnp.zeros(bucket.shape, F32))

    for blk in range(2):
        bucket = _rel_bucket(r - c - blk * tk)
        for h in range(B_HEADS):
            near_ref[blk, h] = lookup(bucket, h)
    bucket = _rel_bucket(-REL_MAX_DIST - c[0:SUBLANES, :])
    for h in range(B_HEADS):
        far_ref[h] = lookup(bucket, h)


def _bias_tiles(rel_table, tk):
    return pl.pallas_call(
        _bias_kernel,
        in_specs=[pl.BlockSpec(memory_space=pltpu.SMEM)],
        out_shape=[jax.ShapeDtypeStruct((2, B_HEADS, tk, tk), F32),
                   jax.ShapeDtypeStruct((B_HEADS, SUBLANES, tk), F32)],
        name="rel_bias",
    )(rel_table)


def _dsa_kernel(q_ref, iq_ref, ikw_ref, k_ref, vt_ref, ik_ref, near_ref, far_ref, o_ref,
                skey, madd_s, s_s, p_s, qh_s, iqh_s, pad_s, w_s, x_s, m_s, l_s, acc_s,
                *, tq, qw, tk, past, length, ntop, idx_bits):
    i = pl.program_id(1)
    q0 = past + i * tq
    nkb = (q0 + tq + tk - 1) // tk
    last = nkb - 1
    ni = IDX_HEADS * IDX_DIM
    groups = B_HEADS // B_KV_HEADS
    lane = lax.broadcasted_iota(I32, (1, qw), 1)
    sub8 = lax.broadcasted_iota(I32, (SUBLANES, qw), 0)
    qchunk = jnp.right_shift(q0 + lane, 6)

    if tq != qw:
        qh_s[...] = jnp.zeros(qh_s.shape, BF16)
        iqh_s[...] = jnp.zeros(iqh_s.shape, BF16)
        pad_s[...] = jnp.zeros(pad_s.shape, F32)
    for h in range(B_HEADS):
        qh_s[h, 0:tq, :] = q_ref[:, B_HEAD_DIM * h:B_HEAD_DIM * (h + 1)]
    for h in range(IDX_HEADS):
        iqh_s[h, 0:tq, :] = iq_ref[:, IDX_DIM * h:IDX_DIM * (h + 1)]
    pad_s[0:tq, :] = ikw_ref[...]
    w_s[...] = pad_s[...].T[IDX_DIM:IDX_DIM + IDX_HEADS, :] * (ni ** -0.5)

    slab = tk

    def scores(kb, masked):
        for sl in range(tk // slab):
            ks = pl.multiple_of(kb * tk + sl * slab, slab)
            ikb = ik_ref[0, pl.ds(ks, slab), :]
            acc = jnp.zeros((slab, qw), F32)
            for h in range(IDX_HEADS):
                acc = acc + w_s[h:h + 1, :] * jnp.maximum(_dot_t(ikb, iqh_s[h]), 0.0)
            if masked:
                kpos = kb * tk + sl * slab + lax.broadcasted_iota(I32, (slab, qw), 0)
                vis = (jnp.right_shift(kpos, 6) <= qchunk) & (kpos < length)
                acc = jnp.where(vis, acc, -jnp.inf)
            bits = lax.bitcast_convert_type(acc, I32)
            skey[kb, sl * slab:(sl + 1) * slab, :] = jnp.where(bits < 0, bits ^ jnp.int32(0x7FFFFFFF), bits)

    def score_body(kb, carry):
        scores(kb, False)
        return carry

    lax.fori_loop(0, last, score_body, 0)
    scores(last, True)

    n_acc = 4

    def count(pred):
        def body(kb, accs):
            accs = list(accs)
            for g in range(tk // SUBLANES):
                blk = skey[kb, g * SUBLANES:(g + 1) * SUBLANES, :]
                accs[g % n_acc] = accs[g % n_acc] + jnp.where(pred(kb, g, blk), 1, 0)
            return tuple(accs)
        accs = lax.fori_loop(0, nkb, body, tuple(jnp.zeros((SUBLANES, qw), I32) for _ in range(n_acc)))
        tot = (accs[0] + accs[1]) + (accs[2] + accs[3])
        return jnp.sum(tot, axis=0, keepdims=True)

    def rep8(v):
        return jnp.broadcast_to(v, (SUBLANES, qw))

    def bit_body(it, prefix):
        cand_u = prefix | jnp.left_shift(jnp.int32(1), 31 - it)
        cand8 = rep8(cand_u ^ jnp.int32(INT_MIN))
        cnt = count(lambda kb, g, blk: blk >= cand8)
        return jnp.where(cnt >= ntop, cand_u, prefix)

    prefix = lax.fori_loop(0, 32, bit_body, jnp.zeros((1, qw), I32))
    tau = prefix ^ jnp.int32(INT_MIN)
    tau8 = rep8(tau)

    cnt_gt = count(lambda kb, g, blk: blk > tau8)
    cnt_ge = count(lambda kb, g, blk: blk >= tau8)
    need = ntop - cnt_gt
    key_ninf = jnp.int32(0x7FFFFFFF) ^ jnp.int32(-8388608)
    finite = tau != key_ninf
    tie_rows = ((cnt_ge - cnt_gt) > need) & finite & (lane < tq)
    x_s[...] = rep8(jnp.where(finite, jnp.int32((1 << idx_bits) - 1), 0))
    any_tie = jnp.max(jnp.where(tie_rows, 1, 0)) > 0

    @pl.when(any_tie)
    def _():
        need8 = rep8(need)

        def xbit(it, xlim):
            cand8 = xlim | jnp.left_shift(jnp.int32(1), idx_bits - 1 - it)
            cnt = count(lambda kb, g, blk: (blk == tau8) & ((kb * tk + g * SUBLANES + sub8) < cand8))
            return jnp.where(rep8(cnt) <= need8, cand8, xlim)
        xlim = lax.fori_loop(0, idx_bits, xbit, jnp.zeros((SUBLANES, qw), I32))
        x_s[...] = jnp.where(rep8(finite), xlim, 0)

    m_s[...] = jnp.full(m_s.shape, NEG_INF, F32)
    l_s[...] = jnp.zeros(l_s.shape, F32)
    acc_s[...] = jnp.zeros(acc_s.shape, F32)
    prow = 16

    tau_ge8 = rep8(jnp.where(finite, tau, tau + 1))

    def make_mask(kb):
        @pl.when(jnp.logical_not(any_tie))
        def _():
            for g in range(tk // SUBLANES):
                rows = slice(g * SUBLANES, (g + 1) * SUBLANES)
                madd_s[rows, :] = jnp.where(skey[kb, rows, :] >= tau_ge8, 0.0, NEG_INF)

        @pl.when(any_tie)
        def _():
            xlim8 = x_s[...]
            for g in range(tk // SUBLANES):
                rows = slice(g * SUBLANES, (g + 1) * SUBLANES)
                key = skey[kb, rows, :]
                sel = (key > tau8) | ((key == tau8) & ((kb * tk + g * SUBLANES + sub8) < xlim8))
                madd_s[rows, :] = jnp.where(sel, 0.0, NEG_INF)

    def attend(kb, near_blk):
        ks = pl.multiple_of(kb * tk, tk)
        make_mask(kb)
        shifts, alphas = [], []
        for h in range(B_HEADS):
            kn = k_ref[0, h // groups, pl.ds(ks, tk), :]
            if near_blk is None:
                t = _dot_t(kn, qh_s[h]) + madd_s[...]
                c_h = far_ref[h, 0:1, 0:qw]
            else:
                t = _dot_t(kn, qh_s[h]) + (madd_s[...] + near_ref[near_blk, h])
                c_h = jnp.zeros((1, qw), F32)
            s_s[h] = t
            mx = [t[j * SUBLANES:(j + 1) * SUBLANES, :] for j in range(n_acc)]
            for g in range(n_acc, tk // SUBLANES):
                mx[g % n_acc] = jnp.maximum(mx[g % n_acc], t[g * SUBLANES:(g + 1) * SUBLANES, :])
            m_cur = jnp.max(jnp.maximum(jnp.maximum(mx[0], mx[1]), jnp.maximum(mx[2], mx[3])), axis=0, keepdims=True)
            m_prev = m_s[h:h + 1, :]
            m_new = jnp.maximum(m_prev, m_cur + c_h)
            m_s[h:h + 1, :] = m_new
            alphas.append(jnp.exp(m_prev - m_new))
            shifts.append(m_new - c_h)
        for h in range(B_HEADS):
            shift = jnp.broadcast_to(shifts[h], (prow, qw))
            lacc = [jnp.zeros((prow, qw), F32), jnp.zeros((prow, qw), F32)]
            for r in range(tk // prow):
                rows = slice(r * prow, (r + 1) * prow)
                p = jnp.exp(s_s[h, rows, :] - shift)
                lacc[r % 2] = lacc[r % 2] + p
                p_s[h, rows, :] = p.astype(BF16)
            l_s[h] = alphas[h] * l_s[h] + (lacc[0] + lacc[1])
        for h in range(B_HEADS):
            hs = slice(h * B_HEAD_DIM, (h + 1) * B_HEAD_DIM)
            acc_s[hs, :] = acc_s[hs, :] * alphas[h] + _dot(vt_ref[0, h // groups, kb], p_s[h])

    def far_body(kb, carry):
        attend(kb, None)
        return carry

    lax.fori_loop(0, nkb - 2, far_body, 0)

    @pl.when(nkb >= 2)
    def _():
        attend(nkb - 2, 1)

    attend(last, 0)

    outs = []
    for h in range(B_HEADS):
        hs = slice(h * B_HEAD_DIM, (h + 1) * B_HEAD_DIM)
        outs.append(acc_s[hs, :] / jnp.sum(l_s[h], axis=0, keepdims=True))
    o = jnp.concatenate(outs, axis=0).T
    o_ref[...] = o[0:tq, :].astype(BF16)


def _dsa(q, iq, ikw, k_att, vt_att, ik_att, near, far, t_len, tq, past, length, ntop):
    tk = KEY_BLOCK
    nb = SUBLANES
    lp = k_att.shape[2]
    nq = B_HEADS * B_HEAD_DIM
    ni = IDX_HEADS * IDX_DIM
    qw = max(tq, LANES)
    assert past % tk == 0 and (tq == tk or t_len == tq) and tq <= tk and lp % tk == 0
    idx_bits = lp.bit_length()

    def view(a):
        return a.reshape(t_len, nb * a.shape[1])

    def qblk(c):
        return pl.BlockSpec((tq, c), lambda b, i: (i, b))

    kern = functools.partial(_dsa_kernel, tq=tq, qw=qw, tk=tk, past=past, length=length, ntop=ntop,
                             idx_bits=idx_bits)
    out = pl.pallas_call(
        kern,
        grid=(nb, t_len // tq),
        in_specs=[qblk(nq), qblk(ni), qblk(LANES),
                  pl.BlockSpec((1, B_KV_HEADS, lp, B_HEAD_DIM), lambda b, i: (b, 0, 0, 0)),
                  pl.BlockSpec((1, B_KV_HEADS, lp // tk, B_HEAD_DIM, tk), lambda b, i: (b, 0, 0, 0, 0)),
                  pl.BlockSpec((1, lp, IDX_DIM), lambda b, i: (b, 0, 0)),
                  pl.BlockSpec((2, B_HEADS, tk, qw), lambda b, i: (0, 0, 0, 0)),
                  pl.BlockSpec((B_HEADS, SUBLANES, tk), lambda b, i: (0, 0, 0))],
        out_specs=qblk(nq),
        out_shape=jax.ShapeDtypeStruct((t_len, nb * nq), BF16),
        scratch_shapes=[pltpu.VMEM((lp // tk, tk, qw), I32),
                        pltpu.VMEM((tk, qw), F32),
                        pltpu.VMEM((B_HEADS, tk, qw), F32),
                        pltpu.VMEM((B_HEADS, tk, qw), BF16),
                        pltpu.VMEM((B_HEADS, qw, B_HEAD_DIM), BF16),
                        pltpu.VMEM((IDX_HEADS, qw, IDX_DIM), BF16),
                        pltpu.VMEM((qw, LANES), F32),
                        pltpu.VMEM((IDX_HEADS, qw), F32),
                        pltpu.VMEM((SUBLANES, qw), I32),
                        pltpu.VMEM((B_HEADS, qw), F32),
                        pltpu.VMEM((B_HEADS, 16, qw), F32),
                        pltpu.VMEM((B_HEADS * B_HEAD_DIM, qw), F32)],
        compiler_params=_params(("arbitrary", "arbitrary")),
        name="dsa",
    )(view(q), view(iq), view(ikw), k_att, vt_att, ik_att, near, far)
    return out.reshape(t_len * nb, nq)


def _xq_tail(x1, gx_ref, wxq_ref, qnx_ref):
    xn = _rms(x1, gx_ref[...]).astype(BF16)
    qx = _head_rms(_dot(xn, wxq_ref[...]), X_HEAD_DIM, qnx_ref[...])
    return (qx * (X_HEAD_DIM ** -0.5)).astype(BF16)


def _even_out_kernel(x_ref, ya_ref, yb_ref, wo_ref, gx_ref, wxq_ref, qnx_ref, x1_ref, qx_ref):
    x1 = x_ref[...] + _dot(ya_ref[...], wo_ref[0:A_WIDTH, :]) + _dot(yb_ref[...], wo_ref[A_WIDTH:, :])
    x1_ref[...] = x1
    qx_ref[...] = _xq_tail(x1, gx_ref, wxq_ref, qnx_ref)


def _even_out(x, ya, yb, w_out, gx, w_xq, qnx, tm):
    rows = x.shape[0]
    d = D_MODEL

    def row(c):
        return pl.BlockSpec((tm, c), lambda i: (i, 0))

    return pl.pallas_call(
        _even_out_kernel,
        grid=(rows // tm,),
        in_specs=[row(d), row(A_WIDTH), row(B_HEADS * B_HEAD_DIM), _const_spec((d, d)),
                  _const_spec((1, d)), _const_spec((d, d)), _const_spec((1, d))],
        out_specs=[row(d), row(d)],
        out_shape=[jax.ShapeDtypeStruct((rows, d), F32), jax.ShapeDtypeStruct((rows, d), BF16)],
        compiler_params=_params(("arbitrary",)),
        name="even_out",
    )(x, ya, yb, w_out, gx, w_xq, qnx)


def _xattn_kernel(q_ref, mk_ref, mv_ref, o_ref):
    for h in range(X_HEADS):
        sl = slice(h * X_HEAD_DIM, (h + 1) * X_HEAD_DIM)
        s = _dot_t(q_ref[:, sl], mk_ref[0, :, sl])
        p = jnp.exp(s - jnp.max(s, axis=1, keepdims=True))
        o = _dot(p.astype(BF16), mv_ref[0, :, sl]) / jnp.sum(p, axis=1, keepdims=True)
        o_ref[:, sl] = o.astype(BF16)


def _xattn(qx, mk, mv, t_len, tq):
    nb = SUBLANES
    d = D_MODEL
    m = mk.shape[1]
    out = pl.pallas_call(
        _xattn_kernel,
        grid=(nb, t_len // tq),
        in_specs=[pl.BlockSpec((tq, d), lambda b, i: (i, b)),
                  pl.BlockSpec((1, m, d), lambda b, i: (b, 0, 0)),
                  pl.BlockSpec((1, m, d), lambda b, i: (b, 0, 0))],
        out_specs=pl.BlockSpec((tq, d), lambda b, i: (i, b)),
        out_shape=jax.ShapeDtypeStruct((t_len, nb * d), BF16),
        compiler_params=_params(("arbitrary", "arbitrary")),
        name="mem_attn",
    )(qx.reshape(t_len, nb * d), mk, mv)
    return out.reshape(t_len * nb, d)


def _ffn_kernel(x_ref, o_ref, wxo_ref, g_ref, wup_ref, cw_ref, cb_ref, wdn_ref, hist_ref,
                y_ref, fh_ref, gbuf, *, chunk):
    tm = x_ref.shape[0]
    nh = (F_CONV - 1) * SUBLANES

    @pl.when(pl.program_id(0) == 0)
    def _():
        gbuf[0:nh, :] = hist_ref[...]

    x2 = x_ref[...] + _dot(o_ref[...], wxo_ref[...])
    xn = _rms(x2, g_ref[...]).astype(BF16)
    acc = jnp.zeros((tm, D_MODEL), F32)
    for c0 in range(0, D_FF, chunk):
        c1 = c0 + chunk
        val = _dot(xn, wup_ref[:, c0:c1])
        gate = _dot(xn, wup_ref[:, D_FF + c0:D_FF + c1])
        gbuf[nh:nh + tm, c0:c1] = gate
        conv = gate * cw_ref[F_CONV - 1:F_CONV, c0:c1]
        for i in range(F_CONV - 1):
            conv = conv + gbuf[i * SUBLANES:i * SUBLANES + tm, c0:c1] * cw_ref[i:i + 1, c0:c1]
        conv = conv + cb_ref[:, c0:c1]
        act = (jax.nn.gelu(conv) * val).astype(BF16)
        acc = acc + _dot(act, wdn_ref[c0:c1, :])
    tail = gbuf[tm:tm + nh, :]
    gbuf[0:nh, :] = tail
    fh_ref[...] = tail
    y_ref[...] = x2 + acc


def _ffn(x, o, w_xo, g, w_up, conv_w, conv_b, w_down, hist, tm):
    rows = x.shape[0]
    d = D_MODEL
    nh = (F_CONV - 1) * SUBLANES

    def row(c):
        return pl.BlockSpec((tm, c), lambda i: (i, 0))

    return pl.pallas_call(
        functools.partial(_ffn_kernel, chunk=MXU_DIM),
        grid=(rows // tm,),
        in_specs=[row(d), row(d), _const_spec((d, d)), _const_spec((1, d)), _const_spec((d, 2 * D_FF)),
                  _const_spec((F_CONV, D_FF)), _const_spec((1, D_FF)), _const_spec((D_FF, d)),
                  _const_spec((nh, D_FF))],
        out_specs=[row(d), pl.BlockSpec((nh, D_FF), lambda i: (0, 0))],
        out_shape=[jax.ShapeDtypeStruct((rows, d), F32), jax.ShapeDtypeStruct((nh, D_FF), F32)],
        scratch_shapes=[pltpu.VMEM((tm + nh, D_FF), F32)],
        compiler_params=_params(("arbitrary",)),
        name="ffn",
    )(x, o, w_xo, g, w_up, conv_w, conv_b, w_down, hist)


def _odd_kernel(x_ref, g_ref, win_ref, cw_ref, cb_ref, wa_ref, ba_ref, wi_ref, bi_ref, lam_ref,
                wo_ref, hist_ref, h0_ref, gx_ref, wxq_ref, qnx_ref,
                x1_ref, qx_ref, ch_ref, hl_ref, xbuf, a_s, b_s, h_s, *, stream_start):
    tm = x_ref.shape[0]
    nh = (C_CONV - 1) * SUBLANES
    first = pl.program_id(0) == 0

    @pl.when(first)
    def _():
        xbuf[0:nh, :] = hist_ref[...]
        h_s[...] = h0_ref[...]

    x = x_ref[...]
    xn = _rms(x, g_ref[...]).astype(BF16)
    xr_in = _dot(xn, win_ref[:, RNN_WIDTH:])
    xbuf[nh:nh + tm, :] = xr_in
    xr = _conv_taps(xbuf, xr_in, cw_ref, C_CONV, tm) + cb_ref[...]
    tail = xbuf[tm:tm + nh, :]
    xbuf[0:nh, :] = tail
    ch_ref[...] = tail

    xrb = xr.astype(BF16)
    lam = -lam_ref[...]
    sp = jnp.maximum(lam, 0.0) + jnp.log1p(jnp.exp(-jnp.abs(lam)))
    rows = lax.broadcasted_iota(I32, (tm, RNN_BLOCK), 0)
    for n in range(RNN_BLOCKS):
        sl = slice(n * RNN_BLOCK, (n + 1) * RNN_BLOCK)
        r = jax.nn.sigmoid(_dot(xrb[:, sl], wa_ref[n]) + ba_ref[:, sl])
        ig = jax.nn.sigmoid(_dot(xrb[:, sl], wi_ref[n]) + bi_ref[:, sl])
        log_a = -RG_C * r * sp[:, sl]
        a = jnp.exp(log_a)
        mult = jnp.sqrt(jnp.tanh(-log_a) * (1.0 + a * a))
        if stream_start:
            mult = jnp.where(first & (rows < SUBLANES), 1.0, mult)
        a_s[:, sl] = a
        b_s[:, sl] = mult * ig * xr[:, sl]

    def step(t, h):
        r0 = pl.multiple_of(t * SUBLANES, SUBLANES)
        h = a_s[pl.ds(r0, SUBLANES), :] * h + b_s[pl.ds(r0, SUBLANES), :]
        b_s[pl.ds(r0, SUBLANES), :] = h
        return h

    h = lax.fori_loop(0, tm // SUBLANES, step, h_s[...], unroll=8)
    h_s[...] = h
    hl_ref[...] = h

    gate = _dot(xn, win_ref[:, :RNN_WIDTH])
    act = (jax.nn.gelu(gate) * b_s[...]).astype(BF16)
    x1 = x + _dot(act, wo_ref[...])
    x1_ref[...] = x1
    qx_ref[...] = _xq_tail(x1, gx_ref, wxq_ref, qnx_ref)


def _odd(x, g, w_in, conv_w, conv_b, w_a, b_a, w_i, b_i, lam, w_out, hist, h0, gx, w_xq, qnx, tm,
         stream_start):
    rows = x.shape[0]
    d = D_MODEL
    r = RNN_WIDTH
    nh = (C_CONV - 1) * SUBLANES

    def row(c):
        return pl.BlockSpec((tm, c), lambda i: (i, 0))

    blk = (RNN_BLOCKS, RNN_BLOCK, RNN_BLOCK)
    return pl.pallas_call(
        functools.partial(_odd_kernel, stream_start=stream_start),
        grid=(rows // tm,),
        in_specs=[row(d), _const_spec((1, d)), _const_spec((d, 2 * r)), _const_spec((C_CONV, r)),
                  _const_spec((1, r)), _const_spec(blk), _const_spec((1, r)), _const_spec(blk),
                  _const_spec((1, r)), _const_spec((1, r)), _const_spec((r, d)), _const_spec((nh, r)),
                  _const_spec((SUBLANES, r)), _const_spec((1, d)), _const_spec((d, d)), _const_spec((1, d))],
        out_specs=[row(d), row(d), pl.BlockSpec((nh, r), lambda i: (0, 0)),
                   pl.BlockSpec((SUBLANES, r), lambda i: (0, 0))],
        out_shape=[jax.ShapeDtypeStruct((rows, d), F32), jax.ShapeDtypeStruct((rows, d), BF16),
                   jax.ShapeDtypeStruct((nh, r), F32), jax.ShapeDtypeStruct((SUBLANES, r), F32)],
        scratch_shapes=[pltpu.VMEM((tm + nh, r), F32), pltpu.VMEM((tm, r), F32), pltpu.VMEM((tm, r), F32),
                        pltpu.VMEM((SUBLANES, r), F32)],
        compiler_params=_params(("arbitrary",)),
        name="odd_mixer",
    )(x, g, w_in, conv_w, conv_b, w_a, b_a, w_i, b_i, lam, w_out, hist, h0, gx, w_xq, qnx)


def _to_tm(a):
    return jnp.transpose(a, (1, 0, 2)).reshape(a.shape[1] * a.shape[0], a.shape[2])


def _from_tm(a, w):
    return jnp.transpose(a.reshape(w, SUBLANES, a.shape[1]), (1, 0, 2))


def _trunk(x, st, mem_k, mem_v, p, bias, tm, tq_dsa, tq_x):
    nb, t_len, d = x.shape
    assert nb == SUBLANES
    rows = t_len * nb
    past = 0 if st is None else st["b_k"].shape[2]
    length = past + t_len
    ntop = min(TOPK_MAX, length // 4)
    near, far = bias
    xt = jnp.transpose(x, (1, 0, 2)).reshape(rows, d)
    out = {}

    def hist(name, l, width, c):
        if st is None:
            return jnp.zeros(((width - 1) * nb, c), F32)
        return _to_tm(st[name][l])

    for l in range(DEPTH):
        if l % 2 == 0:
            e = l // 2
            ya, q, k, v, iq, ikw, uh = _even_in(
                xt, p["g_mix"][l], p["w_in_even"][e], p["a_conv_w"][e], p["b_q_norm"][e], p["b_k_norm"][e],
                hist("a_conv", e, A_CONV, A_WIDTH), tm)
            k_new = _from_tm(k, t_len).reshape(nb, t_len, B_KV_HEADS, B_HEAD_DIM)
            v_new = _from_tm(v, t_len).reshape(nb, t_len, B_KV_HEADS, B_HEAD_DIM)
            ik_new = _from_tm(ikw[:, :IDX_DIM], t_len)
            k_all, v_all, ik_all = k_new, v_new, ik_new
            if st is not None:
                k_all = jnp.concatenate([st["b_k"][e], k_new], axis=1)
                v_all = jnp.concatenate([st["b_v"][e], v_new], axis=1)
                ik_all = jnp.concatenate([st["b_kidx"][e], ik_new], axis=1)
            lp = -(-length // KEY_BLOCK) * KEY_BLOCK
            padl = lp - length
            k_att = jnp.pad(jnp.transpose(k_all, (0, 2, 1, 3)).astype(BF16), ((0, 0), (0, 0), (0, padl), (0, 0)))
            v_att = jnp.pad(jnp.transpose(v_all, (0, 2, 1, 3)).astype(BF16), ((0, 0), (0, 0), (0, padl), (0, 0)))
            vt_att = jnp.transpose(v_att.reshape(nb, B_KV_HEADS, lp // KEY_BLOCK, KEY_BLOCK, B_HEAD_DIM),
                                   (0, 1, 2, 4, 3))
            ik_att = jnp.pad(ik_all.astype(BF16), ((0, 0), (0, padl), (0, 0)))
            yb = _dsa(q, iq, ikw, k_att, vt_att, ik_att, near, far, t_len, tq_dsa, past, length, ntop)
            x1, qx = _even_out(xt, ya, yb, p["w_out_even"][e], p["g_x"][l], p["w_xq"][l], p["x_q_norm"][l], tm)
            out.setdefault("a_conv", []).append(_from_tm(uh, A_CONV - 1))
            out.setdefault("b_k", []).append(k_new)
            out.setdefault("b_v", []).append(v_new)
            out.setdefault("b_kidx", []).append(ik_new)
        else:
            o = l // 2
            h0 = jnp.zeros((nb, RNN_WIDTH), F32) if st is None else st["c_h"][o]
            x1, qx, ch, hl = _odd(
                xt, p["g_mix"][l], p["w_in_odd"][o], p["c_conv_w"][o], p["c_conv_b"][o], p["c_w_a"][o],
                p["c_b_a"][o], p["c_w_i"][o], p["c_b_i"][o], p["c_lambda"][o], p["w_out_odd"][o],
                hist("c_conv", o, C_CONV, RNN_WIDTH), h0, p["g_x"][l], p["w_xq"][l], p["x_q_norm"][l], tm,
                stream_start=(past == 0))
            out.setdefault("c_conv", []).append(_from_tm(ch, C_CONV - 1))
            out.setdefault("c_h", []).append(hl)
        xo = _xattn(qx, mem_k[l], mem_v[l], t_len, tq_x)
        xt, fh = _ffn(x1, xo, p["w_xo"][l], p["g_ffn"][l], p["w_up"][l], p["f_conv_w"][l], p["f_conv_b"][l],
                      p["w_down"][l], hist("f_conv", l, F_CONV, D_FF), tm)
        out.setdefault("f_conv", []).append(_from_tm(fh, F_CONV - 1))
    y = jnp.transpose(xt.reshape(t_len, nb, d), (1, 0, 2))
    return y, {name: jnp.stack(v) for name, v in out.items()}


def kernel(x_prompt, x_sample, cache_b_k, cache_b_v, cache_b_kidx, state_a_conv, state_c_conv, state_c_h, state_ffn_conv, cache_mem_k, cache_mem_v, mem_prompt, rel_table, g_mix, w_in_even, a_conv_w, b_q_norm, b_k_norm, w_out_even, w_in_odd, c_conv_w, c_conv_b, c_w_a, c_b_a, c_w_i, c_b_i, c_lambda, w_out_odd, g_mem, g_x, w_xq, w_xk, w_xv, x_q_norm, x_k_norm, w_xo, g_ffn, w_up, f_conv_w, f_conv_b, w_down):
    d = D_MODEL
    n_even = w_in_even.shape[0]
    n_odd = w_in_odd.shape[0]
    bp, t_p, _ = x_prompt.shape
    m = mem_prompt.shape[1]

    def rowvec(a):
        return a.reshape(a.shape[0], 1, a.shape[-1])

    p = {
        "g_mix": rowvec(g_mix), "g_x": rowvec(g_x), "g_ffn": rowvec(g_ffn),
        "w_in_even": jnp.pad(w_in_even, ((0, 0), (0, 0), (0, EVEN_IN_PAD - EVEN_IN))).astype(BF16),
        "a_conv_w": a_conv_w,
        "b_q_norm": rowvec(jnp.tile(b_q_norm, (1, B_HEADS))),
        "b_k_norm": rowvec(jnp.tile(b_k_norm, (1, B_KV_HEADS))),
        "w_out_even": w_out_even.astype(BF16),
        "w_in_odd": w_in_odd.astype(BF16), "c_conv_w": c_conv_w, "c_conv_b": rowvec(c_conv_b),
        "c_w_a": c_w_a.astype(BF16), "c_b_a": rowvec(c_b_a), "c_w_i": c_w_i.astype(BF16), "c_b_i": rowvec(c_b_i),
        "c_lambda": rowvec(c_lambda), "w_out_odd": w_out_odd.astype(BF16),
        "w_xq": w_xq.astype(BF16), "x_q_norm": rowvec(jnp.tile(x_q_norm, (1, X_HEADS))),
        "w_xo": w_xo.astype(BF16), "w_up": w_up.astype(BF16), "f_conv_w": f_conv_w,
        "f_conv_b": rowvec(f_conv_b), "w_down": w_down.astype(BF16),
    }
    del n_even, n_odd
    bias = _bias_tiles(rel_table, KEY_BLOCK)

    mk, mv = _mem_kv(mem_prompt.reshape(bp * m, d), g_mem, w_xk, x_k_norm, w_xv)
    p_mem_k = mk.reshape(DEPTH, bp, m, X_HEADS, X_HEAD_DIM)
    p_mem_v = mv.reshape(DEPTH, bp, m, X_HEADS, X_HEAD_DIM)
    y_prompt, new_p = _trunk(x_prompt, None, mk.reshape(DEPTH, bp, m, d).astype(BF16),
                             mv.reshape(DEPTH, bp, m, d).astype(BF16), p, bias,
                             tm=512, tq_dsa=KEY_BLOCK, tq_x=512)

    bs, t_s, _ = x_sample.shape
    st_s = {"b_k": cache_b_k, "b_v": cache_b_v, "b_kidx": cache_b_kidx, "a_conv": state_a_conv,
            "c_conv": state_c_conv, "c_h": state_c_h, "f_conv": state_ffn_conv}
    ms = cache_mem_k.shape[2]
    y_sample, new_s = _trunk(x_sample, st_s, cache_mem_k.reshape(DEPTH, bs, ms, d).astype(BF16),
                             cache_mem_v.reshape(DEPTH, bs, ms, d).astype(BF16), p, bias,
                             tm=bs * t_s, tq_dsa=t_s, tq_x=t_s)
    return (y_prompt, y_sample,
            new_p["b_k"], new_p["b_v"], new_p["b_kidx"], new_p["a_conv"], new_p["c_conv"],
            new_p["c_h"], new_p["f_conv"], p_mem_k, p_mem_v,
            new_s["b_k"], new_s["b_v"], new_s["b_kidx"], new_s["a_conv"], new_s["c_conv"],
            new_s["c_h"], new_s["f_conv"])
```

```python
import functools
import math

import jax
import jax.numpy as jnp
from jax import lax
from jax.experimental import pallas as pl
from jax.experimental.pallas import tpu as pltpu

F32 = jnp.float32
BF16 = jnp.bfloat16
I32 = jnp.int32

D_MODEL = 1024
DEPTH = 2
CHUNK = 64
EPS = 1e-6
NEG_INF = -1e30
A_WIDTH = 512
A_CONV = 3
B_HEADS = 8
B_KV_HEADS = 2
B_HEAD_DIM = 64
IDX_HEADS = 8
IDX_DIM = 32
TOPK_MAX = 256
REL_BUCKETS = 32
REL_MAX_DIST = 128
RNN_WIDTH = 1024
RNN_BLOCKS = 8
RNN_BLOCK = 128
C_CONV = 4
RG_C = 8.0
X_HEADS = 4
X_HEAD_DIM = 256
D_FF = 2816
F_CONV = 3
EVEN_IN = 2600

SUBLANES = 8
LANES = 128
MXU_DIM = 256
VMEM_LIMIT = 56 * 1024 * 1024

EVEN_IN_PAD = 2688
KEY_BLOCK = 256
INT_MIN = -2147483648


def _params(sem, vmem=VMEM_LIMIT):
    return pltpu.CompilerParams(dimension_semantics=sem, vmem_limit_bytes=vmem)


def _const_spec(shape):
    nd = len(shape)
    return pl.BlockSpec(shape, lambda *_: (0,) * nd, pipeline_mode=pl.Buffered(1))


def _rms(x, g):
    ms = jnp.mean(x * x, axis=-1, keepdims=True)
    return x * lax.rsqrt(ms + EPS) * g


def _head_rms(x, hd, gain):
    m, c = x.shape
    s = x * x
    parts = []
    if hd >= LANES:
        for h in range(c // hd):
            ms = jnp.mean(s[:, h * hd:(h + 1) * hd], axis=-1, keepdims=True)
            parts.append(x[:, h * hd:(h + 1) * hd] * lax.rsqrt(ms + EPS))
    else:
        lane = lax.broadcasted_iota(I32, (m, LANES), 1)
        for j in range(c // LANES):
            sj = s[:, j * LANES:(j + 1) * LANES]
            inv = jnp.zeros((m, LANES), F32)
            for k in range(LANES // hd):
                msk = (lane >= k * hd) & (lane < (k + 1) * hd)
                ms = jnp.sum(jnp.where(msk, sj, 0.0), axis=-1, keepdims=True) * (1.0 / hd)
                inv = jnp.where(msk, lax.rsqrt(ms + EPS), inv)
            parts.append(x[:, j * LANES:(j + 1) * LANES] * inv)
    y = parts[0] if len(parts) == 1 else jnp.concatenate(parts, axis=-1)
    return y * gain


def _dot(a, b):
    return jnp.dot(a, b, preferred_element_type=F32)


def _dot_t(a, b):
    return lax.dot_general(a, b, (((1,), (1,)), ((), ())), preferred_element_type=F32)


def _conv_taps(buf, cur, w_ref, width, tm):
    y = cur * w_ref[width - 1:width, :]
    for i in range(width - 1):
        y = y + buf[i * SUBLANES:i * SUBLANES + tm, :] * w_ref[i:i + 1, :]
    return y


def _seq_to_tm(src_ref, scr):
    nb, tt, c = src_ref.shape
    for b in range(nb):
        for j in range(c // LANES):
            scr[j, pl.ds(b, tt, stride=nb), :] = src_ref[b, :, j * LANES:(j + 1) * LANES].astype(F32)
    return jnp.concatenate([scr[j] for j in range(c // LANES)], axis=-1)


def _tm_to_seq(val, scr, dst_refs):
    tm, c = val.shape
    tt = tm // SUBLANES
    for j in range(c // LANES):
        scr[j] = val[:, j * LANES:(j + 1) * LANES]
    for b in range(SUBLANES):
        j0 = 0
        for ref, ci in dst_refs:
            nj = ci // LANES
            parts = [scr[j0 + j, pl.ds(b, tt, stride=SUBLANES), :] for j in range(nj)]
            ref[b] = (parts[0] if nj == 1 else jnp.concatenate(parts, axis=-1)).astype(ref.dtype)
            j0 += nj


def _memkv_kernel(mem_ref, g_ref, wk_ref, kn_ref, wv_ref, k_ref, v_ref):
    hm = _rms(mem_ref[...], g_ref[0]).astype(BF16)
    k_ref[0] = _head_rms(_dot(hm, wk_ref[0]), X_HEAD_DIM, kn_ref[0])
    v_ref[0] = _dot(hm, wv_ref[0])


def _mem_kv(mem, g_mem, w_xk, x_k_norm, w_xv):
    rows = mem.shape[0]
    tm = min(512, rows)
    d = D_MODEL
    kn = jnp.tile(x_k_norm, (1, X_HEADS)).reshape(DEPTH, 1, d)
    out = jax.ShapeDtypeStruct((DEPTH, rows, d), F32)
    return pl.pallas_call(
        _memkv_kernel,
        grid=(DEPTH, rows // tm),
        in_specs=[
            pl.BlockSpec((tm, d), lambda l, i: (i, 0)),
            pl.BlockSpec((1, 1, d), lambda l, i: (l, 0, 0)),
            pl.BlockSpec((1, d, d), lambda l, i: (l, 0, 0)),
            pl.BlockSpec((1, 1, d), lambda l, i: (l, 0, 0)),
            pl.BlockSpec((1, d, d), lambda l, i: (l, 0, 0)),
        ],
        out_specs=[pl.BlockSpec((1, tm, d), lambda l, i: (l, i, 0))] * 2,
        out_shape=[out, out],
        compiler_params=_params(("arbitrary", "arbitrary")),
        name="mem_kv",
    )(mem, g_mem.reshape(DEPTH, 1, d), w_xk.astype(BF16), kn, w_xv.astype(BF16))


def _even_in_kernel(x_ref, g_ref, w_ref, cw_ref, qn_ref, kn_ref, hist_ref,
                    xt_ref, ya_ref, q_ref, iq_ref, ikw_ref, k_ref, v_ref, uh_ref, ubuf, xs, cs):
    tm = xt_ref.shape[0]
    nh = (A_CONV - 1) * SUBLANES

    @pl.when(pl.program_id(0) == 0)
    def _():
        ubuf[0:nh, :] = hist_ref[...]

    x = _seq_to_tm(x_ref, xs)
    xt_ref[...] = x
    xn = _rms(x, g_ref[...]).astype(BF16)

    def proj(a, b):
        return _dot(xn, w_ref[:, a:b])

    zc = proj(A_WIDTH, 3 * A_WIDTH)
    u = zc[:, :A_WIDTH] * zc[:, A_WIDTH:]
    ubuf[nh:nh + tm, :] = u
    conv = _conv_taps(ubuf, u, cw_ref, A_CONV, tm)
    ya_ref[...] = (proj(0, A_WIDTH) * conv).astype(BF16)
    tail = ubuf[tm:tm + nh, :]
    ubuf[0:nh, :] = tail
    uh_ref[...] = tail

    o = 3 * A_WIDTH
    nq = B_HEADS * B_HEAD_DIM
    nkv = B_KV_HEADS * B_HEAD_DIM
    ni = IDX_HEADS * IDX_DIM
    q = _head_rms(proj(o, o + nq), B_HEAD_DIM, qn_ref[...]) * (B_HEAD_DIM ** -0.5)
    zkv = proj(o + nq, o + nq + 2 * nkv)
    k = _head_rms(zkv[:, :nkv], B_HEAD_DIM, kn_ref[...])
    zi = proj(o + nq + 2 * nkv, EVEN_IN_PAD)
    seq = jnp.concatenate([q, zi, k, zkv[:, nkv:]], axis=-1)
    _tm_to_seq(seq, cs, [(q_ref, nq), (iq_ref, ni), (ikw_ref, LANES), (k_ref, nkv), (v_ref, nkv)])


def _even_in(x, g, w_pad, conv_w, qn, kn, hist, tt):
    nb, t_len, d = x.shape
    tm = tt * nb
    rows = t_len * nb
    nh = (A_CONV - 1) * SUBLANES
    nq = B_HEADS * B_HEAD_DIM
    nkv = B_KV_HEADS * B_HEAD_DIM
    ni = IDX_HEADS * IDX_DIM
    nseq = nq + ni + LANES + 2 * nkv

    def row(c):
        return pl.BlockSpec((tm, c), lambda i: (i, 0))

    def seq(c):
        return pl.BlockSpec((nb, tt, c), lambda i: (0, i, 0))

    def seq_shape(c, dt):
        return jax.ShapeDtypeStruct((nb, t_len, c), dt)

    return pl.pallas_call(
        _even_in_kernel,
        grid=(t_len // tt,),
        in_specs=[seq(d), _const_spec((1, d)), _const_spec((d, EVEN_IN_PAD)), _const_spec((A_CONV, A_WIDTH)),
                  _const_spec((1, nq)), _const_spec((1, nkv)), _const_spec((nh, A_WIDTH))],
        out_specs=[row(d), row(A_WIDTH), seq(nq), seq(ni), seq(LANES), seq(nkv), seq(nkv),
                   pl.BlockSpec((nh, A_WIDTH), lambda i: (0, 0))],
        out_shape=[jax.ShapeDtypeStruct((rows, d), F32), jax.ShapeDtypeStruct((rows, A_WIDTH), BF16),
                   seq_shape(nq, BF16), seq_shape(ni, BF16), seq_shape(LANES, F32), seq_shape(nkv, F32),
                   seq_shape(nkv, F32), jax.ShapeDtypeStruct((nh, A_WIDTH), F32)],
        scratch_shapes=[pltpu.VMEM((tm + nh, A_WIDTH), F32), pltpu.VMEM((d // LANES, tm, LANES), F32),
                        pltpu.VMEM((nseq // LANES, tm, LANES), F32)],
        compiler_params=_params(("arbitrary",)),
        name="even_in",
    )(x, g, w_pad, conv_w, qn, kn, hist)


def _rel_bucket(rel):
    half = REL_BUCKETS // 2
    max_exact = half // 2
    n = -rel
    ret = jnp.where(n < 0, half, 0)
    n = jnp.abs(n)
    nf = jnp.maximum(n, 1).astype(F32)
    large = max_exact + (jnp.log(nf / max_exact) / math.log(REL_MAX_DIST / max_exact)
                         * (half - max_exact)).astype(I32)
    large = jnp.minimum(large, half - 1)
    return ret + jnp.where(n < max_exact, n, large)


def _bias_kernel(tab_ref, near_ref, far_ref):
    tk = near_ref.shape[-1]
    r = lax.broadcasted_iota(I32, (tk, tk), 0)
    c = lax.broadcasted_iota(I32, (tk, tk), 1)

    def lookup(bucket, h):
        def body(j, acc):
            return jnp.where(bucket == j, tab_ref[j, h], acc)
        return lax.fori_loop(0, REL_BUCKETS, body, jnp.zeros(bucket.shape, F32))

    for blk in range(2):
        bucket = _rel_bucket(r - c - blk * tk)
        for h in range(B_HEADS):
            near_ref[blk, h] = lookup(bucket, h)
    bucket = _rel_bucket(-REL_MAX_DIST - c[0:SUBLANES, :])
    for h in range(B_HEADS):
        far_ref[h] = lookup(bucket, h)


def _bias_tiles(rel_table, tk):
    return pl.pallas_call(
        _bias_kernel,
        in_specs=[pl.BlockSpec(memory_space=pltpu.SMEM)],
        out_shape=[jax.ShapeDtypeStruct((2, B_HEADS, tk, tk), F32),
                   jax.ShapeDtypeStruct((B_HEADS, SUBLANES, tk), F32)],
        name="rel_bias",
    )(rel_table)


def _dsa_kernel(q_ref, iq_ref, ikw_ref, k_ref, vt_ref, ik_ref, near_ref, far_ref, o_ref,
                skey, madd_s, s_s, p_s, qh_s, iqh_s, pad_s, w_s, x_s, m_s, l_s, acc_s,
                *, tq, qw, tk, past, length, ntop, idx_bits):
    i = pl.program_id(1)
    q0 = past + i * tq
    nkb = (q0 + tq + tk - 1) // tk
    last = nkb - 1
    ni = IDX_HEADS * IDX_DIM
    groups = B_HEADS // B_KV_HEADS
    lane = lax.broadcasted_iota(I32, (1, qw), 1)
    sub8 = lax.broadcasted_iota(I32, (SUBLANES, qw), 0)
    qchunk = jnp.right_shift(q0 + lane, 6)

    if tq != qw:
        qh_s[...] = jnp.zeros(qh_s.shape, BF16)
        iqh_s[...] = jnp.zeros(iqh_s.shape, BF16)
        pad_s[...] = jnp.zeros(pad_s.shape, F32)
    for h in range(B_HEADS):
        qh_s[h, 0:tq, :] = q_ref[:, B_HEAD_DIM * h:B_HEAD_DIM * (h + 1)]
    for h in range(IDX_HEADS):
        iqh_s[h, 0:tq, :] = iq_ref[:, IDX_DIM * h:IDX_DIM * (h + 1)]
    pad_s[0:tq, :] = ikw_ref[...]
    w_s[...] = pad_s[...].T[IDX_DIM:IDX_DIM + IDX_HEADS, :] * (ni ** -0.5)

    slab = tk

    def scores(kb, masked):
        for sl in range(tk // slab):
            ks = pl.multiple_of(kb * tk + sl * slab, slab)
            ikb = ik_ref[0, pl.ds(ks, slab), :]
            acc = jnp.zeros((slab, qw), F32)
            for h in range(IDX_HEADS):
                acc = acc + w_s[h:h + 1, :] * jnp.maximum(_dot_t(ikb, iqh_s[h]), 0.0)
            if masked:
                kpos = kb * tk + sl * slab + lax.broadcasted_iota(I32, (slab, qw), 0)
                vis = (jnp.right_shift(kpos, 6) <= qchunk) & (kpos < length)
                acc = jnp.where(vis, acc, -jnp.inf)
            bits = lax.bitcast_convert_type(acc, I32)
            skey[kb, sl * slab:(sl + 1) * slab, :] = jnp.where(bits < 0, bits ^ jnp.int32(0x7FFFFFFF), bits)

    def score_body(kb, carry):
        scores(kb, False)
        return carry

    lax.fori_loop(0, last, score_body, 0)
    scores(last, True)

    n_acc = 4

    def count(pred):
        def body(kb, accs):
            accs = list(accs)
            for g in range(tk // SUBLANES):
                blk = skey[kb, g * SUBLANES:(g + 1) * SUBLANES, :]
                accs[g % n_acc] = accs[g % n_acc] + jnp.where(pred(kb, g, blk), 1, 0)
            return tuple(accs)
        accs = lax.fori_loop(0, nkb, body, tuple(jnp.zeros((SUBLANES, qw), I32) for _ in range(n_acc)))
        tot = (accs[0] + accs[1]) + (accs[2] + accs[3])
        return jnp.sum(tot, axis=0, keepdims=True)

    def rep8(v):
        return jnp.broadcast_to(v, (SUBLANES, qw))

    def bit_body(it, prefix):
        cand_u = prefix | jnp.left_shift(jnp.int32(1), 31 - it)
        cand8 = rep8(cand_u ^ jnp.int32(INT_MIN))
        cnt = count(lambda kb, g, blk: blk >= cand8)
        return jnp.where(cnt >= ntop, cand_u, prefix)

    prefix = lax.fori_loop(0, 32, bit_body, jnp.zeros((1, qw), I32))
    tau = prefix ^ jnp.int32(INT_MIN)
    tau8 = rep8(tau)

    cnt_gt = count(lambda kb, g, blk: blk > tau8)
    cnt_ge = count(lambda kb, g, blk: blk >= tau8)
    need = ntop - cnt_gt
    key_ninf = jnp.int32(0x7FFFFFFF) ^ jnp.int32(-8388608)
    finite = tau != key_ninf
    tie_rows = ((cnt_ge - cnt_gt) > need) & finite & (lane < tq)
    x_s[...] = rep8(jnp.where(finite, jnp.int32((1 << idx_bits) - 1), 0))
    any_tie = jnp.max(jnp.where(tie_rows, 1, 0)) > 0

    @pl.when(any_tie)
    def _():
        need8 = rep8(need)

        def xbit(it, xlim):
            cand8 = xlim | jnp.left_shift(jnp.int32(1), idx_bits - 1 - it)
            cnt = count(lambda kb, g, blk: (blk == tau8) & ((kb * tk + g * SUBLANES + sub8) < cand8))
            return jnp.where(rep8(cnt) <= need8, cand8, xlim)
        xlim = lax.fori_loop(0, idx_bits, xbit, jnp.zeros((SUBLANES, qw), I32))
        x_s[...] = jnp.where(rep8(finite), xlim, 0)

    m_s[...] = jnp.full(m_s.shape, NEG_INF, F32)
    l_s[...] = jnp.zeros(l_s.shape, F32)
    acc_s[...] = jnp.zeros(acc_s.shape, F32)
    prow = 16

    tau_ge8 = rep8(jnp.where(finite, tau, tau + 1))

    def make_mask(kb):
        @pl.when(jnp.logical_not(any_tie))
        def _():
            for g in range(tk // SUBLANES):
                rows = slice(g * SUBLANES, (g + 1) * SUBLANES)
                madd_s[rows, :] = jnp.where(skey[kb, rows, :] >= tau_ge8, 0.0, NEG_INF)

        @pl.when(any_tie)
        def _():
            xlim8 = x_s[...]
            for g in range(tk // SUBLANES):
                rows = slice(g * SUBLANES, (g + 1) * SUBLANES)
                key = skey[kb, rows, :]
                sel = (key > tau8) | ((key == tau8) & ((kb * tk + g * SUBLANES + sub8) < xlim8))
                madd_s[rows, :] = jnp.where(sel, 0.0, NEG_INF)

    def attend(kb, near_blk):
        ks = pl.multiple_of(kb * tk, tk)
        make_mask(kb)
        shifts, alphas = [], []
        for h in range(B_HEADS):
            kn = k_ref[0, h // groups, pl.ds(ks, tk), :]
            if near_blk is None:
                t = _dot_t(kn, qh_s[h]) + madd_s[...]
                c_h = far_ref[h, 0:1, 0:qw]
            else:
                t = _dot_t(kn, qh_s[h]) + (madd_s[...] + near_ref[near_blk, h])
                c_h = jnp.zeros((1, qw), F32)
            s_s[h] = t
            mx = [t[j * SUBLANES:(j + 1) * SUBLANES, :] for j in range(n_acc)]
            for g in range(n_acc, tk // SUBLANES):
                mx[g % n_acc] = jnp.maximum(mx[g % n_acc], t[g * SUBLANES:(g + 1) * SUBLANES, :])
            m_cur = jnp.max(jnp.maximum(jnp.maximum(mx[0], mx[1]), jnp.maximum(mx[2], mx[3])), axis=0, keepdims=True)
            m_prev = m_s[h:h + 1, :]
            m_new = jnp.maximum(m_prev, m_cur + c_h)
            m_s[h:h + 1, :] = m_new
            alphas.append(jnp.exp(m_prev - m_new))
            shifts.append(m_new - c_h)
        for h in range(B_HEADS):
            shift = jnp.broadcast_to(shifts[h], (prow, qw))
            lacc = [jnp.zeros((prow, qw), F32), jnp.zeros((prow, qw), F32)]
            for r in range(tk // prow):
                rows = slice(r * prow, (r + 1) * prow)
                p = jnp.exp(s_s[h, rows, :] - shift)
                lacc[r % 2] = lacc[r % 2] + p
                p_s[h, rows, :] = p.astype(BF16)
            l_s[h] = alphas[h] * l_s[h] + (lacc[0] + lacc[1])
        for h in range(B_HEADS):
            hs = slice(h * B_HEAD_DIM, (h + 1) * B_HEAD_DIM)
            acc_s[hs, :] = acc_s[hs, :] * alphas[h] + _dot(vt_ref[0, h // groups, kb], p_s[h])

    def far_body(kb, carry):
        attend(kb, None)
        return carry

    lax.fori_loop(0, nkb - 2, far_body, 0)

    @pl.when(nkb >= 2)
    def _():
        attend(nkb - 2, 1)

    attend(last, 0)

    outs = []
    for h in range(B_HEADS):
        hs = slice(h * B_HEAD_DIM, (h + 1) * B_HEAD_DIM)
        outs.append(acc_s[hs, :] / jnp.sum(l_s[h], axis=0, keepdims=True))
    o = jnp.concatenate(outs, axis=0).T
    o_ref[...] = o[0:tq, :].astype(BF16)


def _dsa(q, iq, ikw, k_att, vt_att, ik_att, near, far, tq, past, length, ntop):
    tk = KEY_BLOCK
    nb, t_len, _ = q.shape
    lp = k_att.shape[2]
    nq = B_HEADS * B_HEAD_DIM
    ni = IDX_HEADS * IDX_DIM
    qw = max(tq, LANES)
    assert past % tk == 0 and (tq == tk or t_len == tq) and tq <= tk and lp % tk == 0
    idx_bits = lp.bit_length()

    def qblk(c):
        return pl.BlockSpec((None, tq, c), lambda b, i: (b, i, 0))

    kern = functools.partial(_dsa_kernel, tq=tq, qw=qw, tk=tk, past=past, length=length, ntop=ntop,
                             idx_bits=idx_bits)
    return pl.pallas_call(
        kern,
        grid=(nb, t_len // tq),
        in_specs=[qblk(nq), qblk(ni), qblk(LANES),
                  pl.BlockSpec((1, B_KV_HEADS, lp, B_HEAD_DIM), lambda b, i: (b, 0, 0, 0)),
                  pl.BlockSpec((1, B_KV_HEADS, lp // tk, B_HEAD_DIM, tk), lambda b, i: (b, 0, 0, 0, 0)),
                  pl.BlockSpec((1, lp, IDX_DIM), lambda b, i: (b, 0, 0)),
                  pl.BlockSpec((2, B_HEADS, tk, qw), lambda b, i: (0, 0, 0, 0)),
                  pl.BlockSpec((B_HEADS, SUBLANES, tk), lambda b, i: (0, 0, 0))],
        out_specs=qblk(nq),
        out_shape=jax.ShapeDtypeStruct((nb, t_len, nq), BF16),
        scratch_shapes=[pltpu.VMEM((lp // tk, tk, qw), I32),
                        pltpu.VMEM((tk, qw), F32),
                        pltpu.VMEM((B_HEADS, tk, qw), F32),
                        pltpu.VMEM((B_HEADS, tk, qw), BF16),
                        pltpu.VMEM((B_HEADS, qw, B_HEAD_DIM), BF16),
                        pltpu.VMEM((IDX_HEADS, qw, IDX_DIM), BF16),
                        pltpu.VMEM((qw, LANES), F32),
                        pltpu.VMEM((IDX_HEADS, qw), F32),
                        pltpu.VMEM((SUBLANES, qw), I32),
                        pltpu.VMEM((B_HEADS, qw), F32),
                        pltpu.VMEM((B_HEADS, 16, qw), F32),
                        pltpu.VMEM((B_HEADS * B_HEAD_DIM, qw), F32)],
        compiler_params=_params(("arbitrary", "arbitrary")),
        name="dsa",
    )(q, iq, ikw, k_att, vt_att, ik_att, near, far)


def _xq_tail(x1, gx_ref, wxq_ref, qnx_ref, cs, qx_ref):
    xn = _rms(x1, gx_ref[...]).astype(BF16)
    qx = _head_rms(_dot(xn, wxq_ref[...]), X_HEAD_DIM, qnx_ref[...])
    _tm_to_seq(qx * (X_HEAD_DIM ** -0.5), cs, [(qx_ref, D_MODEL)])


def _even_out_kernel(x_ref, ya_ref, yb_ref, wo_ref, gx_ref, wxq_ref, qnx_ref, x1_ref, qx_ref, ys, cs):
    yb = _seq_to_tm(yb_ref, ys).astype(BF16)
    x1 = x_ref[...] + _dot(ya_ref[...], wo_ref[0:A_WIDTH, :]) + _dot(yb, wo_ref[A_WIDTH:, :])
    x1_ref[...] = x1
    _xq_tail(x1, gx_ref, wxq_ref, qnx_ref, cs, qx_ref)


def _even_out(x, ya, yb, w_out, gx, w_xq, qnx, tt):
    nb, t_len, nyb = yb.shape
    tm = tt * nb
    rows = x.shape[0]
    d = D_MODEL

    def row(c):
        return pl.BlockSpec((tm, c), lambda i: (i, 0))

    def seq(c):
        return pl.BlockSpec((nb, tt, c), lambda i: (0, i, 0))

    return pl.pallas_call(
        _even_out_kernel,
        grid=(rows // tm,),
        in_specs=[row(d), row(A_WIDTH), seq(nyb), _const_spec((d, d)),
                  _const_spec((1, d)), _const_spec((d, d)), _const_spec((1, d))],
        out_specs=[row(d), seq(d)],
        out_shape=[jax.ShapeDtypeStruct((rows, d), F32), jax.ShapeDtypeStruct((nb, t_len, d), BF16)],
        scratch_shapes=[pltpu.VMEM((nyb // LANES, tm, LANES), F32), pltpu.VMEM((d // LANES, tm, LANES), F32)],
        compiler_params=_params(("arbitrary",)),
        name="even_out",
    )(x, ya, yb, w_out, gx, w_xq, qnx)


def _xattn_kernel(q_ref, mk_ref, mv_ref, o_ref):
    for h in range(X_HEADS):
        sl = slice(h * X_HEAD_DIM, (h + 1) * X_HEAD_DIM)
        s = _dot_t(q_ref[:, sl], mk_ref[0, :, sl])
        p = jnp.exp(s - jnp.max(s, axis=1, keepdims=True))
        o = _dot(p.astype(BF16), mv_ref[0, :, sl]) / jnp.sum(p, axis=1, keepdims=True)
        o_ref[:, sl] = o.astype(BF16)


def _xattn(qx, mk, mv, tq):
    nb, t_len, d = qx.shape
    m = mk.shape[1]
    return pl.pallas_call(
        _xattn_kernel,
        grid=(nb, t_len // tq),
        in_specs=[pl.BlockSpec((None, tq, d), lambda b, i: (b, i, 0)),
                  pl.BlockSpec((1, m, d), lambda b, i: (b, 0, 0)),
                  pl.BlockSpec((1, m, d), lambda b, i: (b, 0, 0))],
        out_specs=pl.BlockSpec((None, tq, d), lambda b, i: (b, i, 0)),
        out_shape=jax.ShapeDtypeStruct((nb, t_len, d), BF16),
        compiler_params=_params(("arbitrary", "arbitrary")),
        name="mem_attn",
    )(qx, mk, mv)


def _ffn_kernel(x_ref, o_ref, wxo_ref, g_ref, wup_ref, cw_ref, cb_ref, wdn_ref, hist_ref,
                y_ref, fh_ref, gbuf, cs, *, chunk, seq_out):
    tm = x_ref.shape[0]
    nh = (F_CONV - 1) * SUBLANES

    @pl.when(pl.program_id(0) == 0)
    def _():
        gbuf[0:nh, :] = hist_ref[...]

    x2 = x_ref[...] + _dot(_seq_to_tm(o_ref, cs).astype(BF16), wxo_ref[...])
    xn = _rms(x2, g_ref[...]).astype(BF16)
    acc = jnp.zeros((tm, D_MODEL), F32)
    for c0 in range(0, D_FF, chunk):
        c1 = c0 + chunk
        val = _dot(xn, wup_ref[:, c0:c1])
        gate = _dot(xn, wup_ref[:, D_FF + c0:D_FF + c1])
        gbuf[nh:nh + tm, c0:c1] = gate
        conv = gate * cw_ref[F_CONV - 1:F_CONV, c0:c1]
        for i in range(F_CONV - 1):
            conv = conv + gbuf[i * SUBLANES:i * SUBLANES + tm, c0:c1] * cw_ref[i:i + 1, c0:c1]
        conv = conv + cb_ref[:, c0:c1]
        act = (jax.nn.gelu(conv) * val).astype(BF16)
        acc = acc + _dot(act, wdn_ref[c0:c1, :])
    tail = gbuf[tm:tm + nh, :]
    gbuf[0:nh, :] = tail
    fh_ref[...] = tail
    if seq_out:
        _tm_to_seq(x2 + acc, cs, [(y_ref, D_MODEL)])
    else:
        y_ref[...] = x2 + acc


def _ffn(x, o, w_xo, g, w_up, conv_w, conv_b, w_down, hist, tt, seq_out):
    nb, t_len, d = o.shape
    tm = tt * nb
    rows = x.shape[0]
    nh = (F_CONV - 1) * SUBLANES

    def row(c):
        return pl.BlockSpec((tm, c), lambda i: (i, 0))

    def seq(c):
        return pl.BlockSpec((nb, tt, c), lambda i: (0, i, 0))

    y_shape = jax.ShapeDtypeStruct((nb, t_len, d) if seq_out else (rows, d), F32)
    return pl.pallas_call(
        functools.partial(_ffn_kernel, chunk=MXU_DIM, seq_out=seq_out),
        grid=(rows // tm,),
        in_specs=[row(d), seq(d), _const_spec((d, d)), _const_spec((1, d)), _const_spec((d, 2 * D_FF)),
                  _const_spec((F_CONV, D_FF)), _const_spec((1, D_FF)), _const_spec((D_FF, d)),
                  _const_spec((nh, D_FF))],
        out_specs=[seq(d) if seq_out else row(d), pl.BlockSpec((nh, D_FF), lambda i: (0, 0))],
        out_shape=[y_shape, jax.ShapeDtypeStruct((nh, D_FF), F32)],
        scratch_shapes=[pltpu.VMEM((tm + nh, D_FF), F32), pltpu.VMEM((d // LANES, tm, LANES), F32)],
        compiler_params=_params(("arbitrary",)),
        name="ffn",
    )(x, o, w_xo, g, w_up, conv_w, conv_b, w_down, hist)


def _odd_kernel(x_ref, g_ref, win_ref, cw_ref, cb_ref, wa_ref, ba_ref, wi_ref, bi_ref, lam_ref,
                wo_ref, hist_ref, h0_ref, gx_ref, wxq_ref, qnx_ref,
                x1_ref, qx_ref, ch_ref, hl_ref, xbuf, a_s, b_s, h_s, cs, *, stream_start):
    tm = x_ref.shape[0]
    nh = (C_CONV - 1) * SUBLANES
    first = pl.program_id(0) == 0

    @pl.when(first)
    def _():
        xbuf[0:nh, :] = hist_ref[...]
        h_s[...] = h0_ref[...]

    x = x_ref[...]
    xn = _rms(x, g_ref[...]).astype(BF16)
    xr_in = _dot(xn, win_ref[:, RNN_WIDTH:])
    xbuf[nh:nh + tm, :] = xr_in
    xr = _conv_taps(xbuf, xr_in, cw_ref, C_CONV, tm) + cb_ref[...]
    tail = xbuf[tm:tm + nh, :]
    xbuf[0:nh, :] = tail
    ch_ref[...] = tail

    xrb = xr.astype(BF16)
    lam = -lam_ref[...]
    sp = jnp.maximum(lam, 0.0) + jnp.log1p(jnp.exp(-jnp.abs(lam)))
    rows = lax.broadcasted_iota(I32, (tm, RNN_BLOCK), 0)
    for n in range(RNN_BLOCKS):
        sl = slice(n * RNN_BLOCK, (n + 1) * RNN_BLOCK)
        r = jax.nn.sigmoid(_dot(xrb[:, sl], wa_ref[n]) + ba_ref[:, sl])
        ig = jax.nn.sigmoid(_dot(xrb[:, sl], wi_ref[n]) + bi_ref[:, sl])
        log_a = -RG_C * r * sp[:, sl]
        a = jnp.exp(log_a)
        mult = jnp.sqrt(jnp.tanh(-log_a) * (1.0 + a * a))
        if stream_start:
            mult = jnp.where(first & (rows < SUBLANES), 1.0, mult)
        a_s[:, sl] = a
        b_s[:, sl] = mult * ig * xr[:, sl]

    def step(t, h):
        r0 = pl.multiple_of(t * SUBLANES, SUBLANES)
        h = a_s[pl.ds(r0, SUBLANES), :] * h + b_s[pl.ds(r0, SUBLANES), :]
        b_s[pl.ds(r0, SUBLANES), :] = h
        return h

    h = lax.fori_loop(0, tm // SUBLANES, step, h_s[...], unroll=8)
    h_s[...] = h
    hl_ref[...] = h

    gate = _dot(xn, win_ref[:, :RNN_WIDTH])
    act = (jax.nn.gelu(gate) * b_s[...]).astype(BF16)
    x1 = x + _dot(act, wo_ref[...])
    x1_ref[...] = x1
    _xq_tail(x1, gx_ref, wxq_ref, qnx_ref, cs, qx_ref)


def _odd(x, g, w_in, conv_w, conv_b, w_a, b_a, w_i, b_i, lam, w_out, hist, h0, gx, w_xq, qnx, tt,
         stream_start):
    nb = SUBLANES
    tm = tt * nb
    rows = x.shape[0]
    t_len = rows // nb
    d = D_MODEL
    r = RNN_WIDTH
    nh = (C_CONV - 1) * SUBLANES

    def row(c):
        return pl.BlockSpec((tm, c), lambda i: (i, 0))

    blk = (RNN_BLOCKS, RNN_BLOCK, RNN_BLOCK)
    return pl.pallas_call(
        functools.partial(_odd_kernel, stream_start=stream_start),
        grid=(rows // tm,),
        in_specs=[row(d), _const_spec((1, d)), _const_spec((d, 2 * r)), _const_spec((C_CONV, r)),
                  _const_spec((1, r)), _const_spec(blk), _const_spec((1, r)), _const_spec(blk),
                  _const_spec((1, r)), _const_spec((1, r)), _const_spec((r, d)), _const_spec((nh, r)),
                  _const_spec((SUBLANES, r)), _const_spec((1, d)), _const_spec((d, d)), _const_spec((1, d))],
        out_specs=[row(d), pl.BlockSpec((nb, tt, d), lambda i: (0, i, 0)), pl.BlockSpec((nh, r), lambda i: (0, 0)),
                   pl.BlockSpec((SUBLANES, r), lambda i: (0, 0))],
        out_shape=[jax.ShapeDtypeStruct((rows, d), F32), jax.ShapeDtypeStruct((nb, t_len, d), BF16),
                   jax.ShapeDtypeStruct((nh, r), F32), jax.ShapeDtypeStruct((SUBLANES, r), F32)],
        scratch_shapes=[pltpu.VMEM((tm + nh, r), F32), pltpu.VMEM((tm, r), F32), pltpu.VMEM((tm, r), F32),
                        pltpu.VMEM((SUBLANES, r), F32), pltpu.VMEM((d // LANES, tm, LANES), F32)],
        compiler_params=_params(("arbitrary",)),
        name="odd_mixer",
    )(x, g, w_in, conv_w, conv_b, w_a, b_a, w_i, b_i, lam, w_out, hist, h0, gx, w_xq, qnx)


def _to_tm(a):
    return jnp.transpose(a, (1, 0, 2)).reshape(a.shape[1] * a.shape[0], a.shape[2])


def _from_tm(a, w):
    return jnp.transpose(a.reshape(w, SUBLANES, a.shape[1]), (1, 0, 2))


def _trunk(x, st, mem_k, mem_v, p, bias, tt, tq_dsa, tq_x):
    nb, t_len, d = x.shape
    assert nb == SUBLANES
    past = 0 if st is None else st["b_k"].shape[2]
    length = past + t_len
    ntop = min(TOPK_MAX, length // 4)
    near, far = bias
    xt = None
    out = {}

    def hist(name, l, width, c):
        if st is None:
            return jnp.zeros(((width - 1) * nb, c), F32)
        return _to_tm(st[name][l])

    for l in range(DEPTH):
        if l % 2 == 0:
            e = l // 2
            assert l == 0, "the per-sequence input is converted by the first layer's kernel"
            xt, ya, q, iq, ikw, k, v, uh = _even_in(
                x, p["g_mix"][l], p["w_in_even"][e], p["a_conv_w"][e], p["b_q_norm"][e], p["b_k_norm"][e],
                hist("a_conv", e, A_CONV, A_WIDTH), tt)
            k_new = k.reshape(nb, t_len, B_KV_HEADS, B_HEAD_DIM)
            v_new = v.reshape(nb, t_len, B_KV_HEADS, B_HEAD_DIM)
            ik_new = ikw[:, :, :IDX_DIM]
            k_all, v_all, ik_all = k_new, v_new, ik_new
            if st is not None:
                k_all = jnp.concatenate([st["b_k"][e], k_new], axis=1)
                v_all = jnp.concatenate([st["b_v"][e], v_new], axis=1)
                ik_all = jnp.concatenate([st["b_kidx"][e], ik_new], axis=1)
            lp = -(-length // KEY_BLOCK) * KEY_BLOCK
            padl = lp - length
            k_att = jnp.pad(jnp.transpose(k_all, (0, 2, 1, 3)).astype(BF16), ((0, 0), (0, 0), (0, padl), (0, 0)))
            v_att = jnp.pad(jnp.transpose(v_all, (0, 2, 1, 3)).astype(BF16), ((0, 0), (0, 0), (0, padl), (0, 0)))
            vt_att = jnp.transpose(v_att.reshape(nb, B_KV_HEADS, lp // KEY_BLOCK, KEY_BLOCK, B_HEAD_DIM),
                                   (0, 1, 2, 4, 3))
            ik_att = jnp.pad(ik_all.astype(BF16), ((0, 0), (0, padl), (0, 0)))
            yb = _dsa(q, iq, ikw, k_att, vt_att, ik_att, near, far, tq_dsa, past, length, ntop)
            x1, qx = _even_out(xt, ya, yb, p["w_out_even"][e], p["g_x"][l], p["w_xq"][l], p["x_q_norm"][l], tt)
            out.setdefault("a_conv", []).append(_from_tm(uh, A_CONV - 1))
            out.setdefault("b_k", []).append(k_new)
            out.setdefault("b_v", []).append(v_new)
            out.setdefault("b_kidx", []).append(ik_new)
        else:
            o = l // 2
            h0 = jnp.zeros((nb, RNN_WIDTH), F32) if st is None else st["c_h"][o]
            x1, qx, ch, hl = _odd(
                xt, p["g_mix"][l], p["w_in_odd"][o], p["c_conv_w"][o], p["c_conv_b"][o], p["c_w_a"][o],
                p["c_b_a"][o], p["c_w_i"][o], p["c_b_i"][o], p["c_lambda"][o], p["w_out_odd"][o],
                hist("c_conv", o, C_CONV, RNN_WIDTH), h0, p["g_x"][l], p["w_xq"][l], p["x_q_norm"][l], tt,
                stream_start=(past == 0))
            out.setdefault("c_conv", []).append(_from_tm(ch, C_CONV - 1))
            out.setdefault("c_h", []).append(hl)
        xo = _xattn(qx, mem_k[l], mem_v[l], tq_x)
        xt, fh = _ffn(x1, xo, p["w_xo"][l], p["g_ffn"][l], p["w_up"][l], p["f_conv_w"][l], p["f_conv_b"][l],
                      p["w_down"][l], hist("f_conv", l, F_CONV, D_FF), tt, seq_out=(l == DEPTH - 1))
        out.setdefault("f_conv", []).append(_from_tm(fh, F_CONV - 1))
    return xt, {name: jnp.stack(v) for name, v in out.items()}


def kernel(x_prompt, x_sample, cache_b_k, cache_b_v, cache_b_kidx, state_a_conv, state_c_conv, state_c_h, state_ffn_conv, cache_mem_k, cache_mem_v, mem_prompt, rel_table, g_mix, w_in_even, a_conv_w, b_q_norm, b_k_norm, w_out_even, w_in_odd, c_conv_w, c_conv_b, c_w_a, c_b_a, c_w_i, c_b_i, c_lambda, w_out_odd, g_mem, g_x, w_xq, w_xk, w_xv, x_q_norm, x_k_norm, w_xo, g_ffn, w_up, f_conv_w, f_conv_b, w_down):
    d = D_MODEL
    n_even = w_in_even.shape[0]
    n_odd = w_in_odd.shape[0]
    bp, t_p, _ = x_prompt.shape
    m = mem_prompt.shape[1]

    def rowvec(a):
        return a.reshape(a.shape[0], 1, a.shape[-1])

    p = {
        "g_mix": rowvec(g_mix), "g_x": rowvec(g_x), "g_ffn": rowvec(g_ffn),
        "w_in_even": jnp.pad(w_in_even, ((0, 0), (0, 0), (0, EVEN_IN_PAD - EVEN_IN))).astype(BF16),
        "a_conv_w": a_conv_w,
        "b_q_norm": rowvec(jnp.tile(b_q_norm, (1, B_HEADS))),
        "b_k_norm": rowvec(jnp.tile(b_k_norm, (1, B_KV_HEADS))),
        "w_out_even": w_out_even.astype(BF16),
        "w_in_odd": w_in_odd.astype(BF16), "c_conv_w": c_conv_w, "c_conv_b": rowvec(c_conv_b),
        "c_w_a": c_w_a.astype(BF16), "c_b_a": rowvec(c_b_a), "c_w_i": c_w_i.astype(BF16), "c_b_i": rowvec(c_b_i),
        "c_lambda": rowvec(c_lambda), "w_out_odd": w_out_odd.astype(BF16),
        "w_xq": w_xq.astype(BF16), "x_q_norm": rowvec(jnp.tile(x_q_norm, (1, X_HEADS))),
        "w_xo": w_xo.astype(BF16), "w_up": w_up.astype(BF16), "f_conv_w": f_conv_w,
        "f_conv_b": rowvec(f_conv_b), "w_down": w_down.astype(BF16),
    }
    del n_even, n_odd
    bias = _bias_tiles(rel_table, KEY_BLOCK)

    mk, mv = _mem_kv(mem_prompt.reshape(bp * m, d), g_mem, w_xk, x_k_norm, w_xv)
    p_mem_k = mk.reshape(DEPTH, bp, m, X_HEADS, X_HEAD_DIM)
    p_mem_v = mv.reshape(DEPTH, bp, m, X_HEADS, X_HEAD_DIM)
    y_prompt, new_p = _trunk(x_prompt, None, mk.reshape(DEPTH, bp, m, d).astype(BF16),
                             mv.reshape(DEPTH, bp, m, d).astype(BF16), p, bias,
                             tt=64, tq_dsa=KEY_BLOCK, tq_x=512)

    bs, t_s, _ = x_sample.shape
    st_s = {"b_k": cache_b_k, "b_v": cache_b_v, "b_kidx": cache_b_kidx, "a_conv": state_a_conv,
            "c_conv": state_c_conv, "c_h": state_c_h, "f_conv": state_ffn_conv}
    ms = cache_mem_k.shape[2]
    y_sample, new_s = _trunk(x_sample, st_s, cache_mem_k.reshape(DEPTH, bs, ms, d).astype(BF16),
                             cache_mem_v.reshape(DEPTH, bs, ms, d).astype(BF16), p, bias,
                             tt=t_s, tq_dsa=t_s, tq_x=t_s)
    return (y_prompt, y_sample,
            new_p["b_k"], new_p["b_v"], new_p["b_kidx"], new_p["a_conv"], new_p["c_conv"],
            new_p["c_h"], new_p["f_conv"], p_mem_k, p_mem_v,
            new_s["b_k"], new_s["b_v"], new_s["b_kidx"], new_s["a_conv"], new_s["c_conv"],
            new_s["c_h"], new_s["f_conv"])
```

```python
import functools
import math

import jax
import jax.numpy as jnp
from jax import lax
from jax.experimental import pallas as pl
from jax.experimental.pallas import tpu as pltpu

F32 = jnp.float32
BF16 = jnp.bfloat16
I32 = jnp.int32
I16 = jnp.int16

D_MODEL = 1024
DEPTH = 2
CHUNK = 64
EPS = 1e-6
NEG_INF = -1e30
A_WIDTH = 512
A_CONV = 3
B_HEADS = 8
B_KV_HEADS = 2
B_HEAD_DIM = 64
IDX_HEADS = 8
IDX_DIM = 32
TOPK_MAX = 256
REL_BUCKETS = 32
REL_MAX_DIST = 128
RNN_WIDTH = 1024
RNN_BLOCKS = 8
RNN_BLOCK = 128
C_CONV = 4
RG_C = 8.0
X_HEADS = 4
X_HEAD_DIM = 256
D_FF = 2816
F_CONV = 3
EVEN_IN = 2600

SUBLANES = 8
LANES = 128
MXU_DIM = 256
PACKED_ROWS = 16
VMEM_LIMIT = 56 * 1024 * 1024

EVEN_IN_PAD = 2688
KEY_BLOCK = 256
INT_MIN = -2147483648


def _params(sem, vmem=VMEM_LIMIT):
    return pltpu.CompilerParams(dimension_semantics=sem, vmem_limit_bytes=vmem)


def _const_spec(shape):
    nd = len(shape)
    return pl.BlockSpec(shape, lambda *_: (0,) * nd, pipeline_mode=pl.Buffered(1))


def _rms(x, g):
    ms = jnp.mean(x * x, axis=-1, keepdims=True)
    return x * lax.rsqrt(ms + EPS) * g


def _head_rms(x, hd, gain):
    m, c = x.shape
    s = x * x
    parts = []
    if hd >= LANES:
        for h in range(c // hd):
            ms = jnp.mean(s[:, h * hd:(h + 1) * hd], axis=-1, keepdims=True)
            parts.append(x[:, h * hd:(h + 1) * hd] * lax.rsqrt(ms + EPS))
    else:
        lane = lax.broadcasted_iota(I32, (m, LANES), 1)
        for j in range(c // LANES):
            sj = s[:, j * LANES:(j + 1) * LANES]
            inv = jnp.zeros((m, LANES), F32)
            for k in range(LANES // hd):
                msk = (lane >= k * hd) & (lane < (k + 1) * hd)
                ms = jnp.sum(jnp.where(msk, sj, 0.0), axis=-1, keepdims=True) * (1.0 / hd)
                inv = jnp.where(msk, lax.rsqrt(ms + EPS), inv)
            parts.append(x[:, j * LANES:(j + 1) * LANES] * inv)
    y = parts[0] if len(parts) == 1 else jnp.concatenate(parts, axis=-1)
    return y * gain


def _dot(a, b):
    return jnp.dot(a, b, preferred_element_type=F32)


def _dot_t(a, b):
    return lax.dot_general(a, b, (((1,), (1,)), ((), ())), preferred_element_type=F32)


def _conv_taps(buf, cur, w_ref, width, tm):
    y = cur * w_ref[width - 1:width, :]
    for i in range(width - 1):
        y = y + buf[i * SUBLANES:i * SUBLANES + tm, :] * w_ref[i:i + 1, :]
    return y


def _seq_to_tm(src_ref, scr):
    nb, tt, c = src_ref.shape
    for b in range(nb):
        for j in range(c // LANES):
            scr[j, pl.ds(b, tt, stride=nb), :] = src_ref[b, :, j * LANES:(j + 1) * LANES].astype(F32)
    return jnp.concatenate([scr[j] for j in range(c // LANES)], axis=-1)


def _tm_to_seq(val, scr, dst_refs):
    tm, c = val.shape
    tt = tm // SUBLANES
    for j in range(c // LANES):
        scr[j] = val[:, j * LANES:(j + 1) * LANES]
    for b in range(SUBLANES):
        j0 = 0
        for ref, ci in dst_refs:
            nj = ci // LANES
            parts = [scr[j0 + j, pl.ds(b, tt, stride=SUBLANES), :] for j in range(nj)]
            ref[b] = (parts[0] if nj == 1 else jnp.concatenate(parts, axis=-1)).astype(ref.dtype)
            j0 += nj


def _memkv_kernel(mem_ref, g_ref, wk_ref, kn_ref, wv_ref, k_ref, v_ref):
    hm = _rms(mem_ref[...], g_ref[0]).astype(BF16)
    k_ref[0] = _head_rms(_dot(hm, wk_ref[0]), X_HEAD_DIM, kn_ref[0])
    v_ref[0] = _dot(hm, wv_ref[0])


def _mem_kv(mem, g_mem, w_xk, x_k_norm, w_xv):
    rows = mem.shape[0]
    tm = min(512, rows)
    d = D_MODEL
    kn = jnp.tile(x_k_norm, (1, X_HEADS)).reshape(DEPTH, 1, d)
    out = jax.ShapeDtypeStruct((DEPTH, rows, d), F32)
    return pl.pallas_call(
        _memkv_kernel,
        grid=(DEPTH, rows // tm),
        in_specs=[
            pl.BlockSpec((tm, d), lambda l, i: (i, 0)),
            pl.BlockSpec((1, 1, d), lambda l, i: (l, 0, 0)),
            pl.BlockSpec((1, d, d), lambda l, i: (l, 0, 0)),
            pl.BlockSpec((1, 1, d), lambda l, i: (l, 0, 0)),
            pl.BlockSpec((1, d, d), lambda l, i: (l, 0, 0)),
        ],
        out_specs=[pl.BlockSpec((1, tm, d), lambda l, i: (l, i, 0))] * 2,
        out_shape=[out, out],
        compiler_params=_params(("arbitrary", "arbitrary")),
        name="mem_kv",
    )(mem, g_mem.reshape(DEPTH, 1, d), w_xk.astype(BF16), kn, w_xv.astype(BF16))


def _even_in_kernel(x_ref, g_ref, w_ref, cw_ref, qn_ref, kn_ref, hist_ref,
                    xt_ref, ya_ref, q_ref, iq_ref, ikw_ref, k_ref, v_ref, uh_ref, ubuf, xs, cs):
    tm = xt_ref.shape[0]
    nh = (A_CONV - 1) * SUBLANES

    @pl.when(pl.program_id(0) == 0)
    def _():
        ubuf[0:nh, :] = hist_ref[...]

    x = _seq_to_tm(x_ref, xs)
    xt_ref[...] = x
    xn = _rms(x, g_ref[...]).astype(BF16)

    def proj(a, b):
        return _dot(xn, w_ref[:, a:b])

    zc = proj(A_WIDTH, 3 * A_WIDTH)
    u = zc[:, :A_WIDTH] * zc[:, A_WIDTH:]
    ubuf[nh:nh + tm, :] = u
    conv = _conv_taps(ubuf, u, cw_ref, A_CONV, tm)
    ya_ref[...] = (proj(0, A_WIDTH) * conv).astype(BF16)
    tail = ubuf[tm:tm + nh, :]
    ubuf[0:nh, :] = tail
    uh_ref[...] = tail

    o = 3 * A_WIDTH
    nq = B_HEADS * B_HEAD_DIM
    nkv = B_KV_HEADS * B_HEAD_DIM
    ni = IDX_HEADS * IDX_DIM
    q = _head_rms(proj(o, o + nq), B_HEAD_DIM, qn_ref[...]) * (B_HEAD_DIM ** -0.5)
    zkv = proj(o + nq, o + nq + 2 * nkv)
    k = _head_rms(zkv[:, :nkv], B_HEAD_DIM, kn_ref[...])
    zi = proj(o + nq + 2 * nkv, EVEN_IN_PAD)
    seq = jnp.concatenate([q, zi, k, zkv[:, nkv:]], axis=-1)
    _tm_to_seq(seq, cs, [(q_ref, nq), (iq_ref, ni), (ikw_ref, LANES), (k_ref, nkv), (v_ref, nkv)])


def _even_in(x, g, w_pad, conv_w, qn, kn, hist, tt):
    nb, t_len, d = x.shape
    tm = tt * nb
    rows = t_len * nb
    nh = (A_CONV - 1) * SUBLANES
    nq = B_HEADS * B_HEAD_DIM
    nkv = B_KV_HEADS * B_HEAD_DIM
    ni = IDX_HEADS * IDX_DIM
    nseq = nq + ni + LANES + 2 * nkv

    def row(c):
        return pl.BlockSpec((tm, c), lambda i: (i, 0))

    def seq(c):
        return pl.BlockSpec((nb, tt, c), lambda i: (0, i, 0))

    def seq_shape(c, dt):
        return jax.ShapeDtypeStruct((nb, t_len, c), dt)

    return pl.pallas_call(
        _even_in_kernel,
        grid=(t_len // tt,),
        in_specs=[seq(d), _const_spec((1, d)), _const_spec((d, EVEN_IN_PAD)), _const_spec((A_CONV, A_WIDTH)),
                  _const_spec((1, nq)), _const_spec((1, nkv)), _const_spec((nh, A_WIDTH))],
        out_specs=[row(d), row(A_WIDTH), seq(nq), seq(ni), seq(LANES), seq(nkv), seq(nkv),
                   pl.BlockSpec((nh, A_WIDTH), lambda i: (0, 0))],
        out_shape=[jax.ShapeDtypeStruct((rows, d), F32), jax.ShapeDtypeStruct((rows, A_WIDTH), BF16),
                   seq_shape(nq, BF16), seq_shape(ni, BF16), seq_shape(LANES, F32), seq_shape(nkv, F32),
                   seq_shape(nkv, F32), jax.ShapeDtypeStruct((nh, A_WIDTH), F32)],
        scratch_shapes=[pltpu.VMEM((tm + nh, A_WIDTH), F32), pltpu.VMEM((d // LANES, tm, LANES), F32),
                        pltpu.VMEM((nseq // LANES, tm, LANES), F32)],
        compiler_params=_params(("arbitrary",)),
        name="even_in",
    )(x, g, w_pad, conv_w, qn, kn, hist)


def _rel_bucket(rel):
    half = REL_BUCKETS // 2
    max_exact = half // 2
    n = -rel
    ret = jnp.where(n < 0, half, 0)
    n = jnp.abs(n)
    nf = jnp.maximum(n, 1).astype(F32)
    large = max_exact + (jnp.log(nf / max_exact) / math.log(REL_MAX_DIST / max_exact)
                         * (half - max_exact)).astype(I32)
    large = jnp.minimum(large, half - 1)
    return ret + jnp.where(n < max_exact, n, large)


def _bias_kernel(tab_ref, near_ref, far_ref):
    tk = near_ref.shape[-1]
    r = lax.broadcasted_iota(I32, (tk, tk), 0)
    c = lax.broadcasted_iota(I32, (tk, tk), 1)

    def lookup(bucket, h):
        def body(j, acc):
            return jnp.where(bucket == j, tab_ref[j, h], acc)
        return lax.fori_loop(0, REL_BUCKETS, body, jnp.zeros(bucket.shape, F32))

    for blk in range(2):
        bucket = _rel_bucket(r - c - blk * tk)
        for h in range(B_HEADS):
            near_ref[blk, h] = lookup(bucket, h)
    bucket = _rel_bucket(-REL_MAX_DIST - c[0:SUBLANES, :])
    for h in range(B_HEADS):
        far_ref[h] = lookup(bucket, h)


def _bias_tiles(rel_table, tk):
    return pl.pallas_call(
        _bias_kernel,
        in_specs=[pl.BlockSpec(memory_space=pltpu.SMEM)],
        out_shape=[jax.ShapeDtypeStruct((2, B_HEADS, tk, tk), F32),
                   jax.ShapeDtypeStruct((B_HEADS, SUBLANES, tk), F32)],
        name="rel_bias",
    )(rel_table)


def _dsa_kernel(q_ref, iq_ref, ikw_ref, k_ref, vt_ref, ik_ref, near_ref, far_ref, o_ref,
                skey, skh, skl, madd_s, s_s, p_s, qh_s, iqh_s, pad_s, w_s, x_s, m_s, acc_s,
                *, tq, qw, tk, past, length, ntop, idx_bits):
    i = pl.program_id(1)
    q0 = past + i * tq
    nkb = (q0 + tq + tk - 1) // tk
    last = nkb - 1
    ni = IDX_HEADS * IDX_DIM
    groups = B_HEADS // B_KV_HEADS
    lane = lax.broadcasted_iota(I32, (1, qw), 1)
    sub8 = lax.broadcasted_iota(I32, (SUBLANES, qw), 0)
    qchunk = jnp.right_shift(q0 + lane, 6)

    if tq != qw:
        qh_s[...] = jnp.zeros(qh_s.shape, BF16)
        iqh_s[...] = jnp.zeros(iqh_s.shape, BF16)
        pad_s[...] = jnp.zeros(pad_s.shape, F32)
    for h in range(B_HEADS):
        qh_s[h, 0:tq, :] = q_ref[:, B_HEAD_DIM * h:B_HEAD_DIM * (h + 1)]
    for h in range(IDX_HEADS):
        iqh_s[h, 0:tq, :] = iq_ref[:, IDX_DIM * h:IDX_DIM * (h + 1)]
    pad_s[0:tq, :] = ikw_ref[...]
    w_s[...] = pad_s[...].T[IDX_DIM:IDX_DIM + IDX_HEADS, :] * (ni ** -0.5)

    slab = tk

    def scores(kb, masked):
        for sl in range(tk // slab):
            ks = pl.multiple_of(kb * tk + sl * slab, slab)
            ikb = ik_ref[0, pl.ds(ks, slab), :]
            acc = jnp.zeros((slab, qw), F32)
            for h in range(IDX_HEADS):
                acc = acc + w_s[h:h + 1, :] * jnp.maximum(_dot_t(ikb, iqh_s[h]), 0.0)
            if masked:
                kpos = kb * tk + sl * slab + lax.broadcasted_iota(I32, (slab, qw), 0)
                vis = (jnp.right_shift(kpos, 6) <= qchunk) & (kpos < length)
                acc = jnp.where(vis, acc, -jnp.inf)
            bits = lax.bitcast_convert_type(acc, I32)
            key = jnp.where(bits < 0, bits ^ jnp.int32(0x7FFFFFFF), bits)
            rows = slice(sl * slab, (sl + 1) * slab)
            skey[kb, rows, :] = key
            skh[kb, rows, :] = jnp.right_shift(key, 16).astype(I16)
            skl[kb, rows, :] = ((key & 0xFFFF) - 32768).astype(I16)

    def score_body(kb, carry):
        scores(kb, False)
        return carry

    lax.fori_loop(0, last, score_body, 0)
    scores(last, True)

    n_acc = 4
    prow = PACKED_ROWS

    def count16(ref, pred):
        def body(kb, accs):
            accs = list(accs)
            for g in range(tk // prow):
                blk = ref[kb, g * prow:(g + 1) * prow, :]
                accs[g % n_acc] = accs[g % n_acc] + jnp.where(pred(blk), jnp.int16(1), jnp.int16(0))
            return tuple(accs)
        accs = lax.fori_loop(0, nkb, body, tuple(jnp.zeros((prow, qw), I16) for _ in range(n_acc)))
        tot = (accs[0].astype(I32) + accs[1].astype(I32)) + (accs[2].astype(I32) + accs[3].astype(I32))
        return jnp.sum(tot, axis=0, keepdims=True)

    def rep16(v):
        return jnp.broadcast_to(v, (prow, qw)).astype(I16)

    def kth16(ref, kth):
        def bit_body(it, prefix):
            cand_u = prefix | jnp.left_shift(jnp.int32(1), 15 - it)
            cand = rep16(cand_u - 32768)
            cnt = count16(ref, lambda blk: blk >= cand)
            return jnp.where(cnt >= kth, cand_u, prefix)
        return lax.fori_loop(0, 16, bit_body, jnp.zeros((1, qw), I32)) - 32768

    def count(pred):
        def body(kb, accs):
            accs = list(accs)
            for g in range(tk // SUBLANES):
                blk = skey[kb, g * SUBLANES:(g + 1) * SUBLANES, :]
                accs[g % n_acc] = accs[g % n_acc] + jnp.where(pred(kb, g, blk), 1, 0)
            return tuple(accs)
        accs = lax.fori_loop(0, nkb, body, tuple(jnp.zeros((SUBLANES, qw), I32) for _ in range(n_acc)))
        tot = (accs[0] + accs[1]) + (accs[2] + accs[3])
        return jnp.sum(tot, axis=0, keepdims=True)

    def rep8(v):
        return jnp.broadcast_to(v, (SUBLANES, qw))

    p_hi = kth16(skh, ntop)
    hi16 = rep16(p_hi)
    above = count16(skh, lambda blk: blk > hi16)

    def low_body(kb, carry):
        for g in range(tk // prow):
            rows = slice(g * prow, (g + 1) * prow)
            skl[kb, rows, :] = jnp.where(skh[kb, rows, :] == hi16, skl[kb, rows, :], jnp.int16(-32768))
        return carry

    lax.fori_loop(0, nkb, low_body, 0)
    p_lo = kth16(skl, ntop - above)
    tau = p_hi * 65536 + (p_lo + 32768)
    tau8 = rep8(tau)

    cnt_gt = count(lambda kb, g, blk: blk > tau8)
    cnt_ge = count(lambda kb, g, blk: blk >= tau8)
    need = ntop - cnt_gt
    key_ninf = jnp.int32(0x7FFFFFFF) ^ jnp.int32(-8388608)
    finite = tau != key_ninf
    tie_rows = ((cnt_ge - cnt_gt) > need) & finite & (lane < tq)
    x_s[...] = rep8(jnp.where(finite, jnp.int32((1 << idx_bits) - 1), 0))
    any_tie = jnp.max(jnp.where(tie_rows, 1, 0)) > 0

    @pl.when(any_tie)
    def _():
        need8 = rep8(need)

        def xbit(it, xlim):
            cand8 = xlim | jnp.left_shift(jnp.int32(1), idx_bits - 1 - it)
            cnt = count(lambda kb, g, blk: (blk == tau8) & ((kb * tk + g * SUBLANES + sub8) < cand8))
            return jnp.where(rep8(cnt) <= need8, cand8, xlim)
        xlim = lax.fori_loop(0, idx_bits, xbit, jnp.zeros((SUBLANES, qw), I32))
        x_s[...] = jnp.where(rep8(finite), xlim, 0)

    m_s[...] = jnp.full(m_s.shape, NEG_INF, F32)
    acc_s[...] = jnp.zeros(acc_s.shape, F32)
    vrows = vt_ref.shape[3]

    tau_ge8 = rep8(jnp.where(finite, tau, tau + 1))

    def make_mask(kb):
        @pl.when(jnp.logical_not(any_tie))
        def _():
            for g in range(tk // SUBLANES):
                rows = slice(g * SUBLANES, (g + 1) * SUBLANES)
                madd_s[rows, :] = jnp.where(skey[kb, rows, :] >= tau_ge8, 0.0, NEG_INF)

        @pl.when(any_tie)
        def _():
            xlim8 = x_s[...]
            for g in range(tk // SUBLANES):
                rows = slice(g * SUBLANES, (g + 1) * SUBLANES)
                key = skey[kb, rows, :]
                sel = (key > tau8) | ((key == tau8) & ((kb * tk + g * SUBLANES + sub8) < xlim8))
                madd_s[rows, :] = jnp.where(sel, 0.0, NEG_INF)

    def attend(kb, near_blk):
        ks = pl.multiple_of(kb * tk, tk)
        make_mask(kb)
        for h in range(B_HEADS):
            kn = k_ref[0, h // groups, pl.ds(ks, tk), :]
            if near_blk is None:
                s_s[h] = _dot_t(kn, qh_s[h]) + madd_s[...]
            else:
                s_s[h] = _dot_t(kn, qh_s[h]) + (madd_s[...] + near_ref[near_blk, h])
        shifts, alphas = [], []
        for h in range(B_HEADS):
            mx = [s_s[h, j * prow:(j + 1) * prow, :] for j in range(n_acc)]
            for r in range(n_acc, tk // prow):
                mx[r % n_acc] = jnp.maximum(mx[r % n_acc], s_s[h, r * prow:(r + 1) * prow, :])
            m_cur = jnp.max(jnp.maximum(jnp.maximum(mx[0], mx[1]), jnp.maximum(mx[2], mx[3])), axis=0, keepdims=True)
            c_h = far_ref[h, 0:1, 0:qw] if near_blk is None else jnp.zeros((1, qw), F32)
            m_prev = m_s[h:h + 1, :]
            m_new = jnp.maximum(m_prev, m_cur + c_h)
            m_s[h:h + 1, :] = m_new
            alphas.append(jnp.exp(m_prev - m_new))
            shifts.append(m_new - c_h)
        for h in range(B_HEADS):
            shift = jnp.broadcast_to(shifts[h], (prow, qw))
            for r in range(tk // prow):
                rows = slice(r * prow, (r + 1) * prow)
                p_s[h, rows, :] = jnp.exp(s_s[h, rows, :] - shift).astype(BF16)
        for h in range(B_HEADS):
            hs = slice(h * vrows, (h + 1) * vrows)
            acc_s[hs, :] = acc_s[hs, :] * alphas[h] + _dot(vt_ref[0, h // groups, kb], p_s[h])

    def far_body(kb, carry):
        attend(kb, None)
        return carry

    lax.fori_loop(0, nkb - 2, far_body, 0)

    @pl.when(nkb >= 2)
    def _():
        attend(nkb - 2, 1)

    attend(last, 0)

    outs = []
    for h in range(B_HEADS):
        r0 = h * vrows
        outs.append(acc_s[r0:r0 + B_HEAD_DIM, :] / acc_s[r0 + B_HEAD_DIM:r0 + B_HEAD_DIM + 1, :])
    o = jnp.concatenate(outs, axis=0).T
    o_ref[...] = o[0:tq, :].astype(BF16)


def _dsa(q, iq, ikw, k_att, vt_att, ik_att, near, far, tq, past, length, ntop):
    tk = KEY_BLOCK
    nb, t_len, _ = q.shape
    lp = k_att.shape[2]
    vrows = vt_att.shape[3]
    nq = B_HEADS * B_HEAD_DIM
    ni = IDX_HEADS * IDX_DIM
    qw = max(tq, LANES)
    assert past % tk == 0 and (tq == tk or t_len == tq) and tq <= tk and lp % tk == 0
    idx_bits = lp.bit_length()

    def qblk(c):
        return pl.BlockSpec((None, tq, c), lambda b, i: (b, i, 0))

    kern = functools.partial(_dsa_kernel, tq=tq, qw=qw, tk=tk, past=past, length=length, ntop=ntop,
                             idx_bits=idx_bits)
    return pl.pallas_call(
        kern,
        grid=(nb, t_len // tq),
        in_specs=[qblk(nq), qblk(ni), qblk(LANES),
                  pl.BlockSpec((1, B_KV_HEADS, lp, B_HEAD_DIM), lambda b, i: (b, 0, 0, 0)),
                  pl.BlockSpec((1, B_KV_HEADS, lp // tk, vrows, tk), lambda b, i: (b, 0, 0, 0, 0)),
                  pl.BlockSpec((1, lp, IDX_DIM), lambda b, i: (b, 0, 0)),
                  pl.BlockSpec((2, B_HEADS, tk, qw), lambda b, i: (0, 0, 0, 0)),
                  pl.BlockSpec((B_HEADS, SUBLANES, tk), lambda b, i: (0, 0, 0))],
        out_specs=qblk(nq),
        out_shape=jax.ShapeDtypeStruct((nb, t_len, nq), BF16),
        scratch_shapes=[pltpu.VMEM((lp // tk, tk, qw), I32),
                        pltpu.VMEM((lp // tk, tk, qw), I16),
                        pltpu.VMEM((lp // tk, tk, qw), I16),
                        pltpu.VMEM((tk, qw), F32),
                        pltpu.VMEM((B_HEADS, tk, qw), F32),
                        pltpu.VMEM((B_HEADS, tk, qw), BF16),
                        pltpu.VMEM((B_HEADS, qw, B_HEAD_DIM), BF16),
                        pltpu.VMEM((IDX_HEADS, qw, IDX_DIM), BF16),
                        pltpu.VMEM((qw, LANES), F32),
                        pltpu.VMEM((IDX_HEADS, qw), F32),
                        pltpu.VMEM((SUBLANES, qw), I32),
                        pltpu.VMEM((B_HEADS, qw), F32),
                        pltpu.VMEM((B_HEADS * vrows, qw), F32)],
        compiler_params=_params(("arbitrary", "arbitrary")),
        name="dsa",
    )(q, iq, ikw, k_att, vt_att, ik_att, near, far)


def _xq_tail(x1, gx_ref, wxq_ref, qnx_ref, cs, qx_ref):
    xn = _rms(x1, gx_ref[...]).astype(BF16)
    qx = _head_rms(_dot(xn, wxq_ref[...]), X_HEAD_DIM, qnx_ref[...])
    _tm_to_seq(qx * (X_HEAD_DIM ** -0.5), cs, [(qx_ref, D_MODEL)])


def _even_out_kernel(x_ref, ya_ref, yb_ref, wo_ref, gx_ref, wxq_ref, qnx_ref, x1_ref, qx_ref, ys, cs):
    yb = _seq_to_tm(yb_ref, ys).astype(BF16)
    x1 = x_ref[...] + _dot(ya_ref[...], wo_ref[0:A_WIDTH, :]) + _dot(yb, wo_ref[A_WIDTH:, :])
    x1_ref[...] = x1
    _xq_tail(x1, gx_ref, wxq_ref, qnx_ref, cs, qx_ref)


def _even_out(x, ya, yb, w_out, gx, w_xq, qnx, tt):
    nb, t_len, nyb = yb.shape
    tm = tt * nb
    rows = x.shape[0]
    d = D_MODEL

    def row(c):
        return pl.BlockSpec((tm, c), lambda i: (i, 0))

    def seq(c):
        return pl.BlockSpec((nb, tt, c), lambda i: (0, i, 0))

    return pl.pallas_call(
        _even_out_kernel,
        grid=(rows // tm,),
        in_specs=[row(d), row(A_WIDTH), seq(nyb), _const_spec((d, d)),
                  _const_spec((1, d)), _const_spec((d, d)), _const_spec((1, d))],
        out_specs=[row(d), seq(d)],
        out_shape=[jax.ShapeDtypeStruct((rows, d), F32), jax.ShapeDtypeStruct((nb, t_len, d), BF16)],
        scratch_shapes=[pltpu.VMEM((nyb // LANES, tm, LANES), F32), pltpu.VMEM((d // LANES, tm, LANES), F32)],
        compiler_params=_params(("arbitrary",)),
        name="even_out",
    )(x, ya, yb, w_out, gx, w_xq, qnx)


def _xattn_kernel(q_ref, mk_ref, mv_ref, o_ref):
    for h in range(X_HEADS):
        sl = slice(h * X_HEAD_DIM, (h + 1) * X_HEAD_DIM)
        s = _dot_t(q_ref[:, sl], mk_ref[0, :, sl])
        p = jnp.exp(s - jnp.max(s, axis=1, keepdims=True))
        o = _dot(p.astype(BF16), mv_ref[0, :, sl]) / jnp.sum(p, axis=1, keepdims=True)
        o_ref[:, sl] = o.astype(BF16)


def _xattn(qx, mk, mv, tq):
    nb, t_len, d = qx.shape
    m = mk.shape[1]
    return pl.pallas_call(
        _xattn_kernel,
        grid=(nb, t_len // tq),
        in_specs=[pl.BlockSpec((None, tq, d), lambda b, i: (b, i, 0)),
                  pl.BlockSpec((1, m, d), lambda b, i: (b, 0, 0)),
                  pl.BlockSpec((1, m, d), lambda b, i: (b, 0, 0))],
        out_specs=pl.BlockSpec((None, tq, d), lambda b, i: (b, i, 0)),
        out_shape=jax.ShapeDtypeStruct((nb, t_len, d), BF16),
        compiler_params=_params(("arbitrary", "arbitrary")),
        name="mem_attn",
    )(qx, mk, mv)


def _ffn_kernel(x_ref, o_ref, wxo_ref, g_ref, wup_ref, cw_ref, cb_ref, wdn_ref, hist_ref,
                y_ref, fh_ref, gbuf, cs, *, chunk, seq_out):
    tm = x_ref.shape[0]
    nh = (F_CONV - 1) * SUBLANES

    @pl.when(pl.program_id(0) == 0)
    def _():
        gbuf[0:nh, :] = hist_ref[...]

    x2 = x_ref[...] + _dot(_seq_to_tm(o_ref, cs).astype(BF16), wxo_ref[...])
    xn = _rms(x2, g_ref[...]).astype(BF16)
    acc = jnp.zeros((tm, D_MODEL), F32)
    for c0 in range(0, D_FF, chunk):
        c1 = c0 + chunk
        val = _dot(xn, wup_ref[:, c0:c1])
        gate = _dot(xn, wup_ref[:, D_FF + c0:D_FF + c1])
        gbuf[nh:nh + tm, c0:c1] = gate
        conv = gate * cw_ref[F_CONV - 1:F_CONV, c0:c1]
        for i in range(F_CONV - 1):
            conv = conv + gbuf[i * SUBLANES:i * SUBLANES + tm, c0:c1] * cw_ref[i:i + 1, c0:c1]
        conv = conv + cb_ref[:, c0:c1]
        act = (jax.nn.gelu(conv) * val).astype(BF16)
        acc = acc + _dot(act, wdn_ref[c0:c1, :])
    tail = gbuf[tm:tm + nh, :]
    gbuf[0:nh, :] = tail
    fh_ref[...] = tail
    if seq_out:
        _tm_to_seq(x2 + acc, cs, [(y_ref, D_MODEL)])
    else:
        y_ref[...] = x2 + acc


def _ffn(x, o, w_xo, g, w_up, conv_w, conv_b, w_down, hist, tt, seq_out):
    nb, t_len, d = o.shape
    tm = tt * nb
    rows = x.shape[0]
    nh = (F_CONV - 1) * SUBLANES

    def row(c):
        return pl.BlockSpec((tm, c), lambda i: (i, 0))

    def seq(c):
        return pl.BlockSpec((nb, tt, c), lambda i: (0, i, 0))

    y_shape = jax.ShapeDtypeStruct((nb, t_len, d) if seq_out else (rows, d), F32)
    return pl.pallas_call(
        functools.partial(_ffn_kernel, chunk=MXU_DIM, seq_out=seq_out),
        grid=(rows // tm,),
        in_specs=[row(d), seq(d), _const_spec((d, d)), _const_spec((1, d)), _const_spec((d, 2 * D_FF)),
                  _const_spec((F_CONV, D_FF)), _const_spec((1, D_FF)), _const_spec((D_FF, d)),
                  _const_spec((nh, D_FF))],
        out_specs=[seq(d) if seq_out else row(d), pl.BlockSpec((nh, D_FF), lambda i: (0, 0))],
        out_shape=[y_shape, jax.ShapeDtypeStruct((nh, D_FF), F32)],
        scratch_shapes=[pltpu.VMEM((tm + nh, D_FF), F32), pltpu.VMEM((d // LANES, tm, LANES), F32)],
        compiler_params=_params(("arbitrary",)),
        name="ffn",
    )(x, o, w_xo, g, w_up, conv_w, conv_b, w_down, hist)


def _odd_kernel(x_ref, g_ref, win_ref, cw_ref, cb_ref, wa_ref, ba_ref, wi_ref, bi_ref, lam_ref,
                wo_ref, hist_ref, h0_ref, gx_ref, wxq_ref, qnx_ref,
                x1_ref, qx_ref, ch_ref, hl_ref, xbuf, a_s, b_s, h_s, cs, *, stream_start):
    tm = x_ref.shape[0]
    nh = (C_CONV - 1) * SUBLANES
    first = pl.program_id(0) == 0

    @pl.when(first)
    def _():
        xbuf[0:nh, :] = hist_ref[...]
        h_s[...] = h0_ref[...]

    x = x_ref[...]
    xn = _rms(x, g_ref[...]).astype(BF16)
    xr_in = _dot(xn, win_ref[:, RNN_WIDTH:])
    xbuf[nh:nh + tm, :] = xr_in
    xr = _conv_taps(xbuf, xr_in, cw_ref, C_CONV, tm) + cb_ref[...]
    tail = xbuf[tm:tm + nh, :]
    xbuf[0:nh, :] = tail
    ch_ref[...] = tail

    xrb = xr.astype(BF16)
    lam = -lam_ref[...]
    sp = jnp.maximum(lam, 0.0) + jnp.log1p(jnp.exp(-jnp.abs(lam)))
    rows = lax.broadcasted_iota(I32, (tm, RNN_BLOCK), 0)
    for n in range(RNN_BLOCKS):
        sl = slice(n * RNN_BLOCK, (n + 1) * RNN_BLOCK)
        r = jax.nn.sigmoid(_dot(xrb[:, sl], wa_ref[n]) + ba_ref[:, sl])
        ig = jax.nn.sigmoid(_dot(xrb[:, sl], wi_ref[n]) + bi_ref[:, sl])
        log_a = -RG_C * r * sp[:, sl]
        a = jnp.exp(log_a)
        mult = jnp.sqrt(jnp.tanh(-log_a) * (1.0 + a * a))
        if stream_start:
            mult = jnp.where(first & (rows < SUBLANES), 1.0, mult)
        a_s[:, sl] = a
        b_s[:, sl] = mult * ig * xr[:, sl]

    def step(t, h):
        r0 = pl.multiple_of(t * SUBLANES, SUBLANES)
        h = a_s[pl.ds(r0, SUBLANES), :] * h + b_s[pl.ds(r0, SUBLANES), :]
        b_s[pl.ds(r0, SUBLANES), :] = h
        return h

    h = lax.fori_loop(0, tm // SUBLANES, step, h_s[...], unroll=8)
    h_s[...] = h
    hl_ref[...] = h

    gate = _dot(xn, win_ref[:, :RNN_WIDTH])
    act = (jax.nn.gelu(gate) * b_s[...]).astype(BF16)
    x1 = x + _dot(act, wo_ref[...])
    x1_ref[...] = x1
    _xq_tail(x1, gx_ref, wxq_ref, qnx_ref, cs, qx_ref)


def _odd(x, g, w_in, conv_w, conv_b, w_a, b_a, w_i, b_i, lam, w_out, hist, h0, gx, w_xq, qnx, tt,
         stream_start):
    nb = SUBLANES
    tm = tt * nb
    rows = x.shape[0]
    t_len = rows // nb
    d = D_MODEL
    r = RNN_WIDTH
    nh = (C_CONV - 1) * SUBLANES

    def row(c):
        return pl.BlockSpec((tm, c), lambda i: (i, 0))

    blk = (RNN_BLOCKS, RNN_BLOCK, RNN_BLOCK)
    return pl.pallas_call(
        functools.partial(_odd_kernel, stream_start=stream_start),
        grid=(rows // tm,),
        in_specs=[row(d), _const_spec((1, d)), _const_spec((d, 2 * r)), _const_spec((C_CONV, r)),
                  _const_spec((1, r)), _const_spec(blk), _const_spec((1, r)), _const_spec(blk),
                  _const_spec((1, r)), _const_spec((1, r)), _const_spec((r, d)), _const_spec((nh, r)),
                  _const_spec((SUBLANES, r)), _const_spec((1, d)), _const_spec((d, d)), _const_spec((1, d))],
        out_specs=[row(d), pl.BlockSpec((nb, tt, d), lambda i: (0, i, 0)), pl.BlockSpec((nh, r), lambda i: (0, 0)),
                   pl.BlockSpec((SUBLANES, r), lambda i: (0, 0))],
        out_shape=[jax.ShapeDtypeStruct((rows, d), F32), jax.ShapeDtypeStruct((nb, t_len, d), BF16),
                   jax.ShapeDtypeStruct((nh, r), F32), jax.ShapeDtypeStruct((SUBLANES, r), F32)],
        scratch_shapes=[pltpu.VMEM((tm + nh, r), F32), pltpu.VMEM((tm, r), F32), pltpu.VMEM((tm, r), F32),
                        pltpu.VMEM((SUBLANES, r), F32), pltpu.VMEM((d // LANES, tm, LANES), F32)],
        compiler_params=_params(("arbitrary",)),
        name="odd_mixer",
    )(x, g, w_in, conv_w, conv_b, w_a, b_a, w_i, b_i, lam, w_out, hist, h0, gx, w_xq, qnx)


def _to_tm(a):
    return jnp.transpose(a, (1, 0, 2)).reshape(a.shape[1] * a.shape[0], a.shape[2])


def _from_tm(a, w):
    return jnp.transpose(a.reshape(w, SUBLANES, a.shape[1]), (1, 0, 2))


def _trunk(x, st, mem_k, mem_v, p, bias, tt, tq_dsa, tq_x):
    nb, t_len, d = x.shape
    assert nb == SUBLANES
    past = 0 if st is None else st["b_k"].shape[2]
    length = past + t_len
    ntop = min(TOPK_MAX, length // 4)
    near, far = bias
    xt = None
    out = {}

    def hist(name, l, width, c):
        if st is None:
            return jnp.zeros(((width - 1) * nb, c), F32)
        return _to_tm(st[name][l])

    for l in range(DEPTH):
        if l % 2 == 0:
            e = l // 2
            assert l == 0, "the per-sequence input is converted by the first layer's kernel"
            xt, ya, q, iq, ikw, k, v, uh = _even_in(
                x, p["g_mix"][l], p["w_in_even"][e], p["a_conv_w"][e], p["b_q_norm"][e], p["b_k_norm"][e],
                hist("a_conv", e, A_CONV, A_WIDTH), tt)
            k_new = k.reshape(nb, t_len, B_KV_HEADS, B_HEAD_DIM)
            v_new = v.reshape(nb, t_len, B_KV_HEADS, B_HEAD_DIM)
            ik_new = ikw[:, :, :IDX_DIM]
            k_all, v_all, ik_all = k_new, v_new, ik_new
            if st is not None:
                k_all = jnp.concatenate([st["b_k"][e], k_new], axis=1)
                v_all = jnp.concatenate([st["b_v"][e], v_new], axis=1)
                ik_all = jnp.concatenate([st["b_kidx"][e], ik_new], axis=1)
            lp = -(-length // KEY_BLOCK) * KEY_BLOCK
            padl = lp - length
            k_att = jnp.pad(jnp.transpose(k_all, (0, 2, 1, 3)).astype(BF16), ((0, 0), (0, 0), (0, padl), (0, 0)))
            v_att = jnp.pad(jnp.transpose(v_all, (0, 2, 1, 3)).astype(BF16), ((0, 0), (0, 0), (0, padl), (0, 0)))
            vt_att = jnp.transpose(v_att.reshape(nb, B_KV_HEADS, lp // KEY_BLOCK, KEY_BLOCK, B_HEAD_DIM),
                                   (0, 1, 2, 4, 3))
            ones = jnp.ones(vt_att.shape[:3] + (1, KEY_BLOCK), BF16)
            zeros = jnp.zeros(vt_att.shape[:3] + (PACKED_ROWS - 1, KEY_BLOCK), BF16)
            vt_att = jnp.concatenate([vt_att, ones, zeros], axis=3)
            ik_att = jnp.pad(ik_all.astype(BF16), ((0, 0), (0, padl), (0, 0)))
            yb = _dsa(q, iq, ikw, k_att, vt_att, ik_att, near, far, tq_dsa, past, length, ntop)
            x1, qx = _even_out(xt, ya, yb, p["w_out_even"][e], p["g_x"][l], p["w_xq"][l], p["x_q_norm"][l], tt)
            out.setdefault("a_conv", []).append(_from_tm(uh, A_CONV - 1))
            out.setdefault("b_k", []).append(k_new)
            out.setdefault("b_v", []).append(v_new)
            out.setdefault("b_kidx", []).append(ik_new)
        else:
            o = l // 2
            h0 = jnp.zeros((nb, RNN_WIDTH), F32) if st is None else st["c_h"][o]
            x1, qx, ch, hl = _odd(
                xt, p["g_mix"][l], p["w_in_odd"][o], p["c_conv_w"][o], p["c_conv_b"][o], p["c_w_a"][o],
                p["c_b_a"][o], p["c_w_i"][o], p["c_b_i"][o], p["c_lambda"][o], p["w_out_odd"][o],
                hist("c_conv", o, C_CONV, RNN_WIDTH), h0, p["g_x"][l], p["w_xq"][l], p["x_q_norm"][l], tt,
                stream_start=(past == 0))
            out.setdefault("c_conv", []).append(_from_tm(ch, C_CONV - 1))
            out.setdefault("c_h", []).append(hl)
        xo = _xattn(qx, mem_k[l], mem_v[l], tq_x)
        xt, fh = _ffn(x1, xo, p["w_xo"][l], p["g_ffn"][l], p["w_up"][l], p["f_conv_w"][l], p["f_conv_b"][l],
                      p["w_down"][l], hist("f_conv", l, F_CONV, D_FF), tt, seq_out=(l == DEPTH - 1))
        out.setdefault("f_conv", []).append(_from_tm(fh, F_CONV - 1))
    return xt, {name: jnp.stack(v) for name, v in out.items()}


def kernel(x_prompt, x_sample, cache_b_k, cache_b_v, cache_b_kidx, state_a_conv, state_c_conv, state_c_h, state_ffn_conv, cache_mem_k, cache_mem_v, mem_prompt, rel_table, g_mix, w_in_even, a_conv_w, b_q_norm, b_k_norm, w_out_even, w_in_odd, c_conv_w, c_conv_b, c_w_a, c_b_a, c_w_i, c_b_i, c_lambda, w_out_odd, g_mem, g_x, w_xq, w_xk, w_xv, x_q_norm, x_k_norm, w_xo, g_ffn, w_up, f_conv_w, f_conv_b, w_down):
    d = D_MODEL
    n_even = w_in_even.shape[0]
    n_odd = w_in_odd.shape[0]
    bp, t_p, _ = x_prompt.shape
    m = mem_prompt.shape[1]

    def rowvec(a):
        return a.reshape(a.shape[0], 1, a.shape[-1])

    p = {
        "g_mix": rowvec(g_mix), "g_x": rowvec(g_x), "g_ffn": rowvec(g_ffn),
        "w_in_even": jnp.pad(w_in_even, ((0, 0), (0, 0), (0, EVEN_IN_PAD - EVEN_IN))).astype(BF16),
        "a_conv_w": a_conv_w,
        "b_q_norm": rowvec(jnp.tile(b_q_norm, (1, B_HEADS))),
        "b_k_norm": rowvec(jnp.tile(b_k_norm, (1, B_KV_HEADS))),
        "w_out_even": w_out_even.astype(BF16),
        "w_in_odd": w_in_odd.astype(BF16), "c_conv_w": c_conv_w, "c_conv_b": rowvec(c_conv_b),
        "c_w_a": c_w_a.astype(BF16), "c_b_a": rowvec(c_b_a), "c_w_i": c_w_i.astype(BF16), "c_b_i": rowvec(c_b_i),
        "c_lambda": rowvec(c_lambda), "w_out_odd": w_out_odd.astype(BF16),
        "w_xq": w_xq.astype(BF16), "x_q_norm": rowvec(jnp.tile(x_q_norm, (1, X_HEADS))),
        "w_xo": w_xo.astype(BF16), "w_up": w_up.astype(BF16), "f_conv_w": f_conv_w,
        "f_conv_b": rowvec(f_conv_b), "w_down": w_down.astype(BF16),
    }
    del n_even, n_odd
    bias = _bias_tiles(rel_table, KEY_BLOCK)

    mk, mv = _mem_kv(mem_prompt.reshape(bp * m, d), g_mem, w_xk, x_k_norm, w_xv)
    p_mem_k = mk.reshape(DEPTH, bp, m, X_HEADS, X_HEAD_DIM)
    p_mem_v = mv.reshape(DEPTH, bp, m, X_HEADS, X_HEAD_DIM)
    y_prompt, new_p = _trunk(x_prompt, None, mk.reshape(DEPTH, bp, m, d).astype(BF16),
                             mv.reshape(DEPTH, bp, m, d).astype(BF16), p, bias,
                             tt=64, tq_dsa=KEY_BLOCK, tq_x=512)

    bs, t_s, _ = x_sample.shape
    st_s = {"b_k": cache_b_k, "b_v": cache_b_v, "b_kidx": cache_b_kidx, "a_conv": state_a_conv,
            "c_conv": state_c_conv, "c_h": state_c_h, "f_conv": state_ffn_conv}
    ms = cache_mem_k.shape[2]
    y_sample, new_s = _trunk(x_sample, st_s, cache_mem_k.reshape(DEPTH, bs, ms, d).astype(BF16),
                             cache_mem_v.reshape(DEPTH, bs, ms, d).astype(BF16), p, bias,
                             tt=t_s, tq_dsa=t_s, tq_x=t_s)
    return (y_prompt, y_sample,
            new_p["b_k"], new_p["b_v"], new_p["b_kidx"], new_p["a_conv"], new_p["c_conv"],
            new_p["c_h"], new_p["f_conv"], p_mem_k, p_mem_v,
            new_s["b_k"], new_s["b_v"], new_s["b_kidx"], new_s["a_conv"], new_s["c_conv"],
            new_s["c_h"], new_s["f_conv"])
```

```python
import functools
import math

import jax
import jax.numpy as jnp
from jax import lax
from jax.experimental import pallas as pl
from jax.experimental.pallas import tpu as pltpu

F32 = jnp.float32
BF16 = jnp.bfloat16
I32 = jnp.int32
I16 = jnp.int16

D_MODEL = 1024
DEPTH = 2
CHUNK = 64
EPS = 1e-6
NEG_INF = -1e30
LOG2E = math.log2(math.e)
A_WIDTH = 512
A_CONV = 3
B_HEADS = 8
B_KV_HEADS = 2
B_HEAD_DIM = 64
IDX_HEADS = 8
IDX_DIM = 32
TOPK_MAX = 256
REL_BUCKETS = 32
REL_MAX_DIST = 128
RNN_WIDTH = 1024
RNN_BLOCKS = 8
RNN_BLOCK = 128
C_CONV = 4
RG_C = 8.0
X_HEADS = 4
X_HEAD_DIM = 256
D_FF = 2816
F_CONV = 3
EVEN_IN = 2600

SUBLANES = 8
LANES = 128
MXU_DIM = 256
PACKED_ROWS = 16
VMEM_LIMIT = 56 * 1024 * 1024

EVEN_IN_PAD = 2688
KEY_BLOCK = 256
INT_MIN = -2147483648


def _params(sem, vmem=VMEM_LIMIT):
    return pltpu.CompilerParams(dimension_semantics=sem, vmem_limit_bytes=vmem)


def _const_spec(shape):
    nd = len(shape)
    return pl.BlockSpec(shape, lambda *_: (0,) * nd, pipeline_mode=pl.Buffered(1))


def _rms(x, g):
    ms = jnp.mean(x * x, axis=-1, keepdims=True)
    return x * lax.rsqrt(ms + EPS) * g


def _head_rms(x, hd, gain):
    m, c = x.shape
    s = x * x
    parts = []
    if hd >= LANES:
        for h in range(c // hd):
            ms = jnp.mean(s[:, h * hd:(h + 1) * hd], axis=-1, keepdims=True)
            parts.append(x[:, h * hd:(h + 1) * hd] * lax.rsqrt(ms + EPS))
    else:
        lane = lax.broadcasted_iota(I32, (m, LANES), 1)
        for j in range(c // LANES):
            sj = s[:, j * LANES:(j + 1) * LANES]
            inv = jnp.zeros((m, LANES), F32)
            for k in range(LANES // hd):
                msk = (lane >= k * hd) & (lane < (k + 1) * hd)
                ms = jnp.sum(jnp.where(msk, sj, 0.0), axis=-1, keepdims=True) * (1.0 / hd)
                inv = jnp.where(msk, lax.rsqrt(ms + EPS), inv)
            parts.append(x[:, j * LANES:(j + 1) * LANES] * inv)
    y = parts[0] if len(parts) == 1 else jnp.concatenate(parts, axis=-1)
    return y * gain


def _dot(a, b):
    return jnp.dot(a, b, preferred_element_type=F32)


def _dot_t(a, b):
    return lax.dot_general(a, b, (((1,), (1,)), ((), ())), preferred_element_type=F32)


def _conv_taps(buf, cur, w_ref, width, tm):
    y = cur * w_ref[width - 1:width, :]
    for i in range(width - 1):
        y = y + buf[i * SUBLANES:i * SUBLANES + tm, :] * w_ref[i:i + 1, :]
    return y


def _seq_to_tm(src_ref, scr):
    nb, tt, c = src_ref.shape
    for b in range(nb):
        for j in range(c // LANES):
            scr[j, pl.ds(b, tt, stride=nb), :] = src_ref[b, :, j * LANES:(j + 1) * LANES].astype(F32)
    return jnp.concatenate([scr[j] for j in range(c // LANES)], axis=-1)


def _tm_to_seq(val, scr, dst_refs):
    tm, c = val.shape
    tt = tm // SUBLANES
    for j in range(c // LANES):
        scr[j] = val[:, j * LANES:(j + 1) * LANES]
    for b in range(SUBLANES):
        j0 = 0
        for ref, ci in dst_refs:
            nj = ci // LANES
            parts = [scr[j0 + j, pl.ds(b, tt, stride=SUBLANES), :] for j in range(nj)]
            ref[b] = (parts[0] if nj == 1 else jnp.concatenate(parts, axis=-1)).astype(ref.dtype)
            j0 += nj


def _memkv_kernel(mem_ref, g_ref, wk_ref, kn_ref, wv_ref, k_ref, v_ref):
    hm = _rms(mem_ref[...], g_ref[0]).astype(BF16)
    k_ref[0] = _head_rms(_dot(hm, wk_ref[0]), X_HEAD_DIM, kn_ref[0])
    v_ref[0] = _dot(hm, wv_ref[0])


def _mem_kv(mem, g_mem, w_xk, x_k_norm, w_xv):
    rows = mem.shape[0]
    tm = min(512, rows)
    d = D_MODEL
    kn = jnp.tile(x_k_norm, (1, X_HEADS)).reshape(DEPTH, 1, d)
    out = jax.ShapeDtypeStruct((DEPTH, rows, d), F32)
    return pl.pallas_call(
        _memkv_kernel,
        grid=(DEPTH, rows // tm),
        in_specs=[
            pl.BlockSpec((tm, d), lambda l, i: (i, 0)),
            pl.BlockSpec((1, 1, d), lambda l, i: (l, 0, 0)),
            pl.BlockSpec((1, d, d), lambda l, i: (l, 0, 0)),
            pl.BlockSpec((1, 1, d), lambda l, i: (l, 0, 0)),
            pl.BlockSpec((1, d, d), lambda l, i: (l, 0, 0)),
        ],
        out_specs=[pl.BlockSpec((1, tm, d), lambda l, i: (l, i, 0))] * 2,
        out_shape=[out, out],
        compiler_params=_params(("arbitrary", "arbitrary")),
        name="mem_kv",
    )(mem, g_mem.reshape(DEPTH, 1, d), w_xk.astype(BF16), kn, w_xv.astype(BF16))


def _even_in_kernel(x_ref, g_ref, w_ref, cw_ref, qn_ref, kn_ref, hist_ref,
                    xt_ref, ya_ref, q_ref, iq_ref, ikw_ref, k_ref, v_ref, uh_ref, ubuf, xs, cs):
    tm = xt_ref.shape[0]
    nh = (A_CONV - 1) * SUBLANES

    @pl.when(pl.program_id(0) == 0)
    def _():
        ubuf[0:nh, :] = hist_ref[...]

    x = _seq_to_tm(x_ref, xs)
    xt_ref[...] = x
    xn = _rms(x, g_ref[...]).astype(BF16)

    def proj(a, b):
        return _dot(xn, w_ref[:, a:b])

    zc = proj(A_WIDTH, 3 * A_WIDTH)
    u = zc[:, :A_WIDTH] * zc[:, A_WIDTH:]
    ubuf[nh:nh + tm, :] = u
    conv = _conv_taps(ubuf, u, cw_ref, A_CONV, tm)
    ya_ref[...] = (proj(0, A_WIDTH) * conv).astype(BF16)
    tail = ubuf[tm:tm + nh, :]
    ubuf[0:nh, :] = tail
    uh_ref[...] = tail

    o = 3 * A_WIDTH
    nq = B_HEADS * B_HEAD_DIM
    nkv = B_KV_HEADS * B_HEAD_DIM
    ni = IDX_HEADS * IDX_DIM
    q = _head_rms(proj(o, o + nq), B_HEAD_DIM, qn_ref[...]) * (B_HEAD_DIM ** -0.5 * LOG2E)
    zkv = proj(o + nq, o + nq + 2 * nkv)
    k = _head_rms(zkv[:, :nkv], B_HEAD_DIM, kn_ref[...])
    zi = proj(o + nq + 2 * nkv, EVEN_IN_PAD)
    seq = jnp.concatenate([q, zi, k, zkv[:, nkv:]], axis=-1)
    _tm_to_seq(seq, cs, [(q_ref, nq), (iq_ref, ni), (ikw_ref, LANES), (k_ref, nkv), (v_ref, nkv)])


def _even_in(x, g, w_pad, conv_w, qn, kn, hist, tt):
    nb, t_len, d = x.shape
    tm = tt * nb
    rows = t_len * nb
    nh = (A_CONV - 1) * SUBLANES
    nq = B_HEADS * B_HEAD_DIM
    nkv = B_KV_HEADS * B_HEAD_DIM
    ni = IDX_HEADS * IDX_DIM
    nseq = nq + ni + LANES + 2 * nkv

    def row(c):
        return pl.BlockSpec((tm, c), lambda i: (i, 0))

    def seq(c):
        return pl.BlockSpec((nb, tt, c), lambda i: (0, i, 0))

    def seq_shape(c, dt):
        return jax.ShapeDtypeStruct((nb, t_len, c), dt)

    return pl.pallas_call(
        _even_in_kernel,
        grid=(t_len // tt,),
        in_specs=[seq(d), _const_spec((1, d)), _const_spec((d, EVEN_IN_PAD)), _const_spec((A_CONV, A_WIDTH)),
                  _const_spec((1, nq)), _const_spec((1, nkv)), _const_spec((nh, A_WIDTH))],
        out_specs=[row(d), row(A_WIDTH), seq(nq), seq(ni), seq(LANES), seq(nkv), seq(nkv),
                   pl.BlockSpec((nh, A_WIDTH), lambda i: (0, 0))],
        out_shape=[jax.ShapeDtypeStruct((rows, d), F32), jax.ShapeDtypeStruct((rows, A_WIDTH), BF16),
                   seq_shape(nq, BF16), seq_shape(ni, BF16), seq_shape(LANES, F32), seq_shape(nkv, F32),
                   seq_shape(nkv, F32), jax.ShapeDtypeStruct((nh, A_WIDTH), F32)],
        scratch_shapes=[pltpu.VMEM((tm + nh, A_WIDTH), F32), pltpu.VMEM((d // LANES, tm, LANES), F32),
                        pltpu.VMEM((nseq // LANES, tm, LANES), F32)],
        compiler_params=_params(("arbitrary",)),
        name="even_in",
    )(x, g, w_pad, conv_w, qn, kn, hist)


def _rel_bucket(rel):
    half = REL_BUCKETS // 2
    max_exact = half // 2
    n = -rel
    ret = jnp.where(n < 0, half, 0)
    n = jnp.abs(n)
    nf = jnp.maximum(n, 1).astype(F32)
    large = max_exact + (jnp.log(nf / max_exact) / math.log(REL_MAX_DIST / max_exact)
                         * (half - max_exact)).astype(I32)
    large = jnp.minimum(large, half - 1)
    return ret + jnp.where(n < max_exact, n, large)


def _bias_kernel(tab_ref, near_ref, far_ref):
    tk = near_ref.shape[-1]
    r = lax.broadcasted_iota(I32, (tk, tk), 0)
    c = lax.broadcasted_iota(I32, (tk, tk), 1)

    def lookup(bucket, h):
        def body(j, acc):
            return jnp.where(bucket == j, tab_ref[j, h], acc)
        return lax.fori_loop(0, REL_BUCKETS, body, jnp.zeros(bucket.shape, F32))

    for blk in range(2):
        bucket = _rel_bucket(r - c - blk * tk)
        for h in range(B_HEADS):
            near_ref[blk, h] = lookup(bucket, h) * LOG2E
    bucket = _rel_bucket(-REL_MAX_DIST - c[0:SUBLANES, :])
    for h in range(B_HEADS):
        far_ref[h] = lookup(bucket, h) * LOG2E


def _bias_tiles(rel_table, tk):
    return pl.pallas_call(
        _bias_kernel,
        in_specs=[pl.BlockSpec(memory_space=pltpu.SMEM)],
        out_shape=[jax.ShapeDtypeStruct((2, B_HEADS, tk, tk), F32),
                   jax.ShapeDtypeStruct((B_HEADS, SUBLANES, tk), F32)],
        name="rel_bias",
    )(rel_table)


def _dsa_kernel(q_ref, iq_ref, ikw_ref, k_ref, vt_ref, ik_ref, near_ref, far_ref, o_ref,
                skey, skh, skl, madd_s, s_s, p_s, qh_s, iqh_s, pad_s, w_s, m_s, acc_s,
                *, tq, qw, tk, past, length, ntop, idx_bits):
    i = pl.program_id(1)
    q0 = past + i * tq
    nkb = (q0 + tq + tk - 1) // tk
    last = nkb - 1
    ni = IDX_HEADS * IDX_DIM
    groups = B_HEADS // B_KV_HEADS
    lane = lax.broadcasted_iota(I32, (1, qw), 1)
    sub8 = lax.broadcasted_iota(I32, (SUBLANES, qw), 0)
    qchunk = jnp.right_shift(q0 + lane, 6)

    if tq != qw:
        qh_s[...] = jnp.zeros(qh_s.shape, BF16)
        iqh_s[...] = jnp.zeros(iqh_s.shape, BF16)
        pad_s[...] = jnp.zeros(pad_s.shape, F32)
    for h in range(B_HEADS):
        qh_s[h, 0:tq, :] = q_ref[:, B_HEAD_DIM * h:B_HEAD_DIM * (h + 1)]
    for h in range(IDX_HEADS):
        iqh_s[h, 0:tq, :] = iq_ref[:, IDX_DIM * h:IDX_DIM * (h + 1)]
    pad_s[0:tq, :] = ikw_ref[...]
    w_s[...] = pad_s[...].T[IDX_DIM:IDX_DIM + IDX_HEADS, :] * (ni ** -0.5)

    slab = tk

    def scores(kb, masked):
        for sl in range(tk // slab):
            ks = pl.multiple_of(kb * tk + sl * slab, slab)
            ikb = ik_ref[0, pl.ds(ks, slab), :]
            acc = jnp.zeros((slab, qw), F32)
            for h in range(IDX_HEADS):
                acc = acc + w_s[h:h + 1, :] * jnp.maximum(_dot_t(ikb, iqh_s[h]), 0.0)
            if masked:
                kpos = kb * tk + sl * slab + lax.broadcasted_iota(I32, (slab, qw), 0)
                vis = (jnp.right_shift(kpos, 6) <= qchunk) & (kpos < length)
                acc = jnp.where(vis, acc, -jnp.inf)
            bits = lax.bitcast_convert_type(acc, I32)
            key = jnp.where(bits < 0, bits ^ jnp.int32(0x7FFFFFFF), bits)
            rows = slice(sl * slab, (sl + 1) * slab)
            skey[kb, rows, :] = key
            skh[kb, rows, :] = jnp.right_shift(key, 16).astype(I16)
            skl[kb, rows, :] = ((key & 0xFFFF) - 32768).astype(I16)

    def score_body(kb, carry):
        scores(kb, False)
        return carry

    lax.fori_loop(0, last, score_body, 0)
    scores(last, True)

    n_acc = 4
    prow = PACKED_ROWS

    def count16(ref, pred):
        def body(kb, accs):
            accs = list(accs)
            for g in range(tk // prow):
                blk = ref[kb, g * prow:(g + 1) * prow, :]
                accs[g % n_acc] = accs[g % n_acc] + jnp.where(pred(blk), jnp.int16(1), jnp.int16(0))
            return tuple(accs)
        accs = lax.fori_loop(0, nkb, body, tuple(jnp.zeros((prow, qw), I16) for _ in range(n_acc)))
        tot = (accs[0].astype(I32) + accs[1].astype(I32)) + (accs[2].astype(I32) + accs[3].astype(I32))
        return jnp.sum(tot, axis=0, keepdims=True)

    def rep16(v):
        return jnp.broadcast_to(v, (prow, qw)).astype(I16)

    def kth16(ref, kth):
        def bit_body(it, prefix):
            cand_u = prefix | jnp.left_shift(jnp.int32(1), 15 - it)
            cand = rep16(cand_u - 32768)
            cnt = count16(ref, lambda blk: blk >= cand)
            return jnp.where(cnt >= kth, cand_u, prefix)
        return lax.fori_loop(0, 16, bit_body, jnp.zeros((1, qw), I32)) - 32768

    def count(pred):
        def body(kb, accs):
            accs = list(accs)
            for g in range(tk // SUBLANES):
                blk = skey[kb, g * SUBLANES:(g + 1) * SUBLANES, :]
                accs[g % n_acc] = accs[g % n_acc] + jnp.where(pred(kb, g, blk), 1, 0)
            return tuple(accs)
        accs = lax.fori_loop(0, nkb, body, tuple(jnp.zeros((SUBLANES, qw), I32) for _ in range(n_acc)))
        tot = (accs[0] + accs[1]) + (accs[2] + accs[3])
        return jnp.sum(tot, axis=0, keepdims=True)

    def rep8(v):
        return jnp.broadcast_to(v, (SUBLANES, qw))

    p_hi = kth16(skh, ntop)
    hi16 = rep16(p_hi)
    above = count16(skh, lambda blk: blk > hi16)

    def low_body(kb, carry):
        for g in range(tk // prow):
            rows = slice(g * prow, (g + 1) * prow)
            skl[kb, rows, :] = jnp.where(skh[kb, rows, :] == hi16, skl[kb, rows, :], jnp.int16(-32768))
        return carry

    lax.fori_loop(0, nkb, low_body, 0)
    p_lo = kth16(skl, ntop - above)
    tau = p_hi * 65536 + (p_lo + 32768)
    tau8 = rep8(tau)

    cnt_gt = count(lambda kb, g, blk: blk > tau8)
    cnt_ge = count(lambda kb, g, blk: blk >= tau8)
    need = ntop - cnt_gt
    key_ninf = jnp.int32(0x7FFFFFFF) ^ jnp.int32(-8388608)
    finite = tau != key_ninf
    tie_rows = ((cnt_ge - cnt_gt) > need) & finite & (lane < tq)

    @pl.when(jnp.max(jnp.where(tie_rows, 1, 0)) > 0)
    def _():
        need8 = rep8(need)

        def xbit(it, xlim):
            cand8 = xlim | jnp.left_shift(jnp.int32(1), idx_bits - 1 - it)
            cnt = count(lambda kb, g, blk: (blk == tau8) & ((kb * tk + g * SUBLANES + sub8) < cand8))
            return jnp.where(rep8(cnt) <= need8, cand8, xlim)
        xlim8 = lax.fori_loop(0, idx_bits, xbit, jnp.zeros((SUBLANES, qw), I32))

        def demote(kb, carry):
            for g in range(tk // SUBLANES):
                rows = slice(g * SUBLANES, (g + 1) * SUBLANES)
                key = skey[kb, rows, :]
                late = (key == tau8) & ((kb * tk + g * SUBLANES + sub8) >= xlim8)
                skey[kb, rows, :] = jnp.where(late, key - 1, key)
            return carry

        lax.fori_loop(0, nkb, demote, 0)

    m_s[...] = jnp.full(m_s.shape, NEG_INF, F32)
    acc_s[...] = jnp.zeros(acc_s.shape, F32)
    vrows = vt_ref.shape[3]

    tau_ge8 = rep8(jnp.where(finite, tau, tau + 1))

    def make_mask(kb):
        for g in range(tk // SUBLANES):
            rows = slice(g * SUBLANES, (g + 1) * SUBLANES)
            madd_s[rows, :] = jnp.where(skey[kb, rows, :] >= tau_ge8, 0.0, NEG_INF)

    def attend(kb, near_blk):
        ks = pl.multiple_of(kb * tk, tk)
        make_mask(kb)
        for h in range(B_HEADS):
            kn = k_ref[0, h // groups, pl.ds(ks, tk), :]
            if near_blk is None:
                s_s[h] = (_dot_t(kn, qh_s[h]) + madd_s[...]).astype(BF16)
            else:
                s_s[h] = (_dot_t(kn, qh_s[h]) + (madd_s[...] + near_ref[near_blk, h])).astype(BF16)
        shifts, alphas = [], []
        for h in range(B_HEADS):
            mx = [s_s[h, j * prow:(j + 1) * prow, :] for j in range(n_acc)]
            for r in range(n_acc, tk // prow):
                mx[r % n_acc] = jnp.maximum(mx[r % n_acc], s_s[h, r * prow:(r + 1) * prow, :])
            mx = jnp.maximum(jnp.maximum(mx[0], mx[1]), jnp.maximum(mx[2], mx[3]))
            m_cur = jnp.max(mx.astype(F32), axis=0, keepdims=True)
            c_h = far_ref[h, 0:1, 0:qw] if near_blk is None else jnp.zeros((1, qw), F32)
            m_prev = m_s[h:h + 1, :]
            shift = (jnp.maximum(m_prev, m_cur + c_h) - c_h).astype(BF16)
            m_new = shift.astype(F32) + c_h
            m_s[h:h + 1, :] = m_new
            alphas.append(jnp.exp2(m_prev - m_new))
            shifts.append(shift)
        for h in range(B_HEADS):
            shift = jnp.broadcast_to(shifts[h], (prow, qw))
            for r in range(tk // prow):
                rows = slice(r * prow, (r + 1) * prow)
                p_s[h, rows, :] = jnp.exp2(s_s[h, rows, :] - shift)
        for h in range(B_HEADS):
            hs = slice(h * vrows, (h + 1) * vrows)
            acc_s[hs, :] = acc_s[hs, :] * alphas[h] + _dot(vt_ref[0, h // groups, kb], p_s[h])

    def far_body(kb, carry):
        attend(kb, None)
        return carry

    lax.fori_loop(0, nkb - 2, far_body, 0)

    @pl.when(nkb >= 2)
    def _():
        attend(nkb - 2, 1)

    attend(last, 0)

    outs = []
    for h in range(B_HEADS):
        r0 = h * vrows
        outs.append(acc_s[r0:r0 + B_HEAD_DIM, :] / acc_s[r0 + B_HEAD_DIM:r0 + B_HEAD_DIM + 1, :])
    o = jnp.concatenate(outs, axis=0).T
    o_ref[...] = o[0:tq, :].astype(BF16)


def _dsa(q, iq, ikw, k_att, vt_att, ik_att, near, far, tq, past, length, ntop):
    tk = KEY_BLOCK
    nb, t_len, _ = q.shape
    lp = k_att.shape[2]
    vrows = vt_att.shape[3]
    nq = B_HEADS * B_HEAD_DIM
    ni = IDX_HEADS * IDX_DIM
    qw = max(tq, LANES)
    assert past % tk == 0 and (tq == tk or t_len == tq) and tq <= tk and lp % tk == 0
    idx_bits = lp.bit_length()

    def qblk(c):
        return pl.BlockSpec((None, tq, c), lambda b, i: (b, i, 0))

    kern = functools.partial(_dsa_kernel, tq=tq, qw=qw, tk=tk, past=past, length=length, ntop=ntop,
                             idx_bits=idx_bits)
    return pl.pallas_call(
        kern,
        grid=(nb, t_len // tq),
        in_specs=[qblk(nq), qblk(ni), qblk(LANES),
                  pl.BlockSpec((1, B_KV_HEADS, lp, B_HEAD_DIM), lambda b, i: (b, 0, 0, 0)),
                  pl.BlockSpec((1, B_KV_HEADS, lp // tk, vrows, tk), lambda b, i: (b, 0, 0, 0, 0)),
                  pl.BlockSpec((1, lp, IDX_DIM), lambda b, i: (b, 0, 0)),
                  pl.BlockSpec((2, B_HEADS, tk, qw), lambda b, i: (0, 0, 0, 0)),
                  pl.BlockSpec((B_HEADS, SUBLANES, tk), lambda b, i: (0, 0, 0))],
        out_specs=qblk(nq),
        out_shape=jax.ShapeDtypeStruct((nb, t_len, nq), BF16),
        scratch_shapes=[pltpu.VMEM((lp // tk, tk, qw), I32),
                        pltpu.VMEM((lp // tk, tk, qw), I16),
                        pltpu.VMEM((lp // tk, tk, qw), I16),
                        pltpu.VMEM((tk, qw), F32),
                        pltpu.VMEM((B_HEADS, tk, qw), BF16),
                        pltpu.VMEM((B_HEADS, tk, qw), BF16),
                        pltpu.VMEM((B_HEADS, qw, B_HEAD_DIM), BF16),
                        pltpu.VMEM((IDX_HEADS, qw, IDX_DIM), BF16),
                        pltpu.VMEM((qw, LANES), F32),
                        pltpu.VMEM((IDX_HEADS, qw), F32),
                        pltpu.VMEM((B_HEADS, qw), F32),
                        pltpu.VMEM((B_HEADS * vrows, qw), F32)],
        compiler_params=_params(("arbitrary", "arbitrary")),
        name="dsa",
    )(q, iq, ikw, k_att, vt_att, ik_att, near, far)


def _xq_tail(x1, gx_ref, wxq_ref, qnx_ref, cs, qx_ref):
    xn = _rms(x1, gx_ref[...]).astype(BF16)
    qx = _head_rms(_dot(xn, wxq_ref[...]), X_HEAD_DIM, qnx_ref[...])
    _tm_to_seq(qx * (X_HEAD_DIM ** -0.5), cs, [(qx_ref, D_MODEL)])


def _even_out_kernel(x_ref, ya_ref, yb_ref, wo_ref, gx_ref, wxq_ref, qnx_ref, x1_ref, qx_ref, ys, cs):
    yb = _seq_to_tm(yb_ref, ys).astype(BF16)
    x1 = x_ref[...] + _dot(ya_ref[...], wo_ref[0:A_WIDTH, :]) + _dot(yb, wo_ref[A_WIDTH:, :])
    x1_ref[...] = x1
    _xq_tail(x1, gx_ref, wxq_ref, qnx_ref, cs, qx_ref)


def _even_out(x, ya, yb, w_out, gx, w_xq, qnx, tt):
    nb, t_len, nyb = yb.shape
    tm = tt * nb
    rows = x.shape[0]
    d = D_MODEL

    def row(c):
        return pl.BlockSpec((tm, c), lambda i: (i, 0))

    def seq(c):
        return pl.BlockSpec((nb, tt, c), lambda i: (0, i, 0))

    return pl.pallas_call(
        _even_out_kernel,
        grid=(rows // tm,),
        in_specs=[row(d), row(A_WIDTH), seq(nyb), _const_spec((d, d)),
                  _const_spec((1, d)), _const_spec((d, d)), _const_spec((1, d))],
        out_specs=[row(d), seq(d)],
        out_shape=[jax.ShapeDtypeStruct((rows, d), F32), jax.ShapeDtypeStruct((nb, t_len, d), BF16)],
        scratch_shapes=[pltpu.VMEM((nyb // LANES, tm, LANES), F32), pltpu.VMEM((d // LANES, tm, LANES), F32)],
        compiler_params=_params(("arbitrary",)),
        name="even_out",
    )(x, ya, yb, w_out, gx, w_xq, qnx)


def _xattn_kernel(q_ref, mk_ref, mv_ref, o_ref):
    for h in range(X_HEADS):
        sl = slice(h * X_HEAD_DIM, (h + 1) * X_HEAD_DIM)
        s = _dot_t(q_ref[:, sl], mk_ref[0, :, sl])
        p = jnp.exp(s - jnp.max(s, axis=1, keepdims=True))
        o = _dot(p.astype(BF16), mv_ref[0, :, sl]) / jnp.sum(p, axis=1, keepdims=True)
        o_ref[:, sl] = o.astype(BF16)


def _xattn(qx, mk, mv, tq):
    nb, t_len, d = qx.shape
    m = mk.shape[1]
    return pl.pallas_call(
        _xattn_kernel,
        grid=(nb, t_len // tq),
        in_specs=[pl.BlockSpec((None, tq, d), lambda b, i: (b, i, 0)),
                  pl.BlockSpec((1, m, d), lambda b, i: (b, 0, 0)),
                  pl.BlockSpec((1, m, d), lambda b, i: (b, 0, 0))],
        out_specs=pl.BlockSpec((None, tq, d), lambda b, i: (b, i, 0)),
        out_shape=jax.ShapeDtypeStruct((nb, t_len, d), BF16),
        compiler_params=_params(("arbitrary", "arbitrary")),
        name="mem_attn",
    )(qx, mk, mv)


def _ffn_kernel(x_ref, o_ref, wxo_ref, g_ref, wup_ref, cw_ref, cb_ref, wdn_ref, hist_ref,
                y_ref, fh_ref, gbuf, cs, *, chunk, seq_out):
    tm = x_ref.shape[0]
    nh = (F_CONV - 1) * SUBLANES

    @pl.when(pl.program_id(0) == 0)
    def _():
        gbuf[0:nh, :] = hist_ref[...]

    x2 = x_ref[...] + _dot(_seq_to_tm(o_ref, cs).astype(BF16), wxo_ref[...])
    xn = _rms(x2, g_ref[...]).astype(BF16)
    acc = jnp.zeros((tm, D_MODEL), F32)
    for c0 in range(0, D_FF, chunk):
        c1 = min(c0 + chunk, D_FF)
        val = _dot(xn, wup_ref[:, c0:c1])
        gate = _dot(xn, wup_ref[:, D_FF + c0:D_FF + c1])
        gbuf[nh:nh + tm, c0:c1] = gate
        conv = gate * cw_ref[F_CONV - 1:F_CONV, c0:c1]
        for i in range(F_CONV - 1):
            conv = conv + gbuf[i * SUBLANES:i * SUBLANES + tm, c0:c1] * cw_ref[i:i + 1, c0:c1]
        conv = conv + cb_ref[:, c0:c1]
        act = (jax.nn.gelu(conv) * val).astype(BF16)
        acc = acc + _dot(act, wdn_ref[c0:c1, :])
    tail = gbuf[tm:tm + nh, :]
    gbuf[0:nh, :] = tail
    fh_ref[...] = tail
    if seq_out:
        _tm_to_seq(x2 + acc, cs, [(y_ref, D_MODEL)])
    else:
        y_ref[...] = x2 + acc


def _ffn(x, o, w_xo, g, w_up, conv_w, conv_b, w_down, hist, tt, seq_out):
    nb, t_len, d = o.shape
    tm = tt * nb
    rows = x.shape[0]
    nh = (F_CONV - 1) * SUBLANES

    def row(c):
        return pl.BlockSpec((tm, c), lambda i: (i, 0))

    def seq(c):
        return pl.BlockSpec((nb, tt, c), lambda i: (0, i, 0))

    y_shape = jax.ShapeDtypeStruct((nb, t_len, d) if seq_out else (rows, d), F32)
    return pl.pallas_call(
        functools.partial(_ffn_kernel, chunk=MXU_DIM, seq_out=seq_out),
        grid=(rows // tm,),
        in_specs=[row(d), seq(d), _const_spec((d, d)), _const_spec((1, d)), _const_spec((d, 2 * D_FF)),
                  _const_spec((F_CONV, D_FF)), _const_spec((1, D_FF)), _const_spec((D_FF, d)),
                  _const_spec((nh, D_FF))],
        out_specs=[seq(d) if seq_out else row(d), pl.BlockSpec((nh, D_FF), lambda i: (0, 0))],
        out_shape=[y_shape, jax.ShapeDtypeStruct((nh, D_FF), F32)],
        scratch_shapes=[pltpu.VMEM((tm + nh, D_FF), F32), pltpu.VMEM((d // LANES, tm, LANES), F32)],
        compiler_params=_params(("arbitrary",)),
        name="ffn",
    )(x, o, w_xo, g, w_up, conv_w, conv_b, w_down, hist)


def _odd_kernel(x_ref, g_ref, win_ref, cw_ref, cb_ref, wa_ref, ba_ref, wi_ref, bi_ref, lam_ref,
                wo_ref, hist_ref, h0_ref, gx_ref, wxq_ref, qnx_ref,
                x1_ref, qx_ref, ch_ref, hl_ref, xbuf, a_s, b_s, h_s, cs, *, stream_start):
    tm = x_ref.shape[0]
    nh = (C_CONV - 1) * SUBLANES
    first = pl.program_id(0) == 0

    @pl.when(first)
    def _():
        xbuf[0:nh, :] = hist_ref[...]
        h_s[...] = h0_ref[...]

    x = x_ref[...]
    xn = _rms(x, g_ref[...]).astype(BF16)
    xr_in = _dot(xn, win_ref[:, RNN_WIDTH:])
    xbuf[nh:nh + tm, :] = xr_in
    xr = _conv_taps(xbuf, xr_in, cw_ref, C_CONV, tm) + cb_ref[...]
    tail = xbuf[tm:tm + nh, :]
    xbuf[0:nh, :] = tail
    ch_ref[...] = tail

    xrb = xr.astype(BF16)
    lam = -lam_ref[...]
    sp = jnp.maximum(lam, 0.0) + jnp.log1p(jnp.exp(-jnp.abs(lam)))
    rows = lax.broadcasted_iota(I32, (tm, RNN_BLOCK), 0)
    for n in range(RNN_BLOCKS):
        sl = slice(n * RNN_BLOCK, (n + 1) * RNN_BLOCK)
        r = jax.nn.sigmoid(_dot(xrb[:, sl], wa_ref[n]) + ba_ref[:, sl])
        ig = jax.nn.sigmoid(_dot(xrb[:, sl], wi_ref[n]) + bi_ref[:, sl])
        log_a = -RG_C * r * sp[:, sl]
        a = jnp.exp(log_a)
        mult = jnp.sqrt(jnp.tanh(-log_a) * (1.0 + a * a))
        if stream_start:
            mult = jnp.where(first & (rows < SUBLANES), 1.0, mult)
        a_s[:, sl] = a
        b_s[:, sl] = mult * ig * xr[:, sl]

    def step(t, h):
        r0 = pl.multiple_of(t * SUBLANES, SUBLANES)
        h = a_s[pl.ds(r0, SUBLANES), :] * h + b_s[pl.ds(r0, SUBLANES), :]
        b_s[pl.ds(r0, SUBLANES), :] = h
        return h

    h = lax.fori_loop(0, tm // SUBLANES, step, h_s[...], unroll=8)
    h_s[...] = h
    hl_ref[...] = h

    gate = _dot(xn, win_ref[:, :RNN_WIDTH])
    act = (jax.nn.gelu(gate) * b_s[...]).astype(BF16)
    x1 = x + _dot(act, wo_ref[...])
    x1_ref[...] = x1
    _xq_tail(x1, gx_ref, wxq_ref, qnx_ref, cs, qx_ref)


def _odd(x, g, w_in, conv_w, conv_b, w_a, b_a, w_i, b_i, lam, w_out, hist, h0, gx, w_xq, qnx, tt,
         stream_start):
    nb = SUBLANES
    tm = tt * nb
    rows = x.shape[0]
    t_len = rows // nb
    d = D_MODEL
    r = RNN_WIDTH
    nh = (C_CONV - 1) * SUBLANES

    def row(c):
        return pl.BlockSpec((tm, c), lambda i: (i, 0))

    blk = (RNN_BLOCKS, RNN_BLOCK, RNN_BLOCK)
    return pl.pallas_call(
        functools.partial(_odd_kernel, stream_start=stream_start),
        grid=(rows // tm,),
        in_specs=[row(d), _const_spec((1, d)), _const_spec((d, 2 * r)), _const_spec((C_CONV, r)),
                  _const_spec((1, r)), _const_spec(blk), _const_spec((1, r)), _const_spec(blk),
                  _const_spec((1, r)), _const_spec((1, r)), _const_spec((r, d)), _const_spec((nh, r)),
                  _const_spec((SUBLANES, r)), _const_spec((1, d)), _const_spec((d, d)), _const_spec((1, d))],
        out_specs=[row(d), pl.BlockSpec((nb, tt, d), lambda i: (0, i, 0)), pl.BlockSpec((nh, r), lambda i: (0, 0)),
                   pl.BlockSpec((SUBLANES, r), lambda i: (0, 0))],
        out_shape=[jax.ShapeDtypeStruct((rows, d), F32), jax.ShapeDtypeStruct((nb, t_len, d), BF16),
                   jax.ShapeDtypeStruct((nh, r), F32), jax.ShapeDtypeStruct((SUBLANES, r), F32)],
        scratch_shapes=[pltpu.VMEM((tm + nh, r), F32), pltpu.VMEM((tm, r), F32), pltpu.VMEM((tm, r), F32),
                        pltpu.VMEM((SUBLANES, r), F32), pltpu.VMEM((d // LANES, tm, LANES), F32)],
        compiler_params=_params(("arbitrary",)),
        name="odd_mixer",
    )(x, g, w_in, conv_w, conv_b, w_a, b_a, w_i, b_i, lam, w_out, hist, h0, gx, w_xq, qnx)


def _to_tm(a):
    return jnp.transpose(a, (1, 0, 2)).reshape(a.shape[1] * a.shape[0], a.shape[2])


def _from_tm(a, w):
    return jnp.transpose(a.reshape(w, SUBLANES, a.shape[1]), (1, 0, 2))


def _trunk(x, st, mem_k, mem_v, p, bias, tt, tq_dsa, tq_x):
    nb, t_len, d = x.shape
    assert nb == SUBLANES
    past = 0 if st is None else st["b_k"].shape[2]
    length = past + t_len
    ntop = min(TOPK_MAX, length // 4)
    near, far = bias
    xt = None
    out = {}

    def hist(name, l, width, c):
        if st is None:
            return jnp.zeros(((width - 1) * nb, c), F32)
        return _to_tm(st[name][l])

    for l in range(DEPTH):
        if l % 2 == 0:
            e = l // 2
            assert l == 0, "the per-sequence input is converted by the first layer's kernel"
            xt, ya, q, iq, ikw, k, v, uh = _even_in(
                x, p["g_mix"][l], p["w_in_even"][e], p["a_conv_w"][e], p["b_q_norm"][e], p["b_k_norm"][e],
                hist("a_conv", e, A_CONV, A_WIDTH), tt)
            k_new = k.reshape(nb, t_len, B_KV_HEADS, B_HEAD_DIM)
            v_new = v.reshape(nb, t_len, B_KV_HEADS, B_HEAD_DIM)
            ik_new = ikw[:, :, :IDX_DIM]
            k_all, v_all, ik_all = k_new, v_new, ik_new
            if st is not None:
                k_all = jnp.concatenate([st["b_k"][e], k_new], axis=1)
                v_all = jnp.concatenate([st["b_v"][e], v_new], axis=1)
                ik_all = jnp.concatenate([st["b_kidx"][e], ik_new], axis=1)
            lp = -(-length // KEY_BLOCK) * KEY_BLOCK
            padl = lp - length
            k_att = jnp.pad(jnp.transpose(k_all, (0, 2, 1, 3)).astype(BF16), ((0, 0), (0, 0), (0, padl), (0, 0)))
            v_att = jnp.pad(jnp.transpose(v_all, (0, 2, 1, 3)).astype(BF16), ((0, 0), (0, 0), (0, padl), (0, 0)))
            vt_att = jnp.transpose(v_att.reshape(nb, B_KV_HEADS, lp // KEY_BLOCK, KEY_BLOCK, B_HEAD_DIM),
                                   (0, 1, 2, 4, 3))
            ones = jnp.ones(vt_att.shape[:3] + (1, KEY_BLOCK), BF16)
            zeros = jnp.zeros(vt_att.shape[:3] + (PACKED_ROWS - 1, KEY_BLOCK), BF16)
            vt_att = jnp.concatenate([vt_att, ones, zeros], axis=3)
            ik_att = jnp.pad(ik_all.astype(BF16), ((0, 0), (0, padl), (0, 0)))
            yb = _dsa(q, iq, ikw, k_att, vt_att, ik_att, near, far, tq_dsa, past, length, ntop)
            x1, qx = _even_out(xt, ya, yb, p["w_out_even"][e], p["g_x"][l], p["w_xq"][l], p["x_q_norm"][l], tt)
            out.setdefault("a_conv", []).append(_from_tm(uh, A_CONV - 1))
            out.setdefault("b_k", []).append(k_new)
            out.setdefault("b_v", []).append(v_new)
            out.setdefault("b_kidx", []).append(ik_new)
        else:
            o = l // 2
            h0 = jnp.zeros((nb, RNN_WIDTH), F32) if st is None else st["c_h"][o]
            x1, qx, ch, hl = _odd(
                xt, p["g_mix"][l], p["w_in_odd"][o], p["c_conv_w"][o], p["c_conv_b"][o], p["c_w_a"][o],
                p["c_b_a"][o], p["c_w_i"][o], p["c_b_i"][o], p["c_lambda"][o], p["w_out_odd"][o],
                hist("c_conv", o, C_CONV, RNN_WIDTH), h0, p["g_x"][l], p["w_xq"][l], p["x_q_norm"][l], tt,
                stream_start=(past == 0))
            out.setdefault("c_conv", []).append(_from_tm(ch, C_CONV - 1))
            out.setdefault("c_h", []).append(hl)
        xo = _xattn(qx, mem_k[l], mem_v[l], tq_x)
        xt, fh = _ffn(x1, xo, p["w_xo"][l], p["g_ffn"][l], p["w_up"][l], p["f_conv_w"][l], p["f_conv_b"][l],
                      p["w_down"][l], hist("f_conv", l, F_CONV, D_FF), tt, seq_out=(l == DEPTH - 1))
        out.setdefault("f_conv", []).append(_from_tm(fh, F_CONV - 1))
    return xt, {name: jnp.stack(v) for name, v in out.items()}


def kernel(x_prompt, x_sample, cache_b_k, cache_b_v, cache_b_kidx, state_a_conv, state_c_conv, state_c_h, state_ffn_conv, cache_mem_k, cache_mem_v, mem_prompt, rel_table, g_mix, w_in_even, a_conv_w, b_q_norm, b_k_norm, w_out_even, w_in_odd, c_conv_w, c_conv_b, c_w_a, c_b_a, c_w_i, c_b_i, c_lambda, w_out_odd, g_mem, g_x, w_xq, w_xk, w_xv, x_q_norm, x_k_norm, w_xo, g_ffn, w_up, f_conv_w, f_conv_b, w_down):
    d = D_MODEL
    n_even = w_in_even.shape[0]
    n_odd = w_in_odd.shape[0]
    bp, t_p, _ = x_prompt.shape
    m = mem_prompt.shape[1]

    def rowvec(a):
        return a.reshape(a.shape[0], 1, a.shape[-1])

    p = {
        "g_mix": rowvec(g_mix), "g_x": rowvec(g_x), "g_ffn": rowvec(g_ffn),
        "w_in_even": jnp.pad(w_in_even, ((0, 0), (0, 0), (0, EVEN_IN_PAD - EVEN_IN))).astype(BF16),
        "a_conv_w": a_conv_w,
        "b_q_norm": rowvec(jnp.tile(b_q_norm, (1, B_HEADS))),
        "b_k_norm": rowvec(jnp.tile(b_k_norm, (1, B_KV_HEADS))),
        "w_out_even": w_out_even.astype(BF16),
        "w_in_odd": w_in_odd.astype(BF16), "c_conv_w": c_conv_w, "c_conv_b": rowvec(c_conv_b),
        "c_w_a": c_w_a.astype(BF16), "c_b_a": rowvec(c_b_a), "c_w_i": c_w_i.astype(BF16), "c_b_i": rowvec(c_b_i),
        "c_lambda": rowvec(c_lambda), "w_out_odd": w_out_odd.astype(BF16),
        "w_xq": w_xq.astype(BF16), "x_q_norm": rowvec(jnp.tile(x_q_norm, (1, X_HEADS))),
        "w_xo": w_xo.astype(BF16), "w_up": w_up.astype(BF16), "f_conv_w": f_conv_w,
        "f_conv_b": rowvec(f_conv_b), "w_down": w_down.astype(BF16),
    }
    del n_even, n_odd
    bias = _bias_tiles(rel_table, KEY_BLOCK)

    mk, mv = _mem_kv(mem_prompt.reshape(bp * m, d), g_mem, w_xk, x_k_norm, w_xv)
    p_mem_k = mk.reshape(DEPTH, bp, m, X_HEADS, X_HEAD_DIM)
    p_mem_v = mv.reshape(DEPTH, bp, m, X_HEADS, X_HEAD_DIM)
    y_prompt, new_p = _trunk(x_prompt, None, mk.reshape(DEPTH, bp, m, d).astype(BF16),
                             mv.reshape(DEPTH, bp, m, d).astype(BF16), p, bias,
                             tt=64, tq_dsa=KEY_BLOCK, tq_x=512)

    bs, t_s, _ = x_sample.shape
    st_s = {"b_k": cache_b_k, "b_v": cache_b_v, "b_kidx": cache_b_kidx, "a_conv": state_a_conv,
            "c_conv": state_c_conv, "c_h": state_c_h, "f_conv": state_ffn_conv}
    ms = cache_mem_k.shape[2]
    y_sample, new_s = _trunk(x_sample, st_s, cache_mem_k.reshape(DEPTH, bs, ms, d).astype(BF16),
                             cache_mem_v.reshape(DEPTH, bs, ms, d).astype(BF16), p, bias,
                             tt=t_s, tq_dsa=t_s, tq_x=t_s)
    return (y_prompt, y_sample,
            new_p["b_k"], new_p["b_v"], new_p["b_kidx"], new_p["a_conv"], new_p["c_conv"],
            new_p["c_h"], new_p["f_conv"], p_mem_k, p_mem_v,
            new_s["b_k"], new_s["b_v"], new_s["b_kidx"], new_s["a_conv"], new_s["c_conv"],
            new_s["c_h"], new_s["f_conv"])
```

```python
import functools
import math

import jax
import jax.numpy as jnp
from jax import lax
from jax.experimental import pallas as pl
from jax.experimental.pallas import tpu as pltpu

F32 = jnp.float32
BF16 = jnp.bfloat16
I32 = jnp.int32
I16 = jnp.int16

D_MODEL = 1024
DEPTH = 2
CHUNK = 64
EPS = 1e-6
NEG_INF = -1e30
LOG2E = math.log2(math.e)
A_WIDTH = 512
A_CONV = 3
B_HEADS = 8
B_KV_HEADS = 2
B_HEAD_DIM = 64
IDX_HEADS = 8
IDX_DIM = 32
TOPK_MAX = 256
REL_BUCKETS = 32
REL_MAX_DIST = 128
RNN_WIDTH = 1024
RNN_BLOCKS = 8
RNN_BLOCK = 128
C_CONV = 4
RG_C = 8.0
X_HEADS = 4
X_HEAD_DIM = 256
D_FF = 2816
F_CONV = 3
EVEN_IN = 2600

SUBLANES = 8
LANES = 128
MXU_DIM = 256
PACKED_ROWS = 16
VMEM_LIMIT = 56 * 1024 * 1024

EVEN_IN_PAD = 2688
KEY_BLOCK = 256
INT_MIN = -2147483648


def _params(sem, vmem=VMEM_LIMIT):
    return pltpu.CompilerParams(dimension_semantics=sem, vmem_limit_bytes=vmem)


def _const_spec(shape):
    nd = len(shape)
    return pl.BlockSpec(shape, lambda *_: (0,) * nd, pipeline_mode=pl.Buffered(1))


def _rms(x, g):
    ms = jnp.mean(x * x, axis=-1, keepdims=True)
    return x * lax.rsqrt(ms + EPS) * g


def _head_rms(x, hd, gain):
    m, c = x.shape
    s = x * x
    parts = []
    if hd >= LANES:
        for h in range(c // hd):
            ms = jnp.mean(s[:, h * hd:(h + 1) * hd], axis=-1, keepdims=True)
            parts.append(x[:, h * hd:(h + 1) * hd] * lax.rsqrt(ms + EPS))
    else:
        lane = lax.broadcasted_iota(I32, (m, LANES), 1)
        for j in range(c // LANES):
            sj = s[:, j * LANES:(j + 1) * LANES]
            inv = jnp.zeros((m, LANES), F32)
            for k in range(LANES // hd):
                msk = (lane >= k * hd) & (lane < (k + 1) * hd)
                ms = jnp.sum(jnp.where(msk, sj, 0.0), axis=-1, keepdims=True) * (1.0 / hd)
                inv = jnp.where(msk, lax.rsqrt(ms + EPS), inv)
            parts.append(x[:, j * LANES:(j + 1) * LANES] * inv)
    y = parts[0] if len(parts) == 1 else jnp.concatenate(parts, axis=-1)
    return y * gain


def _dot(a, b):
    return jnp.dot(a, b, preferred_element_type=F32)


def _dot_t(a, b):
    return lax.dot_general(a, b, (((1,), (1,)), ((), ())), preferred_element_type=F32)


def _conv_taps(buf, cur, w_ref, width, tm):
    y = cur * w_ref[width - 1:width, :]
    for i in range(width - 1):
        y = y + buf[i * SUBLANES:i * SUBLANES + tm, :] * w_ref[i:i + 1, :]
    return y


def _seq_to_tm(src_ref, scr):
    nb, tt, c = src_ref.shape
    for b in range(nb):
        for j in range(c // LANES):
            scr[j, pl.ds(b, tt, stride=nb), :] = src_ref[b, :, j * LANES:(j + 1) * LANES].astype(F32)
    return jnp.concatenate([scr[j] for j in range(c // LANES)], axis=-1)


def _tm_to_seq(val, scr, dst_refs):
    tm, c = val.shape
    tt = tm // SUBLANES
    for j in range(c // LANES):
        scr[j] = val[:, j * LANES:(j + 1) * LANES]
    for b in range(SUBLANES):
        j0 = 0
        for ref, ci in dst_refs:
            nj = ci // LANES
            parts = [scr[j0 + j, pl.ds(b, tt, stride=SUBLANES), :] for j in range(nj)]
            ref[b] = (parts[0] if nj == 1 else jnp.concatenate(parts, axis=-1)).astype(ref.dtype)
            j0 += nj


def _memkv_kernel(mem_ref, g_ref, wk_ref, kn_ref, wv_ref, k_ref, v_ref):
    hm = _rms(mem_ref[...], g_ref[0]).astype(BF16)
    k_ref[0] = _head_rms(_dot(hm, wk_ref[0]), X_HEAD_DIM, kn_ref[0])
    v_ref[0] = _dot(hm, wv_ref[0])


def _mem_kv(mem, g_mem, w_xk, x_k_norm, w_xv):
    rows = mem.shape[0]
    tm = min(512, rows)
    d = D_MODEL
    kn = jnp.tile(x_k_norm, (1, X_HEADS)).reshape(DEPTH, 1, d)
    out = jax.ShapeDtypeStruct((DEPTH, rows, d), F32)
    return pl.pallas_call(
        _memkv_kernel,
        grid=(DEPTH, rows // tm),
        in_specs=[
            pl.BlockSpec((tm, d), lambda l, i: (i, 0)),
            pl.BlockSpec((1, 1, d), lambda l, i: (l, 0, 0)),
            pl.BlockSpec((1, d, d), lambda l, i: (l, 0, 0)),
            pl.BlockSpec((1, 1, d), lambda l, i: (l, 0, 0)),
            pl.BlockSpec((1, d, d), lambda l, i: (l, 0, 0)),
        ],
        out_specs=[pl.BlockSpec((1, tm, d), lambda l, i: (l, i, 0))] * 2,
        out_shape=[out, out],
        compiler_params=_params(("arbitrary", "arbitrary")),
        name="mem_kv",
    )(mem, g_mem.reshape(DEPTH, 1, d), w_xk.astype(BF16), kn, w_xv.astype(BF16))


def _even_in_kernel(x_ref, g_ref, w_ref, cw_ref, qn_ref, kn_ref, hist_ref,
                    xt_ref, ya_ref, q_ref, iq_ref, ikw_ref, k_ref, v_ref, uh_ref, ubuf, xs, cs):
    tm = xt_ref.shape[0]
    nh = (A_CONV - 1) * SUBLANES

    @pl.when(pl.program_id(0) == 0)
    def _():
        ubuf[0:nh, :] = hist_ref[...]

    x = _seq_to_tm(x_ref, xs)
    xt_ref[...] = x
    xn = _rms(x, g_ref[...]).astype(BF16)

    def proj(a, b):
        return _dot(xn, w_ref[:, a:b])

    zc = proj(A_WIDTH, 3 * A_WIDTH)
    u = zc[:, :A_WIDTH] * zc[:, A_WIDTH:]
    ubuf[nh:nh + tm, :] = u
    conv = _conv_taps(ubuf, u, cw_ref, A_CONV, tm)
    ya_ref[...] = (proj(0, A_WIDTH) * conv).astype(BF16)
    tail = ubuf[tm:tm + nh, :]
    ubuf[0:nh, :] = tail
    uh_ref[...] = tail

    o = 3 * A_WIDTH
    nq = B_HEADS * B_HEAD_DIM
    nkv = B_KV_HEADS * B_HEAD_DIM
    ni = IDX_HEADS * IDX_DIM
    q = _head_rms(proj(o, o + nq), B_HEAD_DIM, qn_ref[...]) * (B_HEAD_DIM ** -0.5 * LOG2E)
    zkv = proj(o + nq, o + nq + 2 * nkv)
    k = _head_rms(zkv[:, :nkv], B_HEAD_DIM, kn_ref[...])
    zi = proj(o + nq + 2 * nkv, EVEN_IN_PAD)
    seq = jnp.concatenate([q, zi, k, zkv[:, nkv:]], axis=-1)
    _tm_to_seq(seq, cs, [(q_ref, nq), (iq_ref, ni), (ikw_ref, LANES), (k_ref, nkv), (v_ref, nkv)])


def _even_in(x, g, w_pad, conv_w, qn, kn, hist, tt):
    nb, t_len, d = x.shape
    tm = tt * nb
    rows = t_len * nb
    nh = (A_CONV - 1) * SUBLANES
    nq = B_HEADS * B_HEAD_DIM
    nkv = B_KV_HEADS * B_HEAD_DIM
    ni = IDX_HEADS * IDX_DIM
    nseq = nq + ni + LANES + 2 * nkv

    def row(c):
        return pl.BlockSpec((tm, c), lambda i: (i, 0))

    def seq(c):
        return pl.BlockSpec((nb, tt, c), lambda i: (0, i, 0))

    def seq_shape(c, dt):
        return jax.ShapeDtypeStruct((nb, t_len, c), dt)

    return pl.pallas_call(
        _even_in_kernel,
        grid=(t_len // tt,),
        in_specs=[seq(d), _const_spec((1, d)), _const_spec((d, EVEN_IN_PAD)), _const_spec((A_CONV, A_WIDTH)),
                  _const_spec((1, nq)), _const_spec((1, nkv)), _const_spec((nh, A_WIDTH))],
        out_specs=[row(d), row(A_WIDTH), seq(nq), seq(ni), seq(LANES), seq(nkv), seq(nkv),
                   pl.BlockSpec((nh, A_WIDTH), lambda i: (0, 0))],
        out_shape=[jax.ShapeDtypeStruct((rows, d), F32), jax.ShapeDtypeStruct((rows, A_WIDTH), BF16),
                   seq_shape(nq, BF16), seq_shape(ni, BF16), seq_shape(LANES, F32), seq_shape(nkv, F32),
                   seq_shape(nkv, F32), jax.ShapeDtypeStruct((nh, A_WIDTH), F32)],
        scratch_shapes=[pltpu.VMEM((tm + nh, A_WIDTH), F32), pltpu.VMEM((d // LANES, tm, LANES), F32),
                        pltpu.VMEM((nseq // LANES, tm, LANES), F32)],
        compiler_params=_params(("arbitrary",)),
        name="even_in",
    )(x, g, w_pad, conv_w, qn, kn, hist)


def _rel_bucket(rel):
    half = REL_BUCKETS // 2
    max_exact = half // 2
    n = -rel
    ret = jnp.where(n < 0, half, 0)
    n = jnp.abs(n)
    nf = jnp.maximum(n, 1).astype(F32)
    large = max_exact + (jnp.log(nf / max_exact) / math.log(REL_MAX_DIST / max_exact)
                         * (half - max_exact)).astype(I32)
    large = jnp.minimum(large, half - 1)
    return ret + jnp.where(n < max_exact, n, large)


def _bias_kernel(tab_ref, near_ref, far_ref):
    tk = near_ref.shape[-1]
    r = lax.broadcasted_iota(I32, (tk, tk), 0)
    c = lax.broadcasted_iota(I32, (tk, tk), 1)

    def lookup(bucket, h):
        def body(j, acc):
            return jnp.where(bucket == j, tab_ref[j, h], acc)
        return lax.fori_loop(0, REL_BUCKETS, body, jnp.zeros(bucket.shape, F32))

    for blk in range(2):
        bucket = _rel_bucket(r - c - blk * tk)
        for h in range(B_HEADS):
            near_ref[blk, h] = lookup(bucket, h) * LOG2E
    bucket = _rel_bucket(-REL_MAX_DIST - c[0:SUBLANES, :])
    for h in range(B_HEADS):
        far_ref[h] = lookup(bucket, h) * LOG2E


def _bias_tiles(rel_table, tk):
    return pl.pallas_call(
        _bias_kernel,
        in_specs=[pl.BlockSpec(memory_space=pltpu.SMEM)],
        out_shape=[jax.ShapeDtypeStruct((2, B_HEADS, tk, tk), F32),
                   jax.ShapeDtypeStruct((B_HEADS, SUBLANES, tk), F32)],
        name="rel_bias",
    )(rel_table)


def _dsa_kernel(q_ref, iq_ref, ikw_ref, k_ref, vt_ref, ik_ref, near_ref, far_ref, o_ref,
                skey, skh, skl, madd_s, s_s, p_s, qh_s, iqh_s, pad_s, w_s, m_s, acc_s,
                *, tq, qw, tk, past, length, ntop, idx_bits):
    i = pl.program_id(1)
    q0 = past + i * tq
    nkb = (q0 + tq + tk - 1) // tk
    last = nkb - 1
    ni = IDX_HEADS * IDX_DIM
    groups = B_HEADS // B_KV_HEADS
    lane = lax.broadcasted_iota(I32, (1, qw), 1)
    sub8 = lax.broadcasted_iota(I32, (SUBLANES, qw), 0)
    qchunk = jnp.right_shift(q0 + lane, 6)

    if tq != qw:
        qh_s[...] = jnp.zeros(qh_s.shape, BF16)
        iqh_s[...] = jnp.zeros(iqh_s.shape, BF16)
        pad_s[...] = jnp.zeros(pad_s.shape, F32)
    for h in range(B_HEADS):
        qh_s[h, 0:tq, :] = q_ref[:, B_HEAD_DIM * h:B_HEAD_DIM * (h + 1)]
    for h in range(IDX_HEADS):
        iqh_s[h, 0:tq, :] = iq_ref[:, IDX_DIM * h:IDX_DIM * (h + 1)]
    pad_s[0:tq, :] = ikw_ref[...]
    w_s[...] = pad_s[...].T[IDX_DIM:IDX_DIM + IDX_HEADS, :] * (ni ** -0.5)

    slab = tk

    def scores(kb, masked):
        for sl in range(tk // slab):
            ks = pl.multiple_of(kb * tk + sl * slab, slab)
            ikb = ik_ref[0, pl.ds(ks, slab), :]
            acc = jnp.zeros((slab, qw), F32)
            for h in range(IDX_HEADS):
                acc = acc + w_s[h:h + 1, :] * jnp.maximum(_dot_t(ikb, iqh_s[h]), 0.0)
            if masked:
                kpos = kb * tk + sl * slab + lax.broadcasted_iota(I32, (slab, qw), 0)
                vis = (jnp.right_shift(kpos, 6) <= qchunk) & (kpos < length)
                acc = jnp.where(vis, acc, -jnp.inf)
            bits = lax.bitcast_convert_type(acc, I32)
            key = jnp.where(bits < 0, bits ^ jnp.int32(0x7FFFFFFF), bits)
            rows = slice(sl * slab, (sl + 1) * slab)
            skey[kb, rows, :] = key
            skh[kb, rows, :] = jnp.right_shift(key, 16).astype(I16)
            skl[kb, rows, :] = ((key & 0xFFFF) - 32768).astype(I16)

    def score_pair(j, carry):
        scores(2 * j, False)
        scores(2 * j + 1, False)
        return carry

    lax.fori_loop(0, last // 2, score_pair, 0)

    @pl.when(last % 2 == 1)
    def _():
        scores(last - 1, False)

    scores(last, True)

    n_acc = 4
    prow = PACKED_ROWS

    def count16(ref, pred):
        def body(kb, accs):
            accs = list(accs)
            for g in range(tk // prow):
                blk = ref[kb, g * prow:(g + 1) * prow, :]
                accs[g % n_acc] = accs[g % n_acc] + jnp.where(pred(blk), jnp.int16(1), jnp.int16(0))
            return tuple(accs)
        accs = lax.fori_loop(0, nkb, body, tuple(jnp.zeros((prow, qw), I16) for _ in range(n_acc)))
        tot = (accs[0].astype(I32) + accs[1].astype(I32)) + (accs[2].astype(I32) + accs[3].astype(I32))
        return jnp.sum(tot, axis=0, keepdims=True)

    def rep16(v):
        return jnp.broadcast_to(v, (prow, qw)).astype(I16)

    def kth16(ref, kth):
        def bit_body(it, prefix):
            cand_u = prefix | jnp.left_shift(jnp.int32(1), 15 - it)
            cand = rep16(cand_u - 32768)
            cnt = count16(ref, lambda blk: blk >= cand)
            return jnp.where(cnt >= kth, cand_u, prefix)
        return lax.fori_loop(0, 16, bit_body, jnp.zeros((1, qw), I32)) - 32768

    def count(pred):
        def body(kb, accs):
            accs = list(accs)
            for g in range(tk // SUBLANES):
                blk = skey[kb, g * SUBLANES:(g + 1) * SUBLANES, :]
                accs[g % n_acc] = accs[g % n_acc] + jnp.where(pred(kb, g, blk), 1, 0)
            return tuple(accs)
        accs = lax.fori_loop(0, nkb, body, tuple(jnp.zeros((SUBLANES, qw), I32) for _ in range(n_acc)))
        tot = (accs[0] + accs[1]) + (accs[2] + accs[3])
        return jnp.sum(tot, axis=0, keepdims=True)

    def rep8(v):
        return jnp.broadcast_to(v, (SUBLANES, qw))

    p_hi = kth16(skh, ntop)
    hi16 = rep16(p_hi)
    above = count16(skh, lambda blk: blk > hi16)

    def low_body(kb, carry):
        for g in range(tk // prow):
            rows = slice(g * prow, (g + 1) * prow)
            skl[kb, rows, :] = jnp.where(skh[kb, rows, :] == hi16, skl[kb, rows, :], jnp.int16(-32768))
        return carry

    lax.fori_loop(0, nkb, low_body, 0)
    p_lo = kth16(skl, ntop - above)
    tau = p_hi * 65536 + (p_lo + 32768)
    tau8 = rep8(tau)

    cnt_gt = count(lambda kb, g, blk: blk > tau8)
    cnt_ge = count(lambda kb, g, blk: blk >= tau8)
    need = ntop - cnt_gt
    key_ninf = jnp.int32(0x7FFFFFFF) ^ jnp.int32(-8388608)
    finite = tau != key_ninf
    tie_rows = ((cnt_ge - cnt_gt) > need) & finite & (lane < tq)

    @pl.when(jnp.max(jnp.where(tie_rows, 1, 0)) > 0)
    def _():
        need8 = rep8(need)

        def xbit(it, xlim):
            cand8 = xlim | jnp.left_shift(jnp.int32(1), idx_bits - 1 - it)
            cnt = count(lambda kb, g, blk: (blk == tau8) & ((kb * tk + g * SUBLANES + sub8) < cand8))
            return jnp.where(rep8(cnt) <= need8, cand8, xlim)
        xlim8 = lax.fori_loop(0, idx_bits, xbit, jnp.zeros((SUBLANES, qw), I32))

        def demote(kb, carry):
            for g in range(tk // SUBLANES):
                rows = slice(g * SUBLANES, (g + 1) * SUBLANES)
                key = skey[kb, rows, :]
                late = (key == tau8) & ((kb * tk + g * SUBLANES + sub8) >= xlim8)
                skey[kb, rows, :] = jnp.where(late, key - 1, key)
            return carry

        lax.fori_loop(0, nkb, demote, 0)

    m_s[...] = jnp.full(m_s.shape, NEG_INF, F32)
    acc_s[...] = jnp.zeros(acc_s.shape, F32)
    vrows = vt_ref.shape[3]

    tau_ge8 = rep8(jnp.where(finite, tau, tau + 1))

    def attend(kbs, near):
        nblk = len(kbs)
        tiles = [nblk - 1 - i for i in range(nblk)] if near else None
        alphas = {}
        for i, kb in enumerate(kbs):
            for g in range(tk // SUBLANES):
                rows = slice(g * SUBLANES, (g + 1) * SUBLANES)
                madd_s[i, rows, :] = jnp.where(skey[kb, rows, :] >= tau_ge8, 0.0, NEG_INF)

        def logits(h):
            for i, kb in enumerate(kbs):
                kn = k_ref[0, h // groups, pl.ds(pl.multiple_of(kb * tk, tk), tk), :]
                bias = madd_s[i] if tiles is None else madd_s[i] + near_ref[tiles[i], h]
                s_s[i, h] = (_dot_t(kn, qh_s[h]) + bias).astype(BF16)

        def softmax(h):
            tiles16 = [s_s[i, h, r * prow:(r + 1) * prow, :] for i in range(nblk) for r in range(tk // prow)]
            mx = tiles16[:n_acc]
            for j, t in enumerate(tiles16[n_acc:]):
                mx[j % n_acc] = jnp.maximum(mx[j % n_acc], t)
            mx = jnp.maximum(jnp.maximum(mx[0], mx[1]), jnp.maximum(mx[2], mx[3]))
            m_cur = jnp.max(mx.astype(F32), axis=0, keepdims=True)
            c_h = jnp.zeros((1, qw), F32) if near else far_ref[h, 0:1, 0:qw]
            m_prev = m_s[h:h + 1, :]
            shift = (jnp.maximum(m_prev, m_cur + c_h) - c_h).astype(BF16)
            m_new = shift.astype(F32) + c_h
            m_s[h:h + 1, :] = m_new
            alphas[h] = jnp.exp2(m_prev - m_new)
            shift = jnp.broadcast_to(shift, (prow, qw))
            for i in range(nblk):
                for r in range(tk // prow):
                    rows = slice(r * prow, (r + 1) * prow)
                    p_s[i, h, rows, :] = jnp.exp2(s_s[i, h, rows, :] - shift)

        def values(h):
            hs = slice(h * vrows, (h + 1) * vrows)
            pv = _dot(vt_ref[0, h // groups, kbs[0]], p_s[0, h])
            for i in range(1, nblk):
                pv = pv + _dot(vt_ref[0, h // groups, kbs[i]], p_s[i, h])
            acc_s[hs, :] = acc_s[hs, :] * alphas[h] + pv

        for phase in (logits, softmax, values):
            for h in range(B_HEADS):
                phase(h)

    nfar = nkb - 2

    def far_pair(j, carry):
        attend([2 * j, 2 * j + 1], False)
        return carry

    lax.fori_loop(0, nfar // 2, far_pair, 0)

    @pl.when((nfar > 0) & (nfar % 2 == 1))
    def _():
        attend([nfar - 1], False)

    @pl.when(nkb >= 2)
    def _():
        attend([nkb - 2, last], True)

    @pl.when(nkb < 2)
    def _():
        attend([last], True)

    outs = []
    for h in range(B_HEADS):
        r0 = h * vrows
        outs.append(acc_s[r0:r0 + B_HEAD_DIM, :] / acc_s[r0 + B_HEAD_DIM:r0 + B_HEAD_DIM + 1, :])
    o = jnp.concatenate(outs, axis=0).T
    o_ref[...] = o[0:tq, :].astype(BF16)


def _dsa(q, iq, ikw, k_att, vt_att, ik_att, near, far, tq, past, length, ntop):
    tk = KEY_BLOCK
    nb, t_len, _ = q.shape
    lp = k_att.shape[2]
    vrows = vt_att.shape[3]
    nq = B_HEADS * B_HEAD_DIM
    ni = IDX_HEADS * IDX_DIM
    qw = max(tq, LANES)
    assert past % tk == 0 and (tq == tk or t_len == tq) and tq <= tk and lp % tk == 0
    idx_bits = lp.bit_length()

    def qblk(c):
        return pl.BlockSpec((None, tq, c), lambda b, i: (b, i, 0))

    kern = functools.partial(_dsa_kernel, tq=tq, qw=qw, tk=tk, past=past, length=length, ntop=ntop,
                             idx_bits=idx_bits)
    return pl.pallas_call(
        kern,
        grid=(nb, t_len // tq),
        in_specs=[qblk(nq), qblk(ni), qblk(LANES),
                  pl.BlockSpec((1, B_KV_HEADS, lp, B_HEAD_DIM), lambda b, i: (b, 0, 0, 0)),
                  pl.BlockSpec((1, B_KV_HEADS, lp // tk, vrows, tk), lambda b, i: (b, 0, 0, 0, 0)),
                  pl.BlockSpec((1, lp, IDX_DIM), lambda b, i: (b, 0, 0)),
                  pl.BlockSpec((2, B_HEADS, tk, qw), lambda b, i: (0, 0, 0, 0)),
                  pl.BlockSpec((B_HEADS, SUBLANES, tk), lambda b, i: (0, 0, 0))],
        out_specs=qblk(nq),
        out_shape=jax.ShapeDtypeStruct((nb, t_len, nq), BF16),
        scratch_shapes=[pltpu.VMEM((lp // tk, tk, qw), I32),
                        pltpu.VMEM((lp // tk, tk, qw), I16),
                        pltpu.VMEM((lp // tk, tk, qw), I16),
                        pltpu.VMEM((2, tk, qw), F32),
                        pltpu.VMEM((2, B_HEADS, tk, qw), BF16),
                        pltpu.VMEM((2, B_HEADS, tk, qw), BF16),
                        pltpu.VMEM((B_HEADS, qw, B_HEAD_DIM), BF16),
                        pltpu.VMEM((IDX_HEADS, qw, IDX_DIM), BF16),
                        pltpu.VMEM((qw, LANES), F32),
                        pltpu.VMEM((IDX_HEADS, qw), F32),
                        pltpu.VMEM((B_HEADS, qw), F32),
                        pltpu.VMEM((B_HEADS * vrows, qw), F32)],
        compiler_params=_params(("arbitrary", "arbitrary")),
        name="dsa",
    )(q, iq, ikw, k_att, vt_att, ik_att, near, far)


def _xq_tail(x1, gx_ref, wxq_ref, qnx_ref, cs, qx_ref):
    xn = _rms(x1, gx_ref[...]).astype(BF16)
    qx = _head_rms(_dot(xn, wxq_ref[...]), X_HEAD_DIM, qnx_ref[...])
    _tm_to_seq(qx * (X_HEAD_DIM ** -0.5), cs, [(qx_ref, D_MODEL)])


def _even_out_kernel(x_ref, ya_ref, yb_ref, wo_ref, gx_ref, wxq_ref, qnx_ref, x1_ref, qx_ref, ys, cs):
    yb = _seq_to_tm(yb_ref, ys).astype(BF16)
    x1 = x_ref[...] + _dot(ya_ref[...], wo_ref[0:A_WIDTH, :]) + _dot(yb, wo_ref[A_WIDTH:, :])
    x1_ref[...] = x1
    _xq_tail(x1, gx_ref, wxq_ref, qnx_ref, cs, qx_ref)


def _even_out(x, ya, yb, w_out, gx, w_xq, qnx, tt):
    nb, t_len, nyb = yb.shape
    tm = tt * nb
    rows = x.shape[0]
    d = D_MODEL

    def row(c):
        return pl.BlockSpec((tm, c), lambda i: (i, 0))

    def seq(c):
        return pl.BlockSpec((nb, tt, c), lambda i: (0, i, 0))

    return pl.pallas_call(
        _even_out_kernel,
        grid=(rows // tm,),
        in_specs=[row(d), row(A_WIDTH), seq(nyb), _const_spec((d, d)),
                  _const_spec((1, d)), _const_spec((d, d)), _const_spec((1, d))],
        out_specs=[row(d), seq(d)],
        out_shape=[jax.ShapeDtypeStruct((rows, d), F32), jax.ShapeDtypeStruct((nb, t_len, d), BF16)],
        scratch_shapes=[pltpu.VMEM((nyb // LANES, tm, LANES), F32), pltpu.VMEM((d // LANES, tm, LANES), F32)],
        compiler_params=_params(("arbitrary",)),
        name="even_out",
    )(x, ya, yb, w_out, gx, w_xq, qnx)


def _xattn_kernel(q_ref, mk_ref, mv_ref, o_ref):
    for h in range(X_HEADS):
        sl = slice(h * X_HEAD_DIM, (h + 1) * X_HEAD_DIM)
        s = _dot_t(q_ref[:, sl], mk_ref[0, :, sl])
        p = jnp.exp(s - jnp.max(s, axis=1, keepdims=True))
        o = _dot(p.astype(BF16), mv_ref[0, :, sl]) / jnp.sum(p, axis=1, keepdims=True)
        o_ref[:, sl] = o.astype(BF16)


def _xattn(qx, mk, mv, tq):
    nb, t_len, d = qx.shape
    m = mk.shape[1]
    return pl.pallas_call(
        _xattn_kernel,
        grid=(nb, t_len // tq),
        in_specs=[pl.BlockSpec((None, tq, d), lambda b, i: (b, i, 0)),
                  pl.BlockSpec((1, m, d), lambda b, i: (b, 0, 0)),
                  pl.BlockSpec((1, m, d), lambda b, i: (b, 0, 0))],
        out_specs=pl.BlockSpec((None, tq, d), lambda b, i: (b, i, 0)),
        out_shape=jax.ShapeDtypeStruct((nb, t_len, d), BF16),
        compiler_params=_params(("arbitrary", "arbitrary")),
        name="mem_attn",
    )(qx, mk, mv)


def _ffn_kernel(x_ref, o_ref, wxo_ref, g_ref, wup_ref, cw_ref, cb_ref, wdn_ref, hist_ref,
                y_ref, fh_ref, gbuf, cs, *, chunk, seq_out):
    tm = x_ref.shape[0]
    nh = (F_CONV - 1) * SUBLANES

    @pl.when(pl.program_id(0) == 0)
    def _():
        gbuf[0:nh, :] = hist_ref[...]

    x2 = x_ref[...] + _dot(_seq_to_tm(o_ref, cs).astype(BF16), wxo_ref[...])
    xn = _rms(x2, g_ref[...]).astype(BF16)
    acc = jnp.zeros((tm, D_MODEL), F32)
    for c0 in range(0, D_FF, chunk):
        c1 = min(c0 + chunk, D_FF)
        val = _dot(xn, wup_ref[:, c0:c1])
        gate = _dot(xn, wup_ref[:, D_FF + c0:D_FF + c1])
        gbuf[nh:nh + tm, c0:c1] = gate
        conv = gate * cw_ref[F_CONV - 1:F_CONV, c0:c1]
        for i in range(F_CONV - 1):
            conv = conv + gbuf[i * SUBLANES:i * SUBLANES + tm, c0:c1] * cw_ref[i:i + 1, c0:c1]
        conv = conv + cb_ref[:, c0:c1]
        act = (jax.nn.gelu(conv) * val).astype(BF16)
        acc = acc + _dot(act, wdn_ref[c0:c1, :])
    tail = gbuf[tm:tm + nh, :]
    gbuf[0:nh, :] = tail
    fh_ref[...] = tail
    if seq_out:
        _tm_to_seq(x2 + acc, cs, [(y_ref, D_MODEL)])
    else:
        y_ref[...] = x2 + acc


def _ffn(x, o, w_xo, g, w_up, conv_w, conv_b, w_down, hist, tt, seq_out):
    nb, t_len, d = o.shape
    tm = tt * nb
    rows = x.shape[0]
    nh = (F_CONV - 1) * SUBLANES

    def row(c):
        return pl.BlockSpec((tm, c), lambda i: (i, 0))

    def seq(c):
        return pl.BlockSpec((nb, tt, c), lambda i: (0, i, 0))

    y_shape = jax.ShapeDtypeStruct((nb, t_len, d) if seq_out else (rows, d), F32)
    return pl.pallas_call(
        functools.partial(_ffn_kernel, chunk=MXU_DIM, seq_out=seq_out),
        grid=(rows // tm,),
        in_specs=[row(d), seq(d), _const_spec((d, d)), _const_spec((1, d)), _const_spec((d, 2 * D_FF)),
                  _const_spec((F_CONV, D_FF)), _const_spec((1, D_FF)), _const_spec((D_FF, d)),
                  _const_spec((nh, D_FF))],
        out_specs=[seq(d) if seq_out else row(d), pl.BlockSpec((nh, D_FF), lambda i: (0, 0))],
        out_shape=[y_shape, jax.ShapeDtypeStruct((nh, D_FF), F32)],
        scratch_shapes=[pltpu.VMEM((tm + nh, D_FF), F32), pltpu.VMEM((d // LANES, tm, LANES), F32)],
        compiler_params=_params(("arbitrary",)),
        name="ffn",
    )(x, o, w_xo, g, w_up, conv_w, conv_b, w_down, hist)


def _odd_kernel(x_ref, g_ref, win_ref, cw_ref, cb_ref, wa_ref, ba_ref, wi_ref, bi_ref, lam_ref,
                wo_ref, hist_ref, h0_ref, gx_ref, wxq_ref, qnx_ref,
                x1_ref, qx_ref, ch_ref, hl_ref, xbuf, a_s, b_s, h_s, cs, *, stream_start):
    tm = x_ref.shape[0]
    nh = (C_CONV - 1) * SUBLANES
    first = pl.program_id(0) == 0

    @pl.when(first)
    def _():
        xbuf[0:nh, :] = hist_ref[...]
        h_s[...] = h0_ref[...]

    x = x_ref[...]
    xn = _rms(x, g_ref[...]).astype(BF16)
    xr_in = _dot(xn, win_ref[:, RNN_WIDTH:])
    xbuf[nh:nh + tm, :] = xr_in
    xr = _conv_taps(xbuf, xr_in, cw_ref, C_CONV, tm) + cb_ref[...]
    tail = xbuf[tm:tm + nh, :]
    xbuf[0:nh, :] = tail
    ch_ref[...] = tail

    xrb = xr.astype(BF16)
    lam = -lam_ref[...]
    sp = jnp.maximum(lam, 0.0) + jnp.log1p(jnp.exp(-jnp.abs(lam)))
    rows = lax.broadcasted_iota(I32, (tm, RNN_BLOCK), 0)
    for n in range(RNN_BLOCKS):
        sl = slice(n * RNN_BLOCK, (n + 1) * RNN_BLOCK)
        r = jax.nn.sigmoid(_dot(xrb[:, sl], wa_ref[n]) + ba_ref[:, sl])
        ig = jax.nn.sigmoid(_dot(xrb[:, sl], wi_ref[n]) + bi_ref[:, sl])
        log_a = -RG_C * r * sp[:, sl]
        a = jnp.exp(log_a)
        mult = jnp.sqrt(jnp.tanh(-log_a) * (1.0 + a * a))
        if stream_start:
            mult = jnp.where(first & (rows < SUBLANES), 1.0, mult)
        a_s[:, sl] = a
        b_s[:, sl] = mult * ig * xr[:, sl]

    def step(t, h):
        r0 = pl.multiple_of(t * SUBLANES, SUBLANES)
        h = a_s[pl.ds(r0, SUBLANES), :] * h + b_s[pl.ds(r0, SUBLANES), :]
        b_s[pl.ds(r0, SUBLANES), :] = h
        return h

    h = lax.fori_loop(0, tm // SUBLANES, step, h_s[...], unroll=8)
    h_s[...] = h
    hl_ref[...] = h

    gate = _dot(xn, win_ref[:, :RNN_WIDTH])
    act = (jax.nn.gelu(gate) * b_s[...]).astype(BF16)
    x1 = x + _dot(act, wo_ref[...])
    x1_ref[...] = x1
    _xq_tail(x1, gx_ref, wxq_ref, qnx_ref, cs, qx_ref)


def _odd(x, g, w_in, conv_w, conv_b, w_a, b_a, w_i, b_i, lam, w_out, hist, h0, gx, w_xq, qnx, tt,
         stream_start):
    nb = SUBLANES
    tm = tt * nb
    rows = x.shape[0]
    t_len = rows // nb
    d = D_MODEL
    r = RNN_WIDTH
    nh = (C_CONV - 1) * SUBLANES

    def row(c):
        return pl.BlockSpec((tm, c), lambda i: (i, 0))

    blk = (RNN_BLOCKS, RNN_BLOCK, RNN_BLOCK)
    return pl.pallas_call(
        functools.partial(_odd_kernel, stream_start=stream_start),
        grid=(rows // tm,),
        in_specs=[row(d), _const_spec((1, d)), _const_spec((d, 2 * r)), _const_spec((C_CONV, r)),
                  _const_spec((1, r)), _const_spec(blk), _const_spec((1, r)), _const_spec(blk),
                  _const_spec((1, r)), _const_spec((1, r)), _const_spec((r, d)), _const_spec((nh, r)),
                  _const_spec((SUBLANES, r)), _const_spec((1, d)), _const_spec((d, d)), _const_spec((1, d))],
        out_specs=[row(d), pl.BlockSpec((nb, tt, d), lambda i: (0, i, 0)), pl.BlockSpec((nh, r), lambda i: (0, 0)),
                   pl.BlockSpec((SUBLANES, r), lambda i: (0, 0))],
        out_shape=[jax.ShapeDtypeStruct((rows, d), F32), jax.ShapeDtypeStruct((nb, t_len, d), BF16),
                   jax.ShapeDtypeStruct((nh, r), F32), jax.ShapeDtypeStruct((SUBLANES, r), F32)],
        scratch_shapes=[pltpu.VMEM((tm + nh, r), F32), pltpu.VMEM((tm, r), F32), pltpu.VMEM((tm, r), F32),
                        pltpu.VMEM((SUBLANES, r), F32), pltpu.VMEM((d // LANES, tm, LANES), F32)],
        compiler_params=_params(("arbitrary",)),
        name="odd_mixer",
    )(x, g, w_in, conv_w, conv_b, w_a, b_a, w_i, b_i, lam, w_out, hist, h0, gx, w_xq, qnx)


def _to_tm(a):
    return jnp.transpose(a, (1, 0, 2)).reshape(a.shape[1] * a.shape[0], a.shape[2])


def _from_tm(a, w):
    return jnp.transpose(a.reshape(w, SUBLANES, a.shape[1]), (1, 0, 2))


def _trunk(x, st, mem_k, mem_v, p, bias, tt, tq_dsa, tq_x):
    nb, t_len, d = x.shape
    assert nb == SUBLANES
    past = 0 if st is None else st["b_k"].shape[2]
    length = past + t_len
    ntop = min(TOPK_MAX, length // 4)
    near, far = bias
    xt = None
    out = {}

    def hist(name, l, width, c):
        if st is None:
            return jnp.zeros(((width - 1) * nb, c), F32)
        return _to_tm(st[name][l])

    for l in range(DEPTH):
        if l % 2 == 0:
            e = l // 2
            assert l == 0, "the per-sequence input is converted by the first layer's kernel"
            xt, ya, q, iq, ikw, k, v, uh = _even_in(
                x, p["g_mix"][l], p["w_in_even"][e], p["a_conv_w"][e], p["b_q_norm"][e], p["b_k_norm"][e],
                hist("a_conv", e, A_CONV, A_WIDTH), tt)
            k_new = k.reshape(nb, t_len, B_KV_HEADS, B_HEAD_DIM)
            v_new = v.reshape(nb, t_len, B_KV_HEADS, B_HEAD_DIM)
            ik_new = ikw[:, :, :IDX_DIM]
            k_all, v_all, ik_all = k_new, v_new, ik_new
            if st is not None:
                k_all = jnp.concatenate([st["b_k"][e], k_new], axis=1)
                v_all = jnp.concatenate([st["b_v"][e], v_new], axis=1)
                ik_all = jnp.concatenate([st["b_kidx"][e], ik_new], axis=1)
            lp = -(-length // KEY_BLOCK) * KEY_BLOCK
            padl = lp - length
            k_att = jnp.pad(jnp.transpose(k_all, (0, 2, 1, 3)).astype(BF16), ((0, 0), (0, 0), (0, padl), (0, 0)))
            v_att = jnp.pad(jnp.transpose(v_all, (0, 2, 1, 3)).astype(BF16), ((0, 0), (0, 0), (0, padl), (0, 0)))
            vt_att = jnp.transpose(v_att.reshape(nb, B_KV_HEADS, lp // KEY_BLOCK, KEY_BLOCK, B_HEAD_DIM),
                                   (0, 1, 2, 4, 3))
            ones = jnp.ones(vt_att.shape[:3] + (1, KEY_BLOCK), BF16)
            zeros = jnp.zeros(vt_att.shape[:3] + (PACKED_ROWS - 1, KEY_BLOCK), BF16)
            vt_att = jnp.concatenate([vt_att, ones, zeros], axis=3)
            ik_att = jnp.pad(ik_all.astype(BF16), ((0, 0), (0, padl), (0, 0)))
            yb = _dsa(q, iq, ikw, k_att, vt_att, ik_att, near, far, tq_dsa, past, length, ntop)
            x1, qx = _even_out(xt, ya, yb, p["w_out_even"][e], p["g_x"][l], p["w_xq"][l], p["x_q_norm"][l], tt)
            out.setdefault("a_conv", []).append(_from_tm(uh, A_CONV - 1))
            out.setdefault("b_k", []).append(k_new)
            out.setdefault("b_v", []).append(v_new)
            out.setdefault("b_kidx", []).append(ik_new)
        else:
            o = l // 2
            h0 = jnp.zeros((nb, RNN_WIDTH), F32) if st is None else st["c_h"][o]
            x1, qx, ch, hl = _odd(
                xt, p["g_mix"][l], p["w_in_odd"][o], p["c_conv_w"][o], p["c_conv_b"][o], p["c_w_a"][o],
                p["c_b_a"][o], p["c_w_i"][o], p["c_b_i"][o], p["c_lambda"][o], p["w_out_odd"][o],
                hist("c_conv", o, C_CONV, RNN_WIDTH), h0, p["g_x"][l], p["w_xq"][l], p["x_q_norm"][l], tt,
                stream_start=(past == 0))
            out.setdefault("c_conv", []).append(_from_tm(ch, C_CONV - 1))
            out.setdefault("c_h", []).append(hl)
        xo = _xattn(qx, mem_k[l], mem_v[l], tq_x)
        xt, fh = _ffn(x1, xo, p["w_xo"][l], p["g_ffn"][l], p["w_up"][l], p["f_conv_w"][l], p["f_conv_b"][l],
                      p["w_down"][l], hist("f_conv", l, F_CONV, D_FF), tt, seq_out=(l == DEPTH - 1))
        out.setdefault("f_conv", []).append(_from_tm(fh, F_CONV - 1))
    return xt, {name: jnp.stack(v) for name, v in out.items()}


def kernel(x_prompt, x_sample, cache_b_k, cache_b_v, cache_b_kidx, state_a_conv, state_c_conv, state_c_h, state_ffn_conv, cache_mem_k, cache_mem_v, mem_prompt, rel_table, g_mix, w_in_even, a_conv_w, b_q_norm, b_k_norm, w_out_even, w_in_odd, c_conv_w, c_conv_b, c_w_a, c_b_a, c_w_i, c_b_i, c_lambda, w_out_odd, g_mem, g_x, w_xq, w_xk, w_xv, x_q_norm, x_k_norm, w_xo, g_ffn, w_up, f_conv_w, f_conv_b, w_down):
    d = D_MODEL
    n_even = w_in_even.shape[0]
    n_odd = w_in_odd.shape[0]
    bp, t_p, _ = x_prompt.shape
    m = mem_prompt.shape[1]

    def rowvec(a):
        return a.reshape(a.shape[0], 1, a.shape[-1])

    p = {
        "g_mix": rowvec(g_mix), "g_x": rowvec(g_x), "g_ffn": rowvec(g_ffn),
        "w_in_even": jnp.pad(w_in_even, ((0, 0), (0, 0), (0, EVEN_IN_PAD - EVEN_IN))).astype(BF16),
        "a_conv_w": a_conv_w,
        "b_q_norm": rowvec(jnp.tile(b_q_norm, (1, B_HEADS))),
        "b_k_norm": rowvec(jnp.tile(b_k_norm, (1, B_KV_HEADS))),
        "w_out_even": w_out_even.astype(BF16),
        "w_in_odd": w_in_odd.astype(BF16), "c_conv_w": c_conv_w, "c_conv_b": rowvec(c_conv_b),
        "c_w_a": c_w_a.astype(BF16), "c_b_a": rowvec(c_b_a), "c_w_i": c_w_i.astype(BF16), "c_b_i": rowvec(c_b_i),
        "c_lambda": rowvec(c_lambda), "w_out_odd": w_out_odd.astype(BF16),
        "w_xq": w_xq.astype(BF16), "x_q_norm": rowvec(jnp.tile(x_q_norm, (1, X_HEADS))),
        "w_xo": w_xo.astype(BF16), "w_up": w_up.astype(BF16), "f_conv_w": f_conv_w,
        "f_conv_b": rowvec(f_conv_b), "w_down": w_down.astype(BF16),
    }
    del n_even, n_odd
    bias = _bias_tiles(rel_table, KEY_BLOCK)

    mk, mv = _mem_kv(mem_prompt.reshape(bp * m, d), g_mem, w_xk, x_k_norm, w_xv)
    p_mem_k = mk.reshape(DEPTH, bp, m, X_HEADS, X_HEAD_DIM)
    p_mem_v = mv.reshape(DEPTH, bp, m, X_HEADS, X_HEAD_DIM)
    y_prompt, new_p = _trunk(x_prompt, None, mk.reshape(DEPTH, bp, m, d).astype(BF16),
                             mv.reshape(DEPTH, bp, m, d).astype(BF16), p, bias,
                             tt=64, tq_dsa=KEY_BLOCK, tq_x=512)

    bs, t_s, _ = x_sample.shape
    st_s = {"b_k": cache_b_k, "b_v": cache_b_v, "b_kidx": cache_b_kidx, "a_conv": state_a_conv,
            "c_conv": state_c_conv, "c_h": state_c_h, "f_conv": state_ffn_conv}
    ms = cache_mem_k.shape[2]
    y_sample, new_s = _trunk(x_sample, st_s, cache_mem_k.reshape(DEPTH, bs, ms, d).astype(BF16),
                             cache_mem_v.reshape(DEPTH, bs, ms, d).astype(BF16), p, bias,
                             tt=t_s, tq_dsa=t_s, tq_x=t_s)
    return (y_prompt, y_sample,
            new_p["b_k"], new_p["b_v"], new_p["b_kidx"], new_p["a_conv"], new_p["c_conv"],
            new_p["c_h"], new_p["f_conv"], p_mem_k, p_mem_v,
            new_s["b_k"], new_s["b_v"], new_s["b_kidx"], new_s["a_conv"], new_s["c_conv"],
            new_s["c_h"], new_s["f_conv"])
```

```python
import functools
import math

import jax
import jax.numpy as jnp
from jax import lax
from jax.experimental import pallas as pl
from jax.experimental.pallas import tpu as pltpu

F32 = jnp.float32
BF16 = jnp.bfloat16
I32 = jnp.int32
I16 = jnp.int16

D_MODEL = 1024
DEPTH = 2
CHUNK = 64
EPS = 1e-6
NEG_INF = -1e30
LOG2E = math.log2(math.e)
A_WIDTH = 512
A_CONV = 3
B_HEADS = 8
B_KV_HEADS = 2
B_HEAD_DIM = 64
IDX_HEADS = 8
IDX_DIM = 32
TOPK_MAX = 256
REL_BUCKETS = 32
REL_MAX_DIST = 128
RNN_WIDTH = 1024
RNN_BLOCKS = 8
RNN_BLOCK = 128
C_CONV = 4
RG_C = 8.0
X_HEADS = 4
X_HEAD_DIM = 256
D_FF = 2816
F_CONV = 3
EVEN_IN = 2600

SUBLANES = 8
LANES = 128
MXU_DIM = 256
PACKED_ROWS = 16
VMEM_LIMIT = 56 * 1024 * 1024

EVEN_IN_PAD = 2688
KEY_BLOCK = 256
INT_MIN = -2147483648


def _params(sem, vmem=VMEM_LIMIT):
    return pltpu.CompilerParams(dimension_semantics=sem, vmem_limit_bytes=vmem)


def _const_spec(shape):
    nd = len(shape)
    return pl.BlockSpec(shape, lambda *_: (0,) * nd, pipeline_mode=pl.Buffered(1))


def _rms(x, g):
    ms = jnp.mean(x * x, axis=-1, keepdims=True)
    return x * lax.rsqrt(ms + EPS) * g


def _head_rms(x, hd, gain):
    m, c = x.shape
    s = x * x
    parts = []
    if hd >= LANES:
        for h in range(c // hd):
            ms = jnp.mean(s[:, h * hd:(h + 1) * hd], axis=-1, keepdims=True)
            parts.append(x[:, h * hd:(h + 1) * hd] * lax.rsqrt(ms + EPS))
    else:
        lane = lax.broadcasted_iota(I32, (m, LANES), 1)
        for j in range(c // LANES):
            sj = s[:, j * LANES:(j + 1) * LANES]
            inv = jnp.zeros((m, LANES), F32)
            for k in range(LANES // hd):
                msk = (lane >= k * hd) & (lane < (k + 1) * hd)
                ms = jnp.sum(jnp.where(msk, sj, 0.0), axis=-1, keepdims=True) * (1.0 / hd)
                inv = jnp.where(msk, lax.rsqrt(ms + EPS), inv)
            parts.append(x[:, j * LANES:(j + 1) * LANES] * inv)
    y = parts[0] if len(parts) == 1 else jnp.concatenate(parts, axis=-1)
    return y * gain


def _dot(a, b):
    return jnp.dot(a, b, preferred_element_type=F32)


def _dot_t(a, b):
    return lax.dot_general(a, b, (((1,), (1,)), ((), ())), preferred_element_type=F32)


def _conv_taps(buf, cur, w_ref, width, tm):
    y = cur * w_ref[width - 1:width, :]
    for i in range(width - 1):
        y = y + buf[i * SUBLANES:i * SUBLANES + tm, :] * w_ref[i:i + 1, :]
    return y


def _seq_to_tm(src_ref, scr):
    nb, tt, c = src_ref.shape
    for b in range(nb):
        for j in range(c // LANES):
            scr[j, pl.ds(b, tt, stride=nb), :] = src_ref[b, :, j * LANES:(j + 1) * LANES].astype(F32)
    return jnp.concatenate([scr[j] for j in range(c // LANES)], axis=-1)


def _tm_to_seq(val, scr, dst_refs):
    tm, c = val.shape
    tt = tm // SUBLANES
    for j in range(c // LANES):
        scr[j] = val[:, j * LANES:(j + 1) * LANES]
    for b in range(SUBLANES):
        j0 = 0
        for ref, ci in dst_refs:
            nj = ci // LANES
            parts = [scr[j0 + j, pl.ds(b, tt, stride=SUBLANES), :] for j in range(nj)]
            ref[b] = (parts[0] if nj == 1 else jnp.concatenate(parts, axis=-1)).astype(ref.dtype)
            j0 += nj


def _memkv_kernel(mem_ref, g_ref, wk_ref, kn_ref, wv_ref, k_ref, v_ref):
    hm = _rms(mem_ref[...], g_ref[0]).astype(BF16)
    k_ref[0] = _head_rms(_dot(hm, wk_ref[0]), X_HEAD_DIM, kn_ref[0])
    v_ref[0] = _dot(hm, wv_ref[0])


def _mem_kv(mem, g_mem, w_xk, x_k_norm, w_xv):
    rows = mem.shape[0]
    tm = min(512, rows)
    d = D_MODEL
    kn = jnp.tile(x_k_norm, (1, X_HEADS)).reshape(DEPTH, 1, d)
    out = jax.ShapeDtypeStruct((DEPTH, rows, d), F32)
    return pl.pallas_call(
        _memkv_kernel,
        grid=(DEPTH, rows // tm),
        in_specs=[
            pl.BlockSpec((tm, d), lambda l, i: (i, 0)),
            pl.BlockSpec((1, 1, d), lambda l, i: (l, 0, 0)),
            pl.BlockSpec((1, d, d), lambda l, i: (l, 0, 0)),
            pl.BlockSpec((1, 1, d), lambda l, i: (l, 0, 0)),
            pl.BlockSpec((1, d, d), lambda l, i: (l, 0, 0)),
        ],
        out_specs=[pl.BlockSpec((1, tm, d), lambda l, i: (l, i, 0))] * 2,
        out_shape=[out, out],
        compiler_params=_params(("arbitrary", "arbitrary")),
        name="mem_kv",
    )(mem, g_mem.reshape(DEPTH, 1, d), w_xk.astype(BF16), kn, w_xv.astype(BF16))


def _even_in_kernel(x_ref, g_ref, w_ref, cw_ref, qn_ref, kn_ref, hist_ref,
                    xt_ref, ya_ref, q_ref, iq_ref, ikw_ref, k_ref, v_ref, uh_ref, ubuf, xs, cs):
    tm = xt_ref.shape[0]
    nh = (A_CONV - 1) * SUBLANES

    @pl.when(pl.program_id(0) == 0)
    def _():
        ubuf[0:nh, :] = hist_ref[...]

    x = _seq_to_tm(x_ref, xs)
    xt_ref[...] = x
    xn = _rms(x, g_ref[...]).astype(BF16)

    def proj(a, b):
        return _dot(xn, w_ref[:, a:b])

    zc = proj(A_WIDTH, 3 * A_WIDTH)
    u = zc[:, :A_WIDTH] * zc[:, A_WIDTH:]
    ubuf[nh:nh + tm, :] = u
    conv = _conv_taps(ubuf, u, cw_ref, A_CONV, tm)
    ya_ref[...] = (proj(0, A_WIDTH) * conv).astype(BF16)
    tail = ubuf[tm:tm + nh, :]
    ubuf[0:nh, :] = tail
    uh_ref[...] = tail

    o = 3 * A_WIDTH
    nq = B_HEADS * B_HEAD_DIM
    nkv = B_KV_HEADS * B_HEAD_DIM
    ni = IDX_HEADS * IDX_DIM
    q = _head_rms(proj(o, o + nq), B_HEAD_DIM, qn_ref[...]) * (B_HEAD_DIM ** -0.5 * LOG2E)
    zkv = proj(o + nq, o + nq + 2 * nkv)
    k = _head_rms(zkv[:, :nkv], B_HEAD_DIM, kn_ref[...])
    zi = proj(o + nq + 2 * nkv, EVEN_IN_PAD)
    seq = jnp.concatenate([q, zi, k, zkv[:, nkv:]], axis=-1)
    _tm_to_seq(seq, cs, [(q_ref, nq), (iq_ref, ni), (ikw_ref, LANES), (k_ref, nkv), (v_ref, nkv)])


def _even_in(x, g, w_pad, conv_w, qn, kn, hist, tt):
    nb, t_len, d = x.shape
    tm = tt * nb
    rows = t_len * nb
    nh = (A_CONV - 1) * SUBLANES
    nq = B_HEADS * B_HEAD_DIM
    nkv = B_KV_HEADS * B_HEAD_DIM
    ni = IDX_HEADS * IDX_DIM
    nseq = nq + ni + LANES + 2 * nkv

    def row(c):
        return pl.BlockSpec((tm, c), lambda i: (i, 0))

    def seq(c):
        return pl.BlockSpec((nb, tt, c), lambda i: (0, i, 0))

    def seq_shape(c, dt):
        return jax.ShapeDtypeStruct((nb, t_len, c), dt)

    return pl.pallas_call(
        _even_in_kernel,
        grid=(t_len // tt,),
        in_specs=[seq(d), _const_spec((1, d)), _const_spec((d, EVEN_IN_PAD)), _const_spec((A_CONV, A_WIDTH)),
                  _const_spec((1, nq)), _const_spec((1, nkv)), _const_spec((nh, A_WIDTH))],
        out_specs=[row(d), row(A_WIDTH), seq(nq), seq(ni), seq(LANES), seq(nkv), seq(nkv),
                   pl.BlockSpec((nh, A_WIDTH), lambda i: (0, 0))],
        out_shape=[jax.ShapeDtypeStruct((rows, d), F32), jax.ShapeDtypeStruct((rows, A_WIDTH), BF16),
                   seq_shape(nq, BF16), seq_shape(ni, BF16), seq_shape(LANES, F32), seq_shape(nkv, F32),
                   seq_shape(nkv, F32), jax.ShapeDtypeStruct((nh, A_WIDTH), F32)],
        scratch_shapes=[pltpu.VMEM((tm + nh, A_WIDTH), F32), pltpu.VMEM((d // LANES, tm, LANES), F32),
                        pltpu.VMEM((nseq // LANES, tm, LANES), F32)],
        compiler_params=_params(("arbitrary",)),
        name="even_in",
    )(x, g, w_pad, conv_w, qn, kn, hist)


def _rel_bucket(rel):
    half = REL_BUCKETS // 2
    max_exact = half // 2
    n = -rel
    ret = jnp.where(n < 0, half, 0)
    n = jnp.abs(n)
    nf = jnp.maximum(n, 1).astype(F32)
    large = max_exact + (jnp.log(nf / max_exact) / math.log(REL_MAX_DIST / max_exact)
                         * (half - max_exact)).astype(I32)
    large = jnp.minimum(large, half - 1)
    return ret + jnp.where(n < max_exact, n, large)


def _bias_kernel(tab_ref, near_ref, far_ref):
    tk = near_ref.shape[-1]
    r = lax.broadcasted_iota(I32, (tk, tk), 0)
    c = lax.broadcasted_iota(I32, (tk, tk), 1)

    def lookup(bucket, h):
        def body(j, acc):
            return jnp.where(bucket == j, tab_ref[j, h], acc)
        return lax.fori_loop(0, REL_BUCKETS, body, jnp.zeros(bucket.shape, F32))

    for blk in range(2):
        bucket = _rel_bucket(r - c - blk * tk)
        for h in range(B_HEADS):
            near_ref[blk, h] = lookup(bucket, h) * LOG2E
    bucket = _rel_bucket(-REL_MAX_DIST - c[0:SUBLANES, :])
    for h in range(B_HEADS):
        far_ref[h] = lookup(bucket, h) * LOG2E


def _bias_tiles(rel_table, tk):
    return pl.pallas_call(
        _bias_kernel,
        in_specs=[pl.BlockSpec(memory_space=pltpu.SMEM)],
        out_shape=[jax.ShapeDtypeStruct((2, B_HEADS, tk, tk), F32),
                   jax.ShapeDtypeStruct((B_HEADS, SUBLANES, tk), F32)],
        name="rel_bias",
    )(rel_table)


def _dsa_kernel(q_ref, iq_ref, ikw_ref, k_ref, vt_ref, ik_ref, near_ref, far_ref, o_ref,
                skey, skh, skl, madd_s, s_s, p_s, qh_s, iqh_s, pad_s, w_s, m_s, acc_s,
                *, tq, qw, tk, past, length, ntop, idx_bits):
    i = pl.program_id(1)
    q0 = past + i * tq
    nkb = (q0 + tq + tk - 1) // tk
    last = nkb - 1
    ni = IDX_HEADS * IDX_DIM
    groups = B_HEADS // B_KV_HEADS
    lane = lax.broadcasted_iota(I32, (1, qw), 1)
    sub8 = lax.broadcasted_iota(I32, (SUBLANES, qw), 0)
    qchunk = jnp.right_shift(q0 + lane, 6)

    if tq != qw:
        qh_s[...] = jnp.zeros(qh_s.shape, BF16)
        iqh_s[...] = jnp.zeros(iqh_s.shape, BF16)
        pad_s[...] = jnp.zeros(pad_s.shape, F32)
    for h in range(B_HEADS):
        qh_s[h, 0:tq, :] = q_ref[:, B_HEAD_DIM * h:B_HEAD_DIM * (h + 1)]
    for h in range(IDX_HEADS):
        iqh_s[h, 0:tq, :] = iq_ref[:, IDX_DIM * h:IDX_DIM * (h + 1)]
    pad_s[0:tq, :] = ikw_ref[...]
    w_s[...] = pad_s[...].T[IDX_DIM:IDX_DIM + IDX_HEADS, :] * (ni ** -0.5)

    slab = tk

    def scores(kb, masked):
        for sl in range(tk // slab):
            ks = pl.multiple_of(kb * tk + sl * slab, slab)
            ikb = ik_ref[0, pl.ds(ks, slab), :]
            acc = jnp.zeros((slab, qw), F32)
            for h in range(IDX_HEADS):
                acc = acc + w_s[h:h + 1, :] * jnp.maximum(_dot_t(ikb, iqh_s[h]), 0.0)
            if masked:
                kpos = kb * tk + sl * slab + lax.broadcasted_iota(I32, (slab, qw), 0)
                vis = (jnp.right_shift(kpos, 6) <= qchunk) & (kpos < length)
                acc = jnp.where(vis, acc, -jnp.inf)
            bits = lax.bitcast_convert_type(acc, I32)
            key = jnp.where(bits < 0, bits ^ jnp.int32(0x7FFFFFFF), bits)
            rows = slice(sl * slab, (sl + 1) * slab)
            skey[kb, rows, :] = key
            skh[kb, rows, :] = jnp.right_shift(key, 16).astype(I16)
            skl[kb, rows, :] = ((key & 0xFFFF) - 32768).astype(I16)

    def score_pair(j, carry):
        scores(2 * j, False)
        scores(2 * j + 1, False)
        return carry

    lax.fori_loop(0, last // 2, score_pair, 0)

    @pl.when(last % 2 == 1)
    def _():
        scores(last - 1, False)

    scores(last, True)

    n_acc = 4
    prow = PACKED_ROWS

    def count16(ref, pred):
        def body(kb, accs):
            accs = list(accs)
            for g in range(tk // prow):
                blk = ref[kb, g * prow:(g + 1) * prow, :]
                accs[g % n_acc] = accs[g % n_acc] + jnp.where(pred(blk), jnp.int16(1), jnp.int16(0))
            return tuple(accs)
        accs = lax.fori_loop(0, nkb, body, tuple(jnp.zeros((prow, qw), I16) for _ in range(n_acc)))
        tot = (accs[0].astype(I32) + accs[1].astype(I32)) + (accs[2].astype(I32) + accs[3].astype(I32))
        return jnp.sum(tot, axis=0, keepdims=True)

    def rep16(v):
        return jnp.broadcast_to(v, (prow, qw)).astype(I16)

    def kth16(ref, kth):
        def bit_body(it, prefix):
            cand_u = prefix | jnp.left_shift(jnp.int32(1), 15 - it)
            cand = rep16(cand_u - 32768)
            cnt = count16(ref, lambda blk: blk >= cand)
            return jnp.where(cnt >= kth, cand_u, prefix)
        return lax.fori_loop(0, 16, bit_body, jnp.zeros((1, qw), I32)) - 32768

    def count(pred):
        def body(kb, accs):
            accs = list(accs)
            for g in range(tk // SUBLANES):
                blk = skey[kb, g * SUBLANES:(g + 1) * SUBLANES, :]
                accs[g % n_acc] = accs[g % n_acc] + jnp.where(pred(kb, g, blk), 1, 0)
            return tuple(accs)
        accs = lax.fori_loop(0, nkb, body, tuple(jnp.zeros((SUBLANES, qw), I32) for _ in range(n_acc)))
        tot = (accs[0] + accs[1]) + (accs[2] + accs[3])
        return jnp.sum(tot, axis=0, keepdims=True)

    def rep8(v):
        return jnp.broadcast_to(v, (SUBLANES, qw))

    p_hi = kth16(skh, ntop)
    hi16 = rep16(p_hi)
    above = count16(skh, lambda blk: blk > hi16)

    def low_body(kb, carry):
        for g in range(tk // prow):
            rows = slice(g * prow, (g + 1) * prow)
            skl[kb, rows, :] = jnp.where(skh[kb, rows, :] == hi16, skl[kb, rows, :], jnp.int16(-32768))
        return carry

    lax.fori_loop(0, nkb, low_body, 0)
    p_lo = kth16(skl, ntop - above)
    tau = p_hi * 65536 + (p_lo + 32768)
    tau8 = rep8(tau)

    cnt_gt = count(lambda kb, g, blk: blk > tau8)
    cnt_ge = count(lambda kb, g, blk: blk >= tau8)
    need = ntop - cnt_gt
    key_ninf = jnp.int32(0x7FFFFFFF) ^ jnp.int32(-8388608)
    finite = tau != key_ninf
    tie_rows = ((cnt_ge - cnt_gt) > need) & finite & (lane < tq)

    @pl.when(jnp.max(jnp.where(tie_rows, 1, 0)) > 0)
    def _():
        need8 = rep8(need)

        def xbit(it, xlim):
            cand8 = xlim | jnp.left_shift(jnp.int32(1), idx_bits - 1 - it)
            cnt = count(lambda kb, g, blk: (blk == tau8) & ((kb * tk + g * SUBLANES + sub8) < cand8))
            return jnp.where(rep8(cnt) <= need8, cand8, xlim)
        xlim8 = lax.fori_loop(0, idx_bits, xbit, jnp.zeros((SUBLANES, qw), I32))

        def demote(kb, carry):
            for g in range(tk // SUBLANES):
                rows = slice(g * SUBLANES, (g + 1) * SUBLANES)
                key = skey[kb, rows, :]
                late = (key == tau8) & ((kb * tk + g * SUBLANES + sub8) >= xlim8)
                skey[kb, rows, :] = jnp.where(late, key - 1, key)
            return carry

        lax.fori_loop(0, nkb, demote, 0)

    m_s[...] = jnp.full(m_s.shape, NEG_INF, F32)
    acc_s[...] = jnp.zeros(acc_s.shape, F32)
    vrows = B_HEAD_DIM + PACKED_ROWS
    ones_rows = jnp.ones((PACKED_ROWS, tk), BF16)

    tau_ge8 = rep8(jnp.where(finite, tau, tau + 1))

    def attend(kbs, near):
        nblk = len(kbs)
        tiles = [nblk - 1 - i for i in range(nblk)] if near else None
        alphas = {}
        for i, kb in enumerate(kbs):
            for g in range(tk // SUBLANES):
                rows = slice(g * SUBLANES, (g + 1) * SUBLANES)
                madd_s[i, rows, :] = jnp.where(skey[kb, rows, :] >= tau_ge8, 0.0, NEG_INF)

        def logits(h):
            for i, kb in enumerate(kbs):
                kn = k_ref[0, h // groups, pl.ds(pl.multiple_of(kb * tk, tk), tk), :]
                bias = madd_s[i] if tiles is None else madd_s[i] + near_ref[tiles[i], h]
                s_s[i, h] = (_dot_t(kn, qh_s[h]) + bias).astype(BF16)

        def softmax(h):
            tiles16 = [s_s[i, h, r * prow:(r + 1) * prow, :] for i in range(nblk) for r in range(tk // prow)]
            mx = tiles16[:n_acc]
            for j, t in enumerate(tiles16[n_acc:]):
                mx[j % n_acc] = jnp.maximum(mx[j % n_acc], t)
            mx = jnp.maximum(jnp.maximum(mx[0], mx[1]), jnp.maximum(mx[2], mx[3]))
            m_cur = jnp.max(mx.astype(F32), axis=0, keepdims=True)
            c_h = jnp.zeros((1, qw), F32) if near else far_ref[h, 0:1, 0:qw]
            m_prev = m_s[h:h + 1, :]
            shift = (jnp.maximum(m_prev, m_cur + c_h) - c_h).astype(BF16)
            m_new = shift.astype(F32) + c_h
            m_s[h:h + 1, :] = m_new
            alphas[h] = jnp.exp2(m_prev - m_new)
            shift = jnp.broadcast_to(shift, (prow, qw))
            for i in range(nblk):
                for r in range(tk // prow):
                    rows = slice(r * prow, (r + 1) * prow)
                    p_s[i, h, rows, :] = jnp.exp2(s_s[i, h, rows, :] - shift)

        def values(h):
            hs = slice(h * vrows, (h + 1) * vrows)
            pv = None
            for i in range(nblk):
                vt1 = jnp.concatenate([vt_ref[0, h // groups, kbs[i]], ones_rows], axis=0)
                pv = _dot(vt1, p_s[i, h]) if pv is None else pv + _dot(vt1, p_s[i, h])
            acc_s[hs, :] = acc_s[hs, :] * alphas[h] + pv

        for phase in (logits, softmax, values):
            for h in range(B_HEADS):
                phase(h)

    nfar = nkb - 2

    def far_pair(j, carry):
        attend([2 * j, 2 * j + 1], False)
        return carry

    lax.fori_loop(0, nfar // 2, far_pair, 0)

    @pl.when((nfar > 0) & (nfar % 2 == 1))
    def _():
        attend([nfar - 1], False)

    @pl.when(nkb >= 2)
    def _():
        attend([nkb - 2, last], True)

    @pl.when(nkb < 2)
    def _():
        attend([last], True)

    outs = []
    for h in range(B_HEADS):
        r0 = h * vrows
        outs.append(acc_s[r0:r0 + B_HEAD_DIM, :] / acc_s[r0 + B_HEAD_DIM:r0 + B_HEAD_DIM + 1, :])
    o = jnp.concatenate(outs, axis=0).T
    o_ref[...] = o[0:tq, :].astype(BF16)


def _dsa(q, iq, ikw, k_att, vt_att, ik_att, near, far, tq, past, length, ntop):
    tk = KEY_BLOCK
    nb, t_len, _ = q.shape
    lp = k_att.shape[2]
    vrows = B_HEAD_DIM + PACKED_ROWS
    nq = B_HEADS * B_HEAD_DIM
    ni = IDX_HEADS * IDX_DIM
    qw = max(tq, LANES)
    assert past % tk == 0 and (tq == tk or t_len == tq) and tq <= tk and lp % tk == 0
    idx_bits = lp.bit_length()

    def qblk(c):
        return pl.BlockSpec((None, tq, c), lambda b, i: (b, i, 0))

    kern = functools.partial(_dsa_kernel, tq=tq, qw=qw, tk=tk, past=past, length=length, ntop=ntop,
                             idx_bits=idx_bits)
    return pl.pallas_call(
        kern,
        grid=(nb, t_len // tq),
        in_specs=[qblk(nq), qblk(ni), qblk(LANES),
                  pl.BlockSpec((1, B_KV_HEADS, lp, B_HEAD_DIM), lambda b, i: (b, 0, 0, 0)),
                  pl.BlockSpec((1, B_KV_HEADS, lp // tk, B_HEAD_DIM, tk), lambda b, i: (b, 0, 0, 0, 0)),
                  pl.BlockSpec((1, lp, IDX_DIM), lambda b, i: (b, 0, 0)),
                  pl.BlockSpec((2, B_HEADS, tk, qw), lambda b, i: (0, 0, 0, 0)),
                  pl.BlockSpec((B_HEADS, SUBLANES, tk), lambda b, i: (0, 0, 0))],
        out_specs=qblk(nq),
        out_shape=jax.ShapeDtypeStruct((nb, t_len, nq), BF16),
        scratch_shapes=[pltpu.VMEM((lp // tk, tk, qw), I32),
                        pltpu.VMEM((lp // tk, tk, qw), I16),
                        pltpu.VMEM((lp // tk, tk, qw), I16),
                        pltpu.VMEM((2, tk, qw), F32),
                        pltpu.VMEM((2, B_HEADS, tk, qw), BF16),
                        pltpu.VMEM((2, B_HEADS, tk, qw), BF16),
                        pltpu.VMEM((B_HEADS, qw, B_HEAD_DIM), BF16),
                        pltpu.VMEM((IDX_HEADS, qw, IDX_DIM), BF16),
                        pltpu.VMEM((qw, LANES), F32),
                        pltpu.VMEM((IDX_HEADS, qw), F32),
                        pltpu.VMEM((B_HEADS, qw), F32),
                        pltpu.VMEM((B_HEADS * vrows, qw), F32)],
        compiler_params=_params(("arbitrary", "arbitrary")),
        name="dsa",
    )(q, iq, ikw, k_att, vt_att, ik_att, near, far)


def _xq_tail(x1, gx_ref, wxq_ref, qnx_ref, cs, qx_ref):
    xn = _rms(x1, gx_ref[...]).astype(BF16)
    qx = _head_rms(_dot(xn, wxq_ref[...]), X_HEAD_DIM, qnx_ref[...])
    _tm_to_seq(qx * (X_HEAD_DIM ** -0.5), cs, [(qx_ref, D_MODEL)])


def _even_out_kernel(x_ref, ya_ref, yb_ref, wo_ref, gx_ref, wxq_ref, qnx_ref, x1_ref, qx_ref, ys, cs):
    yb = _seq_to_tm(yb_ref, ys).astype(BF16)
    x1 = x_ref[...] + _dot(ya_ref[...], wo_ref[0:A_WIDTH, :]) + _dot(yb, wo_ref[A_WIDTH:, :])
    x1_ref[...] = x1
    _xq_tail(x1, gx_ref, wxq_ref, qnx_ref, cs, qx_ref)


def _even_out(x, ya, yb, w_out, gx, w_xq, qnx, tt):
    nb, t_len, nyb = yb.shape
    tm = tt * nb
    rows = x.shape[0]
    d = D_MODEL

    def row(c):
        return pl.BlockSpec((tm, c), lambda i: (i, 0))

    def seq(c):
        return pl.BlockSpec((nb, tt, c), lambda i: (0, i, 0))

    return pl.pallas_call(
        _even_out_kernel,
        grid=(rows // tm,),
        in_specs=[row(d), row(A_WIDTH), seq(nyb), _const_spec((d, d)),
                  _const_spec((1, d)), _const_spec((d, d)), _const_spec((1, d))],
        out_specs=[row(d), seq(d)],
        out_shape=[jax.ShapeDtypeStruct((rows, d), F32), jax.ShapeDtypeStruct((nb, t_len, d), BF16)],
        scratch_shapes=[pltpu.VMEM((nyb // LANES, tm, LANES), F32), pltpu.VMEM((d // LANES, tm, LANES), F32)],
        compiler_params=_params(("arbitrary",)),
        name="even_out",
    )(x, ya, yb, w_out, gx, w_xq, qnx)


def _xattn_kernel(q_ref, mk_ref, mv_ref, o_ref):
    for h in range(X_HEADS):
        sl = slice(h * X_HEAD_DIM, (h + 1) * X_HEAD_DIM)
        s = _dot_t(q_ref[:, sl], mk_ref[0, :, sl])
        p = jnp.exp(s - jnp.max(s, axis=1, keepdims=True))
        o = _dot(p.astype(BF16), mv_ref[0, :, sl]) / jnp.sum(p, axis=1, keepdims=True)
        o_ref[:, sl] = o.astype(BF16)


def _xattn(qx, mk, mv, tq):
    nb, t_len, d = qx.shape
    m = mk.shape[1]
    return pl.pallas_call(
        _xattn_kernel,
        grid=(nb, t_len // tq),
        in_specs=[pl.BlockSpec((None, tq, d), lambda b, i: (b, i, 0)),
                  pl.BlockSpec((1, m, d), lambda b, i: (b, 0, 0)),
                  pl.BlockSpec((1, m, d), lambda b, i: (b, 0, 0))],
        out_specs=pl.BlockSpec((None, tq, d), lambda b, i: (b, i, 0)),
        out_shape=jax.ShapeDtypeStruct((nb, t_len, d), BF16),
        compiler_params=_params(("arbitrary", "arbitrary")),
        name="mem_attn",
    )(qx, mk, mv)


def _ffn_kernel(x_ref, o_ref, wxo_ref, g_ref, wup_ref, cw_ref, cb_ref, wdn_ref, hist_ref,
                y_ref, fh_ref, gbuf, cs, *, chunk, seq_out):
    tm = x_ref.shape[0]
    nh = (F_CONV - 1) * SUBLANES

    @pl.when(pl.program_id(0) == 0)
    def _():
        gbuf[0:nh, :] = hist_ref[...]

    x2 = x_ref[...] + _dot(_seq_to_tm(o_ref, cs).astype(BF16), wxo_ref[...])
    xn = _rms(x2, g_ref[...]).astype(BF16)
    acc = jnp.zeros((tm, D_MODEL), F32)
    for c0 in range(0, D_FF, chunk):
        c1 = min(c0 + chunk, D_FF)
        val = _dot(xn, wup_ref[:, c0:c1])
        gate = _dot(xn, wup_ref[:, D_FF + c0:D_FF + c1])
        gbuf[nh:nh + tm, c0:c1] = gate
        conv = gate * cw_ref[F_CONV - 1:F_CONV, c0:c1]
        for i in range(F_CONV - 1):
            conv = conv + gbuf[i * SUBLANES:i * SUBLANES + tm, c0:c1] * cw_ref[i:i + 1, c0:c1]
        conv = conv + cb_ref[:, c0:c1]
        act = (jax.nn.gelu(conv) * val).astype(BF16)
        acc = acc + _dot(act, wdn_ref[c0:c1, :])
    tail = gbuf[tm:tm + nh, :]
    gbuf[0:nh, :] = tail
    fh_ref[...] = tail
    if seq_out:
        _tm_to_seq(x2 + acc, cs, [(y_ref, D_MODEL)])
    else:
        y_ref[...] = x2 + acc


def _ffn(x, o, w_xo, g, w_up, conv_w, conv_b, w_down, hist, tt, seq_out):
    nb, t_len, d = o.shape
    tm = tt * nb
    rows = x.shape[0]
    nh = (F_CONV - 1) * SUBLANES

    def row(c):
        return pl.BlockSpec((tm, c), lambda i: (i, 0))

    def seq(c):
        return pl.BlockSpec((nb, tt, c), lambda i: (0, i, 0))

    y_shape = jax.ShapeDtypeStruct((nb, t_len, d) if seq_out else (rows, d), F32)
    return pl.pallas_call(
        functools.partial(_ffn_kernel, chunk=MXU_DIM, seq_out=seq_out),
        grid=(rows // tm,),
        in_specs=[row(d), seq(d), _const_spec((d, d)), _const_spec((1, d)), _const_spec((d, 2 * D_FF)),
                  _const_spec((F_CONV, D_FF)), _const_spec((1, D_FF)), _const_spec((D_FF, d)),
                  _const_spec((nh, D_FF))],
        out_specs=[seq(d) if seq_out else row(d), pl.BlockSpec((nh, D_FF), lambda i: (0, 0))],
        out_shape=[y_shape, jax.ShapeDtypeStruct((nh, D_FF), F32)],
        scratch_shapes=[pltpu.VMEM((tm + nh, D_FF), F32), pltpu.VMEM((d // LANES, tm, LANES), F32)],
        compiler_params=_params(("arbitrary",)),
        name="ffn",
    )(x, o, w_xo, g, w_up, conv_w, conv_b, w_down, hist)


def _odd_kernel(x_ref, g_ref, win_ref, cw_ref, cb_ref, wai_ref, ba_ref, bi_ref, lam_ref,
                wo_ref, hist_ref, h0_ref, gx_ref, wxq_ref, qnx_ref,
                x1_ref, qx_ref, ch_ref, hl_ref, xbuf, a_s, b_s, h_s, cs, *, stream_start):
    tm = x_ref.shape[0]
    nh = (C_CONV - 1) * SUBLANES
    first = pl.program_id(0) == 0

    @pl.when(first)
    def _():
        xbuf[0:nh, :] = hist_ref[...]
        h_s[...] = h0_ref[...]

    x = x_ref[...]
    xn = _rms(x, g_ref[...]).astype(BF16)
    xr_in = _dot(xn, win_ref[:, RNN_WIDTH:])
    xbuf[nh:nh + tm, :] = xr_in
    xr = _conv_taps(xbuf, xr_in, cw_ref, C_CONV, tm) + cb_ref[...]
    tail = xbuf[tm:tm + nh, :]
    xbuf[0:nh, :] = tail
    ch_ref[...] = tail

    xrb = xr.astype(BF16)
    lam = -lam_ref[...]
    sp = jnp.maximum(lam, 0.0) + jnp.log1p(jnp.exp(-jnp.abs(lam)))
    rows = lax.broadcasted_iota(I32, (tm, RNN_BLOCK), 0)
    for n in range(RNN_BLOCKS):
        sl = slice(n * RNN_BLOCK, (n + 1) * RNN_BLOCK)
        gates = _dot(xrb[:, sl], wai_ref[n])
        r = jax.nn.sigmoid(gates[:, :RNN_BLOCK] + ba_ref[:, sl])
        ig = jax.nn.sigmoid(gates[:, RNN_BLOCK:] + bi_ref[:, sl])
        log_a = -RG_C * r * sp[:, sl]
        a = jnp.exp(log_a)
        m2 = jnp.tanh(-log_a) * (1.0 + a * a)
        mult = jnp.where(m2 > 0.0, m2 * lax.rsqrt(m2), 0.0)
        if stream_start:
            mult = jnp.where(first & (rows < SUBLANES), 1.0, mult)
        a_s[:, sl] = a
        b_s[:, sl] = mult * ig * xr[:, sl]

    def step(t, h):
        r0 = pl.multiple_of(t * SUBLANES, SUBLANES)
        h = a_s[pl.ds(r0, SUBLANES), :] * h + b_s[pl.ds(r0, SUBLANES), :]
        b_s[pl.ds(r0, SUBLANES), :] = h
        return h

    h = lax.fori_loop(0, tm // SUBLANES, step, h_s[...], unroll=True)
    h_s[...] = h
    hl_ref[...] = h

    gate = _dot(xn, win_ref[:, :RNN_WIDTH])
    act = (jax.nn.gelu(gate) * b_s[...]).astype(BF16)
    x1 = x + _dot(act, wo_ref[...])
    x1_ref[...] = x1
    _xq_tail(x1, gx_ref, wxq_ref, qnx_ref, cs, qx_ref)


def _odd(x, g, w_in, conv_w, conv_b, w_ai, b_a, b_i, lam, w_out, hist, h0, gx, w_xq, qnx, tt,
         stream_start):
    nb = SUBLANES
    tm = tt * nb
    rows = x.shape[0]
    t_len = rows // nb
    d = D_MODEL
    r = RNN_WIDTH
    nh = (C_CONV - 1) * SUBLANES

    def row(c):
        return pl.BlockSpec((tm, c), lambda i: (i, 0))

    blk = (RNN_BLOCKS, RNN_BLOCK, 2 * RNN_BLOCK)
    return pl.pallas_call(
        functools.partial(_odd_kernel, stream_start=stream_start),
        grid=(rows // tm,),
        in_specs=[row(d), _const_spec((1, d)), _const_spec((d, 2 * r)), _const_spec((C_CONV, r)),
                  _const_spec((1, r)), _const_spec(blk), _const_spec((1, r)),
                  _const_spec((1, r)), _const_spec((1, r)), _const_spec((r, d)), _const_spec((nh, r)),
                  _const_spec((SUBLANES, r)), _const_spec((1, d)), _const_spec((d, d)), _const_spec((1, d))],
        out_specs=[row(d), pl.BlockSpec((nb, tt, d), lambda i: (0, i, 0)), pl.BlockSpec((nh, r), lambda i: (0, 0)),
                   pl.BlockSpec((SUBLANES, r), lambda i: (0, 0))],
        out_shape=[jax.ShapeDtypeStruct((rows, d), F32), jax.ShapeDtypeStruct((nb, t_len, d), BF16),
                   jax.ShapeDtypeStruct((nh, r), F32), jax.ShapeDtypeStruct((SUBLANES, r), F32)],
        scratch_shapes=[pltpu.VMEM((tm + nh, r), F32), pltpu.VMEM((tm, r), F32), pltpu.VMEM((tm, r), F32),
                        pltpu.VMEM((SUBLANES, r), F32), pltpu.VMEM((d // LANES, tm, LANES), F32)],
        compiler_params=_params(("arbitrary",)),
        name="odd_mixer",
    )(x, g, w_in, conv_w, conv_b, w_ai, b_a, b_i, lam, w_out, hist, h0, gx, w_xq, qnx)


def _to_tm(a):
    return jnp.transpose(a, (1, 0, 2)).reshape(a.shape[1] * a.shape[0], a.shape[2])


def _from_tm(a, w):
    return jnp.transpose(a.reshape(w, SUBLANES, a.shape[1]), (1, 0, 2))


def _trunk(x, st, mem_k, mem_v, p, bias, tt, tq_dsa, tq_x):
    nb, t_len, d = x.shape
    assert nb == SUBLANES
    past = 0 if st is None else st["b_k"].shape[2]
    length = past + t_len
    ntop = min(TOPK_MAX, length // 4)
    near, far = bias
    xt = None
    out = {}

    def hist(name, l, width, c):
        if st is None:
            return jnp.zeros(((width - 1) * nb, c), F32)
        return _to_tm(st[name][l])

    for l in range(DEPTH):
        if l % 2 == 0:
            e = l // 2
            assert l == 0, "the per-sequence input is converted by the first layer's kernel"
            xt, ya, q, iq, ikw, k, v, uh = _even_in(
                x, p["g_mix"][l], p["w_in_even"][e], p["a_conv_w"][e], p["b_q_norm"][e], p["b_k_norm"][e],
                hist("a_conv", e, A_CONV, A_WIDTH), tt)
            k_new = k.reshape(nb, t_len, B_KV_HEADS, B_HEAD_DIM)
            v_new = v.reshape(nb, t_len, B_KV_HEADS, B_HEAD_DIM)
            ik_new = ikw[:, :, :IDX_DIM]
            k_all, v_all, ik_all = k_new, v_new, ik_new
            if st is not None:
                k_all = jnp.concatenate([st["b_k"][e], k_new], axis=1)
                v_all = jnp.concatenate([st["b_v"][e], v_new], axis=1)
                ik_all = jnp.concatenate([st["b_kidx"][e], ik_new], axis=1)
            lp = -(-length // KEY_BLOCK) * KEY_BLOCK
            padl = lp - length
            k_att = jnp.pad(jnp.transpose(k_all, (0, 2, 1, 3)).astype(BF16), ((0, 0), (0, 0), (0, padl), (0, 0)))
            v_att = jnp.pad(jnp.transpose(v_all, (0, 2, 1, 3)).astype(BF16), ((0, 0), (0, 0), (0, padl), (0, 0)))
            vt_att = jnp.transpose(v_att.reshape(nb, B_KV_HEADS, lp // KEY_BLOCK, KEY_BLOCK, B_HEAD_DIM),
                                   (0, 1, 2, 4, 3))
            ik_att = jnp.pad(ik_all.astype(BF16), ((0, 0), (0, padl), (0, 0)))
            yb = _dsa(q, iq, ikw, k_att, vt_att, ik_att, near, far, tq_dsa, past, length, ntop)
            x1, qx = _even_out(xt, ya, yb, p["w_out_even"][e], p["g_x"][l], p["w_xq"][l], p["x_q_norm"][l], tt)
            out.setdefault("a_conv", []).append(_from_tm(uh, A_CONV - 1))
            out.setdefault("b_k", []).append(k_new)
            out.setdefault("b_v", []).append(v_new)
            out.setdefault("b_kidx", []).append(ik_new)
        else:
            o = l // 2
            h0 = jnp.zeros((nb, RNN_WIDTH), F32) if st is None else st["c_h"][o]
            x1, qx, ch, hl = _odd(
                xt, p["g_mix"][l], p["w_in_odd"][o], p["c_conv_w"][o], p["c_conv_b"][o], p["c_w_ai"][o],
                p["c_b_a"][o], p["c_b_i"][o], p["c_lambda"][o], p["w_out_odd"][o],
                hist("c_conv", o, C_CONV, RNN_WIDTH), h0, p["g_x"][l], p["w_xq"][l], p["x_q_norm"][l], tt,
                stream_start=(past == 0))
            out.setdefault("c_conv", []).append(_from_tm(ch, C_CONV - 1))
            out.setdefault("c_h", []).append(hl)
        xo = _xattn(qx, mem_k[l], mem_v[l], tq_x)
        xt, fh = _ffn(x1, xo, p["w_xo"][l], p["g_ffn"][l], p["w_up"][l], p["f_conv_w"][l], p["f_conv_b"][l],
                      p["w_down"][l], hist("f_conv", l, F_CONV, D_FF), tt, seq_out=(l == DEPTH - 1))
        out.setdefault("f_conv", []).append(_from_tm(fh, F_CONV - 1))
    return xt, {name: jnp.stack(v) for name, v in out.items()}


def kernel(x_prompt, x_sample, cache_b_k, cache_b_v, cache_b_kidx, state_a_conv, state_c_conv, state_c_h, state_ffn_conv, cache_mem_k, cache_mem_v, mem_prompt, rel_table, g_mix, w_in_even, a_conv_w, b_q_norm, b_k_norm, w_out_even, w_in_odd, c_conv_w, c_conv_b, c_w_a, c_b_a, c_w_i, c_b_i, c_lambda, w_out_odd, g_mem, g_x, w_xq, w_xk, w_xv, x_q_norm, x_k_norm, w_xo, g_ffn, w_up, f_conv_w, f_conv_b, w_down):
    d = D_MODEL
    bp, t_p, _ = x_prompt.shape
    m = mem_prompt.shape[1]

    def rowvec(a):
        return a.reshape(a.shape[0], 1, a.shape[-1])

    def mxu(a):
        return [a[l].astype(BF16) for l in range(a.shape[0])]

    p = {
        "g_mix": rowvec(g_mix), "g_x": rowvec(g_x), "g_ffn": rowvec(g_ffn),
        "w_in_even": [jnp.pad(w, ((0, 0), (0, EVEN_IN_PAD - EVEN_IN))) for w in mxu(w_in_even)],
        "a_conv_w": a_conv_w,
        "b_q_norm": rowvec(jnp.tile(b_q_norm, (1, B_HEADS))),
        "b_k_norm": rowvec(jnp.tile(b_k_norm, (1, B_KV_HEADS))),
        "w_out_even": mxu(w_out_even),
        "w_in_odd": mxu(w_in_odd), "c_conv_w": c_conv_w, "c_conv_b": rowvec(c_conv_b),
        "c_w_ai": mxu(jnp.concatenate([c_w_a, c_w_i], axis=-1)), "c_b_a": rowvec(c_b_a), "c_b_i": rowvec(c_b_i),
        "c_lambda": rowvec(c_lambda), "w_out_odd": mxu(w_out_odd),
        "w_xq": mxu(w_xq), "x_q_norm": rowvec(jnp.tile(x_q_norm, (1, X_HEADS))),
        "w_xo": mxu(w_xo), "w_up": mxu(w_up), "f_conv_w": f_conv_w,
        "f_conv_b": rowvec(f_conv_b), "w_down": mxu(w_down),
    }
    bias = _bias_tiles(rel_table, KEY_BLOCK)

    mk, mv = _mem_kv(mem_prompt.reshape(bp * m, d), g_mem, w_xk, x_k_norm, w_xv)
    p_mem_k = mk.reshape(DEPTH, bp, m, X_HEADS, X_HEAD_DIM)
    p_mem_v = mv.reshape(DEPTH, bp, m, X_HEADS, X_HEAD_DIM)
    y_prompt, new_p = _trunk(x_prompt, None, mk.reshape(DEPTH, bp, m, d).astype(BF16),
                             mv.reshape(DEPTH, bp, m, d).astype(BF16), p, bias,
                             tt=64, tq_dsa=KEY_BLOCK, tq_x=512)

    bs, t_s, _ = x_sample.shape
    st_s = {"b_k": cache_b_k, "b_v": cache_b_v, "b_kidx": cache_b_kidx, "a_conv": state_a_conv,
            "c_conv": state_c_conv, "c_h": state_c_h, "f_conv": state_ffn_conv}
    ms = cache_mem_k.shape[2]
    y_sample, new_s = _trunk(x_sample, st_s, cache_mem_k.reshape(DEPTH, bs, ms, d).astype(BF16),
                             cache_mem_v.reshape(DEPTH, bs, ms, d).astype(BF16), p, bias,
                             tt=t_s, tq_dsa=t_s, tq_x=t_s)
    return (y_prompt, y_sample,
            new_p["b_k"], new_p["b_v"], new_p["b_kidx"], new_p["a_conv"], new_p["c_conv"],
            new_p["c_h"], new_p["f_conv"], p_mem_k, p_mem_v,
            new_s["b_k"], new_s["b_v"], new_s["b_kidx"], new_s["a_conv"], new_s["c_conv"],
            new_s["c_h"], new_s["f_conv"])
```

```python
import functools
import math

import jax
import jax.numpy as jnp
from jax import lax
from jax.experimental import pallas as pl
from jax.experimental.pallas import tpu as pltpu

F32 = jnp.float32
BF16 = jnp.bfloat16
I32 = jnp.int32
I16 = jnp.int16

D_MODEL = 1024
DEPTH = 2
CHUNK = 64
EPS = 1e-6
NEG_INF = -1e30
LOG2E = math.log2(math.e)
A_WIDTH = 512
A_CONV = 3
B_HEADS = 8
B_KV_HEADS = 2
B_HEAD_DIM = 64
IDX_HEADS = 8
IDX_DIM = 32
TOPK_MAX = 256
REL_BUCKETS = 32
REL_MAX_DIST = 128
RNN_WIDTH = 1024
RNN_BLOCKS = 8
RNN_BLOCK = 128
C_CONV = 4
RG_C = 8.0
X_HEADS = 4
X_HEAD_DIM = 256
D_FF = 2816
F_CONV = 3
EVEN_IN = 2600

SUBLANES = 8
LANES = 128
MXU_DIM = 256
PACKED_ROWS = 16
VMEM_LIMIT = 56 * 1024 * 1024

EVEN_IN_PAD = 2688
KEY_BLOCK = 256
INT_MIN = -2147483648


def _params(sem, vmem=VMEM_LIMIT):
    return pltpu.CompilerParams(dimension_semantics=sem, vmem_limit_bytes=vmem)


def _const_spec(shape):
    nd = len(shape)
    return pl.BlockSpec(shape, lambda *_: (0,) * nd, pipeline_mode=pl.Buffered(1))


def _rms(x, g):
    ms = jnp.mean(x * x, axis=-1, keepdims=True)
    return x * lax.rsqrt(ms + EPS) * g


def _head_rms(x, hd, gain):
    m, c = x.shape
    s = x * x
    parts = []
    if hd >= LANES:
        for h in range(c // hd):
            ms = jnp.mean(s[:, h * hd:(h + 1) * hd], axis=-1, keepdims=True)
            parts.append(x[:, h * hd:(h + 1) * hd] * lax.rsqrt(ms + EPS))
    else:
        lane = lax.broadcasted_iota(I32, (m, LANES), 1)
        for j in range(c // LANES):
            sj = s[:, j * LANES:(j + 1) * LANES]
            inv = jnp.zeros((m, LANES), F32)
            for k in range(LANES // hd):
                msk = (lane >= k * hd) & (lane < (k + 1) * hd)
                ms = jnp.sum(jnp.where(msk, sj, 0.0), axis=-1, keepdims=True) * (1.0 / hd)
                inv = jnp.where(msk, lax.rsqrt(ms + EPS), inv)
            parts.append(x[:, j * LANES:(j + 1) * LANES] * inv)
    y = parts[0] if len(parts) == 1 else jnp.concatenate(parts, axis=-1)
    return y * gain


def _dot(a, b):
    return jnp.dot(a, b, preferred_element_type=F32)


def _dot_t(a, b):
    return lax.dot_general(a, b, (((1,), (1,)), ((), ())), preferred_element_type=F32)


def _conv_taps(buf, cur, w_ref, width, tm):
    y = cur * w_ref[width - 1:width, :]
    for i in range(width - 1):
        y = y + buf[i * SUBLANES:i * SUBLANES + tm, :] * w_ref[i:i + 1, :]
    return y


def _seq_to_tm(src_ref, scr):
    nb, tt, c = src_ref.shape
    for b in range(nb):
        for j in range(c // LANES):
            scr[j, pl.ds(b, tt, stride=nb), :] = src_ref[b, :, j * LANES:(j + 1) * LANES].astype(F32)
    return jnp.concatenate([scr[j] for j in range(c // LANES)], axis=-1)


def _tm_to_seq(val, scr, dst_refs):
    tm, c = val.shape
    tt = tm // SUBLANES
    for j in range(c // LANES):
        scr[j] = val[:, j * LANES:(j + 1) * LANES]
    for b in range(SUBLANES):
        j0 = 0
        for ref, ci in dst_refs:
            nj = ci // LANES
            parts = [scr[j0 + j, pl.ds(b, tt, stride=SUBLANES), :] for j in range(nj)]
            ref[b] = (parts[0] if nj == 1 else jnp.concatenate(parts, axis=-1)).astype(ref.dtype)
            j0 += nj


def _memkv_kernel(mem_ref, g_ref, wk_ref, kn_ref, wv_ref, k_ref, v_ref):
    hm = _rms(mem_ref[...], g_ref[0]).astype(BF16)
    k_ref[0] = _head_rms(_dot(hm, wk_ref[0]), X_HEAD_DIM, kn_ref[0])
    v_ref[0] = _dot(hm, wv_ref[0])


def _mem_kv(mem, g_mem, w_xk, x_k_norm, w_xv):
    rows = mem.shape[0]
    tm = min(512, rows)
    d = D_MODEL
    kn = jnp.tile(x_k_norm, (1, X_HEADS)).reshape(DEPTH, 1, d)
    out = jax.ShapeDtypeStruct((DEPTH, rows, d), F32)
    return pl.pallas_call(
        _memkv_kernel,
        grid=(DEPTH, rows // tm),
        in_specs=[
            pl.BlockSpec((tm, d), lambda l, i: (i, 0)),
            pl.BlockSpec((1, 1, d), lambda l, i: (l, 0, 0)),
            pl.BlockSpec((1, d, d), lambda l, i: (l, 0, 0)),
            pl.BlockSpec((1, 1, d), lambda l, i: (l, 0, 0)),
            pl.BlockSpec((1, d, d), lambda l, i: (l, 0, 0)),
        ],
        out_specs=[pl.BlockSpec((1, tm, d), lambda l, i: (l, i, 0))] * 2,
        out_shape=[out, out],
        compiler_params=_params(("arbitrary", "arbitrary")),
        name="mem_kv",
    )(mem, g_mem.reshape(DEPTH, 1, d), w_xk.astype(BF16), kn, w_xv.astype(BF16))


def _even_in_kernel(x_ref, g_ref, w_ref, cw_ref, qn_ref, kn_ref, hist_ref,
                    xt_ref, ya_ref, q_ref, iq_ref, ikw_ref, k_ref, v_ref, uh_ref, ubuf, xs, cs):
    tm = xt_ref.shape[0]
    nh = (A_CONV - 1) * SUBLANES

    @pl.when(pl.program_id(0) == 0)
    def _():
        ubuf[0:nh, :] = hist_ref[...]

    x = _seq_to_tm(x_ref, xs)
    xt_ref[...] = x
    xn = _rms(x, g_ref[...]).astype(BF16)

    def proj(a, b):
        return _dot(xn, w_ref[:, a:b])

    zc = proj(A_WIDTH, 3 * A_WIDTH)
    u = zc[:, :A_WIDTH] * zc[:, A_WIDTH:]
    ubuf[nh:nh + tm, :] = u
    conv = _conv_taps(ubuf, u, cw_ref, A_CONV, tm)
    ya_ref[...] = (proj(0, A_WIDTH) * conv).astype(BF16)
    tail = ubuf[tm:tm + nh, :]
    ubuf[0:nh, :] = tail
    uh_ref[...] = tail

    o = 3 * A_WIDTH
    nq = B_HEADS * B_HEAD_DIM
    nkv = B_KV_HEADS * B_HEAD_DIM
    ni = IDX_HEADS * IDX_DIM
    q = _head_rms(proj(o, o + nq), B_HEAD_DIM, qn_ref[...]) * (B_HEAD_DIM ** -0.5 * LOG2E)
    zkv = proj(o + nq, o + nq + 2 * nkv)
    k = _head_rms(zkv[:, :nkv], B_HEAD_DIM, kn_ref[...])
    zi = proj(o + nq + 2 * nkv, EVEN_IN_PAD)
    seq = jnp.concatenate([q, zi, k, zkv[:, nkv:]], axis=-1)
    _tm_to_seq(seq, cs, [(q_ref, nq), (iq_ref, ni), (ikw_ref, LANES), (k_ref, nkv), (v_ref, nkv)])


def _even_in(x, g, w_pad, conv_w, qn, kn, hist, tt):
    nb, t_len, d = x.shape
    tm = tt * nb
    rows = t_len * nb
    nh = (A_CONV - 1) * SUBLANES
    nq = B_HEADS * B_HEAD_DIM
    nkv = B_KV_HEADS * B_HEAD_DIM
    ni = IDX_HEADS * IDX_DIM
    nseq = nq + ni + LANES + 2 * nkv

    def row(c):
        return pl.BlockSpec((tm, c), lambda i: (i, 0))

    def seq(c):
        return pl.BlockSpec((nb, tt, c), lambda i: (0, i, 0))

    def seq_shape(c, dt):
        return jax.ShapeDtypeStruct((nb, t_len, c), dt)

    return pl.pallas_call(
        _even_in_kernel,
        grid=(t_len // tt,),
        in_specs=[seq(d), _const_spec((1, d)), _const_spec((d, EVEN_IN_PAD)), _const_spec((A_CONV, A_WIDTH)),
                  _const_spec((1, nq)), _const_spec((1, nkv)), _const_spec((nh, A_WIDTH))],
        out_specs=[row(d), row(A_WIDTH), seq(nq), seq(ni), seq(LANES), seq(nkv), seq(nkv),
                   pl.BlockSpec((nh, A_WIDTH), lambda i: (0, 0))],
        out_shape=[jax.ShapeDtypeStruct((rows, d), F32), jax.ShapeDtypeStruct((rows, A_WIDTH), BF16),
                   seq_shape(nq, BF16), seq_shape(ni, BF16), seq_shape(LANES, F32), seq_shape(nkv, F32),
                   seq_shape(nkv, F32), jax.ShapeDtypeStruct((nh, A_WIDTH), F32)],
        scratch_shapes=[pltpu.VMEM((tm + nh, A_WIDTH), F32), pltpu.VMEM((d // LANES, tm, LANES), F32),
                        pltpu.VMEM((nseq // LANES, tm, LANES), F32)],
        compiler_params=_params(("arbitrary",)),
        name="even_in",
    )(x, g, w_pad, conv_w, qn, kn, hist)


def _rel_bucket(rel):
    half = REL_BUCKETS // 2
    max_exact = half // 2
    n = -rel
    ret = jnp.where(n < 0, half, 0)
    n = jnp.abs(n)
    nf = jnp.maximum(n, 1).astype(F32)
    large = max_exact + (jnp.log(nf / max_exact) / math.log(REL_MAX_DIST / max_exact)
                         * (half - max_exact)).astype(I32)
    large = jnp.minimum(large, half - 1)
    return ret + jnp.where(n < max_exact, n, large)


def _bias_kernel(tab_ref, near_ref, far_ref):
    tk = near_ref.shape[-1]
    r = lax.broadcasted_iota(I32, (tk, tk), 0)
    c = lax.broadcasted_iota(I32, (tk, tk), 1)

    def lookup(bucket, h):
        def body(j, acc):
            return jnp.where(bucket == j, tab_ref[j, h], acc)
        return lax.fori_loop(0, REL_BUCKETS, body, jnp.zeros(bucket.shape, F32))

    for blk in range(2):
        bucket = _rel_bucket(r - c - blk * tk)
        for h in range(B_HEADS):
            near_ref[blk, h] = lookup(bucket, h) * LOG2E
    bucket = _rel_bucket(-REL_MAX_DIST - c[0:SUBLANES, :])
    for h in range(B_HEADS):
        far_ref[h] = lookup(bucket, h) * LOG2E


def _bias_tiles(rel_table, tk):
    return pl.pallas_call(
        _bias_kernel,
        in_specs=[pl.BlockSpec(memory_space=pltpu.SMEM)],
        out_shape=[jax.ShapeDtypeStruct((2, B_HEADS, tk, tk), F32),
                   jax.ShapeDtypeStruct((B_HEADS, SUBLANES, tk), F32)],
        name="rel_bias",
    )(rel_table)


def _dsa_kernel(q_ref, iq_ref, ikw_ref, k_ref, vt_ref, ik_ref, near_ref, far_ref, o_ref,
                skey, skh, skl, madd_s, s_s, p_s, qh_s, iqh_s, pad_s, w_s, m_s, acc_s,
                *, tq, qw, tk, past, length, ntop, idx_bits):
    i = pl.program_id(1)
    q0 = past + i * tq
    nkb = (q0 + tq + tk - 1) // tk
    last = nkb - 1
    ni = IDX_HEADS * IDX_DIM
    groups = B_HEADS // B_KV_HEADS
    lane = lax.broadcasted_iota(I32, (1, qw), 1)
    sub8 = lax.broadcasted_iota(I32, (SUBLANES, qw), 0)
    qchunk = jnp.right_shift(q0 + lane, 6)

    if tq != qw:
        qh_s[...] = jnp.zeros(qh_s.shape, BF16)
        iqh_s[...] = jnp.zeros(iqh_s.shape, BF16)
        pad_s[...] = jnp.zeros(pad_s.shape, F32)
    for h in range(B_HEADS):
        qh_s[h, 0:tq, :] = q_ref[:, B_HEAD_DIM * h:B_HEAD_DIM * (h + 1)]
    for h in range(IDX_HEADS):
        iqh_s[h, 0:tq, :] = iq_ref[:, IDX_DIM * h:IDX_DIM * (h + 1)]
    pad_s[0:tq, :] = ikw_ref[...]
    w_s[...] = pad_s[...].T[IDX_DIM:IDX_DIM + IDX_HEADS, :] * (ni ** -0.5)

    slab = tk

    def scores(kb, masked):
        for sl in range(tk // slab):
            ks = pl.multiple_of(kb * tk + sl * slab, slab)
            ikb = ik_ref[0, pl.ds(ks, slab), :]
            acc = jnp.zeros((slab, qw), F32)
            for h in range(IDX_HEADS):
                acc = acc + w_s[h:h + 1, :] * jnp.maximum(_dot_t(ikb, iqh_s[h]), 0.0)
            if masked:
                kpos = kb * tk + sl * slab + lax.broadcasted_iota(I32, (slab, qw), 0)
                vis = (jnp.right_shift(kpos, 6) <= qchunk) & (kpos < length)
                acc = jnp.where(vis, acc, -jnp.inf)
            bits = lax.bitcast_convert_type(acc, I32)
            key = jnp.where(bits < 0, bits ^ jnp.int32(0x7FFFFFFF), bits)
            rows = slice(sl * slab, (sl + 1) * slab)
            skey[kb, rows, :] = key
            skh[kb, rows, :] = jnp.right_shift(key, 16).astype(I16)
            skl[kb, rows, :] = ((key & 0xFFFF) - 32768).astype(I16)

    def score_pair(j, carry):
        scores(2 * j, False)
        scores(2 * j + 1, False)
        return carry

    lax.fori_loop(0, last // 2, score_pair, 0)

    @pl.when(last % 2 == 1)
    def _():
        scores(last - 1, False)

    scores(last, True)

    n_acc = 4
    prow = PACKED_ROWS

    @pl.when(nkb % 2 == 1)
    def _():
        skh[nkb] = jnp.full((tk, qw), -32768, I16)
        skl[nkb] = jnp.full((tk, qw), -32768, I16)

    def count16(ref, pred):
        def body(j, accs):
            accs = list(accs)
            for kb in (2 * j, 2 * j + 1):
                for g in range(tk // prow):
                    blk = ref[kb, g * prow:(g + 1) * prow, :]
                    accs[g % n_acc] = accs[g % n_acc] + jnp.where(pred(blk), jnp.int16(1), jnp.int16(0))
            return tuple(accs)
        accs = lax.fori_loop(0, (nkb + 1) // 2, body, tuple(jnp.zeros((prow, qw), I16) for _ in range(n_acc)))
        tot = (accs[0].astype(I32) + accs[1].astype(I32)) + (accs[2].astype(I32) + accs[3].astype(I32))
        return jnp.sum(tot, axis=0, keepdims=True)

    def rep16(v):
        return jnp.broadcast_to(v, (prow, qw)).astype(I16)

    def kth16(ref, kth):
        def bit_body(it, prefix):
            cand_u = prefix | jnp.left_shift(jnp.int32(1), 15 - it)
            cand = rep16(cand_u - 32768)
            cnt = count16(ref, lambda blk: blk >= cand)
            return jnp.where(cnt >= kth, cand_u, prefix)
        return lax.fori_loop(0, 16, bit_body, jnp.zeros((1, qw), I32)) - 32768

    def count(pred):
        def body(kb, accs):
            accs = list(accs)
            for g in range(tk // SUBLANES):
                blk = skey[kb, g * SUBLANES:(g + 1) * SUBLANES, :]
                accs[g % n_acc] = accs[g % n_acc] + jnp.where(pred(kb, g, blk), 1, 0)
            return tuple(accs)
        accs = lax.fori_loop(0, nkb, body, tuple(jnp.zeros((SUBLANES, qw), I32) for _ in range(n_acc)))
        tot = (accs[0] + accs[1]) + (accs[2] + accs[3])
        return jnp.sum(tot, axis=0, keepdims=True)

    def rep8(v):
        return jnp.broadcast_to(v, (SUBLANES, qw))

    p_hi = kth16(skh, ntop)
    hi16 = rep16(p_hi)
    above = count16(skh, lambda blk: blk > hi16)

    def low_body(kb, carry):
        for g in range(tk // prow):
            rows = slice(g * prow, (g + 1) * prow)
            skl[kb, rows, :] = jnp.where(skh[kb, rows, :] == hi16, skl[kb, rows, :], jnp.int16(-32768))
        return carry

    lax.fori_loop(0, nkb, low_body, 0)
    p_lo = kth16(skl, ntop - above)
    tau = p_hi * 65536 + (p_lo + 32768)
    tau8 = rep8(tau)

    cnt_ge = count(lambda kb, g, blk: blk >= tau8)
    key_ninf = jnp.int32(0x7FFFFFFF) ^ jnp.int32(-8388608)
    finite = tau != key_ninf
    tie_rows = (cnt_ge > ntop) & finite & (lane < tq)

    @pl.when(jnp.max(jnp.where(tie_rows, 1, 0)) > 0)
    def _():
        need8 = rep8(ntop - count(lambda kb, g, blk: blk > tau8))

        def xbit(it, xlim):
            cand8 = xlim | jnp.left_shift(jnp.int32(1), idx_bits - 1 - it)
            cnt = count(lambda kb, g, blk: (blk == tau8) & ((kb * tk + g * SUBLANES + sub8) < cand8))
            return jnp.where(rep8(cnt) <= need8, cand8, xlim)
        xlim8 = lax.fori_loop(0, idx_bits, xbit, jnp.zeros((SUBLANES, qw), I32))

        def demote(kb, carry):
            for g in range(tk // SUBLANES):
                rows = slice(g * SUBLANES, (g + 1) * SUBLANES)
                key = skey[kb, rows, :]
                late = (key == tau8) & ((kb * tk + g * SUBLANES + sub8) >= xlim8)
                skey[kb, rows, :] = jnp.where(late, key - 1, key)
            return carry

        lax.fori_loop(0, nkb, demote, 0)

    m_s[...] = jnp.full(m_s.shape, NEG_INF, F32)
    acc_s[...] = jnp.zeros(acc_s.shape, F32)
    vrows = B_HEAD_DIM + PACKED_ROWS
    ones_rows = jnp.ones((PACKED_ROWS, tk), BF16)

    tau_ge8 = rep8(jnp.where(finite, tau, tau + 1))

    def attend(kbs, near):
        nblk = len(kbs)
        tiles = [nblk - 1 - i for i in range(nblk)] if near else None
        alphas = {}
        for i, kb in enumerate(kbs):
            for g in range(tk // SUBLANES):
                rows = slice(g * SUBLANES, (g + 1) * SUBLANES)
                madd_s[i, rows, :] = jnp.where(skey[kb, rows, :] >= tau_ge8, 0.0, NEG_INF)

        def logits(h):
            for i, kb in enumerate(kbs):
                kn = k_ref[0, h // groups, pl.ds(pl.multiple_of(kb * tk, tk), tk), :]
                bias = madd_s[i] if tiles is None else madd_s[i] + near_ref[tiles[i], h]
                s_s[i, h] = (_dot_t(kn, qh_s[h]) + bias).astype(BF16)

        def softmax(h):
            tiles16 = [s_s[i, h, r * prow:(r + 1) * prow, :] for i in range(nblk) for r in range(tk // prow)]
            mx = tiles16[:n_acc]
            for j, t in enumerate(tiles16[n_acc:]):
                mx[j % n_acc] = jnp.maximum(mx[j % n_acc], t)
            mx = jnp.maximum(jnp.maximum(mx[0], mx[1]), jnp.maximum(mx[2], mx[3]))
            m_cur = jnp.max(mx.astype(F32), axis=0, keepdims=True)
            c_h = jnp.zeros((1, qw), F32) if near else far_ref[h, 0:1, 0:qw]
            m_prev = m_s[h:h + 1, :]
            shift = (jnp.maximum(m_prev, m_cur + c_h) - c_h).astype(BF16)
            m_new = shift.astype(F32) + c_h
            m_s[h:h + 1, :] = m_new
            alphas[h] = jnp.exp2(m_prev - m_new)
            shift = jnp.broadcast_to(shift, (prow, qw))
            for i in range(nblk):
                for r in range(tk // prow):
                    rows = slice(r * prow, (r + 1) * prow)
                    p_s[i, h, rows, :] = jnp.exp2(s_s[i, h, rows, :] - shift)

        def values(h):
            hs = slice(h * vrows, (h + 1) * vrows)
            pv = None
            for i in range(nblk):
                vt1 = jnp.concatenate([vt_ref[0, h // groups, kbs[i]], ones_rows], axis=0)
                pv = _dot(vt1, p_s[i, h]) if pv is None else pv + _dot(vt1, p_s[i, h])
            acc_s[hs, :] = acc_s[hs, :] * alphas[h] + pv

        for phase in (logits, softmax, values):
            for h in range(B_HEADS):
                phase(h)

    nfar = nkb - 2

    def far_pair(j, carry):
        attend([2 * j, 2 * j + 1], False)
        return carry

    lax.fori_loop(0, nfar // 2, far_pair, 0)

    @pl.when((nfar > 0) & (nfar % 2 == 1))
    def _():
        attend([nfar - 1], False)

    @pl.when(nkb >= 2)
    def _():
        attend([nkb - 2, last], True)

    @pl.when(nkb < 2)
    def _():
        attend([last], True)

    outs = []
    for h in range(B_HEADS):
        r0 = h * vrows
        outs.append(acc_s[r0:r0 + B_HEAD_DIM, :] / acc_s[r0 + B_HEAD_DIM:r0 + B_HEAD_DIM + 1, :])
    o = jnp.concatenate(outs, axis=0).T
    o_ref[...] = o[0:tq, :].astype(BF16)


def _dsa(q, iq, ikw, k_att, vt_att, ik_att, near, far, tq, past, length, ntop):
    tk = KEY_BLOCK
    nb, t_len, _ = q.shape
    lp = k_att.shape[2]
    vrows = B_HEAD_DIM + PACKED_ROWS
    nq = B_HEADS * B_HEAD_DIM
    ni = IDX_HEADS * IDX_DIM
    qw = max(tq, LANES)
    assert past % tk == 0 and (tq == tk or t_len == tq) and tq <= tk and lp % tk == 0
    idx_bits = lp.bit_length()

    def qblk(c):
        return pl.BlockSpec((None, tq, c), lambda b, i: (b, i, 0))

    kern = functools.partial(_dsa_kernel, tq=tq, qw=qw, tk=tk, past=past, length=length, ntop=ntop,
                             idx_bits=idx_bits)
    return pl.pallas_call(
        kern,
        grid=(nb, t_len // tq),
        in_specs=[qblk(nq), qblk(ni), qblk(LANES),
                  pl.BlockSpec((1, B_KV_HEADS, lp, B_HEAD_DIM), lambda b, i: (b, 0, 0, 0)),
                  pl.BlockSpec((1, B_KV_HEADS, lp // tk, B_HEAD_DIM, tk), lambda b, i: (b, 0, 0, 0, 0)),
                  pl.BlockSpec((1, lp, IDX_DIM), lambda b, i: (b, 0, 0)),
                  pl.BlockSpec((2, B_HEADS, tk, qw), lambda b, i: (0, 0, 0, 0)),
                  pl.BlockSpec((B_HEADS, SUBLANES, tk), lambda b, i: (0, 0, 0))],
        out_specs=qblk(nq),
        out_shape=jax.ShapeDtypeStruct((nb, t_len, nq), BF16),
        scratch_shapes=[pltpu.VMEM((lp // tk, tk, qw), I32),
                        pltpu.VMEM((lp // tk + 1, tk, qw), I16),
                        pltpu.VMEM((lp // tk + 1, tk, qw), I16),
                        pltpu.VMEM((2, tk, qw), F32),
                        pltpu.VMEM((2, B_HEADS, tk, qw), BF16),
                        pltpu.VMEM((2, B_HEADS, tk, qw), BF16),
                        pltpu.VMEM((B_HEADS, qw, B_HEAD_DIM), BF16),
                        pltpu.VMEM((IDX_HEADS, qw, IDX_DIM), BF16),
                        pltpu.VMEM((qw, LANES), F32),
                        pltpu.VMEM((IDX_HEADS, qw), F32),
                        pltpu.VMEM((B_HEADS, qw), F32),
                        pltpu.VMEM((B_HEADS * vrows, qw), F32)],
        compiler_params=_params(("arbitrary", "arbitrary")),
        name="dsa",
    )(q, iq, ikw, k_att, vt_att, ik_att, near, far)


def _xq_tail(x1, gx_ref, wxq_ref, qnx_ref, cs, qx_ref):
    xn = _rms(x1, gx_ref[...]).astype(BF16)
    qx = _head_rms(_dot(xn, wxq_ref[...]), X_HEAD_DIM, qnx_ref[...])
    _tm_to_seq(qx * (X_HEAD_DIM ** -0.5), cs, [(qx_ref, D_MODEL)])


def _even_out_kernel(x_ref, ya_ref, yb_ref, wo_ref, gx_ref, wxq_ref, qnx_ref, x1_ref, qx_ref, ys, cs):
    yb = _seq_to_tm(yb_ref, ys).astype(BF16)
    x1 = x_ref[...] + _dot(ya_ref[...], wo_ref[0:A_WIDTH, :]) + _dot(yb, wo_ref[A_WIDTH:, :])
    x1_ref[...] = x1
    _xq_tail(x1, gx_ref, wxq_ref, qnx_ref, cs, qx_ref)


def _even_out(x, ya, yb, w_out, gx, w_xq, qnx, tt):
    nb, t_len, nyb = yb.shape
    tm = tt * nb
    rows = x.shape[0]
    d = D_MODEL

    def row(c):
        return pl.BlockSpec((tm, c), lambda i: (i, 0))

    def seq(c):
        return pl.BlockSpec((nb, tt, c), lambda i: (0, i, 0))

    return pl.pallas_call(
        _even_out_kernel,
        grid=(rows // tm,),
        in_specs=[row(d), row(A_WIDTH), seq(nyb), _const_spec((d, d)),
                  _const_spec((1, d)), _const_spec((d, d)), _const_spec((1, d))],
        out_specs=[row(d), seq(d)],
        out_shape=[jax.ShapeDtypeStruct((rows, d), F32), jax.ShapeDtypeStruct((nb, t_len, d), BF16)],
        scratch_shapes=[pltpu.VMEM((nyb // LANES, tm, LANES), F32), pltpu.VMEM((d // LANES, tm, LANES), F32)],
        compiler_params=_params(("arbitrary",)),
        name="even_out",
    )(x, ya, yb, w_out, gx, w_xq, qnx)


def _xattn_kernel(q_ref, mk_ref, mv_ref, o_ref):
    for h in range(X_HEADS):
        sl = slice(h * X_HEAD_DIM, (h + 1) * X_HEAD_DIM)
        s = _dot_t(q_ref[:, sl], mk_ref[0, :, sl])
        p = jnp.exp(s - jnp.max(s, axis=1, keepdims=True))
        o = _dot(p.astype(BF16), mv_ref[0, :, sl]) / jnp.sum(p, axis=1, keepdims=True)
        o_ref[:, sl] = o.astype(BF16)


def _xattn(qx, mk, mv, tq):
    nb, t_len, d = qx.shape
    m = mk.shape[1]
    return pl.pallas_call(
        _xattn_kernel,
        grid=(nb, t_len // tq),
        in_specs=[pl.BlockSpec((None, tq, d), lambda b, i: (b, i, 0)),
                  pl.BlockSpec((1, m, d), lambda b, i: (b, 0, 0)),
                  pl.BlockSpec((1, m, d), lambda b, i: (b, 0, 0))],
        out_specs=pl.BlockSpec((None, tq, d), lambda b, i: (b, i, 0)),
        out_shape=jax.ShapeDtypeStruct((nb, t_len, d), BF16),
        compiler_params=_params(("arbitrary", "arbitrary")),
        name="mem_attn",
    )(qx, mk, mv)


def _ffn_kernel(x_ref, o_ref, wxo_ref, g_ref, wup_ref, cw_ref, cb_ref, wdn_ref, hist_ref,
                y_ref, fh_ref, gbuf, cs, *, chunk, seq_out):
    tm = x_ref.shape[0]
    nh = (F_CONV - 1) * SUBLANES

    @pl.when(pl.program_id(0) == 0)
    def _():
        gbuf[0:nh, :] = hist_ref[...]

    x2 = x_ref[...] + _dot(_seq_to_tm(o_ref, cs).astype(BF16), wxo_ref[...])
    xn = _rms(x2, g_ref[...]).astype(BF16)
    acc = jnp.zeros((tm, D_MODEL), F32)
    for c0 in range(0, D_FF, chunk):
        c1 = min(c0 + chunk, D_FF)
        val = _dot(xn, wup_ref[:, c0:c1])
        gate = _dot(xn, wup_ref[:, D_FF + c0:D_FF + c1])
        gbuf[nh:nh + tm, c0:c1] = gate
        conv = gate * cw_ref[F_CONV - 1:F_CONV, c0:c1]
        for i in range(F_CONV - 1):
            conv = conv + gbuf[i * SUBLANES:i * SUBLANES + tm, c0:c1] * cw_ref[i:i + 1, c0:c1]
        conv = conv + cb_ref[:, c0:c1]
        act = (jax.nn.gelu(conv) * val).astype(BF16)
        acc = acc + _dot(act, wdn_ref[c0:c1, :])
    tail = gbuf[tm:tm + nh, :]
    gbuf[0:nh, :] = tail
    fh_ref[...] = tail
    if seq_out:
        _tm_to_seq(x2 + acc, cs, [(y_ref, D_MODEL)])
    else:
        y_ref[...] = x2 + acc


def _ffn(x, o, w_xo, g, w_up, conv_w, conv_b, w_down, hist, tt, seq_out):
    nb, t_len, d = o.shape
    tm = tt * nb
    rows = x.shape[0]
    nh = (F_CONV - 1) * SUBLANES

    def row(c):
        return pl.BlockSpec((tm, c), lambda i: (i, 0))

    def seq(c):
        return pl.BlockSpec((nb, tt, c), lambda i: (0, i, 0))

    y_shape = jax.ShapeDtypeStruct((nb, t_len, d) if seq_out else (rows, d), F32)
    return pl.pallas_call(
        functools.partial(_ffn_kernel, chunk=MXU_DIM, seq_out=seq_out),
        grid=(rows // tm,),
        in_specs=[row(d), seq(d), _const_spec((d, d)), _const_spec((1, d)), _const_spec((d, 2 * D_FF)),
                  _const_spec((F_CONV, D_FF)), _const_spec((1, D_FF)), _const_spec((D_FF, d)),
                  _const_spec((nh, D_FF))],
        out_specs=[seq(d) if seq_out else row(d), pl.BlockSpec((nh, D_FF), lambda i: (0, 0))],
        out_shape=[y_shape, jax.ShapeDtypeStruct((nh, D_FF), F32)],
        scratch_shapes=[pltpu.VMEM((tm + nh, D_FF), F32), pltpu.VMEM((d // LANES, tm, LANES), F32)],
        compiler_params=_params(("arbitrary",)),
        name="ffn",
    )(x, o, w_xo, g, w_up, conv_w, conv_b, w_down, hist)


def _odd_kernel(x_ref, g_ref, win_ref, cw_ref, cb_ref, wai_ref, ba_ref, bi_ref, lam_ref,
                wo_ref, hist_ref, h0_ref, gx_ref, wxq_ref, qnx_ref,
                x1_ref, qx_ref, ch_ref, hl_ref, xbuf, a_s, b_s, h_s, cs, *, stream_start):
    tm = x_ref.shape[0]
    nh = (C_CONV - 1) * SUBLANES
    first = pl.program_id(0) == 0

    @pl.when(first)
    def _():
        xbuf[0:nh, :] = hist_ref[...]
        h_s[...] = h0_ref[...]

    x = x_ref[...]
    xn = _rms(x, g_ref[...]).astype(BF16)
    xr_in = _dot(xn, win_ref[:, RNN_WIDTH:])
    xbuf[nh:nh + tm, :] = xr_in
    xr = _conv_taps(xbuf, xr_in, cw_ref, C_CONV, tm) + cb_ref[...]
    tail = xbuf[tm:tm + nh, :]
    xbuf[0:nh, :] = tail
    ch_ref[...] = tail

    xrb = xr.astype(BF16)
    lam = -lam_ref[...]
    sp = jnp.maximum(lam, 0.0) + jnp.log1p(jnp.exp(-jnp.abs(lam)))
    rows = lax.broadcasted_iota(I32, (tm, RNN_BLOCK), 0)
    for n in range(RNN_BLOCKS):
        sl = slice(n * RNN_BLOCK, (n + 1) * RNN_BLOCK)
        gates = _dot(xrb[:, sl], wai_ref[n])
        r = jax.nn.sigmoid(gates[:, :RNN_BLOCK] + ba_ref[:, sl])
        ig = jax.nn.sigmoid(gates[:, RNN_BLOCK:] + bi_ref[:, sl])
        log_a = -RG_C * r * sp[:, sl]
        a = jnp.exp(log_a)
        m2 = jnp.tanh(-log_a) * (1.0 + a * a)
        mult = jnp.where(m2 > 0.0, m2 * lax.rsqrt(m2), 0.0)
        if stream_start:
            mult = jnp.where(first & (rows < SUBLANES), 1.0, mult)
        a_s[:, sl] = a
        b_s[:, sl] = mult * ig * xr[:, sl]

    def step(t, h):
        r0 = pl.multiple_of(t * SUBLANES, SUBLANES)
        h = a_s[pl.ds(r0, SUBLANES), :] * h + b_s[pl.ds(r0, SUBLANES), :]
        b_s[pl.ds(r0, SUBLANES), :] = h
        return h

    h = lax.fori_loop(0, tm // SUBLANES, step, h_s[...], unroll=True)
    h_s[...] = h
    hl_ref[...] = h

    gate = _dot(xn, win_ref[:, :RNN_WIDTH])
    act = (jax.nn.gelu(gate) * b_s[...]).astype(BF16)
    x1 = x + _dot(act, wo_ref[...])
    x1_ref[...] = x1
    _xq_tail(x1, gx_ref, wxq_ref, qnx_ref, cs, qx_ref)


def _odd(x, g, w_in, conv_w, conv_b, w_ai, b_a, b_i, lam, w_out, hist, h0, gx, w_xq, qnx, tt,
         stream_start):
    nb = SUBLANES
    tm = tt * nb
    rows = x.shape[0]
    t_len = rows // nb
    d = D_MODEL
    r = RNN_WIDTH
    nh = (C_CONV - 1) * SUBLANES

    def row(c):
        return pl.BlockSpec((tm, c), lambda i: (i, 0))

    blk = (RNN_BLOCKS, RNN_BLOCK, 2 * RNN_BLOCK)
    return pl.pallas_call(
        functools.partial(_odd_kernel, stream_start=stream_start),
        grid=(rows // tm,),
        in_specs=[row(d), _const_spec((1, d)), _const_spec((d, 2 * r)), _const_spec((C_CONV, r)),
                  _const_spec((1, r)), _const_spec(blk), _const_spec((1, r)),
                  _const_spec((1, r)), _const_spec((1, r)), _const_spec((r, d)), _const_spec((nh, r)),
                  _const_spec((SUBLANES, r)), _const_spec((1, d)), _const_spec((d, d)), _const_spec((1, d))],
        out_specs=[row(d), pl.BlockSpec((nb, tt, d), lambda i: (0, i, 0)), pl.BlockSpec((nh, r), lambda i: (0, 0)),
                   pl.BlockSpec((SUBLANES, r), lambda i: (0, 0))],
        out_shape=[jax.ShapeDtypeStruct((rows, d), F32), jax.ShapeDtypeStruct((nb, t_len, d), BF16),
                   jax.ShapeDtypeStruct((nh, r), F32), jax.ShapeDtypeStruct((SUBLANES, r), F32)],
        scratch_shapes=[pltpu.VMEM((tm + nh, r), F32), pltpu.VMEM((tm, r), F32), pltpu.VMEM((tm, r), F32),
                        pltpu.VMEM((SUBLANES, r), F32), pltpu.VMEM((d // LANES, tm, LANES), F32)],
        compiler_params=_params(("arbitrary",)),
        name="odd_mixer",
    )(x, g, w_in, conv_w, conv_b, w_ai, b_a, b_i, lam, w_out, hist, h0, gx, w_xq, qnx)


def _to_tm(a):
    return jnp.transpose(a, (1, 0, 2)).reshape(a.shape[1] * a.shape[0], a.shape[2])


def _from_tm(a, w):
    return jnp.transpose(a.reshape(w, SUBLANES, a.shape[1]), (1, 0, 2))


def _trunk(x, st, mem_k, mem_v, p, bias, tt, tq_dsa, tq_x):
    nb, t_len, d = x.shape
    assert nb == SUBLANES
    past = 0 if st is None else st["b_k"].shape[2]
    length = past + t_len
    ntop = min(TOPK_MAX, length // 4)
    near, far = bias
    xt = None
    out = {}

    def hist(name, l, width, c):
        if st is None:
            return jnp.zeros(((width - 1) * nb, c), F32)
        return _to_tm(st[name][l])

    for l in range(DEPTH):
        if l % 2 == 0:
            e = l // 2
            assert l == 0, "the per-sequence input is converted by the first layer's kernel"
            xt, ya, q, iq, ikw, k, v, uh = _even_in(
                x, p["g_mix"][l], p["w_in_even"][e], p["a_conv_w"][e], p["b_q_norm"][e], p["b_k_norm"][e],
                hist("a_conv", e, A_CONV, A_WIDTH), tt)
            k_new = k.reshape(nb, t_len, B_KV_HEADS, B_HEAD_DIM)
            v_new = v.reshape(nb, t_len, B_KV_HEADS, B_HEAD_DIM)
            ik_new = ikw[:, :, :IDX_DIM]
            k_all, v_all, ik_all = k_new, v_new, ik_new
            if st is not None:
                k_all = jnp.concatenate([st["b_k"][e], k_new], axis=1)
                v_all = jnp.concatenate([st["b_v"][e], v_new], axis=1)
                ik_all = jnp.concatenate([st["b_kidx"][e], ik_new], axis=1)
            lp = -(-length // KEY_BLOCK) * KEY_BLOCK
            padl = lp - length
            k_att = jnp.pad(jnp.transpose(k_all, (0, 2, 1, 3)).astype(BF16), ((0, 0), (0, 0), (0, padl), (0, 0)))
            v_att = jnp.pad(jnp.transpose(v_all, (0, 2, 1, 3)).astype(BF16), ((0, 0), (0, 0), (0, padl), (0, 0)))
            vt_att = jnp.transpose(v_att.reshape(nb, B_KV_HEADS, lp // KEY_BLOCK, KEY_BLOCK, B_HEAD_DIM),
                                   (0, 1, 2, 4, 3))
            ik_att = jnp.pad(ik_all.astype(BF16), ((0, 0), (0, padl), (0, 0)))
            yb = _dsa(q, iq, ikw, k_att, vt_att, ik_att, near, far, tq_dsa, past, length, ntop)
            x1, qx = _even_out(xt, ya, yb, p["w_out_even"][e], p["g_x"][l], p["w_xq"][l], p["x_q_norm"][l], tt)
            out.setdefault("a_conv", []).append(_from_tm(uh, A_CONV - 1))
            out.setdefault("b_k", []).append(k_new)
            out.setdefault("b_v", []).append(v_new)
            out.setdefault("b_kidx", []).append(ik_new)
        else:
            o = l // 2
            h0 = jnp.zeros((nb, RNN_WIDTH), F32) if st is None else st["c_h"][o]
            x1, qx, ch, hl = _odd(
                xt, p["g_mix"][l], p["w_in_odd"][o], p["c_conv_w"][o], p["c_conv_b"][o], p["c_w_ai"][o],
                p["c_b_a"][o], p["c_b_i"][o], p["c_lambda"][o], p["w_out_odd"][o],
                hist("c_conv", o, C_CONV, RNN_WIDTH), h0, p["g_x"][l], p["w_xq"][l], p["x_q_norm"][l], tt,
                stream_start=(past == 0))
            out.setdefault("c_conv", []).append(_from_tm(ch, C_CONV - 1))
            out.setdefault("c_h", []).append(hl)
        xo = _xattn(qx, mem_k[l], mem_v[l], tq_x)
        xt, fh = _ffn(x1, xo, p["w_xo"][l], p["g_ffn"][l], p["w_up"][l], p["f_conv_w"][l], p["f_conv_b"][l],
                      p["w_down"][l], hist("f_conv", l, F_CONV, D_FF), tt, seq_out=(l == DEPTH - 1))
        out.setdefault("f_conv", []).append(_from_tm(fh, F_CONV - 1))
    return xt, {name: jnp.stack(v) for name, v in out.items()}


def kernel(x_prompt, x_sample, cache_b_k, cache_b_v, cache_b_kidx, state_a_conv, state_c_conv, state_c_h, state_ffn_conv, cache_mem_k, cache_mem_v, mem_prompt, rel_table, g_mix, w_in_even, a_conv_w, b_q_norm, b_k_norm, w_out_even, w_in_odd, c_conv_w, c_conv_b, c_w_a, c_b_a, c_w_i, c_b_i, c_lambda, w_out_odd, g_mem, g_x, w_xq, w_xk, w_xv, x_q_norm, x_k_norm, w_xo, g_ffn, w_up, f_conv_w, f_conv_b, w_down):
    d = D_MODEL
    bp, t_p, _ = x_prompt.shape
    m = mem_prompt.shape[1]

    def rowvec(a):
        return a.reshape(a.shape[0], 1, a.shape[-1])

    def mxu(a):
        return [a[l].astype(BF16) for l in range(a.shape[0])]

    p = {
        "g_mix": rowvec(g_mix), "g_x": rowvec(g_x), "g_ffn": rowvec(g_ffn),
        "w_in_even": [jnp.pad(w, ((0, 0), (0, EVEN_IN_PAD - EVEN_IN))) for w in mxu(w_in_even)],
        "a_conv_w": a_conv_w,
        "b_q_norm": rowvec(jnp.tile(b_q_norm, (1, B_HEADS))),
        "b_k_norm": rowvec(jnp.tile(b_k_norm, (1, B_KV_HEADS))),
        "w_out_even": mxu(w_out_even),
        "w_in_odd": mxu(w_in_odd), "c_conv_w": c_conv_w, "c_conv_b": rowvec(c_conv_b),
        "c_w_ai": mxu(jnp.concatenate([c_w_a, c_w_i], axis=-1)), "c_b_a": rowvec(c_b_a), "c_b_i": rowvec(c_b_i),
        "c_lambda": rowvec(c_lambda), "w_out_odd": mxu(w_out_odd),
        "w_xq": mxu(w_xq), "x_q_norm": rowvec(jnp.tile(x_q_norm, (1, X_HEADS))),
        "w_xo": mxu(w_xo), "w_up": mxu(w_up), "f_conv_w": f_conv_w,
        "f_conv_b": rowvec(f_conv_b), "w_down": mxu(w_down),
    }
    bias = _bias_tiles(rel_table, KEY_BLOCK)

    mk, mv = _mem_kv(mem_prompt.reshape(bp * m, d), g_mem, w_xk, x_k_norm, w_xv)
    p_mem_k = mk.reshape(DEPTH, bp, m, X_HEADS, X_HEAD_DIM)
    p_mem_v = mv.reshape(DEPTH, bp, m, X_HEADS, X_HEAD_DIM)
    y_prompt, new_p = _trunk(x_prompt, None, mk.reshape(DEPTH, bp, m, d).astype(BF16),
                             mv.reshape(DEPTH, bp, m, d).astype(BF16), p, bias,
                             tt=64, tq_dsa=KEY_BLOCK, tq_x=512)

    bs, t_s, _ = x_sample.shape
    st_s = {"b_k": cache_b_k, "b_v": cache_b_v, "b_kidx": cache_b_kidx, "a_conv": state_a_conv,
            "c_conv": state_c_conv, "c_h": state_c_h, "f_conv": state_ffn_conv}
    ms = cache_mem_k.shape[2]
    y_sample, new_s = _trunk(x_sample, st_s, cache_mem_k.reshape(DEPTH, bs, ms, d).astype(BF16),
                             cache_mem_v.reshape(DEPTH, bs, ms, d).astype(BF16), p, bias,
                             tt=t_s, tq_dsa=t_s, tq_x=t_s)
    return (y_prompt, y_sample,
            new_p["b_k"], new_p["b_v"], new_p["b_kidx"], new_p["a_conv"], new_p["c_conv"],
            new_p["c_h"], new_p["f_conv"], p_mem_k, p_mem_v,
            new_s["b_k"], new_s["b_v"], new_s["b_kidx"], new_s["a_conv"], new_s["c_conv"],
            new_s["c_h"], new_s["f_conv"])
```

```python
import functools
import math

import jax
import jax.numpy as jnp
from jax import lax
from jax.experimental import pallas as pl
from jax.experimental.pallas import tpu as pltpu

F32 = jnp.float32
BF16 = jnp.bfloat16
I32 = jnp.int32
I16 = jnp.int16

D_MODEL = 1024
DEPTH = 2
CHUNK = 64
CHUNK_SHIFT = CHUNK.bit_length() - 1
EPS = 1e-6
NEG_INF = -1e30
LOG2E = math.log2(math.e)
A_WIDTH = 512
A_CONV = 3
B_HEADS = 8
B_KV_HEADS = 2
B_HEAD_DIM = 64
IDX_HEADS = 8
IDX_DIM = 32
TOPK_MAX = 256
REL_BUCKETS = 32
REL_MAX_DIST = 128
RNN_WIDTH = 1024
RNN_BLOCKS = 8
RNN_BLOCK = 128
C_CONV = 4
RG_C = 8.0
X_HEADS = 4
X_HEAD_DIM = 256
D_FF = 2816
F_CONV = 3
EVEN_IN = 2600

SUBLANES = 8
LANES = 128
MXU_DIM = 256
PACKED_ROWS = 16
VMEM_LIMIT = 56 * 1024 * 1024

EVEN_IN_PAD = 2688
KEY_BLOCK = 256
INT_MIN = -2147483648


def _params(sem, vmem=VMEM_LIMIT):
    return pltpu.CompilerParams(dimension_semantics=sem, vmem_limit_bytes=vmem)


def _const_spec(shape, layer=None):
    nd = len(shape)
    if layer is None:
        return pl.BlockSpec(shape, lambda *_: (0,) * nd, pipeline_mode=pl.Buffered(1))
    return pl.BlockSpec((None,) + tuple(shape), lambda *_: (layer,) + (0,) * nd, pipeline_mode=pl.Buffered(1))


def _rms(x, g):
    ms = jnp.mean(x * x, axis=-1, keepdims=True)
    return x * lax.rsqrt(ms + EPS) * g


def _head_rms(x, hd, gain):
    m, c = x.shape
    s = x * x
    parts = []
    if hd >= LANES:
        for h in range(c // hd):
            ms = jnp.mean(s[:, h * hd:(h + 1) * hd], axis=-1, keepdims=True)
            parts.append(x[:, h * hd:(h + 1) * hd] * lax.rsqrt(ms + EPS))
    else:
        lane = lax.broadcasted_iota(I32, (m, LANES), 1)
        for j in range(c // LANES):
            sj = s[:, j * LANES:(j + 1) * LANES]
            inv = jnp.zeros((m, LANES), F32)
            for k in range(LANES // hd):
                msk = (lane >= k * hd) & (lane < (k + 1) * hd)
                ms = jnp.sum(jnp.where(msk, sj, 0.0), axis=-1, keepdims=True) * (1.0 / hd)
                inv = jnp.where(msk, lax.rsqrt(ms + EPS), inv)
            parts.append(x[:, j * LANES:(j + 1) * LANES] * inv)
    y = parts[0] if len(parts) == 1 else jnp.concatenate(parts, axis=-1)
    return y * gain


def _dot(a, b):
    return jnp.dot(a, b, preferred_element_type=F32)


def _dot_t(a, b):
    return lax.dot_general(a, b, (((1,), (1,)), ((), ())), preferred_element_type=F32)


def _conv_taps(buf, cur, w_ref, width, tm):
    y = cur * w_ref[width - 1:width, :]
    for i in range(width - 1):
        y = y + buf[i * SUBLANES:i * SUBLANES + tm, :] * w_ref[i:i + 1, :]
    return y


def _seq_to_tm(src_ref, scr):
    nb, tt, c = src_ref.shape
    for b in range(nb):
        for j in range(c // LANES):
            scr[j, pl.ds(b, tt, stride=nb), :] = src_ref[b, :, j * LANES:(j + 1) * LANES].astype(F32)
    return jnp.concatenate([scr[j] for j in range(c // LANES)], axis=-1)


def _tm_to_seq(val, scr, dst_refs):
    tm, c = val.shape
    tt = tm // SUBLANES
    for j in range(c // LANES):
        scr[j] = val[:, j * LANES:(j + 1) * LANES]
    for b in range(SUBLANES):
        j0 = 0
        for ref, ci in dst_refs:
            nj = ci // LANES
            parts = [scr[j0 + j, pl.ds(b, tt, stride=SUBLANES), :] for j in range(nj)]
            ref[b] = (parts[0] if nj == 1 else jnp.concatenate(parts, axis=-1)).astype(ref.dtype)
            j0 += nj


def _memkv_kernel(mem_ref, g_ref, wk_ref, kn_ref, wv_ref, k_ref, v_ref):
    hm = _rms(mem_ref[...], g_ref[0]).astype(BF16)
    k_ref[0] = _head_rms(_dot(hm, wk_ref[0]), X_HEAD_DIM, kn_ref[0])
    v_ref[0] = _dot(hm, wv_ref[0])


def _mem_kv(mem, g_mem, w_xk, x_k_norm, w_xv):
    rows = mem.shape[0]
    tm = min(512, rows)
    d = D_MODEL
    kn = jnp.tile(x_k_norm, (1, X_HEADS)).reshape(DEPTH, 1, d)
    out = jax.ShapeDtypeStruct((DEPTH, rows, d), F32)
    return pl.pallas_call(
        _memkv_kernel,
        grid=(DEPTH, rows // tm),
        in_specs=[
            pl.BlockSpec((tm, d), lambda l, i: (i, 0)),
            pl.BlockSpec((1, 1, d), lambda l, i: (l, 0, 0)),
            pl.BlockSpec((1, d, d), lambda l, i: (l, 0, 0)),
            pl.BlockSpec((1, 1, d), lambda l, i: (l, 0, 0)),
            pl.BlockSpec((1, d, d), lambda l, i: (l, 0, 0)),
        ],
        out_specs=[pl.BlockSpec((1, tm, d), lambda l, i: (l, i, 0))] * 2,
        out_shape=[out, out],
        compiler_params=_params(("arbitrary", "arbitrary")),
        name="mem_kv",
    )(mem, g_mem.reshape(DEPTH, 1, d), w_xk.astype(BF16), kn, w_xv.astype(BF16))


def _even_in_kernel(x_ref, g_ref, w_ref, cw_ref, qn_ref, kn_ref, hist_ref,
                    xt_ref, ya_ref, q_ref, iq_ref, ikw_ref, k_ref, v_ref, uh_ref, ubuf, xs, cs):
    tm = xt_ref.shape[0]
    nh = (A_CONV - 1) * SUBLANES

    @pl.when(pl.program_id(0) == 0)
    def _():
        ubuf[0:nh, :] = hist_ref[...]

    x = _seq_to_tm(x_ref, xs)
    xt_ref[...] = x
    xn = _rms(x, g_ref[...]).astype(BF16)

    def proj(a, b):
        return _dot(xn, w_ref[:, a:b])

    zc = proj(A_WIDTH, 3 * A_WIDTH)
    u = zc[:, :A_WIDTH] * zc[:, A_WIDTH:]
    ubuf[nh:nh + tm, :] = u
    conv = _conv_taps(ubuf, u, cw_ref, A_CONV, tm)
    ya_ref[...] = (proj(0, A_WIDTH) * conv).astype(BF16)
    tail = ubuf[tm:tm + nh, :]
    ubuf[0:nh, :] = tail
    uh_ref[...] = tail

    o = 3 * A_WIDTH
    nq = B_HEADS * B_HEAD_DIM
    nkv = B_KV_HEADS * B_HEAD_DIM
    ni = IDX_HEADS * IDX_DIM
    q = _head_rms(proj(o, o + nq), B_HEAD_DIM, qn_ref[...]) * (B_HEAD_DIM ** -0.5 * LOG2E)
    zkv = proj(o + nq, o + nq + 2 * nkv)
    k = _head_rms(zkv[:, :nkv], B_HEAD_DIM, kn_ref[...])
    zi = proj(o + nq + 2 * nkv, EVEN_IN_PAD)
    seq = jnp.concatenate([q, zi, k, zkv[:, nkv:]], axis=-1)
    _tm_to_seq(seq, cs, [(q_ref, nq), (iq_ref, ni), (ikw_ref, LANES), (k_ref, nkv), (v_ref, nkv)])


def _even_in(x, g, w_pad, conv_w, qn, kn, hist, tt):
    nb, t_len, d = x.shape
    tm = tt * nb
    rows = t_len * nb
    nh = (A_CONV - 1) * SUBLANES
    nq = B_HEADS * B_HEAD_DIM
    nkv = B_KV_HEADS * B_HEAD_DIM
    ni = IDX_HEADS * IDX_DIM
    nseq = nq + ni + LANES + 2 * nkv

    def row(c):
        return pl.BlockSpec((tm, c), lambda i: (i, 0))

    def seq(c):
        return pl.BlockSpec((nb, tt, c), lambda i: (0, i, 0))

    def seq_shape(c, dt):
        return jax.ShapeDtypeStruct((nb, t_len, c), dt)

    return pl.pallas_call(
        _even_in_kernel,
        grid=(t_len // tt,),
        in_specs=[seq(d), _const_spec((1, d)), _const_spec((d, EVEN_IN_PAD)), _const_spec((A_CONV, A_WIDTH)),
                  _const_spec((1, nq)), _const_spec((1, nkv)), _const_spec((nh, A_WIDTH))],
        out_specs=[row(d), row(A_WIDTH), seq(nq), seq(ni), seq(LANES), seq(nkv), seq(nkv),
                   pl.BlockSpec((nh, A_WIDTH), lambda i: (0, 0))],
        out_shape=[jax.ShapeDtypeStruct((rows, d), F32), jax.ShapeDtypeStruct((rows, A_WIDTH), BF16),
                   seq_shape(nq, BF16), seq_shape(ni, BF16), seq_shape(LANES, F32), seq_shape(nkv, F32),
                   seq_shape(nkv, F32), jax.ShapeDtypeStruct((nh, A_WIDTH), F32)],
        scratch_shapes=[pltpu.VMEM((tm + nh, A_WIDTH), F32), pltpu.VMEM((d // LANES, tm, LANES), F32),
                        pltpu.VMEM((nseq // LANES, tm, LANES), F32)],
        compiler_params=_params(("arbitrary",)),
        name="even_in",
    )(x, g, w_pad, conv_w, qn, kn, hist)


def _rel_bucket(rel):
    half = REL_BUCKETS // 2
    max_exact = half // 2
    n = -rel
    ret = jnp.where(n < 0, half, 0)
    n = jnp.abs(n)
    nf = jnp.maximum(n, 1).astype(F32)
    large = max_exact + (jnp.log(nf / max_exact) / math.log(REL_MAX_DIST / max_exact)
                         * (half - max_exact)).astype(I32)
    large = jnp.minimum(large, half - 1)
    return ret + jnp.where(n < max_exact, n, large)


def _bias_kernel(tab_ref, near_ref, far_ref):
    tk = near_ref.shape[-1]
    r = lax.broadcasted_iota(I32, (tk, tk), 0)
    c = lax.broadcasted_iota(I32, (tk, tk), 1)

    def lookup(bucket, h):
        def body(j, acc):
            return jnp.where(bucket == j, tab_ref[j, h], acc)
        return lax.fori_loop(0, REL_BUCKETS, body, jnp.zeros(bucket.shape, F32))

    for blk in range(2):
        bucket = _rel_bucket(r - c - blk * tk)
        for h in range(B_HEADS):
            near_ref[blk, h] = lookup(bucket, h) * LOG2E
    bucket = _rel_bucket(-REL_MAX_DIST - c[0:SUBLANES, :])
    for h in range(B_HEADS):
        far_ref[h] = lookup(bucket, h) * LOG2E


def _bias_tiles(rel_table, tk):
    return pl.pallas_call(
        _bias_kernel,
        in_specs=[pl.BlockSpec(memory_space=pltpu.SMEM)],
        out_shape=[jax.ShapeDtypeStruct((2, B_HEADS, tk, tk), F32),
                   jax.ShapeDtypeStruct((B_HEADS, SUBLANES, tk), F32)],
        name="rel_bias",
    )(rel_table)


def _dsa_kernel(q_ref, iq_ref, ikw_ref, k_ref, vt_ref, ik_ref, near_ref, far_ref, o_ref,
                skey, skh, skl, madd_s, s_s, p_s, qh_s, iqh_s, pad_s, w_s, m_s, acc_s,
                *, tq, qw, tk, past, length, ntop, idx_bits):
    i = pl.program_id(1)
    q0 = past + i * tq
    nkb = (q0 + tq + tk - 1) // tk
    last = nkb - 1
    ni = IDX_HEADS * IDX_DIM
    groups = B_HEADS // B_KV_HEADS
    lane = lax.broadcasted_iota(I32, (1, qw), 1)
    sub8 = lax.broadcasted_iota(I32, (SUBLANES, qw), 0)
    qchunk = jnp.right_shift(q0 + lane, CHUNK_SHIFT)

    if tq != qw:
        qh_s[...] = jnp.zeros(qh_s.shape, BF16)
        iqh_s[...] = jnp.zeros(iqh_s.shape, BF16)
        pad_s[...] = jnp.zeros(pad_s.shape, F32)
    for h in range(B_HEADS):
        qh_s[h, 0:tq, :] = q_ref[:, B_HEAD_DIM * h:B_HEAD_DIM * (h + 1)]
    for h in range(IDX_HEADS):
        iqh_s[h, 0:tq, :] = iq_ref[:, IDX_DIM * h:IDX_DIM * (h + 1)]
    pad_s[0:tq, :] = ikw_ref[...]
    w_s[...] = pad_s[...].T[IDX_DIM:IDX_DIM + IDX_HEADS, :] * (ni ** -0.5)

    slab = tk

    def scores(kb, masked):
        for sl in range(tk // slab):
            ks = pl.multiple_of(kb * tk + sl * slab, slab)
            ikb = ik_ref[0, pl.ds(ks, slab), :]
            acc = jnp.zeros((slab, qw), F32)
            for h in range(IDX_HEADS):
                acc = acc + w_s[h:h + 1, :] * jnp.maximum(_dot_t(ikb, iqh_s[h]), 0.0)
            if masked:
                kpos = kb * tk + sl * slab + lax.broadcasted_iota(I32, (slab, qw), 0)
                vis = (jnp.right_shift(kpos, CHUNK_SHIFT) <= qchunk) & (kpos < length)
                acc = jnp.where(vis, acc, -jnp.inf)
            bits = lax.bitcast_convert_type(acc, I32)
            key = jnp.where(bits < 0, bits ^ jnp.int32(0x7FFFFFFF), bits)
            rows = slice(sl * slab, (sl + 1) * slab)
            skey[kb, rows, :] = key
            skh[kb, rows, :] = jnp.right_shift(key, 16).astype(I16)
            skl[kb, rows, :] = ((key & 0xFFFF) - 32768).astype(I16)

    def score_pair(j, carry):
        scores(2 * j, False)
        scores(2 * j + 1, False)
        return carry

    lax.fori_loop(0, last // 2, score_pair, 0)

    @pl.when(last % 2 == 1)
    def _():
        scores(last - 1, False)

    scores(last, True)

    n_acc = 4
    prow = PACKED_ROWS

    @pl.when(nkb % 2 == 1)
    def _():
        skh[nkb] = jnp.full((tk, qw), -32768, I16)
        skl[nkb] = jnp.full((tk, qw), -32768, I16)

    def count16(ref, pred):
        def body(j, accs):
            accs = list(accs)
            for kb in (2 * j, 2 * j + 1):
                for g in range(tk // prow):
                    blk = ref[kb, g * prow:(g + 1) * prow, :]
                    accs[g % n_acc] = accs[g % n_acc] + jnp.where(pred(blk), jnp.int16(1), jnp.int16(0))
            return tuple(accs)
        accs = lax.fori_loop(0, (nkb + 1) // 2, body, tuple(jnp.zeros((prow, qw), I16) for _ in range(n_acc)))
        tot = (accs[0].astype(I32) + accs[1].astype(I32)) + (accs[2].astype(I32) + accs[3].astype(I32))
        return jnp.sum(tot, axis=0, keepdims=True)

    def rep16(v):
        return jnp.broadcast_to(v, (prow, qw)).astype(I16)

    def kth16(ref, kth):
        def bit_body(it, prefix):
            cand_u = prefix | jnp.left_shift(jnp.int32(1), 15 - it)
            cand = rep16(cand_u - 32768)
            cnt = count16(ref, lambda blk: blk >= cand)
            return jnp.where(cnt >= kth, cand_u, prefix)
        return lax.fori_loop(0, 16, bit_body, jnp.zeros((1, qw), I32)) - 32768

    def count(pred):
        def body(kb, accs):
            accs = list(accs)
            for g in range(tk // SUBLANES):
                blk = skey[kb, g * SUBLANES:(g + 1) * SUBLANES, :]
                accs[g % n_acc] = accs[g % n_acc] + jnp.where(pred(kb, g, blk), 1, 0)
            return tuple(accs)
        accs = lax.fori_loop(0, nkb, body, tuple(jnp.zeros((SUBLANES, qw), I32) for _ in range(n_acc)))
        tot = (accs[0] + accs[1]) + (accs[2] + accs[3])
        return jnp.sum(tot, axis=0, keepdims=True)

    def rep8(v):
        return jnp.broadcast_to(v, (SUBLANES, qw))

    p_hi = kth16(skh, ntop)
    hi16 = rep16(p_hi)
    above = count16(skh, lambda blk: blk > hi16)

    def low_body(kb, carry):
        for g in range(tk // prow):
            rows = slice(g * prow, (g + 1) * prow)
            skl[kb, rows, :] = jnp.where(skh[kb, rows, :] == hi16, skl[kb, rows, :], jnp.int16(-32768))
        return carry

    lax.fori_loop(0, nkb, low_body, 0)
    p_lo = kth16(skl, ntop - above)
    tau = p_hi * 65536 + (p_lo + 32768)
    tau8 = rep8(tau)

    cnt_ge = count(lambda kb, g, blk: blk >= tau8)
    key_ninf = jnp.int32(0x7FFFFFFF) ^ jnp.int32(-8388608)
    finite = tau != key_ninf
    tie_rows = (cnt_ge > ntop) & finite & (lane < tq)

    @pl.when(jnp.max(jnp.where(tie_rows, 1, 0)) > 0)
    def _():
        need8 = rep8(ntop - count(lambda kb, g, blk: blk > tau8))

        def xbit(it, xlim):
            cand8 = xlim | jnp.left_shift(jnp.int32(1), idx_bits - 1 - it)
            cnt = count(lambda kb, g, blk: (blk == tau8) & ((kb * tk + g * SUBLANES + sub8) < cand8))
            return jnp.where(rep8(cnt) <= need8, cand8, xlim)
        xlim8 = lax.fori_loop(0, idx_bits, xbit, jnp.zeros((SUBLANES, qw), I32))

        def demote(kb, carry):
            for g in range(tk // SUBLANES):
                rows = slice(g * SUBLANES, (g + 1) * SUBLANES)
                key = skey[kb, rows, :]
                late = (key == tau8) & ((kb * tk + g * SUBLANES + sub8) >= xlim8)
                skey[kb, rows, :] = jnp.where(late, key - 1, key)
            return carry

        lax.fori_loop(0, nkb, demote, 0)

    m_s[...] = jnp.full(m_s.shape, NEG_INF, F32)
    acc_s[...] = jnp.zeros(acc_s.shape, F32)
    vrows = B_HEAD_DIM + PACKED_ROWS
    ones_rows = jnp.ones((PACKED_ROWS, tk), BF16)

    tau_ge8 = rep8(jnp.where(finite, tau, tau + 1))

    def attend(kbs, near):
        nblk = len(kbs)
        tiles = [nblk - 1 - i for i in range(nblk)] if near else None
        alphas = {}
        for i, kb in enumerate(kbs):
            for g in range(tk // SUBLANES):
                rows = slice(g * SUBLANES, (g + 1) * SUBLANES)
                madd_s[i, rows, :] = jnp.where(skey[kb, rows, :] >= tau_ge8, 0.0, NEG_INF)

        def logits(h):
            for i, kb in enumerate(kbs):
                kn = k_ref[0, h // groups, pl.ds(pl.multiple_of(kb * tk, tk), tk), :]
                bias = madd_s[i] if tiles is None else madd_s[i] + near_ref[tiles[i], h]
                s_s[i, h] = (_dot_t(kn, qh_s[h]) + bias).astype(BF16)

        def softmax(h):
            tiles16 = [s_s[i, h, r * prow:(r + 1) * prow, :] for i in range(nblk) for r in range(tk // prow)]
            mx = tiles16[:n_acc]
            for j, t in enumerate(tiles16[n_acc:]):
                mx[j % n_acc] = jnp.maximum(mx[j % n_acc], t)
            mx = jnp.maximum(jnp.maximum(mx[0], mx[1]), jnp.maximum(mx[2], mx[3]))
            m_cur = jnp.max(mx.astype(F32), axis=0, keepdims=True)
            c_h = jnp.zeros((1, qw), F32) if near else far_ref[h, 0:1, 0:qw]
            m_prev = m_s[h:h + 1, :]
            shift = (jnp.maximum(m_prev, m_cur + c_h) - c_h).astype(BF16)
            m_new = shift.astype(F32) + c_h
            m_s[h:h + 1, :] = m_new
            alphas[h] = jnp.exp2(m_prev - m_new)
            shift = jnp.broadcast_to(shift, (prow, qw))
            for i in range(nblk):
                for r in range(tk // prow):
                    rows = slice(r * prow, (r + 1) * prow)
                    p_s[i, h, rows, :] = jnp.exp2(s_s[i, h, rows, :] - shift)

        def values(h):
            hs = slice(h * vrows, (h + 1) * vrows)
            pv = None
            for i in range(nblk):
                vt1 = jnp.concatenate([vt_ref[0, h // groups, kbs[i]], ones_rows], axis=0)
                pv = _dot(vt1, p_s[i, h]) if pv is None else pv + _dot(vt1, p_s[i, h])
            acc_s[hs, :] = acc_s[hs, :] * alphas[h] + pv

        for phase in (logits, softmax, values):
            for h in range(B_HEADS):
                phase(h)

    nfar = nkb - 2

    def far_pair(j, carry):
        attend([2 * j, 2 * j + 1], False)
        return carry

    lax.fori_loop(0, nfar // 2, far_pair, 0)

    @pl.when((nfar > 0) & (nfar % 2 == 1))
    def _():
        attend([nfar - 1], False)

    @pl.when(nkb >= 2)
    def _():
        attend([nkb - 2, last], True)

    @pl.when(nkb < 2)
    def _():
        attend([last], True)

    outs = []
    for h in range(B_HEADS):
        r0 = h * vrows
        outs.append(acc_s[r0:r0 + B_HEAD_DIM, :] / acc_s[r0 + B_HEAD_DIM:r0 + B_HEAD_DIM + 1, :])
    o = jnp.concatenate(outs, axis=0).T
    o_ref[...] = o[0:tq, :].astype(BF16)


def _dsa(q, iq, ikw, k_att, vt_att, ik_att, near, far, tq, past, length, ntop):
    tk = KEY_BLOCK
    nb, t_len, _ = q.shape
    lp = k_att.shape[2]
    vrows = B_HEAD_DIM + PACKED_ROWS
    nq = B_HEADS * B_HEAD_DIM
    ni = IDX_HEADS * IDX_DIM
    qw = max(tq, LANES)
    assert past % tk == 0 and (tq == tk or t_len == tq) and tq <= tk and lp % tk == 0
    idx_bits = lp.bit_length()

    def qblk(c):
        return pl.BlockSpec((None, tq, c), lambda b, i: (b, i, 0))

    kern = functools.partial(_dsa_kernel, tq=tq, qw=qw, tk=tk, past=past, length=length, ntop=ntop,
                             idx_bits=idx_bits)
    return pl.pallas_call(
        kern,
        grid=(nb, t_len // tq),
        in_specs=[qblk(nq), qblk(ni), qblk(LANES),
                  pl.BlockSpec((1, B_KV_HEADS, lp, B_HEAD_DIM), lambda b, i: (b, 0, 0, 0)),
                  pl.BlockSpec((1, B_KV_HEADS, lp // tk, B_HEAD_DIM, tk), lambda b, i: (b, 0, 0, 0, 0)),
                  pl.BlockSpec((1, lp, IDX_DIM), lambda b, i: (b, 0, 0)),
                  pl.BlockSpec((2, B_HEADS, tk, qw), lambda b, i: (0, 0, 0, 0)),
                  pl.BlockSpec((B_HEADS, SUBLANES, tk), lambda b, i: (0, 0, 0))],
        out_specs=qblk(nq),
        out_shape=jax.ShapeDtypeStruct((nb, t_len, nq), BF16),
        scratch_shapes=[pltpu.VMEM((lp // tk, tk, qw), I32),
                        pltpu.VMEM((lp // tk + 1, tk, qw), I16),
                        pltpu.VMEM((lp // tk + 1, tk, qw), I16),
                        pltpu.VMEM((2, tk, qw), F32),
                        pltpu.VMEM((2, B_HEADS, tk, qw), BF16),
                        pltpu.VMEM((2, B_HEADS, tk, qw), BF16),
                        pltpu.VMEM((B_HEADS, qw, B_HEAD_DIM), BF16),
                        pltpu.VMEM((IDX_HEADS, qw, IDX_DIM), BF16),
                        pltpu.VMEM((qw, LANES), F32),
                        pltpu.VMEM((IDX_HEADS, qw), F32),
                        pltpu.VMEM((B_HEADS, qw), F32),
                        pltpu.VMEM((B_HEADS * vrows, qw), F32)],
        compiler_params=_params(("arbitrary", "arbitrary")),
        name="dsa",
    )(q, iq, ikw, k_att, vt_att, ik_att, near, far)


def _xq_tail(x1, gx_ref, wxq_ref, qnx_ref, cs, qx_ref):
    xn = _rms(x1, gx_ref[...]).astype(BF16)
    qx = _head_rms(_dot(xn, wxq_ref[...]), X_HEAD_DIM, qnx_ref[...])
    _tm_to_seq(qx * (X_HEAD_DIM ** -0.5), cs, [(qx_ref, D_MODEL)])


def _even_out_kernel(x_ref, ya_ref, yb_ref, wo_ref, gx_ref, wxq_ref, qnx_ref, x1_ref, qx_ref, ys, cs):
    yb = _seq_to_tm(yb_ref, ys).astype(BF16)
    x1 = x_ref[...] + _dot(ya_ref[...], wo_ref[0:A_WIDTH, :]) + _dot(yb, wo_ref[A_WIDTH:, :])
    x1_ref[...] = x1
    _xq_tail(x1, gx_ref, wxq_ref, qnx_ref, cs, qx_ref)


def _even_out(x, ya, yb, w_out, gx, w_xq, qnx, tt, layer):
    nb, t_len, nyb = yb.shape
    tm = tt * nb
    rows = x.shape[0]
    d = D_MODEL

    def row(c):
        return pl.BlockSpec((tm, c), lambda i: (i, 0))

    def seq(c):
        return pl.BlockSpec((nb, tt, c), lambda i: (0, i, 0))

    return pl.pallas_call(
        _even_out_kernel,
        grid=(rows // tm,),
        in_specs=[row(d), row(A_WIDTH), seq(nyb), _const_spec((d, d)),
                  _const_spec((1, d)), _const_spec((d, d), layer), _const_spec((1, d))],
        out_specs=[row(d), seq(d)],
        out_shape=[jax.ShapeDtypeStruct((rows, d), F32), jax.ShapeDtypeStruct((nb, t_len, d), BF16)],
        scratch_shapes=[pltpu.VMEM((nyb // LANES, tm, LANES), F32), pltpu.VMEM((d // LANES, tm, LANES), F32)],
        compiler_params=_params(("arbitrary",)),
        name="even_out",
    )(x, ya, yb, w_out, gx, w_xq, qnx)


def _xattn_kernel(q_ref, mk_ref, mv_ref, o_ref):
    for h in range(X_HEADS):
        sl = slice(h * X_HEAD_DIM, (h + 1) * X_HEAD_DIM)
        s = _dot_t(q_ref[:, sl], mk_ref[0, :, sl])
        p = jnp.exp(s - jnp.max(s, axis=1, keepdims=True))
        o = _dot(p.astype(BF16), mv_ref[0, :, sl]) / jnp.sum(p, axis=1, keepdims=True)
        o_ref[:, sl] = o.astype(BF16)


def _xattn(qx, mk, mv, tq):
    nb, t_len, d = qx.shape
    m = mk.shape[1]
    return pl.pallas_call(
        _xattn_kernel,
        grid=(nb, t_len // tq),
        in_specs=[pl.BlockSpec((None, tq, d), lambda b, i: (b, i, 0)),
                  pl.BlockSpec((1, m, d), lambda b, i: (b, 0, 0)),
                  pl.BlockSpec((1, m, d), lambda b, i: (b, 0, 0))],
        out_specs=pl.BlockSpec((None, tq, d), lambda b, i: (b, i, 0)),
        out_shape=jax.ShapeDtypeStruct((nb, t_len, d), BF16),
        compiler_params=_params(("arbitrary", "arbitrary")),
        name="mem_attn",
    )(qx, mk, mv)


def _ffn_kernel(x_ref, o_ref, wxo_ref, g_ref, wup_ref, cw_ref, cb_ref, wdn_ref, hist_ref,
                y_ref, fh_ref, gbuf, cs, *, chunk, seq_out):
    tm = x_ref.shape[0]
    nh = (F_CONV - 1) * SUBLANES

    @pl.when(pl.program_id(0) == 0)
    def _():
        gbuf[0:nh, :] = hist_ref[...]

    x2 = x_ref[...] + _dot(_seq_to_tm(o_ref, cs).astype(BF16), wxo_ref[...])
    xn = _rms(x2, g_ref[...]).astype(BF16)
    acc = jnp.zeros((tm, D_MODEL), F32)
    for c0 in range(0, D_FF, chunk):
        c1 = min(c0 + chunk, D_FF)
        val = _dot(xn, wup_ref[:, c0:c1])
        gate = _dot(xn, wup_ref[:, D_FF + c0:D_FF + c1])
        gbuf[nh:nh + tm, c0:c1] = gate
        conv = gate * cw_ref[F_CONV - 1:F_CONV, c0:c1]
        for i in range(F_CONV - 1):
            conv = conv + gbuf[i * SUBLANES:i * SUBLANES + tm, c0:c1] * cw_ref[i:i + 1, c0:c1]
        conv = conv + cb_ref[:, c0:c1]
        act = (jax.nn.gelu(conv) * val).astype(BF16)
        acc = acc + _dot(act, wdn_ref[c0:c1, :])
    tail = gbuf[tm:tm + nh, :]
    gbuf[0:nh, :] = tail
    fh_ref[...] = tail
    if seq_out:
        _tm_to_seq(x2 + acc, cs, [(y_ref, D_MODEL)])
    else:
        y_ref[...] = x2 + acc


def _ffn(x, o, w_xo, g, w_up, conv_w, conv_b, w_down, hist, tt, layer, seq_out):
    nb, t_len, d = o.shape
    tm = tt * nb
    rows = x.shape[0]
    nh = (F_CONV - 1) * SUBLANES

    def row(c):
        return pl.BlockSpec((tm, c), lambda i: (i, 0))

    def seq(c):
        return pl.BlockSpec((nb, tt, c), lambda i: (0, i, 0))

    y_shape = jax.ShapeDtypeStruct((nb, t_len, d) if seq_out else (rows, d), F32)
    return pl.pallas_call(
        functools.partial(_ffn_kernel, chunk=MXU_DIM, seq_out=seq_out),
        grid=(rows // tm,),
        in_specs=[row(d), seq(d), _const_spec((d, d), layer), _const_spec((1, d)),
                  _const_spec((d, 2 * D_FF), layer), _const_spec((F_CONV, D_FF)), _const_spec((1, D_FF)),
                  _const_spec((D_FF, d), layer), _const_spec((nh, D_FF))],
        out_specs=[seq(d) if seq_out else row(d), pl.BlockSpec((nh, D_FF), lambda i: (0, 0))],
        out_shape=[y_shape, jax.ShapeDtypeStruct((nh, D_FF), F32)],
        scratch_shapes=[pltpu.VMEM((tm + nh, D_FF), F32), pltpu.VMEM((d // LANES, tm, LANES), F32)],
        compiler_params=_params(("arbitrary",)),
        name="ffn",
    )(x, o, w_xo, g, w_up, conv_w, conv_b, w_down, hist)


def _odd_kernel(x_ref, g_ref, win_ref, cw_ref, cb_ref, wai_ref, ba_ref, bi_ref, lam_ref,
                wo_ref, hist_ref, h0_ref, gx_ref, wxq_ref, qnx_ref,
                x1_ref, qx_ref, ch_ref, hl_ref, xbuf, a_s, b_s, h_s, cs, *, stream_start):
    tm = x_ref.shape[0]
    nh = (C_CONV - 1) * SUBLANES
    first = pl.program_id(0) == 0

    @pl.when(first)
    def _():
        xbuf[0:nh, :] = hist_ref[...]
        h_s[...] = h0_ref[...]

    x = x_ref[...]
    xn = _rms(x, g_ref[...]).astype(BF16)
    xr_in = _dot(xn, win_ref[:, RNN_WIDTH:])
    xbuf[nh:nh + tm, :] = xr_in
    xr = _conv_taps(xbuf, xr_in, cw_ref, C_CONV, tm) + cb_ref[...]
    tail = xbuf[tm:tm + nh, :]
    xbuf[0:nh, :] = tail
    ch_ref[...] = tail

    xrb = xr.astype(BF16)
    lam = -lam_ref[...]
    sp = jnp.maximum(lam, 0.0) + jnp.log1p(jnp.exp(-jnp.abs(lam)))
    rows = lax.broadcasted_iota(I32, (tm, RNN_BLOCK), 0)
    for n in range(RNN_BLOCKS):
        sl = slice(n * RNN_BLOCK, (n + 1) * RNN_BLOCK)
        gates = _dot(xrb[:, sl], wai_ref[n])
        r = jax.nn.sigmoid(gates[:, :RNN_BLOCK] + ba_ref[:, sl])
        ig = jax.nn.sigmoid(gates[:, RNN_BLOCK:] + bi_ref[:, sl])
        log_a = -RG_C * r * sp[:, sl]
        a = jnp.exp(log_a)
        m2 = jnp.tanh(-log_a) * (1.0 + a * a)
        mult = jnp.where(m2 > 0.0, m2 * lax.rsqrt(m2), 0.0)
        if stream_start:
            mult = jnp.where(first & (rows < SUBLANES), 1.0, mult)
        a_s[:, sl] = a
        b_s[:, sl] = mult * ig * xr[:, sl]

    def step(t, h):
        r0 = pl.multiple_of(t * SUBLANES, SUBLANES)
        h = a_s[pl.ds(r0, SUBLANES), :] * h + b_s[pl.ds(r0, SUBLANES), :]
        b_s[pl.ds(r0, SUBLANES), :] = h
        return h

    h = lax.fori_loop(0, tm // SUBLANES, step, h_s[...], unroll=True)
    h_s[...] = h
    hl_ref[...] = h

    gate = _dot(xn, win_ref[:, :RNN_WIDTH])
    act = (jax.nn.gelu(gate) * b_s[...]).astype(BF16)
    x1 = x + _dot(act, wo_ref[...])
    x1_ref[...] = x1
    _xq_tail(x1, gx_ref, wxq_ref, qnx_ref, cs, qx_ref)


def _odd(x, g, w_in, conv_w, conv_b, w_ai, b_a, b_i, lam, w_out, hist, h0, gx, w_xq, qnx, tt, layer,
         stream_start):
    nb = SUBLANES
    tm = tt * nb
    rows = x.shape[0]
    t_len = rows // nb
    d = D_MODEL
    r = RNN_WIDTH
    nh = (C_CONV - 1) * SUBLANES

    def row(c):
        return pl.BlockSpec((tm, c), lambda i: (i, 0))

    blk = (RNN_BLOCKS, RNN_BLOCK, 2 * RNN_BLOCK)
    return pl.pallas_call(
        functools.partial(_odd_kernel, stream_start=stream_start),
        grid=(rows // tm,),
        in_specs=[row(d), _const_spec((1, d)), _const_spec((d, 2 * r)), _const_spec((C_CONV, r)),
                  _const_spec((1, r)), _const_spec(blk), _const_spec((1, r)),
                  _const_spec((1, r)), _const_spec((1, r)), _const_spec((r, d)), _const_spec((nh, r)),
                  _const_spec((SUBLANES, r)), _const_spec((1, d)), _const_spec((d, d), layer), _const_spec((1, d))],
        out_specs=[row(d), pl.BlockSpec((nb, tt, d), lambda i: (0, i, 0)), pl.BlockSpec((nh, r), lambda i: (0, 0)),
                   pl.BlockSpec((SUBLANES, r), lambda i: (0, 0))],
        out_shape=[jax.ShapeDtypeStruct((rows, d), F32), jax.ShapeDtypeStruct((nb, t_len, d), BF16),
                   jax.ShapeDtypeStruct((nh, r), F32), jax.ShapeDtypeStruct((SUBLANES, r), F32)],
        scratch_shapes=[pltpu.VMEM((tm + nh, r), F32), pltpu.VMEM((tm, r), F32), pltpu.VMEM((tm, r), F32),
                        pltpu.VMEM((SUBLANES, r), F32), pltpu.VMEM((d // LANES, tm, LANES), F32)],
        compiler_params=_params(("arbitrary",)),
        name="odd_mixer",
    )(x, g, w_in, conv_w, conv_b, w_ai, b_a, b_i, lam, w_out, hist, h0, gx, w_xq, qnx)


def _to_tm(a):
    return jnp.transpose(a, (1, 0, 2)).reshape(a.shape[1] * a.shape[0], a.shape[2])


def _from_tm(a, w):
    return jnp.transpose(a.reshape(w, SUBLANES, a.shape[1]), (1, 0, 2))


def _trunk(x, st, mem_k, mem_v, p, bias, tt, tq_dsa, tq_x):
    nb, t_len, d = x.shape
    assert nb == SUBLANES
    past = 0 if st is None else st["b_k"].shape[2]
    length = past + t_len
    ntop = min(TOPK_MAX, length // 4)
    near, far = bias
    xt = None
    out = {}

    def hist(name, l, width, c):
        if st is None:
            return jnp.zeros(((width - 1) * nb, c), F32)
        return _to_tm(st[name][l])

    for l in range(DEPTH):
        if l % 2 == 0:
            e = l // 2
            assert l == 0, "the per-sequence input is converted by the first layer's kernel"
            xt, ya, q, iq, ikw, k, v, uh = _even_in(
                x, p["g_mix"][l], p["w_in_even"][e], p["a_conv_w"][e], p["b_q_norm"][e], p["b_k_norm"][e],
                hist("a_conv", e, A_CONV, A_WIDTH), tt)
            k_new = k.reshape(nb, t_len, B_KV_HEADS, B_HEAD_DIM)
            v_new = v.reshape(nb, t_len, B_KV_HEADS, B_HEAD_DIM)
            ik_new = ikw[:, :, :IDX_DIM]
            k_all, v_all, ik_all = k_new, v_new, ik_new
            if st is not None:
                k_all = jnp.concatenate([st["b_k"][e], k_new], axis=1)
                v_all = jnp.concatenate([st["b_v"][e], v_new], axis=1)
                ik_all = jnp.concatenate([st["b_kidx"][e], ik_new], axis=1)
            lp = -(-length // KEY_BLOCK) * KEY_BLOCK
            padl = lp - length
            k_att = jnp.pad(jnp.transpose(k_all, (0, 2, 1, 3)).astype(BF16), ((0, 0), (0, 0), (0, padl), (0, 0)))
            v_att = jnp.pad(jnp.transpose(v_all, (0, 2, 1, 3)).astype(BF16), ((0, 0), (0, 0), (0, padl), (0, 0)))
            vt_att = jnp.transpose(v_att.reshape(nb, B_KV_HEADS, lp // KEY_BLOCK, KEY_BLOCK, B_HEAD_DIM),
                                   (0, 1, 2, 4, 3))
            ik_att = jnp.pad(ik_all.astype(BF16), ((0, 0), (0, padl), (0, 0)))
            yb = _dsa(q, iq, ikw, k_att, vt_att, ik_att, near, far, tq_dsa, past, length, ntop)
            x1, qx = _even_out(xt, ya, yb, p["w_out_even"][e], p["g_x"][l], p["w_xq"], p["x_q_norm"][l], tt, l)
            out.setdefault("a_conv", []).append(_from_tm(uh, A_CONV - 1))
            out.setdefault("b_k", []).append(k_new)
            out.setdefault("b_v", []).append(v_new)
            out.setdefault("b_kidx", []).append(ik_new)
        else:
            o = l // 2
            h0 = jnp.zeros((nb, RNN_WIDTH), F32) if st is None else st["c_h"][o]
            x1, qx, ch, hl = _odd(
                xt, p["g_mix"][l], p["w_in_odd"][o], p["c_conv_w"][o], p["c_conv_b"][o], p["c_w_ai"][o],
                p["c_b_a"][o], p["c_b_i"][o], p["c_lambda"][o], p["w_out_odd"][o],
                hist("c_conv", o, C_CONV, RNN_WIDTH), h0, p["g_x"][l], p["w_xq"], p["x_q_norm"][l], tt, l,
                stream_start=(past == 0))
            out.setdefault("c_conv", []).append(_from_tm(ch, C_CONV - 1))
            out.setdefault("c_h", []).append(hl)
        xo = _xattn(qx, mem_k[l], mem_v[l], tq_x)
        xt, fh = _ffn(x1, xo, p["w_xo"], p["g_ffn"][l], p["w_up"], p["f_conv_w"][l], p["f_conv_b"][l],
                      p["w_down"], hist("f_conv", l, F_CONV, D_FF), tt, l, seq_out=(l == DEPTH - 1))
        out.setdefault("f_conv", []).append(_from_tm(fh, F_CONV - 1))
    return xt, {name: jnp.stack(v) for name, v in out.items()}


def kernel(x_prompt, x_sample, cache_b_k, cache_b_v, cache_b_kidx, state_a_conv, state_c_conv, state_c_h, state_ffn_conv, cache_mem_k, cache_mem_v, mem_prompt, rel_table, g_mix, w_in_even, a_conv_w, b_q_norm, b_k_norm, w_out_even, w_in_odd, c_conv_w, c_conv_b, c_w_a, c_b_a, c_w_i, c_b_i, c_lambda, w_out_odd, g_mem, g_x, w_xq, w_xk, w_xv, x_q_norm, x_k_norm, w_xo, g_ffn, w_up, f_conv_w, f_conv_b, w_down):
    d = D_MODEL
    bp, t_p, _ = x_prompt.shape
    m = mem_prompt.shape[1]

    def rowvec(a):
        return a.reshape(a.shape[0], 1, a.shape[-1])

    def mxu(a):
        return [a[l].astype(BF16) for l in range(a.shape[0])]

    p = {
        "g_mix": rowvec(g_mix), "g_x": rowvec(g_x), "g_ffn": rowvec(g_ffn),
        "w_in_even": [jnp.pad(w, ((0, 0), (0, EVEN_IN_PAD - EVEN_IN))) for w in mxu(w_in_even)],
        "a_conv_w": a_conv_w,
        "b_q_norm": rowvec(jnp.tile(b_q_norm, (1, B_HEADS))),
        "b_k_norm": rowvec(jnp.tile(b_k_norm, (1, B_KV_HEADS))),
        "w_out_even": mxu(w_out_even),
        "w_in_odd": mxu(w_in_odd), "c_conv_w": c_conv_w, "c_conv_b": rowvec(c_conv_b),
        "c_w_ai": mxu(jnp.concatenate([c_w_a, c_w_i], axis=-1)), "c_b_a": rowvec(c_b_a), "c_b_i": rowvec(c_b_i),
        "c_lambda": rowvec(c_lambda), "w_out_odd": mxu(w_out_odd),
        "w_xq": w_xq.astype(BF16), "x_q_norm": rowvec(jnp.tile(x_q_norm, (1, X_HEADS))),
        "w_xo": w_xo.astype(BF16), "w_up": w_up.astype(BF16), "f_conv_w": f_conv_w,
        "f_conv_b": rowvec(f_conv_b), "w_down": w_down.astype(BF16),
    }
    bias = _bias_tiles(rel_table, KEY_BLOCK)

    mk, mv = _mem_kv(mem_prompt.reshape(bp * m, d), g_mem, w_xk, x_k_norm, w_xv)
    p_mem_k = mk.reshape(DEPTH, bp, m, X_HEADS, X_HEAD_DIM)
    p_mem_v = mv.reshape(DEPTH, bp, m, X_HEADS, X_HEAD_DIM)
    y_prompt, new_p = _trunk(x_prompt, None, mk.reshape(DEPTH, bp, m, d).astype(BF16),
                             mv.reshape(DEPTH, bp, m, d).astype(BF16), p, bias,
                             tt=64, tq_dsa=KEY_BLOCK, tq_x=512)

    bs, t_s, _ = x_sample.shape
    st_s = {"b_k": cache_b_k, "b_v": cache_b_v, "b_kidx": cache_b_kidx, "a_conv": state_a_conv,
            "c_conv": state_c_conv, "c_h": state_c_h, "f_conv": state_ffn_conv}
    ms = cache_mem_k.shape[2]
    y_sample, new_s = _trunk(x_sample, st_s, cache_mem_k.reshape(DEPTH, bs, ms, d).astype(BF16),
                             cache_mem_v.reshape(DEPTH, bs, ms, d).astype(BF16), p, bias,
                             tt=t_s, tq_dsa=t_s, tq_x=t_s)
    return (y_prompt, y_sample,
            new_p["b_k"], new_p["b_v"], new_p["b_kidx"], new_p["a_conv"], new_p["c_conv"],
            new_p["c_h"], new_p["f_conv"], p_mem_k, p_mem_v,
            new_s["b_k"], new_s["b_v"], new_s["b_kidx"], new_s["a_conv"], new_s["c_conv"],
            new_s["c_h"], new_s["f_conv"])
```

```python
import functools
import math

import jax
import jax.numpy as jnp
from jax import lax
from jax.experimental import pallas as pl
from jax.experimental.pallas import tpu as pltpu

F32 = jnp.float32
BF16 = jnp.bfloat16
I32 = jnp.int32
I16 = jnp.int16

D_MODEL = 1024
DEPTH = 2
CHUNK = 64
CHUNK_SHIFT = CHUNK.bit_length() - 1
EPS = 1e-6
NEG_INF = -1e30
LOG2E = math.log2(math.e)
A_WIDTH = 512
A_CONV = 3
B_HEADS = 8
B_KV_HEADS = 2
B_HEAD_DIM = 64
IDX_HEADS = 8
IDX_DIM = 32
TOPK_MAX = 256
REL_BUCKETS = 32
REL_MAX_DIST = 128
RNN_WIDTH = 1024
RNN_BLOCKS = 8
RNN_BLOCK = 128
C_CONV = 4
RG_C = 8.0
X_HEADS = 4
X_HEAD_DIM = 256
D_FF = 2816
F_CONV = 3
EVEN_IN = 2600

SUBLANES = 8
LANES = 128
PACKED_ROWS = 16
VMEM_LIMIT = 56 * 1024 * 1024

EVEN_IN_PAD = 2688
KEY_BLOCK = 256
INT_MIN = -2147483648


def _params(sem, vmem=VMEM_LIMIT):
    return pltpu.CompilerParams(dimension_semantics=sem, vmem_limit_bytes=vmem)


def _const_spec(shape, layer=None):
    nd = len(shape)
    if layer is None:
        return pl.BlockSpec(shape, lambda *_: (0,) * nd, pipeline_mode=pl.Buffered(1))
    return pl.BlockSpec((None,) + tuple(shape), lambda *_: (layer,) + (0,) * nd, pipeline_mode=pl.Buffered(1))


def _rms(x, g):
    ms = jnp.mean(x * x, axis=-1, keepdims=True)
    return x * lax.rsqrt(ms + EPS) * g


def _head_rms(x, hd, gain):
    m, c = x.shape
    s = x * x
    parts = []
    if hd >= LANES:
        for h in range(c // hd):
            ms = jnp.mean(s[:, h * hd:(h + 1) * hd], axis=-1, keepdims=True)
            parts.append(x[:, h * hd:(h + 1) * hd] * lax.rsqrt(ms + EPS))
    else:
        lane = lax.broadcasted_iota(I32, (m, LANES), 1)
        for j in range(c // LANES):
            sj = s[:, j * LANES:(j + 1) * LANES]
            inv = jnp.zeros((m, LANES), F32)
            for k in range(LANES // hd):
                msk = (lane >= k * hd) & (lane < (k + 1) * hd)
                ms = jnp.sum(jnp.where(msk, sj, 0.0), axis=-1, keepdims=True) * (1.0 / hd)
                inv = jnp.where(msk, lax.rsqrt(ms + EPS), inv)
            parts.append(x[:, j * LANES:(j + 1) * LANES] * inv)
    y = parts[0] if len(parts) == 1 else jnp.concatenate(parts, axis=-1)
    return y * gain


def _dot(a, b):
    return jnp.dot(a, b, preferred_element_type=F32)


def _dot_t(a, b):
    return lax.dot_general(a, b, (((1,), (1,)), ((), ())), preferred_element_type=F32)


def _conv_taps(buf, cur, w_ref, width, tm):
    y = cur * w_ref[width - 1:width, :]
    for i in range(width - 1):
        y = y + buf[i * SUBLANES:i * SUBLANES + tm, :] * w_ref[i:i + 1, :]
    return y


def _seq_to_tm(src_ref, scr):
    nb, tt, c = src_ref.shape
    for b in range(nb):
        for j in range(c // LANES):
            scr[j, pl.ds(b, tt, stride=nb), :] = src_ref[b, :, j * LANES:(j + 1) * LANES].astype(F32)
    return jnp.concatenate([scr[j] for j in range(c // LANES)], axis=-1)


def _tm_to_seq(val, scr, dst_refs):
    tm, c = val.shape
    tt = tm // SUBLANES
    for j in range(c // LANES):
        scr[j] = val[:, j * LANES:(j + 1) * LANES]
    for b in range(SUBLANES):
        j0 = 0
        for ref, ci in dst_refs:
            nj = ci // LANES
            parts = [scr[j0 + j, pl.ds(b, tt, stride=SUBLANES), :] for j in range(nj)]
            ref[b] = (parts[0] if nj == 1 else jnp.concatenate(parts, axis=-1)).astype(ref.dtype)
            j0 += nj


def _memkv_kernel(mem_ref, g_ref, wk_ref, kn_ref, wv_ref, k_ref, v_ref):
    hm = _rms(mem_ref[...], g_ref[0]).astype(BF16)
    k_ref[0] = _head_rms(_dot(hm, wk_ref[0]), X_HEAD_DIM, kn_ref[0])
    v_ref[0] = _dot(hm, wv_ref[0])


def _mem_kv(mem, g_mem, w_xk, x_k_norm, w_xv):
    rows = mem.shape[0]
    tm = min(512, rows)
    d = D_MODEL
    kn = jnp.tile(x_k_norm, (1, X_HEADS)).reshape(DEPTH, 1, d)
    out = jax.ShapeDtypeStruct((DEPTH, rows, d), F32)
    return pl.pallas_call(
        _memkv_kernel,
        grid=(DEPTH, rows // tm),
        in_specs=[
            pl.BlockSpec((tm, d), lambda l, i: (i, 0)),
            pl.BlockSpec((1, 1, d), lambda l, i: (l, 0, 0)),
            pl.BlockSpec((1, d, d), lambda l, i: (l, 0, 0)),
            pl.BlockSpec((1, 1, d), lambda l, i: (l, 0, 0)),
            pl.BlockSpec((1, d, d), lambda l, i: (l, 0, 0)),
        ],
        out_specs=[pl.BlockSpec((1, tm, d), lambda l, i: (l, i, 0))] * 2,
        out_shape=[out, out],
        compiler_params=_params(("arbitrary", "arbitrary")),
        name="mem_kv",
    )(mem, g_mem.reshape(DEPTH, 1, d), w_xk.astype(BF16), kn, w_xv.astype(BF16))


def _even_in_kernel(x_ref, g_ref, w_ref, cw_ref, qn_ref, kn_ref, hist_ref,
                    xt_ref, ya_ref, q_ref, iq_ref, ikw_ref, k_ref, v_ref, uh_ref, ubuf, xs, cs):
    tm = xt_ref.shape[0]
    nh = (A_CONV - 1) * SUBLANES

    @pl.when(pl.program_id(0) == 0)
    def _():
        ubuf[0:nh, :] = hist_ref[...]

    x = _seq_to_tm(x_ref, xs)
    xt_ref[...] = x
    xn = _rms(x, g_ref[...]).astype(BF16)

    def proj(a, b):
        return _dot(xn, w_ref[:, a:b])

    zc = proj(A_WIDTH, 3 * A_WIDTH)
    u = zc[:, :A_WIDTH] * zc[:, A_WIDTH:]
    ubuf[nh:nh + tm, :] = u
    conv = _conv_taps(ubuf, u, cw_ref, A_CONV, tm)
    ya_ref[...] = (proj(0, A_WIDTH) * conv).astype(BF16)
    tail = ubuf[tm:tm + nh, :]
    ubuf[0:nh, :] = tail
    uh_ref[...] = tail

    o = 3 * A_WIDTH
    nq = B_HEADS * B_HEAD_DIM
    nkv = B_KV_HEADS * B_HEAD_DIM
    ni = IDX_HEADS * IDX_DIM
    q = _head_rms(proj(o, o + nq), B_HEAD_DIM, qn_ref[...]) * (B_HEAD_DIM ** -0.5 * LOG2E)
    zkv = proj(o + nq, o + nq + 2 * nkv)
    k = _head_rms(zkv[:, :nkv], B_HEAD_DIM, kn_ref[...])
    zi = proj(o + nq + 2 * nkv, EVEN_IN_PAD)
    seq = jnp.concatenate([q, zi, k, zkv[:, nkv:]], axis=-1)
    _tm_to_seq(seq, cs, [(q_ref, nq), (iq_ref, ni), (ikw_ref, LANES), (k_ref, nkv), (v_ref, nkv)])


def _even_in(x, g, w_pad, conv_w, qn, kn, hist, tt):
    nb, t_len, d = x.shape
    tm = tt * nb
    rows = t_len * nb
    nh = (A_CONV - 1) * SUBLANES
    nq = B_HEADS * B_HEAD_DIM
    nkv = B_KV_HEADS * B_HEAD_DIM
    ni = IDX_HEADS * IDX_DIM
    nseq = nq + ni + LANES + 2 * nkv

    def row(c):
        return pl.BlockSpec((tm, c), lambda i: (i, 0))

    def seq(c):
        return pl.BlockSpec((nb, tt, c), lambda i: (0, i, 0))

    def seq_shape(c, dt):
        return jax.ShapeDtypeStruct((nb, t_len, c), dt)

    return pl.pallas_call(
        _even_in_kernel,
        grid=(t_len // tt,),
        in_specs=[seq(d), _const_spec((1, d)), _const_spec((d, EVEN_IN_PAD)), _const_spec((A_CONV, A_WIDTH)),
                  _const_spec((1, nq)), _const_spec((1, nkv)), _const_spec((nh, A_WIDTH))],
        out_specs=[row(d), row(A_WIDTH), seq(nq), seq(ni), seq(LANES), seq(nkv), seq(nkv),
                   pl.BlockSpec((nh, A_WIDTH), lambda i: (0, 0))],
        out_shape=[jax.ShapeDtypeStruct((rows, d), F32), jax.ShapeDtypeStruct((rows, A_WIDTH), BF16),
                   seq_shape(nq, BF16), seq_shape(ni, BF16), seq_shape(LANES, F32), seq_shape(nkv, F32),
                   seq_shape(nkv, F32), jax.ShapeDtypeStruct((nh, A_WIDTH), F32)],
        scratch_shapes=[pltpu.VMEM((tm + nh, A_WIDTH), F32), pltpu.VMEM((d // LANES, tm, LANES), F32),
                        pltpu.VMEM((nseq // LANES, tm, LANES), F32)],
        compiler_params=_params(("arbitrary",)),
        name="even_in",
    )(x, g, w_pad, conv_w, qn, kn, hist)


def _rel_bucket(rel):
    half = REL_BUCKETS // 2
    max_exact = half // 2
    n = -rel
    ret = jnp.where(n < 0, half, 0)
    n = jnp.abs(n)
    nf = jnp.maximum(n, 1).astype(F32)
    large = max_exact + (jnp.log(nf / max_exact) / math.log(REL_MAX_DIST / max_exact)
                         * (half - max_exact)).astype(I32)
    large = jnp.minimum(large, half - 1)
    return ret + jnp.where(n < max_exact, n, large)


def _bias_kernel(tab_ref, near_ref, far_ref):
    tk = near_ref.shape[-1]
    r = lax.broadcasted_iota(I32, (tk, tk), 0)
    c = lax.broadcasted_iota(I32, (tk, tk), 1)

    def lookup(bucket, h):
        def body(j, acc):
            return jnp.where(bucket == j, tab_ref[j, h], acc)
        return lax.fori_loop(0, REL_BUCKETS, body, jnp.zeros(bucket.shape, F32))

    for blk in range(2):
        bucket = _rel_bucket(r - c - blk * tk)
        for h in range(B_HEADS):
            near_ref[blk, h] = lookup(bucket, h) * LOG2E
    bucket = _rel_bucket(-REL_MAX_DIST - c[0:SUBLANES, :])
    for h in range(B_HEADS):
        far_ref[h] = lookup(bucket, h) * LOG2E


def _bias_tiles(rel_table, tk):
    return pl.pallas_call(
        _bias_kernel,
        in_specs=[pl.BlockSpec(memory_space=pltpu.SMEM)],
        out_shape=[jax.ShapeDtypeStruct((2, B_HEADS, tk, tk), F32),
                   jax.ShapeDtypeStruct((B_HEADS, SUBLANES, tk), F32)],
        name="rel_bias",
    )(rel_table)


def _dsa_kernel(q_ref, iq_ref, ikw_ref, k_ref, vt_ref, ik_ref, near_ref, far_ref, o_ref,
                skey, skh, skl, madd_s, s_s, p_s, qh_s, iqh_s, pad_s, w_s, m_s, acc_s,
                *, tq, qw, tk, past, length, ntop, idx_bits):
    i = pl.program_id(1)
    q0 = past + i * tq
    nkb = (q0 + tq + tk - 1) // tk
    last = nkb - 1
    ni = IDX_HEADS * IDX_DIM
    groups = B_HEADS // B_KV_HEADS
    lane = lax.broadcasted_iota(I32, (1, qw), 1)
    sub8 = lax.broadcasted_iota(I32, (SUBLANES, qw), 0)
    qchunk = jnp.right_shift(q0 + lane, CHUNK_SHIFT)

    if tq != qw:
        qh_s[...] = jnp.zeros(qh_s.shape, BF16)
        iqh_s[...] = jnp.zeros(iqh_s.shape, BF16)
        pad_s[...] = jnp.zeros(pad_s.shape, F32)
    for h in range(B_HEADS):
        qh_s[h, 0:tq, :] = q_ref[:, B_HEAD_DIM * h:B_HEAD_DIM * (h + 1)]
    for h in range(IDX_HEADS):
        iqh_s[h, 0:tq, :] = iq_ref[:, IDX_DIM * h:IDX_DIM * (h + 1)]
    pad_s[0:tq, :] = ikw_ref[...]
    w_s[...] = pad_s[...].T[IDX_DIM:IDX_DIM + IDX_HEADS, :] * (ni ** -0.5)

    slab = tk

    def scores(kb, masked):
        for sl in range(tk // slab):
            ks = pl.multiple_of(kb * tk + sl * slab, slab)
            ikb = ik_ref[0, pl.ds(ks, slab), :]
            acc = jnp.zeros((slab, qw), F32)
            for h in range(IDX_HEADS):
                acc = acc + w_s[h:h + 1, :] * jnp.maximum(_dot_t(ikb, iqh_s[h]), 0.0)
            if masked:
                kpos = kb * tk + sl * slab + lax.broadcasted_iota(I32, (slab, qw), 0)
                vis = (jnp.right_shift(kpos, CHUNK_SHIFT) <= qchunk) & (kpos < length)
                acc = jnp.where(vis, acc, -jnp.inf)
            bits = lax.bitcast_convert_type(acc, I32)
            key = jnp.where(bits < 0, bits ^ jnp.int32(0x7FFFFFFF), bits)
            rows = slice(sl * slab, (sl + 1) * slab)
            skey[kb, rows, :] = key
            skh[kb, rows, :] = jnp.right_shift(key, 16).astype(I16)
            skl[kb, rows, :] = ((key & 0xFFFF) - 32768).astype(I16)

    def score_pair(j, carry):
        scores(2 * j, False)
        scores(2 * j + 1, False)
        return carry

    lax.fori_loop(0, last // 2, score_pair, 0)

    @pl.when(last % 2 == 1)
    def _():
        scores(last - 1, False)

    scores(last, True)

    n_acc = 4
    prow = PACKED_ROWS

    @pl.when(nkb % 2 == 1)
    def _():
        skh[nkb] = jnp.full((tk, qw), -32768, I16)
        skl[nkb] = jnp.full((tk, qw), -32768, I16)

    def count16(ref, pred):
        def body(j, accs):
            accs = list(accs)
            for kb in (2 * j, 2 * j + 1):
                for g in range(tk // prow):
                    blk = ref[kb, g * prow:(g + 1) * prow, :]
                    accs[g % n_acc] = accs[g % n_acc] + jnp.where(pred(blk), jnp.int16(1), jnp.int16(0))
            return tuple(accs)
        accs = lax.fori_loop(0, (nkb + 1) // 2, body, tuple(jnp.zeros((prow, qw), I16) for _ in range(n_acc)))
        tot = (accs[0].astype(I32) + accs[1].astype(I32)) + (accs[2].astype(I32) + accs[3].astype(I32))
        return jnp.sum(tot, axis=0, keepdims=True)

    def rep16(v):
        return jnp.broadcast_to(v, (prow, qw)).astype(I16)

    def kth16(ref, kth):
        def bit_body(it, prefix):
            cand_u = prefix | jnp.left_shift(jnp.int32(1), 15 - it)
            cand = rep16(cand_u - 32768)
            cnt = count16(ref, lambda blk: blk >= cand)
            return jnp.where(cnt >= kth, cand_u, prefix)
        return lax.fori_loop(0, 16, bit_body, jnp.zeros((1, qw), I32)) - 32768

    def count(pred):
        def body(kb, accs):
            accs = list(accs)
            for g in range(tk // SUBLANES):
                blk = skey[kb, g * SUBLANES:(g + 1) * SUBLANES, :]
                accs[g % n_acc] = accs[g % n_acc] + jnp.where(pred(kb, g, blk), 1, 0)
            return tuple(accs)
        accs = lax.fori_loop(0, nkb, body, tuple(jnp.zeros((SUBLANES, qw), I32) for _ in range(n_acc)))
        tot = (accs[0] + accs[1]) + (accs[2] + accs[3])
        return jnp.sum(tot, axis=0, keepdims=True)

    def rep8(v):
        return jnp.broadcast_to(v, (SUBLANES, qw))

    p_hi = kth16(skh, ntop)
    hi16 = rep16(p_hi)
    above = count16(skh, lambda blk: blk > hi16)

    def low_body(kb, carry):
        for g in range(tk // prow):
            rows = slice(g * prow, (g + 1) * prow)
            skl[kb, rows, :] = jnp.where(skh[kb, rows, :] == hi16, skl[kb, rows, :], jnp.int16(-32768))
        return carry

    lax.fori_loop(0, nkb, low_body, 0)
    p_lo = kth16(skl, ntop - above)
    tau = p_hi * 65536 + (p_lo + 32768)
    tau8 = rep8(tau)

    cnt_ge = count(lambda kb, g, blk: blk >= tau8)
    key_ninf = jnp.int32(0x7FFFFFFF) ^ jnp.int32(-8388608)
    finite = tau != key_ninf
    tie_rows = (cnt_ge > ntop) & finite & (lane < tq)

    @pl.when(jnp.max(jnp.where(tie_rows, 1, 0)) > 0)
    def _():
        need8 = rep8(ntop - count(lambda kb, g, blk: blk > tau8))

        def xbit(it, xlim):
            cand8 = xlim | jnp.left_shift(jnp.int32(1), idx_bits - 1 - it)
            cnt = count(lambda kb, g, blk: (blk == tau8) & ((kb * tk + g * SUBLANES + sub8) < cand8))
            return jnp.where(rep8(cnt) <= need8, cand8, xlim)
        xlim8 = lax.fori_loop(0, idx_bits, xbit, jnp.zeros((SUBLANES, qw), I32))

        def demote(kb, carry):
            for g in range(tk // SUBLANES):
                rows = slice(g * SUBLANES, (g + 1) * SUBLANES)
                key = skey[kb, rows, :]
                late = (key == tau8) & ((kb * tk + g * SUBLANES + sub8) >= xlim8)
                skey[kb, rows, :] = jnp.where(late, key - 1, key)
            return carry

        lax.fori_loop(0, nkb, demote, 0)

    m_s[...] = jnp.full(m_s.shape, NEG_INF, F32)
    acc_s[...] = jnp.zeros(acc_s.shape, F32)
    vrows = B_HEAD_DIM + PACKED_ROWS
    ones_rows = jnp.ones((PACKED_ROWS, tk), BF16)

    tau_ge8 = rep8(jnp.where(finite, tau, tau + 1))

    def attend(kbs, near):
        nblk = len(kbs)
        tiles = [nblk - 1 - i for i in range(nblk)] if near else None
        alphas = {}
        for i, kb in enumerate(kbs):
            for g in range(tk // SUBLANES):
                rows = slice(g * SUBLANES, (g + 1) * SUBLANES)
                madd_s[i, rows, :] = jnp.where(skey[kb, rows, :] >= tau_ge8, 0.0, NEG_INF)

        def logits(h):
            for i, kb in enumerate(kbs):
                kn = k_ref[0, h // groups, pl.ds(pl.multiple_of(kb * tk, tk), tk), :]
                bias = madd_s[i] if tiles is None else madd_s[i] + near_ref[tiles[i], h]
                s_s[i, h] = (_dot_t(kn, qh_s[h]) + bias).astype(BF16)

        def softmax(h):
            tiles16 = [s_s[i, h, r * prow:(r + 1) * prow, :] for i in range(nblk) for r in range(tk // prow)]
            mx = tiles16[:n_acc]
            for j, t in enumerate(tiles16[n_acc:]):
                mx[j % n_acc] = jnp.maximum(mx[j % n_acc], t)
            mx = jnp.maximum(jnp.maximum(mx[0], mx[1]), jnp.maximum(mx[2], mx[3]))
            m_cur = jnp.max(mx.astype(F32), axis=0, keepdims=True)
            c_h = jnp.zeros((1, qw), F32) if near else far_ref[h, 0:1, 0:qw]
            m_prev = m_s[h:h + 1, :]
            shift = (jnp.maximum(m_prev, m_cur + c_h) - c_h).astype(BF16)
            m_new = shift.astype(F32) + c_h
            m_s[h:h + 1, :] = m_new
            alphas[h] = jnp.exp2(m_prev - m_new)
            shift = jnp.broadcast_to(shift, (prow, qw))
            for i in range(nblk):
                for r in range(tk // prow):
                    rows = slice(r * prow, (r + 1) * prow)
                    p_s[i, h, rows, :] = jnp.exp2(s_s[i, h, rows, :] - shift)

        def values(h):
            hs = slice(h * vrows, (h + 1) * vrows)
            pv = None
            for i in range(nblk):
                vt1 = jnp.concatenate([vt_ref[0, h // groups, kbs[i]], ones_rows], axis=0)
                pv = _dot(vt1, p_s[i, h]) if pv is None else pv + _dot(vt1, p_s[i, h])
            acc_s[hs, :] = acc_s[hs, :] * alphas[h] + pv

        for phase in (logits, softmax, values):
            for h in range(B_HEADS):
                phase(h)

    nfar = nkb - 2

    def far_pair(j, carry):
        attend([2 * j, 2 * j + 1], False)
        return carry

    lax.fori_loop(0, nfar // 2, far_pair, 0)

    @pl.when((nfar > 0) & (nfar % 2 == 1))
    def _():
        attend([nfar - 1], False)

    @pl.when(nkb >= 2)
    def _():
        attend([nkb - 2, last], True)

    @pl.when(nkb < 2)
    def _():
        attend([last], True)

    outs = []
    for h in range(B_HEADS):
        r0 = h * vrows
        outs.append(acc_s[r0:r0 + B_HEAD_DIM, :] / acc_s[r0 + B_HEAD_DIM:r0 + B_HEAD_DIM + 1, :])
    o = jnp.concatenate(outs, axis=0).T
    o_ref[...] = o[0:tq, :].astype(BF16)


def _dsa(q, iq, ikw, k_att, vt_att, ik_att, near, far, tq, past, length, ntop):
    tk = KEY_BLOCK
    nb, t_len, _ = q.shape
    lp = k_att.shape[2]
    vrows = B_HEAD_DIM + PACKED_ROWS
    nq = B_HEADS * B_HEAD_DIM
    ni = IDX_HEADS * IDX_DIM
    qw = max(tq, LANES)
    assert past % tk == 0 and (tq == tk or t_len == tq) and tq <= tk and lp % tk == 0
    idx_bits = lp.bit_length()

    def qblk(c):
        return pl.BlockSpec((None, tq, c), lambda b, i: (b, i, 0))

    kern = functools.partial(_dsa_kernel, tq=tq, qw=qw, tk=tk, past=past, length=length, ntop=ntop,
                             idx_bits=idx_bits)
    return pl.pallas_call(
        kern,
        grid=(nb, t_len // tq),
        in_specs=[qblk(nq), qblk(ni), qblk(LANES),
                  pl.BlockSpec((1, B_KV_HEADS, lp, B_HEAD_DIM), lambda b, i: (b, 0, 0, 0)),
                  pl.BlockSpec((1, B_KV_HEADS, lp // tk, B_HEAD_DIM, tk), lambda b, i: (b, 0, 0, 0, 0)),
                  pl.BlockSpec((1, lp, IDX_DIM), lambda b, i: (b, 0, 0)),
                  pl.BlockSpec((2, B_HEADS, tk, qw), lambda b, i: (0, 0, 0, 0)),
                  pl.BlockSpec((B_HEADS, SUBLANES, tk), lambda b, i: (0, 0, 0))],
        out_specs=qblk(nq),
        out_shape=jax.ShapeDtypeStruct((nb, t_len, nq), BF16),
        scratch_shapes=[pltpu.VMEM((lp // tk, tk, qw), I32),
                        pltpu.VMEM((lp // tk + 1, tk, qw), I16),
                        pltpu.VMEM((lp // tk + 1, tk, qw), I16),
                        pltpu.VMEM((2, tk, qw), F32),
                        pltpu.VMEM((2, B_HEADS, tk, qw), BF16),
                        pltpu.VMEM((2, B_HEADS, tk, qw), BF16),
                        pltpu.VMEM((B_HEADS, qw, B_HEAD_DIM), BF16),
                        pltpu.VMEM((IDX_HEADS, qw, IDX_DIM), BF16),
                        pltpu.VMEM((qw, LANES), F32),
                        pltpu.VMEM((IDX_HEADS, qw), F32),
                        pltpu.VMEM((B_HEADS, qw), F32),
                        pltpu.VMEM((B_HEADS * vrows, qw), F32)],
        compiler_params=_params(("arbitrary", "arbitrary")),
        name="dsa",
    )(q, iq, ikw, k_att, vt_att, ik_att, near, far)


def _xq_tail(x1, gx_ref, wxq_ref, qnx_ref, cs, qx_ref):
    xn = _rms(x1, gx_ref[...]).astype(BF16)
    qx = _head_rms(_dot(xn, wxq_ref[...]), X_HEAD_DIM, qnx_ref[...])
    _tm_to_seq(qx * (X_HEAD_DIM ** -0.5), cs, [(qx_ref, D_MODEL)])


def _even_out_kernel(x_ref, ya_ref, yb_ref, wo_ref, gx_ref, wxq_ref, qnx_ref, x1_ref, qx_ref, ys, cs):
    yb = _seq_to_tm(yb_ref, ys).astype(BF16)
    x1 = x_ref[...] + _dot(ya_ref[...], wo_ref[0:A_WIDTH, :]) + _dot(yb, wo_ref[A_WIDTH:, :])
    x1_ref[...] = x1
    _xq_tail(x1, gx_ref, wxq_ref, qnx_ref, cs, qx_ref)


def _even_out(x, ya, yb, w_out, gx, w_xq, qnx, tt, layer):
    nb, t_len, nyb = yb.shape
    tm = tt * nb
    rows = x.shape[0]
    d = D_MODEL

    def row(c):
        return pl.BlockSpec((tm, c), lambda i: (i, 0))

    def seq(c):
        return pl.BlockSpec((nb, tt, c), lambda i: (0, i, 0))

    return pl.pallas_call(
        _even_out_kernel,
        grid=(rows // tm,),
        in_specs=[row(d), row(A_WIDTH), seq(nyb), _const_spec((d, d)),
                  _const_spec((1, d)), _const_spec((d, d), layer), _const_spec((1, d))],
        out_specs=[row(d), seq(d)],
        out_shape=[jax.ShapeDtypeStruct((rows, d), F32), jax.ShapeDtypeStruct((nb, t_len, d), BF16)],
        scratch_shapes=[pltpu.VMEM((nyb // LANES, tm, LANES), F32), pltpu.VMEM((d // LANES, tm, LANES), F32)],
        compiler_params=_params(("arbitrary",)),
        name="even_out",
    )(x, ya, yb, w_out, gx, w_xq, qnx)


def _xattn_kernel(q_ref, mk_ref, mv_ref, o_ref):
    for h in range(X_HEADS):
        sl = slice(h * X_HEAD_DIM, (h + 1) * X_HEAD_DIM)
        s = _dot_t(q_ref[:, sl], mk_ref[0, :, sl])
        p = jnp.exp(s - jnp.max(s, axis=1, keepdims=True))
        o = _dot(p.astype(BF16), mv_ref[0, :, sl]) / jnp.sum(p, axis=1, keepdims=True)
        o_ref[:, sl] = o.astype(BF16)


def _xattn(qx, mk, mv, tq):
    nb, t_len, d = qx.shape
    m = mk.shape[1]
    return pl.pallas_call(
        _xattn_kernel,
        grid=(nb, t_len // tq),
        in_specs=[pl.BlockSpec((None, tq, d), lambda b, i: (b, i, 0)),
                  pl.BlockSpec((1, m, d), lambda b, i: (b, 0, 0)),
                  pl.BlockSpec((1, m, d), lambda b, i: (b, 0, 0))],
        out_specs=pl.BlockSpec((None, tq, d), lambda b, i: (b, i, 0)),
        out_shape=jax.ShapeDtypeStruct((nb, t_len, d), BF16),
        compiler_params=_params(("arbitrary", "arbitrary")),
        name="mem_attn",
    )(qx, mk, mv)


def _ffn_kernel(x_ref, o_ref, wxo_ref, g_ref, wup_ref, cw_ref, cb_ref, wdn_ref, hist_ref,
                y_ref, fh_ref, gbuf, cs, *, seq_out):
    tm = x_ref.shape[0]
    nh = (F_CONV - 1) * SUBLANES

    @pl.when(pl.program_id(0) == 0)
    def _():
        gbuf[0:nh, :] = hist_ref[...]

    x2 = x_ref[...] + _dot(_seq_to_tm(o_ref, cs).astype(BF16), wxo_ref[...])
    xn = _rms(x2, g_ref[...]).astype(BF16)
    val = _dot(xn, wup_ref[:, :D_FF])
    gate = _dot(xn, wup_ref[:, D_FF:])
    gbuf[nh:nh + tm, :] = gate
    conv = _conv_taps(gbuf, gate, cw_ref, F_CONV, tm) + cb_ref[...]
    act = (jax.nn.gelu(conv) * val).astype(BF16)
    y = x2 + _dot(act, wdn_ref[...])
    tail = gbuf[tm:tm + nh, :]
    gbuf[0:nh, :] = tail
    fh_ref[...] = tail
    if seq_out:
        _tm_to_seq(y, cs, [(y_ref, D_MODEL)])
    else:
        y_ref[...] = y


def _ffn(x, o, w_xo, g, w_up, conv_w, conv_b, w_down, hist, tt, layer, seq_out):
    nb, t_len, d = o.shape
    tm = tt * nb
    rows = x.shape[0]
    nh = (F_CONV - 1) * SUBLANES

    def row(c):
        return pl.BlockSpec((tm, c), lambda i: (i, 0))

    def seq(c):
        return pl.BlockSpec((nb, tt, c), lambda i: (0, i, 0))

    y_shape = jax.ShapeDtypeStruct((nb, t_len, d) if seq_out else (rows, d), F32)
    return pl.pallas_call(
        functools.partial(_ffn_kernel, seq_out=seq_out),
        grid=(rows // tm,),
        in_specs=[row(d), seq(d), _const_spec((d, d), layer), _const_spec((1, d)),
                  _const_spec((d, 2 * D_FF), layer), _const_spec((F_CONV, D_FF)), _const_spec((1, D_FF)),
                  _const_spec((D_FF, d), layer), _const_spec((nh, D_FF))],
        out_specs=[seq(d) if seq_out else row(d), pl.BlockSpec((nh, D_FF), lambda i: (0, 0))],
        out_shape=[y_shape, jax.ShapeDtypeStruct((nh, D_FF), F32)],
        scratch_shapes=[pltpu.VMEM((tm + nh, D_FF), F32), pltpu.VMEM((d // LANES, tm, LANES), F32)],
        compiler_params=_params(("arbitrary",)),
        name="ffn",
    )(x, o, w_xo, g, w_up, conv_w, conv_b, w_down, hist)


def _odd_kernel(x_ref, g_ref, win_ref, cw_ref, cb_ref, wai_ref, ba_ref, bi_ref, lam_ref,
                wo_ref, hist_ref, h0_ref, gx_ref, wxq_ref, qnx_ref,
                x1_ref, qx_ref, ch_ref, hl_ref, xbuf, a_s, b_s, h_s, cs, *, stream_start):
    tm = x_ref.shape[0]
    nh = (C_CONV - 1) * SUBLANES
    first = pl.program_id(0) == 0

    @pl.when(first)
    def _():
        xbuf[0:nh, :] = hist_ref[...]
        h_s[...] = h0_ref[...]

    x = x_ref[...]
    xn = _rms(x, g_ref[...]).astype(BF16)
    xr_in = _dot(xn, win_ref[:, RNN_WIDTH:])
    xbuf[nh:nh + tm, :] = xr_in
    xr = _conv_taps(xbuf, xr_in, cw_ref, C_CONV, tm) + cb_ref[...]
    tail = xbuf[tm:tm + nh, :]
    xbuf[0:nh, :] = tail
    ch_ref[...] = tail

    xrb = xr.astype(BF16)
    lam = -lam_ref[...]
    sp = jnp.maximum(lam, 0.0) + jnp.log1p(jnp.exp(-jnp.abs(lam)))
    rows = lax.broadcasted_iota(I32, (tm, RNN_BLOCK), 0)
    for n in range(RNN_BLOCKS):
        sl = slice(n * RNN_BLOCK, (n + 1) * RNN_BLOCK)
        gates = _dot(xrb[:, sl], wai_ref[n])
        r = jax.nn.sigmoid(gates[:, :RNN_BLOCK] + ba_ref[:, sl])
        ig = jax.nn.sigmoid(gates[:, RNN_BLOCK:] + bi_ref[:, sl])
        log_a = -RG_C * r * sp[:, sl]
        a = jnp.exp(log_a)
        m2 = jnp.tanh(-log_a) * (1.0 + a * a)
        mult = jnp.where(m2 > 0.0, m2 * lax.rsqrt(m2), 0.0)
        if stream_start:
            mult = jnp.where(first & (rows < SUBLANES), 1.0, mult)
        a_s[:, sl] = a
        b_s[:, sl] = mult * ig * xr[:, sl]

    def step(t, h):
        r0 = pl.multiple_of(t * SUBLANES, SUBLANES)
        h = a_s[pl.ds(r0, SUBLANES), :] * h + b_s[pl.ds(r0, SUBLANES), :]
        b_s[pl.ds(r0, SUBLANES), :] = h
        return h

    h = lax.fori_loop(0, tm // SUBLANES, step, h_s[...], unroll=True)
    h_s[...] = h
    hl_ref[...] = h

    gate = _dot(xn, win_ref[:, :RNN_WIDTH])
    act = (jax.nn.gelu(gate) * b_s[...]).astype(BF16)
    x1 = x + _dot(act, wo_ref[...])
    x1_ref[...] = x1
    _xq_tail(x1, gx_ref, wxq_ref, qnx_ref, cs, qx_ref)


def _odd(x, g, w_in, conv_w, conv_b, w_ai, b_a, b_i, lam, w_out, hist, h0, gx, w_xq, qnx, tt, layer,
         stream_start):
    nb = SUBLANES
    tm = tt * nb
    rows = x.shape[0]
    t_len = rows // nb
    d = D_MODEL
    r = RNN_WIDTH
    nh = (C_CONV - 1) * SUBLANES

    def row(c):
        return pl.BlockSpec((tm, c), lambda i: (i, 0))

    blk = (RNN_BLOCKS, RNN_BLOCK, 2 * RNN_BLOCK)
    return pl.pallas_call(
        functools.partial(_odd_kernel, stream_start=stream_start),
        grid=(rows // tm,),
        in_specs=[row(d), _const_spec((1, d)), _const_spec((d, 2 * r)), _const_spec((C_CONV, r)),
                  _const_spec((1, r)), _const_spec(blk), _const_spec((1, r)),
                  _const_spec((1, r)), _const_spec((1, r)), _const_spec((r, d)), _const_spec((nh, r)),
                  _const_spec((SUBLANES, r)), _const_spec((1, d)), _const_spec((d, d), layer), _const_spec((1, d))],
        out_specs=[row(d), pl.BlockSpec((nb, tt, d), lambda i: (0, i, 0)), pl.BlockSpec((nh, r), lambda i: (0, 0)),
                   pl.BlockSpec((SUBLANES, r), lambda i: (0, 0))],
        out_shape=[jax.ShapeDtypeStruct((rows, d), F32), jax.ShapeDtypeStruct((nb, t_len, d), BF16),
                   jax.ShapeDtypeStruct((nh, r), F32), jax.ShapeDtypeStruct((SUBLANES, r), F32)],
        scratch_shapes=[pltpu.VMEM((tm + nh, r), F32), pltpu.VMEM((tm, r), F32), pltpu.VMEM((tm, r), F32),
                        pltpu.VMEM((SUBLANES, r), F32), pltpu.VMEM((d // LANES, tm, LANES), F32)],
        compiler_params=_params(("arbitrary",)),
        name="odd_mixer",
    )(x, g, w_in, conv_w, conv_b, w_ai, b_a, b_i, lam, w_out, hist, h0, gx, w_xq, qnx)


def _to_tm(a):
    return jnp.transpose(a, (1, 0, 2)).reshape(a.shape[1] * a.shape[0], a.shape[2])


def _from_tm(a, w):
    return jnp.transpose(a.reshape(w, SUBLANES, a.shape[1]), (1, 0, 2))


def _trunk(x, st, mem_k, mem_v, p, bias, tt, tq_dsa, tq_x):
    nb, t_len, d = x.shape
    assert nb == SUBLANES
    past = 0 if st is None else st["b_k"].shape[2]
    length = past + t_len
    ntop = min(TOPK_MAX, length // 4)
    near, far = bias
    xt = None
    out = {}

    def hist(name, l, width, c):
        if st is None:
            return jnp.zeros(((width - 1) * nb, c), F32)
        return _to_tm(st[name][l])

    for l in range(DEPTH):
        if l % 2 == 0:
            e = l // 2
            assert l == 0, "the per-sequence input is converted by the first layer's kernel"
            xt, ya, q, iq, ikw, k, v, uh = _even_in(
                x, p["g_mix"][l], p["w_in_even"][e], p["a_conv_w"][e], p["b_q_norm"][e], p["b_k_norm"][e],
                hist("a_conv", e, A_CONV, A_WIDTH), tt)
            k_new = k.reshape(nb, t_len, B_KV_HEADS, B_HEAD_DIM)
            v_new = v.reshape(nb, t_len, B_KV_HEADS, B_HEAD_DIM)
            ik_new = ikw[:, :, :IDX_DIM]
            k_all, v_all, ik_all = k_new, v_new, ik_new
            if st is not None:
                k_all = jnp.concatenate([st["b_k"][e], k_new], axis=1)
                v_all = jnp.concatenate([st["b_v"][e], v_new], axis=1)
                ik_all = jnp.concatenate([st["b_kidx"][e], ik_new], axis=1)
            lp = -(-length // KEY_BLOCK) * KEY_BLOCK
            padl = lp - length
            k_att = jnp.pad(jnp.transpose(k_all, (0, 2, 1, 3)).astype(BF16), ((0, 0), (0, 0), (0, padl), (0, 0)))
            v_att = jnp.pad(jnp.transpose(v_all, (0, 2, 1, 3)).astype(BF16), ((0, 0), (0, 0), (0, padl), (0, 0)))
            vt_att = jnp.transpose(v_att.reshape(nb, B_KV_HEADS, lp // KEY_BLOCK, KEY_BLOCK, B_HEAD_DIM),
                                   (0, 1, 2, 4, 3))
            ik_att = jnp.pad(ik_all.astype(BF16), ((0, 0), (0, padl), (0, 0)))
            yb = _dsa(q, iq, ikw, k_att, vt_att, ik_att, near, far, tq_dsa, past, length, ntop)
            x1, qx = _even_out(xt, ya, yb, p["w_out_even"][e], p["g_x"][l], p["w_xq"], p["x_q_norm"][l], tt, l)
            out.setdefault("a_conv", []).append(_from_tm(uh, A_CONV - 1))
            out.setdefault("b_k", []).append(k_new)
            out.setdefault("b_v", []).append(v_new)
            out.setdefault("b_kidx", []).append(ik_new)
        else:
            o = l // 2
            h0 = jnp.zeros((nb, RNN_WIDTH), F32) if st is None else st["c_h"][o]
            x1, qx, ch, hl = _odd(
                xt, p["g_mix"][l], p["w_in_odd"][o], p["c_conv_w"][o], p["c_conv_b"][o], p["c_w_ai"][o],
                p["c_b_a"][o], p["c_b_i"][o], p["c_lambda"][o], p["w_out_odd"][o],
                hist("c_conv", o, C_CONV, RNN_WIDTH), h0, p["g_x"][l], p["w_xq"], p["x_q_norm"][l], tt, l,
                stream_start=(past == 0))
            out.setdefault("c_conv", []).append(_from_tm(ch, C_CONV - 1))
            out.setdefault("c_h", []).append(hl)
        xo = _xattn(qx, mem_k[l], mem_v[l], tq_x)
        xt, fh = _ffn(x1, xo, p["w_xo"], p["g_ffn"][l], p["w_up"], p["f_conv_w"][l], p["f_conv_b"][l],
                      p["w_down"], hist("f_conv", l, F_CONV, D_FF), tt, l, seq_out=(l == DEPTH - 1))
        out.setdefault("f_conv", []).append(_from_tm(fh, F_CONV - 1))
    return xt, {name: jnp.stack(v) for name, v in out.items()}


def kernel(x_prompt, x_sample, cache_b_k, cache_b_v, cache_b_kidx, state_a_conv, state_c_conv, state_c_h, state_ffn_conv, cache_mem_k, cache_mem_v, mem_prompt, rel_table, g_mix, w_in_even, a_conv_w, b_q_norm, b_k_norm, w_out_even, w_in_odd, c_conv_w, c_conv_b, c_w_a, c_b_a, c_w_i, c_b_i, c_lambda, w_out_odd, g_mem, g_x, w_xq, w_xk, w_xv, x_q_norm, x_k_norm, w_xo, g_ffn, w_up, f_conv_w, f_conv_b, w_down):
    d = D_MODEL
    bp, t_p, _ = x_prompt.shape
    m = mem_prompt.shape[1]

    def rowvec(a):
        return a.reshape(a.shape[0], 1, a.shape[-1])

    def mxu(a):
        return [a[l].astype(BF16) for l in range(a.shape[0])]

    p = {
        "g_mix": rowvec(g_mix), "g_x": rowvec(g_x), "g_ffn": rowvec(g_ffn),
        "w_in_even": [jnp.pad(w, ((0, 0), (0, EVEN_IN_PAD - EVEN_IN))) for w in mxu(w_in_even)],
        "a_conv_w": a_conv_w,
        "b_q_norm": rowvec(jnp.tile(b_q_norm, (1, B_HEADS))),
        "b_k_norm": rowvec(jnp.tile(b_k_norm, (1, B_KV_HEADS))),
        "w_out_even": mxu(w_out_even),
        "w_in_odd": mxu(w_in_odd), "c_conv_w": c_conv_w, "c_conv_b": rowvec(c_conv_b),
        "c_w_ai": mxu(jnp.concatenate([c_w_a, c_w_i], axis=-1)), "c_b_a": rowvec(c_b_a), "c_b_i": rowvec(c_b_i),
        "c_lambda": rowvec(c_lambda), "w_out_odd": mxu(w_out_odd),
        "w_xq": w_xq.astype(BF16), "x_q_norm": rowvec(jnp.tile(x_q_norm, (1, X_HEADS))),
        "w_xo": w_xo.astype(BF16), "w_up": w_up.astype(BF16), "f_conv_w": f_conv_w,
        "f_conv_b": rowvec(f_conv_b), "w_down": w_down.astype(BF16),
    }
    bias = _bias_tiles(rel_table, KEY_BLOCK)

    mk, mv = _mem_kv(mem_prompt.reshape(bp * m, d), g_mem, w_xk, x_k_norm, w_xv)
    p_mem_k = mk.reshape(DEPTH, bp, m, X_HEADS, X_HEAD_DIM)
    p_mem_v = mv.reshape(DEPTH, bp, m, X_HEADS, X_HEAD_DIM)
    y_prompt, new_p = _trunk(x_prompt, None, mk.reshape(DEPTH, bp, m, d).astype(BF16),
                             mv.reshape(DEPTH, bp, m, d).astype(BF16), p, bias,
                             tt=64, tq_dsa=KEY_BLOCK, tq_x=512)

    bs, t_s, _ = x_sample.shape
    st_s = {"b_k": cache_b_k, "b_v": cache_b_v, "b_kidx": cache_b_kidx, "a_conv": state_a_conv,
            "c_conv": state_c_conv, "c_h": state_c_h, "f_conv": state_ffn_conv}
    ms = cache_mem_k.shape[2]
    y_sample, new_s = _trunk(x_sample, st_s, cache_mem_k.reshape(DEPTH, bs, ms, d).astype(BF16),
                             cache_mem_v.reshape(DEPTH, bs, ms, d).astype(BF16), p, bias,
                             tt=t_s, tq_dsa=t_s, tq_x=t_s)
    return (y_prompt, y_sample,
            new_p["b_k"], new_p["b_v"], new_p["b_kidx"], new_p["a_conv"], new_p["c_conv"],
            new_p["c_h"], new_p["f_conv"], p_mem_k, p_mem_v,
            new_s["b_k"], new_s["b_v"], new_s["b_kidx"], new_s["a_conv"], new_s["c_conv"],
            new_s["c_h"], new_s["f_conv"])
```

```python
import functools
import math

import jax
import jax.numpy as jnp
from jax import lax
from jax.experimental import pallas as pl
from jax.experimental.pallas import tpu as pltpu

F32 = jnp.float32
BF16 = jnp.bfloat16
I32 = jnp.int32
I16 = jnp.int16

D_MODEL = 1024
DEPTH = 2
CHUNK = 64
CHUNK_SHIFT = CHUNK.bit_length() - 1
EPS = 1e-6
NEG_INF = -1e30
LOG2E = math.log2(math.e)
A_WIDTH = 512
A_CONV = 3
B_HEADS = 8
B_KV_HEADS = 2
B_HEAD_DIM = 64
IDX_HEADS = 8
IDX_DIM = 32
TOPK_MAX = 256
REL_BUCKETS = 32
REL_MAX_DIST = 128
RNN_WIDTH = 1024
RNN_BLOCKS = 8
RNN_BLOCK = 128
C_CONV = 4
RG_C = 8.0
X_HEADS = 4
X_HEAD_DIM = 256
D_FF = 2816
F_CONV = 3
EVEN_IN = 2600

SUBLANES = 8
LANES = 128
PACKED_ROWS = 16
VMEM_LIMIT = 56 * 1024 * 1024

EVEN_IN_PAD = 2688
KEY_BLOCK = 256
INT_MIN = -2147483648


def _params(sem, vmem=VMEM_LIMIT):
    return pltpu.CompilerParams(dimension_semantics=sem, vmem_limit_bytes=vmem)


def _const_spec(shape, layer=None):
    nd = len(shape)
    if layer is None:
        return pl.BlockSpec(shape, lambda *_: (0,) * nd, pipeline_mode=pl.Buffered(1))
    return pl.BlockSpec((None,) + tuple(shape), lambda *_: (layer,) + (0,) * nd, pipeline_mode=pl.Buffered(1))


def _rms(x, g):
    ms = jnp.mean(x * x, axis=-1, keepdims=True)
    return x * lax.rsqrt(ms + EPS) * g


def _head_rms(x, hd, gain):
    m, c = x.shape
    s = x * x
    parts = []
    if hd >= LANES:
        for h in range(c // hd):
            ms = jnp.mean(s[:, h * hd:(h + 1) * hd], axis=-1, keepdims=True)
            parts.append(x[:, h * hd:(h + 1) * hd] * lax.rsqrt(ms + EPS))
    else:
        lane = lax.broadcasted_iota(I32, (m, LANES), 1)
        for j in range(c // LANES):
            sj = s[:, j * LANES:(j + 1) * LANES]
            inv = jnp.zeros((m, LANES), F32)
            for k in range(LANES // hd):
                msk = (lane >= k * hd) & (lane < (k + 1) * hd)
                ms = jnp.sum(jnp.where(msk, sj, 0.0), axis=-1, keepdims=True) * (1.0 / hd)
                inv = jnp.where(msk, lax.rsqrt(ms + EPS), inv)
            parts.append(x[:, j * LANES:(j + 1) * LANES] * inv)
    y = parts[0] if len(parts) == 1 else jnp.concatenate(parts, axis=-1)
    return y * gain


def _dot(a, b):
    return jnp.dot(a, b, preferred_element_type=F32)


def _dot_t(a, b):
    return lax.dot_general(a, b, (((1,), (1,)), ((), ())), preferred_element_type=F32)


def _conv_taps(buf, cur, w_ref, width, tm):
    y = cur * w_ref[width - 1:width, :]
    for i in range(width - 1):
        y = y + buf[i * SUBLANES:i * SUBLANES + tm, :] * w_ref[i:i + 1, :]
    return y


def _seq_to_tm(src_ref, scr):
    nb, tt, c = src_ref.shape
    for b in range(nb):
        for j in range(c // LANES):
            scr[j, pl.ds(b, tt, stride=nb), :] = src_ref[b, :, j * LANES:(j + 1) * LANES].astype(F32)
    return jnp.concatenate([scr[j] for j in range(c // LANES)], axis=-1)


def _tm_to_seq(val, scr, dst_refs):
    tm, c = val.shape
    tt = tm // SUBLANES
    for j in range(c // LANES):
        scr[j] = val[:, j * LANES:(j + 1) * LANES]
    for b in range(SUBLANES):
        j0 = 0
        for ref, ci in dst_refs:
            nj = ci // LANES
            parts = [scr[j0 + j, pl.ds(b, tt, stride=SUBLANES), :] for j in range(nj)]
            ref[b] = (parts[0] if nj == 1 else jnp.concatenate(parts, axis=-1)).astype(ref.dtype)
            j0 += nj


def _memkv_kernel(mem_ref, g_ref, wk_ref, kn_ref, wv_ref, k_ref, v_ref):
    hm = _rms(mem_ref[...], g_ref[0]).astype(BF16)
    k_ref[0] = _head_rms(_dot(hm, wk_ref[0]), X_HEAD_DIM, kn_ref[0])
    v_ref[0] = _dot(hm, wv_ref[0])


def _mem_kv(mem, g_mem, w_xk, x_k_norm, w_xv):
    rows = mem.shape[0]
    tm = min(512, rows)
    d = D_MODEL
    kn = jnp.tile(x_k_norm, (1, X_HEADS)).reshape(DEPTH, 1, d)
    out = jax.ShapeDtypeStruct((DEPTH, rows, d), F32)
    return pl.pallas_call(
        _memkv_kernel,
        grid=(DEPTH, rows // tm),
        in_specs=[
            pl.BlockSpec((tm, d), lambda l, i: (i, 0)),
            pl.BlockSpec((1, 1, d), lambda l, i: (l, 0, 0)),
            pl.BlockSpec((1, d, d), lambda l, i: (l, 0, 0)),
            pl.BlockSpec((1, 1, d), lambda l, i: (l, 0, 0)),
            pl.BlockSpec((1, d, d), lambda l, i: (l, 0, 0)),
        ],
        out_specs=[pl.BlockSpec((1, tm, d), lambda l, i: (l, i, 0))] * 2,
        out_shape=[out, out],
        compiler_params=_params(("arbitrary", "arbitrary")),
        name="mem_kv",
    )(mem, g_mem.reshape(DEPTH, 1, d), w_xk.astype(BF16), kn, w_xv.astype(BF16))


def _even_in_kernel(x_ref, g_ref, w_ref, cw_ref, qn_ref, kn_ref, hist_ref,
                    xt_ref, ya_ref, q_ref, iq_ref, ikw_ref, k_ref, v_ref, uh_ref, ubuf, xs, cs):
    tm = xt_ref.shape[0]
    nh = (A_CONV - 1) * SUBLANES

    @pl.when(pl.program_id(0) == 0)
    def _():
        ubuf[0:nh, :] = hist_ref[...]

    x = _seq_to_tm(x_ref, xs)
    xt_ref[...] = x
    xn = _rms(x, g_ref[...]).astype(BF16)

    def proj(a, b):
        return _dot(xn, w_ref[:, a:b])

    zc = proj(A_WIDTH, 3 * A_WIDTH)
    u = zc[:, :A_WIDTH] * zc[:, A_WIDTH:]
    ubuf[nh:nh + tm, :] = u
    conv = _conv_taps(ubuf, u, cw_ref, A_CONV, tm)
    ya_ref[...] = (proj(0, A_WIDTH) * conv).astype(BF16)
    tail = ubuf[tm:tm + nh, :]
    ubuf[0:nh, :] = tail
    uh_ref[...] = tail

    o = 3 * A_WIDTH
    nq = B_HEADS * B_HEAD_DIM
    nkv = B_KV_HEADS * B_HEAD_DIM
    ni = IDX_HEADS * IDX_DIM
    q = _head_rms(proj(o, o + nq), B_HEAD_DIM, qn_ref[...]) * (B_HEAD_DIM ** -0.5 * LOG2E)
    zkv = proj(o + nq, o + nq + 2 * nkv)
    k = _head_rms(zkv[:, :nkv], B_HEAD_DIM, kn_ref[...])
    zi = proj(o + nq + 2 * nkv, EVEN_IN_PAD)
    seq = jnp.concatenate([q, zi, k, zkv[:, nkv:]], axis=-1)
    _tm_to_seq(seq, cs, [(q_ref, nq), (iq_ref, ni), (ikw_ref, LANES), (k_ref, nkv), (v_ref, nkv)])


def _even_in(x, g, w_pad, conv_w, qn, kn, hist, tt):
    nb, t_len, d = x.shape
    tm = tt * nb
    rows = t_len * nb
    nh = (A_CONV - 1) * SUBLANES
    nq = B_HEADS * B_HEAD_DIM
    nkv = B_KV_HEADS * B_HEAD_DIM
    ni = IDX_HEADS * IDX_DIM
    nseq = nq + ni + LANES + 2 * nkv

    def row(c):
        return pl.BlockSpec((tm, c), lambda i: (i, 0))

    def seq(c):
        return pl.BlockSpec((nb, tt, c), lambda i: (0, i, 0))

    def seq_shape(c, dt):
        return jax.ShapeDtypeStruct((nb, t_len, c), dt)

    return pl.pallas_call(
        _even_in_kernel,
        grid=(t_len // tt,),
        in_specs=[seq(d), _const_spec((1, d)), _const_spec((d, EVEN_IN_PAD)), _const_spec((A_CONV, A_WIDTH)),
                  _const_spec((1, nq)), _const_spec((1, nkv)), _const_spec((nh, A_WIDTH))],
        out_specs=[row(d), row(A_WIDTH), seq(nq), seq(ni), seq(LANES), seq(nkv), seq(nkv),
                   pl.BlockSpec((nh, A_WIDTH), lambda i: (0, 0))],
        out_shape=[jax.ShapeDtypeStruct((rows, d), F32), jax.ShapeDtypeStruct((rows, A_WIDTH), BF16),
                   seq_shape(nq, BF16), seq_shape(ni, BF16), seq_shape(LANES, F32), seq_shape(nkv, F32),
                   seq_shape(nkv, F32), jax.ShapeDtypeStruct((nh, A_WIDTH), F32)],
        scratch_shapes=[pltpu.VMEM((tm + nh, A_WIDTH), F32), pltpu.VMEM((d // LANES, tm, LANES), F32),
                        pltpu.VMEM((nseq // LANES, tm, LANES), F32)],
        compiler_params=_params(("arbitrary",)),
        name="even_in",
    )(x, g, w_pad, conv_w, qn, kn, hist)


def _rel_bucket(rel):
    half = REL_BUCKETS // 2
    max_exact = half // 2
    n = -rel
    ret = jnp.where(n < 0, half, 0)
    n = jnp.abs(n)
    nf = jnp.maximum(n, 1).astype(F32)
    large = max_exact + (jnp.log(nf / max_exact) / math.log(REL_MAX_DIST / max_exact)
                         * (half - max_exact)).astype(I32)
    large = jnp.minimum(large, half - 1)
    return ret + jnp.where(n < max_exact, n, large)


def _bias_kernel(tab_ref, near_ref, far_ref):
    tk = near_ref.shape[-1]
    r = lax.broadcasted_iota(I32, (tk, tk), 0)
    c = lax.broadcasted_iota(I32, (tk, tk), 1)

    def lookup(bucket, h):
        def body(j, acc):
            return jnp.where(bucket == j, tab_ref[j, h], acc)
        return lax.fori_loop(0, REL_BUCKETS, body, jnp.zeros(bucket.shape, F32))

    for blk in range(2):
        bucket = _rel_bucket(r - c - blk * tk)
        for h in range(B_HEADS):
            near_ref[blk, h] = lookup(bucket, h) * LOG2E
    bucket = _rel_bucket(-REL_MAX_DIST - c[0:SUBLANES, :])
    for h in range(B_HEADS):
        far_ref[h] = lookup(bucket, h) * LOG2E


def _bias_tiles(rel_table, tk):
    return pl.pallas_call(
        _bias_kernel,
        in_specs=[pl.BlockSpec(memory_space=pltpu.SMEM)],
        out_shape=[jax.ShapeDtypeStruct((2, B_HEADS, tk, tk), F32),
                   jax.ShapeDtypeStruct((B_HEADS, SUBLANES, tk), F32)],
        name="rel_bias",
    )(rel_table)


def _dsa_kernel(q_ref, iq_ref, ikw_ref, k_ref, vt_ref, ik_ref, near_ref, far_ref, o_ref,
                skey, skh, skl, madd_s, s_s, p_s, qh_s, iqh_s, pad_s, w_s, m_s, acc_s,
                *, tq, qw, tk, past, length, ntop, idx_bits):
    i = pl.program_id(1)
    q0 = past + i * tq
    nkb = (q0 + tq + tk - 1) // tk
    last = nkb - 1
    ni = IDX_HEADS * IDX_DIM
    groups = B_HEADS // B_KV_HEADS
    lane = lax.broadcasted_iota(I32, (1, qw), 1)
    sub8 = lax.broadcasted_iota(I32, (SUBLANES, qw), 0)
    qchunk = jnp.right_shift(q0 + lane, CHUNK_SHIFT)

    if tq != qw:
        qh_s[...] = jnp.zeros(qh_s.shape, BF16)
        iqh_s[...] = jnp.zeros(iqh_s.shape, BF16)
        pad_s[...] = jnp.zeros(pad_s.shape, F32)
    for h in range(B_HEADS):
        qh_s[h, 0:tq, :] = q_ref[:, B_HEAD_DIM * h:B_HEAD_DIM * (h + 1)]
    for h in range(IDX_HEADS):
        iqh_s[h, 0:tq, :] = iq_ref[:, IDX_DIM * h:IDX_DIM * (h + 1)]
    pad_s[0:tq, :] = ikw_ref[...]
    w_s[...] = pad_s[...].T[IDX_DIM:IDX_DIM + IDX_HEADS, :] * (ni ** -0.5)

    slab = tk

    def scores(kb, masked):
        for sl in range(tk // slab):
            ks = pl.multiple_of(kb * tk + sl * slab, slab)
            ikb = ik_ref[0, pl.ds(ks, slab), :]
            acc = jnp.zeros((slab, qw), F32)
            for h in range(IDX_HEADS):
                acc = acc + w_s[h:h + 1, :] * jnp.maximum(_dot_t(ikb, iqh_s[h]), 0.0)
            if masked:
                kpos = kb * tk + sl * slab + lax.broadcasted_iota(I32, (slab, qw), 0)
                vis = (jnp.right_shift(kpos, CHUNK_SHIFT) <= qchunk) & (kpos < length)
                acc = jnp.where(vis, acc, -jnp.inf)
            bits = lax.bitcast_convert_type(acc, I32)
            key = jnp.where(bits < 0, bits ^ jnp.int32(0x7FFFFFFF), bits)
            rows = slice(sl * slab, (sl + 1) * slab)
            skey[kb, rows, :] = key
            skh[kb, rows, :] = jnp.right_shift(key, 16).astype(I16)
            skl[kb, rows, :] = ((key & 0xFFFF) - 32768).astype(I16)

    def score_pair(j, carry):
        scores(2 * j, False)
        scores(2 * j + 1, False)
        return carry

    lax.fori_loop(0, last // 2, score_pair, 0)

    @pl.when(last % 2 == 1)
    def _():
        scores(last - 1, False)

    scores(last, True)

    n_acc = 4
    prow = PACKED_ROWS

    @pl.when(nkb % 2 == 1)
    def _():
        skh[nkb] = jnp.full((tk, qw), -32768, I16)
        skl[nkb] = jnp.full((tk, qw), -32768, I16)

    def count16(ref, pred):
        def body(j, accs):
            accs = list(accs)
            for kb in (2 * j, 2 * j + 1):
                for g in range(tk // prow):
                    blk = ref[kb, g * prow:(g + 1) * prow, :]
                    accs[g % n_acc] = accs[g % n_acc] + jnp.where(pred(blk), jnp.int16(1), jnp.int16(0))
            return tuple(accs)
        accs = lax.fori_loop(0, (nkb + 1) // 2, body, tuple(jnp.zeros((prow, qw), I16) for _ in range(n_acc)))
        tot = (accs[0].astype(I32) + accs[1].astype(I32)) + (accs[2].astype(I32) + accs[3].astype(I32))
        return jnp.sum(tot, axis=0, keepdims=True)

    def rep16(v):
        return jnp.broadcast_to(v, (prow, qw)).astype(I16)

    def kth16(ref, kth):
        def bit_body(it, prefix):
            cand_u = prefix | jnp.left_shift(jnp.int32(1), 15 - it)
            cand = rep16(cand_u - 32768)
            cnt = count16(ref, lambda blk: blk >= cand)
            return jnp.where(cnt >= kth, cand_u, prefix)
        return lax.fori_loop(0, 16, bit_body, jnp.zeros((1, qw), I32)) - 32768

    def count(pred):
        def body(kb, accs):
            accs = list(accs)
            for g in range(tk // SUBLANES):
                blk = skey[kb, g * SUBLANES:(g + 1) * SUBLANES, :]
                accs[g % n_acc] = accs[g % n_acc] + jnp.where(pred(kb, g, blk), 1, 0)
            return tuple(accs)
        accs = lax.fori_loop(0, nkb, body, tuple(jnp.zeros((SUBLANES, qw), I32) for _ in range(n_acc)))
        tot = (accs[0] + accs[1]) + (accs[2] + accs[3])
        return jnp.sum(tot, axis=0, keepdims=True)

    def rep8(v):
        return jnp.broadcast_to(v, (SUBLANES, qw))

    p_hi = kth16(skh, ntop)
    hi16 = rep16(p_hi)
    above = count16(skh, lambda blk: blk > hi16)

    def low_body(kb, carry):
        for g in range(tk // prow):
            rows = slice(g * prow, (g + 1) * prow)
            skl[kb, rows, :] = jnp.where(skh[kb, rows, :] == hi16, skl[kb, rows, :], jnp.int16(-32768))
        return carry

    lax.fori_loop(0, nkb, low_body, 0)
    p_lo = kth16(skl, ntop - above)
    tau = p_hi * 65536 + (p_lo + 32768)
    tau8 = rep8(tau)

    cnt_ge = count(lambda kb, g, blk: blk >= tau8)
    key_ninf = jnp.int32(0x7FFFFFFF) ^ jnp.int32(-8388608)
    finite = tau != key_ninf
    tie_rows = (cnt_ge > ntop) & finite & (lane < tq)

    @pl.when(jnp.max(jnp.where(tie_rows, 1, 0)) > 0)
    def _():
        need8 = rep8(ntop - count(lambda kb, g, blk: blk > tau8))

        def xbit(it, xlim):
            cand8 = xlim | jnp.left_shift(jnp.int32(1), idx_bits - 1 - it)
            cnt = count(lambda kb, g, blk: (blk == tau8) & ((kb * tk + g * SUBLANES + sub8) < cand8))
            return jnp.where(rep8(cnt) <= need8, cand8, xlim)
        xlim8 = lax.fori_loop(0, idx_bits, xbit, jnp.zeros((SUBLANES, qw), I32))

        def demote(kb, carry):
            for g in range(tk // SUBLANES):
                rows = slice(g * SUBLANES, (g + 1) * SUBLANES)
                key = skey[kb, rows, :]
                late = (key == tau8) & ((kb * tk + g * SUBLANES + sub8) >= xlim8)
                skey[kb, rows, :] = jnp.where(late, key - 1, key)
            return carry

        lax.fori_loop(0, nkb, demote, 0)

    m_s[...] = jnp.full(m_s.shape, NEG_INF, F32)
    acc_s[...] = jnp.zeros(acc_s.shape, F32)
    vrows = B_HEAD_DIM + PACKED_ROWS
    ones_rows = jnp.ones((PACKED_ROWS, tk), BF16)

    tau_ge8 = rep8(jnp.where(finite, tau, tau + 1))

    def attend(kbs, near):
        nblk = len(kbs)
        tiles = [nblk - 1 - i for i in range(nblk)] if near else None
        alphas = {}
        for i, kb in enumerate(kbs):
            for g in range(tk // SUBLANES):
                rows = slice(g * SUBLANES, (g + 1) * SUBLANES)
                madd_s[i, rows, :] = jnp.where(skey[kb, rows, :] >= tau_ge8, 0.0, NEG_INF)

        def logits(h):
            for i, kb in enumerate(kbs):
                kn = k_ref[0, h // groups, pl.ds(pl.multiple_of(kb * tk, tk), tk), :]
                bias = madd_s[i] if tiles is None else madd_s[i] + near_ref[tiles[i], h]
                s_s[i, h] = (_dot_t(kn, qh_s[h]) + bias).astype(BF16)

        def softmax(h):
            tiles16 = [s_s[i, h, r * prow:(r + 1) * prow, :] for i in range(nblk) for r in range(tk // prow)]
            mx = tiles16[:n_acc]
            for j, t in enumerate(tiles16[n_acc:]):
                mx[j % n_acc] = jnp.maximum(mx[j % n_acc], t)
            mx = jnp.maximum(jnp.maximum(mx[0], mx[1]), jnp.maximum(mx[2], mx[3]))
            m_cur = jnp.max(mx.astype(F32), axis=0, keepdims=True)
            c_h = jnp.zeros((1, qw), F32) if near else far_ref[h, 0:1, 0:qw]
            m_prev = m_s[h:h + 1, :]
            shift = (jnp.maximum(m_prev, m_cur + c_h) - c_h).astype(BF16)
            m_new = shift.astype(F32) + c_h
            m_s[h:h + 1, :] = m_new
            alphas[h] = jnp.exp2(m_prev - m_new)
            shift = jnp.broadcast_to(shift, (prow, qw))
            for i in range(nblk):
                for r in range(tk // prow):
                    rows = slice(r * prow, (r + 1) * prow)
                    p_s[i, h, rows, :] = jnp.exp2(s_s[i, h, rows, :] - shift)

        def values(h):
            hs = slice(h * vrows, (h + 1) * vrows)
            pv = None
            for i in range(nblk):
                vt1 = jnp.concatenate([vt_ref[0, h // groups, kbs[i]], ones_rows], axis=0)
                pv = _dot(vt1, p_s[i, h]) if pv is None else pv + _dot(vt1, p_s[i, h])
            acc_s[hs, :] = acc_s[hs, :] * alphas[h] + pv

        for phase in (logits, softmax, values):
            for h in range(B_HEADS):
                phase(h)

    nfar = nkb - 2

    def far_pair(j, carry):
        attend([2 * j, 2 * j + 1], False)
        return carry

    lax.fori_loop(0, nfar // 2, far_pair, 0)

    @pl.when((nfar > 0) & (nfar % 2 == 1))
    def _():
        attend([nfar - 1], False)

    @pl.when(nkb >= 2)
    def _():
        attend([nkb - 2, last], True)

    @pl.when(nkb < 2)
    def _():
        attend([last], True)

    outs = []
    for h in range(B_HEADS):
        r0 = h * vrows
        outs.append(acc_s[r0:r0 + B_HEAD_DIM, :] / acc_s[r0 + B_HEAD_DIM:r0 + B_HEAD_DIM + 1, :])
    o = jnp.concatenate(outs, axis=0).T
    o_ref[...] = o[0:tq, :].astype(BF16)


def _dsa(q, iq, ikw, k_att, vt_att, ik_att, near, far, tq, past, length, ntop):
    tk = KEY_BLOCK
    nb, t_len, _ = q.shape
    lp = k_att.shape[2]
    vrows = B_HEAD_DIM + PACKED_ROWS
    nq = B_HEADS * B_HEAD_DIM
    ni = IDX_HEADS * IDX_DIM
    qw = max(tq, LANES)
    assert past % tk == 0 and (tq == tk or t_len == tq) and tq <= tk and lp % tk == 0
    idx_bits = lp.bit_length()

    def qblk(c):
        return pl.BlockSpec((None, tq, c), lambda b, i: (b, i, 0))

    kern = functools.partial(_dsa_kernel, tq=tq, qw=qw, tk=tk, past=past, length=length, ntop=ntop,
                             idx_bits=idx_bits)
    return pl.pallas_call(
        kern,
        grid=(nb, t_len // tq),
        in_specs=[qblk(nq), qblk(ni), qblk(LANES),
                  pl.BlockSpec((1, B_KV_HEADS, lp, B_HEAD_DIM), lambda b, i: (b, 0, 0, 0)),
                  pl.BlockSpec((1, B_KV_HEADS, lp // tk, B_HEAD_DIM, tk), lambda b, i: (b, 0, 0, 0, 0)),
                  pl.BlockSpec((1, lp, IDX_DIM), lambda b, i: (b, 0, 0)),
                  pl.BlockSpec((2, B_HEADS, tk, qw), lambda b, i: (0, 0, 0, 0)),
                  pl.BlockSpec((B_HEADS, SUBLANES, tk), lambda b, i: (0, 0, 0))],
        out_specs=qblk(nq),
        out_shape=jax.ShapeDtypeStruct((nb, t_len, nq), BF16),
        scratch_shapes=[pltpu.VMEM((lp // tk, tk, qw), I32),
                        pltpu.VMEM((lp // tk + 1, tk, qw), I16),
                        pltpu.VMEM((lp // tk + 1, tk, qw), I16),
                        pltpu.VMEM((2, tk, qw), F32),
                        pltpu.VMEM((2, B_HEADS, tk, qw), BF16),
                        pltpu.VMEM((2, B_HEADS, tk, qw), BF16),
                        pltpu.VMEM((B_HEADS, qw, B_HEAD_DIM), BF16),
                        pltpu.VMEM((IDX_HEADS, qw, IDX_DIM), BF16),
                        pltpu.VMEM((qw, LANES), F32),
                        pltpu.VMEM((IDX_HEADS, qw), F32),
                        pltpu.VMEM((B_HEADS, qw), F32),
                        pltpu.VMEM((B_HEADS * vrows, qw), F32)],
        compiler_params=_params(("arbitrary", "arbitrary")),
        name="dsa",
    )(q, iq, ikw, k_att, vt_att, ik_att, near, far)


def _xq_tail(x1, gx_ref, wxq_ref, qnx_ref, cs, qx_ref):
    xn = _rms(x1, gx_ref[...]).astype(BF16)
    qx = _head_rms(_dot(xn, wxq_ref[...]), X_HEAD_DIM, qnx_ref[...])
    _tm_to_seq(qx * (X_HEAD_DIM ** -0.5), cs, [(qx_ref, D_MODEL)])


def _even_out_kernel(x_ref, ya_ref, yb_ref, wo_ref, gx_ref, wxq_ref, qnx_ref, x1_ref, qx_ref, ys, cs):
    yb = _seq_to_tm(yb_ref, ys).astype(BF16)
    x1 = x_ref[...] + _dot(ya_ref[...], wo_ref[0:A_WIDTH, :]) + _dot(yb, wo_ref[A_WIDTH:, :])
    x1_ref[...] = x1
    _xq_tail(x1, gx_ref, wxq_ref, qnx_ref, cs, qx_ref)


def _even_out(x, ya, yb, w_out, gx, w_xq, qnx, tt, layer):
    nb, t_len, nyb = yb.shape
    tm = tt * nb
    rows = x.shape[0]
    d = D_MODEL

    def row(c):
        return pl.BlockSpec((tm, c), lambda i: (i, 0))

    def seq(c):
        return pl.BlockSpec((nb, tt, c), lambda i: (0, i, 0))

    return pl.pallas_call(
        _even_out_kernel,
        grid=(rows // tm,),
        in_specs=[row(d), row(A_WIDTH), seq(nyb), _const_spec((d, d)),
                  _const_spec((1, d)), _const_spec((d, d), layer), _const_spec((1, d))],
        out_specs=[row(d), seq(d)],
        out_shape=[jax.ShapeDtypeStruct((rows, d), F32), jax.ShapeDtypeStruct((nb, t_len, d), BF16)],
        scratch_shapes=[pltpu.VMEM((nyb // LANES, tm, LANES), F32), pltpu.VMEM((d // LANES, tm, LANES), F32)],
        compiler_params=_params(("arbitrary",)),
        name="even_out",
    )(x, ya, yb, w_out, gx, w_xq, qnx)


def _xattn_kernel(q_ref, mk_ref, mv_ref, o_ref):
    for h in range(X_HEADS):
        sl = slice(h * X_HEAD_DIM, (h + 1) * X_HEAD_DIM)
        s = _dot_t(q_ref[:, sl], mk_ref[0, :, sl])
        p = jnp.exp(s - jnp.max(s, axis=1, keepdims=True))
        o = _dot(p.astype(BF16), mv_ref[0, :, sl]) / jnp.sum(p, axis=1, keepdims=True)
        o_ref[:, sl] = o.astype(BF16)


def _xattn(qx, mk, mv, tq):
    nb, t_len, d = qx.shape
    m = mk.shape[1]
    return pl.pallas_call(
        _xattn_kernel,
        grid=(nb, t_len // tq),
        in_specs=[pl.BlockSpec((None, tq, d), lambda b, i: (b, i, 0)),
                  pl.BlockSpec((1, m, d), lambda b, i: (b, 0, 0)),
                  pl.BlockSpec((1, m, d), lambda b, i: (b, 0, 0))],
        out_specs=pl.BlockSpec((None, tq, d), lambda b, i: (b, i, 0)),
        out_shape=jax.ShapeDtypeStruct((nb, t_len, d), BF16),
        compiler_params=_params(("arbitrary", "arbitrary")),
        name="mem_attn",
    )(qx, mk, mv)


def _ffn_kernel(x_ref, o_ref, wxo_ref, g_ref, wup_ref, cw_ref, cb_ref, wdn_ref, hist_ref,
                y_ref, fh_ref, gbuf, cs, *, seq_out):
    tm = x_ref.shape[0]
    nh = (F_CONV - 1) * SUBLANES

    @pl.when(pl.program_id(0) == 0)
    def _():
        gbuf[0:nh, :] = hist_ref[...]

    x2 = x_ref[...] + _dot(_seq_to_tm(o_ref, cs).astype(BF16), wxo_ref[...])
    xn = _rms(x2, g_ref[...]).astype(BF16)
    val = _dot(xn, wup_ref[:, :D_FF])
    gate = _dot(xn, wup_ref[:, D_FF:])
    gbuf[nh:nh + tm, :] = gate
    conv = _conv_taps(gbuf, gate, cw_ref, F_CONV, tm) + cb_ref[...]
    act = (jax.nn.gelu(conv) * val).astype(BF16)
    y = x2 + _dot(act, wdn_ref[...])
    tail = gbuf[tm:tm + nh, :]
    gbuf[0:nh, :] = tail
    fh_ref[...] = tail
    if seq_out:
        _tm_to_seq(y, cs, [(y_ref, D_MODEL)])
    else:
        y_ref[...] = y


def _ffn(x, o, w_xo, g, w_up, conv_w, conv_b, w_down, hist, tt, layer, seq_out):
    nb, t_len, d = o.shape
    tm = tt * nb
    rows = x.shape[0]
    nh = (F_CONV - 1) * SUBLANES

    def row(c):
        return pl.BlockSpec((tm, c), lambda i: (i, 0))

    def seq(c):
        return pl.BlockSpec((nb, tt, c), lambda i: (0, i, 0))

    y_shape = jax.ShapeDtypeStruct((nb, t_len, d) if seq_out else (rows, d), F32)
    return pl.pallas_call(
        functools.partial(_ffn_kernel, seq_out=seq_out),
        grid=(rows // tm,),
        in_specs=[row(d), seq(d), _const_spec((d, d), layer), _const_spec((1, d)),
                  _const_spec((d, 2 * D_FF), layer), _const_spec((F_CONV, D_FF)), _const_spec((1, D_FF)),
                  _const_spec((D_FF, d), layer), _const_spec((nh, D_FF))],
        out_specs=[seq(d) if seq_out else row(d), pl.BlockSpec((nh, D_FF), lambda i: (0, 0))],
        out_shape=[y_shape, jax.ShapeDtypeStruct((nh, D_FF), F32)],
        scratch_shapes=[pltpu.VMEM((tm + nh, D_FF), F32), pltpu.VMEM((d // LANES, tm, LANES), F32)],
        compiler_params=_params(("arbitrary",)),
        name="ffn",
    )(x, o, w_xo, g, w_up, conv_w, conv_b, w_down, hist)


def _odd_kernel(x_ref, g_ref, win_ref, cw_ref, cb_ref, wai_ref, ba_ref, bi_ref, lam_ref,
                wo_ref, hist_ref, h0_ref, gx_ref, wxq_ref, qnx_ref,
                x1_ref, qx_ref, ch_ref, hl_ref, xbuf, a_s, b_s, h_s, cs, *, stream_start):
    tm = x_ref.shape[0]
    nh = (C_CONV - 1) * SUBLANES
    first = pl.program_id(0) == 0

    @pl.when(first)
    def _():
        xbuf[0:nh, :] = hist_ref[...]
        h_s[...] = h0_ref[...]

    x = x_ref[...]
    xn = _rms(x, g_ref[...]).astype(BF16)
    xr_in = _dot(xn, win_ref[:, RNN_WIDTH:])
    xbuf[nh:nh + tm, :] = xr_in
    xr = _conv_taps(xbuf, xr_in, cw_ref, C_CONV, tm) + cb_ref[...]
    tail = xbuf[tm:tm + nh, :]
    xbuf[0:nh, :] = tail
    ch_ref[...] = tail

    xrb = xr.astype(BF16)
    lam = -lam_ref[...]
    sp = jnp.maximum(lam, 0.0) + jnp.log1p(jnp.exp(-jnp.abs(lam)))
    gates = [_dot(xrb[:, n * RNN_BLOCK:(n + 1) * RNN_BLOCK], wai_ref[n]) for n in range(RNN_BLOCKS)]
    r = jax.nn.sigmoid(jnp.concatenate([g[:, :RNN_BLOCK] for g in gates], axis=-1) + ba_ref[...])
    ig = jax.nn.sigmoid(jnp.concatenate([g[:, RNN_BLOCK:] for g in gates], axis=-1) + bi_ref[...])
    log_a = -RG_C * r * sp
    a = jnp.exp(log_a)
    m2 = jnp.tanh(-log_a) * (1.0 + a * a)
    mult = jnp.where(m2 > 0.0, m2 * lax.rsqrt(m2), 0.0)
    if stream_start:
        rows = lax.broadcasted_iota(I32, (tm, RNN_WIDTH), 0)
        mult = jnp.where(first & (rows < SUBLANES), 1.0, mult)
    a_s[...] = a
    b_s[...] = mult * ig * xr

    def step(t, h):
        r0 = pl.multiple_of(t * SUBLANES, SUBLANES)
        h = a_s[pl.ds(r0, SUBLANES), :] * h + b_s[pl.ds(r0, SUBLANES), :]
        b_s[pl.ds(r0, SUBLANES), :] = h
        return h

    h = lax.fori_loop(0, tm // SUBLANES, step, h_s[...], unroll=True)
    h_s[...] = h
    hl_ref[...] = h

    gate = _dot(xn, win_ref[:, :RNN_WIDTH])
    act = (jax.nn.gelu(gate) * b_s[...]).astype(BF16)
    x1 = x + _dot(act, wo_ref[...])
    x1_ref[...] = x1
    _xq_tail(x1, gx_ref, wxq_ref, qnx_ref, cs, qx_ref)


def _odd(x, g, w_in, conv_w, conv_b, w_ai, b_a, b_i, lam, w_out, hist, h0, gx, w_xq, qnx, tt, layer,
         stream_start):
    nb = SUBLANES
    tm = tt * nb
    rows = x.shape[0]
    t_len = rows // nb
    d = D_MODEL
    r = RNN_WIDTH
    nh = (C_CONV - 1) * SUBLANES

    def row(c):
        return pl.BlockSpec((tm, c), lambda i: (i, 0))

    blk = (RNN_BLOCKS, RNN_BLOCK, 2 * RNN_BLOCK)
    return pl.pallas_call(
        functools.partial(_odd_kernel, stream_start=stream_start),
        grid=(rows // tm,),
        in_specs=[row(d), _const_spec((1, d)), _const_spec((d, 2 * r)), _const_spec((C_CONV, r)),
                  _const_spec((1, r)), _const_spec(blk), _const_spec((1, r)),
                  _const_spec((1, r)), _const_spec((1, r)), _const_spec((r, d)), _const_spec((nh, r)),
                  _const_spec((SUBLANES, r)), _const_spec((1, d)), _const_spec((d, d), layer), _const_spec((1, d))],
        out_specs=[row(d), pl.BlockSpec((nb, tt, d), lambda i: (0, i, 0)), pl.BlockSpec((nh, r), lambda i: (0, 0)),
                   pl.BlockSpec((SUBLANES, r), lambda i: (0, 0))],
        out_shape=[jax.ShapeDtypeStruct((rows, d), F32), jax.ShapeDtypeStruct((nb, t_len, d), BF16),
                   jax.ShapeDtypeStruct((nh, r), F32), jax.ShapeDtypeStruct((SUBLANES, r), F32)],
        scratch_shapes=[pltpu.VMEM((tm + nh, r), F32), pltpu.VMEM((tm, r), F32), pltpu.VMEM((tm, r), F32),
                        pltpu.VMEM((SUBLANES, r), F32), pltpu.VMEM((d // LANES, tm, LANES), F32)],
        compiler_params=_params(("arbitrary",)),
        name="odd_mixer",
    )(x, g, w_in, conv_w, conv_b, w_ai, b_a, b_i, lam, w_out, hist, h0, gx, w_xq, qnx)


def _to_tm(a):
    return jnp.transpose(a, (1, 0, 2)).reshape(a.shape[1] * a.shape[0], a.shape[2])


def _from_tm(a, w):
    return jnp.transpose(a.reshape(w, SUBLANES, a.shape[1]), (1, 0, 2))


def _trunk(x, st, mem_k, mem_v, p, bias, tt, tq_dsa, tq_x):
    nb, t_len, d = x.shape
    assert nb == SUBLANES
    past = 0 if st is None else st["b_k"].shape[2]
    length = past + t_len
    ntop = min(TOPK_MAX, length // 4)
    near, far = bias
    xt = None
    out = {}

    def hist(name, l, width, c):
        if st is None:
            return jnp.zeros(((width - 1) * nb, c), F32)
        return _to_tm(st[name][l])

    for l in range(DEPTH):
        if l % 2 == 0:
            e = l // 2
            assert l == 0, "the per-sequence input is converted by the first layer's kernel"
            xt, ya, q, iq, ikw, k, v, uh = _even_in(
                x, p["g_mix"][l], p["w_in_even"][e], p["a_conv_w"][e], p["b_q_norm"][e], p["b_k_norm"][e],
                hist("a_conv", e, A_CONV, A_WIDTH), tt)
            k_new = k.reshape(nb, t_len, B_KV_HEADS, B_HEAD_DIM)
            v_new = v.reshape(nb, t_len, B_KV_HEADS, B_HEAD_DIM)
            ik_new = ikw[:, :, :IDX_DIM]
            k_all, v_all, ik_all = k_new, v_new, ik_new
            if st is not None:
                k_all = jnp.concatenate([st["b_k"][e], k_new], axis=1)
                v_all = jnp.concatenate([st["b_v"][e], v_new], axis=1)
                ik_all = jnp.concatenate([st["b_kidx"][e], ik_new], axis=1)
            lp = -(-length // KEY_BLOCK) * KEY_BLOCK
            padl = lp - length
            k_att = jnp.pad(jnp.transpose(k_all, (0, 2, 1, 3)).astype(BF16), ((0, 0), (0, 0), (0, padl), (0, 0)))
            v_att = jnp.pad(jnp.transpose(v_all, (0, 2, 1, 3)).astype(BF16), ((0, 0), (0, 0), (0, padl), (0, 0)))
            vt_att = jnp.transpose(v_att.reshape(nb, B_KV_HEADS, lp // KEY_BLOCK, KEY_BLOCK, B_HEAD_DIM),
                                   (0, 1, 2, 4, 3))
            ik_att = jnp.pad(ik_all.astype(BF16), ((0, 0), (0, padl), (0, 0)))
            yb = _dsa(q, iq, ikw, k_att, vt_att, ik_att, near, far, tq_dsa, past, length, ntop)
            x1, qx = _even_out(xt, ya, yb, p["w_out_even"][e], p["g_x"][l], p["w_xq"], p["x_q_norm"][l], tt, l)
            out.setdefault("a_conv", []).append(_from_tm(uh, A_CONV - 1))
            out.setdefault("b_k", []).append(k_new)
            out.setdefault("b_v", []).append(v_new)
            out.setdefault("b_kidx", []).append(ik_new)
        else:
            o = l // 2
            h0 = jnp.zeros((nb, RNN_WIDTH), F32) if st is None else st["c_h"][o]
            x1, qx, ch, hl = _odd(
                xt, p["g_mix"][l], p["w_in_odd"][o], p["c_conv_w"][o], p["c_conv_b"][o], p["c_w_ai"][o],
                p["c_b_a"][o], p["c_b_i"][o], p["c_lambda"][o], p["w_out_odd"][o],
                hist("c_conv", o, C_CONV, RNN_WIDTH), h0, p["g_x"][l], p["w_xq"], p["x_q_norm"][l], tt, l,
                stream_start=(past == 0))
            out.setdefault("c_conv", []).append(_from_tm(ch, C_CONV - 1))
            out.setdefault("c_h", []).append(hl)
        xo = _xattn(qx, mem_k[l], mem_v[l], tq_x)
        xt, fh = _ffn(x1, xo, p["w_xo"], p["g_ffn"][l], p["w_up"], p["f_conv_w"][l], p["f_conv_b"][l],
                      p["w_down"], hist("f_conv", l, F_CONV, D_FF), tt, l, seq_out=(l == DEPTH - 1))
        out.setdefault("f_conv", []).append(_from_tm(fh, F_CONV - 1))
    return xt, {name: jnp.stack(v) for name, v in out.items()}


def kernel(x_prompt, x_sample, cache_b_k, cache_b_v, cache_b_kidx, state_a_conv, state_c_conv, state_c_h, state_ffn_conv, cache_mem_k, cache_mem_v, mem_prompt, rel_table, g_mix, w_in_even, a_conv_w, b_q_norm, b_k_norm, w_out_even, w_in_odd, c_conv_w, c_conv_b, c_w_a, c_b_a, c_w_i, c_b_i, c_lambda, w_out_odd, g_mem, g_x, w_xq, w_xk, w_xv, x_q_norm, x_k_norm, w_xo, g_ffn, w_up, f_conv_w, f_conv_b, w_down):
    d = D_MODEL
    bp, t_p, _ = x_prompt.shape
    m = mem_prompt.shape[1]

    def rowvec(a):
        return a.reshape(a.shape[0], 1, a.shape[-1])

    def mxu(a):
        return [a[l].astype(BF16) for l in range(a.shape[0])]

    p = {
        "g_mix": rowvec(g_mix), "g_x": rowvec(g_x), "g_ffn": rowvec(g_ffn),
        "w_in_even": [jnp.pad(w, ((0, 0), (0, EVEN_IN_PAD - EVEN_IN))) for w in mxu(w_in_even)],
        "a_conv_w": a_conv_w,
        "b_q_norm": rowvec(jnp.tile(b_q_norm, (1, B_HEADS))),
        "b_k_norm": rowvec(jnp.tile(b_k_norm, (1, B_KV_HEADS))),
        "w_out_even": mxu(w_out_even),
        "w_in_odd": mxu(w_in_odd), "c_conv_w": c_conv_w, "c_conv_b": rowvec(c_conv_b),
        "c_w_ai": mxu(jnp.concatenate([c_w_a, c_w_i], axis=-1)), "c_b_a": rowvec(c_b_a), "c_b_i": rowvec(c_b_i),
        "c_lambda": rowvec(c_lambda), "w_out_odd": mxu(w_out_odd),
        "w_xq": w_xq.astype(BF16), "x_q_norm": rowvec(jnp.tile(x_q_norm, (1, X_HEADS))),
        "w_xo": w_xo.astype(BF16), "w_up": w_up.astype(BF16), "f_conv_w": f_conv_w,
        "f_conv_b": rowvec(f_conv_b), "w_down": w_down.astype(BF16),
    }
    bias = _bias_tiles(rel_table, KEY_BLOCK)

    mk, mv = _mem_kv(mem_prompt.reshape(bp * m, d), g_mem, w_xk, x_k_norm, w_xv)
    p_mem_k = mk.reshape(DEPTH, bp, m, X_HEADS, X_HEAD_DIM)
    p_mem_v = mv.reshape(DEPTH, bp, m, X_HEADS, X_HEAD_DIM)
    y_prompt, new_p = _trunk(x_prompt, None, mk.reshape(DEPTH, bp, m, d).astype(BF16),
                             mv.reshape(DEPTH, bp, m, d).astype(BF16), p, bias,
                             tt=64, tq_dsa=KEY_BLOCK, tq_x=t_p)

    bs, t_s, _ = x_sample.shape
    st_s = {"b_k": cache_b_k, "b_v": cache_b_v, "b_kidx": cache_b_kidx, "a_conv": state_a_conv,
            "c_conv": state_c_conv, "c_h": state_c_h, "f_conv": state_ffn_conv}
    ms = cache_mem_k.shape[2]
    y_sample, new_s = _trunk(x_sample, st_s, cache_mem_k.reshape(DEPTH, bs, ms, d).astype(BF16),
                             cache_mem_v.reshape(DEPTH, bs, ms, d).astype(BF16), p, bias,
                             tt=t_s, tq_dsa=t_s, tq_x=t_s)
    return (y_prompt, y_sample,
            new_p["b_k"], new_p["b_v"], new_p["b_kidx"], new_p["a_conv"], new_p["c_conv"],
            new_p["c_h"], new_p["f_conv"], p_mem_k, p_mem_v,
            new_s["b_k"], new_s["b_v"], new_s["b_kidx"], new_s["a_conv"], new_s["c_conv"],
            new_s["c_h"], new_s["f_conv"])
```

```python
import functools
import math

import jax
import jax.numpy as jnp
from jax import lax
from jax.experimental import pallas as pl
from jax.experimental.pallas import tpu as pltpu

F32 = jnp.float32
BF16 = jnp.bfloat16
I32 = jnp.int32
I16 = jnp.int16

D_MODEL = 1024
DEPTH = 2
CHUNK = 64
CHUNK_SHIFT = CHUNK.bit_length() - 1
EPS = 1e-6
NEG_INF = -1e30
LOG2E = math.log2(math.e)
A_WIDTH = 512
A_CONV = 3
B_HEADS = 8
B_KV_HEADS = 2
B_HEAD_DIM = 64
IDX_HEADS = 8
IDX_DIM = 32
TOPK_MAX = 256
REL_BUCKETS = 32
REL_MAX_DIST = 128
RNN_WIDTH = 1024
RNN_BLOCKS = 8
RNN_BLOCK = 128
C_CONV = 4
RG_C = 8.0
X_HEADS = 4
X_HEAD_DIM = 256
D_FF = 2816
F_CONV = 3
EVEN_IN = 2600

SUBLANES = 8
LANES = 128
PACKED_ROWS = 16
VMEM_LIMIT = 56 * 1024 * 1024

EVEN_IN_PAD = 2688
KEY_BLOCK = 256
INT_MIN = -2147483648


def _params(sem, vmem=VMEM_LIMIT):
    return pltpu.CompilerParams(dimension_semantics=sem, vmem_limit_bytes=vmem)


def _const_spec(shape, layer=None):
    nd = len(shape)
    if layer is None:
        return pl.BlockSpec(shape, lambda *_: (0,) * nd, pipeline_mode=pl.Buffered(1))
    return pl.BlockSpec((None,) + tuple(shape), lambda *_: (layer,) + (0,) * nd, pipeline_mode=pl.Buffered(1))


def _rms(x, g):
    ms = jnp.mean(x * x, axis=-1, keepdims=True)
    return x * lax.rsqrt(ms + EPS) * g


def _head_rms(x, hd, gain):
    m, c = x.shape
    s = x * x
    parts = []
    if hd >= LANES:
        for h in range(c // hd):
            ms = jnp.mean(s[:, h * hd:(h + 1) * hd], axis=-1, keepdims=True)
            parts.append(x[:, h * hd:(h + 1) * hd] * lax.rsqrt(ms + EPS))
    else:
        lane = lax.broadcasted_iota(I32, (m, LANES), 1)
        for j in range(c // LANES):
            sj = s[:, j * LANES:(j + 1) * LANES]
            inv = jnp.zeros((m, LANES), F32)
            for k in range(LANES // hd):
                msk = (lane >= k * hd) & (lane < (k + 1) * hd)
                ms = jnp.sum(jnp.where(msk, sj, 0.0), axis=-1, keepdims=True) * (1.0 / hd)
                inv = jnp.where(msk, lax.rsqrt(ms + EPS), inv)
            parts.append(x[:, j * LANES:(j + 1) * LANES] * inv)
    y = parts[0] if len(parts) == 1 else jnp.concatenate(parts, axis=-1)
    return y * gain


def _dot(a, b):
    return jnp.dot(a, b, preferred_element_type=F32)


def _dot_t(a, b):
    return lax.dot_general(a, b, (((1,), (1,)), ((), ())), preferred_element_type=F32)


def _conv_taps(buf, cur, w_ref, width, tm):
    y = cur * w_ref[width - 1:width, :]
    for i in range(width - 1):
        y = y + buf[i * SUBLANES:i * SUBLANES + tm, :] * w_ref[i:i + 1, :]
    return y


def _seq_to_tm(src_ref, scr):
    nb, tt, c = src_ref.shape
    for b in range(nb):
        for j in range(c // LANES):
            scr[j, pl.ds(b, tt, stride=nb), :] = src_ref[b, :, j * LANES:(j + 1) * LANES].astype(F32)
    return jnp.concatenate([scr[j] for j in range(c // LANES)], axis=-1)


def _tm_to_seq(val, scr, dst_refs):
    tm, c = val.shape
    tt = tm // SUBLANES
    for j in range(c // LANES):
        scr[j] = val[:, j * LANES:(j + 1) * LANES]
    for b in range(SUBLANES):
        j0 = 0
        for ref, ci in dst_refs:
            nj = ci // LANES
            parts = [scr[j0 + j, pl.ds(b, tt, stride=SUBLANES), :] for j in range(nj)]
            ref[b] = (parts[0] if nj == 1 else jnp.concatenate(parts, axis=-1)).astype(ref.dtype)
            j0 += nj


def _memkv_kernel(mem_ref, g_ref, wk_ref, kn_ref, wv_ref, k_ref, v_ref):
    hm = _rms(mem_ref[...], g_ref[0]).astype(BF16)
    k_ref[0] = _head_rms(_dot(hm, wk_ref[0]), X_HEAD_DIM, kn_ref[0])
    v_ref[0] = _dot(hm, wv_ref[0])


def _mem_kv(mem, g_mem, w_xk, x_k_norm, w_xv):
    rows = mem.shape[0]
    tm = min(512, rows)
    d = D_MODEL
    kn = jnp.tile(x_k_norm, (1, X_HEADS)).reshape(DEPTH, 1, d)
    out = jax.ShapeDtypeStruct((DEPTH, rows, d), F32)
    return pl.pallas_call(
        _memkv_kernel,
        grid=(DEPTH, rows // tm),
        in_specs=[
            pl.BlockSpec((tm, d), lambda l, i: (i, 0)),
            pl.BlockSpec((1, 1, d), lambda l, i: (l, 0, 0)),
            pl.BlockSpec((1, d, d), lambda l, i: (l, 0, 0)),
            pl.BlockSpec((1, 1, d), lambda l, i: (l, 0, 0)),
            pl.BlockSpec((1, d, d), lambda l, i: (l, 0, 0)),
        ],
        out_specs=[pl.BlockSpec((1, tm, d), lambda l, i: (l, i, 0))] * 2,
        out_shape=[out, out],
        compiler_params=_params(("arbitrary", "arbitrary")),
        name="mem_kv",
    )(mem, g_mem.reshape(DEPTH, 1, d), w_xk.astype(BF16), kn, w_xv.astype(BF16))


def _even_in_kernel(x_ref, g_ref, w_ref, cw_ref, qn_ref, kn_ref, hist_ref,
                    xt_ref, ya_ref, q_ref, iq_ref, ikw_ref, k_ref, v_ref, uh_ref, ubuf, xs, cs):
    tm = xt_ref.shape[0]
    nh = (A_CONV - 1) * SUBLANES

    @pl.when(pl.program_id(0) == 0)
    def _():
        ubuf[0:nh, :] = hist_ref[...]

    x = _seq_to_tm(x_ref, xs)
    xt_ref[...] = x
    xn = _rms(x, g_ref[...]).astype(BF16)

    def proj(a, b):
        return _dot(xn, w_ref[:, a:b])

    zc = proj(A_WIDTH, 3 * A_WIDTH)
    u = zc[:, :A_WIDTH] * zc[:, A_WIDTH:]
    ubuf[nh:nh + tm, :] = u
    conv = _conv_taps(ubuf, u, cw_ref, A_CONV, tm)
    ya_ref[...] = (proj(0, A_WIDTH) * conv).astype(BF16)
    tail = ubuf[tm:tm + nh, :]
    ubuf[0:nh, :] = tail
    uh_ref[...] = tail

    o = 3 * A_WIDTH
    nq = B_HEADS * B_HEAD_DIM
    nkv = B_KV_HEADS * B_HEAD_DIM
    ni = IDX_HEADS * IDX_DIM
    q = _head_rms(proj(o, o + nq), B_HEAD_DIM, qn_ref[...]) * (B_HEAD_DIM ** -0.5 * LOG2E)
    zkv = proj(o + nq, o + nq + 2 * nkv)
    k = _head_rms(zkv[:, :nkv], B_HEAD_DIM, kn_ref[...])
    zi = proj(o + nq + 2 * nkv, EVEN_IN_PAD)
    seq = jnp.concatenate([q, zi, k, zkv[:, nkv:]], axis=-1)
    _tm_to_seq(seq, cs, [(q_ref, nq), (iq_ref, ni), (ikw_ref, LANES), (k_ref, nkv), (v_ref, nkv)])


def _even_in(x, g, w_pad, conv_w, qn, kn, hist, tt):
    nb, t_len, d = x.shape
    tm = tt * nb
    rows = t_len * nb
    nh = (A_CONV - 1) * SUBLANES
    nq = B_HEADS * B_HEAD_DIM
    nkv = B_KV_HEADS * B_HEAD_DIM
    ni = IDX_HEADS * IDX_DIM
    nseq = nq + ni + LANES + 2 * nkv

    def row(c):
        return pl.BlockSpec((tm, c), lambda i: (i, 0))

    def seq(c):
        return pl.BlockSpec((nb, tt, c), lambda i: (0, i, 0))

    def seq_shape(c, dt):
        return jax.ShapeDtypeStruct((nb, t_len, c), dt)

    return pl.pallas_call(
        _even_in_kernel,
        grid=(t_len // tt,),
        in_specs=[seq(d), _const_spec((1, d)), _const_spec((d, EVEN_IN_PAD)), _const_spec((A_CONV, A_WIDTH)),
                  _const_spec((1, nq)), _const_spec((1, nkv)), _const_spec((nh, A_WIDTH))],
        out_specs=[row(d), row(A_WIDTH), seq(nq), seq(ni), seq(LANES), seq(nkv), seq(nkv),
                   pl.BlockSpec((nh, A_WIDTH), lambda i: (0, 0))],
        out_shape=[jax.ShapeDtypeStruct((rows, d), F32), jax.ShapeDtypeStruct((rows, A_WIDTH), BF16),
                   seq_shape(nq, BF16), seq_shape(ni, BF16), seq_shape(LANES, F32), seq_shape(nkv, F32),
                   seq_shape(nkv, F32), jax.ShapeDtypeStruct((nh, A_WIDTH), F32)],
        scratch_shapes=[pltpu.VMEM((tm + nh, A_WIDTH), F32), pltpu.VMEM((d // LANES, tm, LANES), F32),
                        pltpu.VMEM((nseq // LANES, tm, LANES), F32)],
        compiler_params=_params(("arbitrary",)),
        name="even_in",
    )(x, g, w_pad, conv_w, qn, kn, hist)


def _rel_bucket(rel):
    half = REL_BUCKETS // 2
    max_exact = half // 2
    n = -rel
    ret = jnp.where(n < 0, half, 0)
    n = jnp.abs(n)
    nf = jnp.maximum(n, 1).astype(F32)
    large = max_exact + (jnp.log(nf / max_exact) / math.log(REL_MAX_DIST / max_exact)
                         * (half - max_exact)).astype(I32)
    large = jnp.minimum(large, half - 1)
    return ret + jnp.where(n < max_exact, n, large)


def _bias_kernel(tab_ref, near_ref, far_ref):
    tk = near_ref.shape[-1]
    r = lax.broadcasted_iota(I32, (tk, tk), 0)
    c = lax.broadcasted_iota(I32, (tk, tk), 1)

    def lookup(bucket, h):
        def body(j, acc):
            return jnp.where(bucket == j, tab_ref[j, h], acc)
        return lax.fori_loop(0, REL_BUCKETS, body, jnp.zeros(bucket.shape, F32))

    for blk in range(2):
        bucket = _rel_bucket(r - c - blk * tk)
        for h in range(B_HEADS):
            near_ref[blk, h] = lookup(bucket, h) * LOG2E
    bucket = _rel_bucket(-REL_MAX_DIST - c[0:SUBLANES, :])
    for h in range(B_HEADS):
        far_ref[h] = lookup(bucket, h) * LOG2E


def _bias_tiles(rel_table, tk):
    return pl.pallas_call(
        _bias_kernel,
        in_specs=[pl.BlockSpec(memory_space=pltpu.SMEM)],
        out_shape=[jax.ShapeDtypeStruct((2, B_HEADS, tk, tk), F32),
                   jax.ShapeDtypeStruct((B_HEADS, SUBLANES, tk), F32)],
        name="rel_bias",
    )(rel_table)


def _dsa_kernel(q_ref, iq_ref, ikw_ref, k_ref, vt_ref, ik_ref, near_ref, far_ref, o_ref,
                skey, skh, skl, madd_s, s_s, p_s, qh_s, iqh_s, pad_s, w_s, m_s, acc_s,
                *, tq, qw, tk, past, length, ntop, idx_bits):
    i = pl.program_id(1)
    q0 = past + i * tq
    nkb = (q0 + tq + tk - 1) // tk
    last = nkb - 1
    ni = IDX_HEADS * IDX_DIM
    groups = B_HEADS // B_KV_HEADS
    lane = lax.broadcasted_iota(I32, (1, qw), 1)
    sub8 = lax.broadcasted_iota(I32, (SUBLANES, qw), 0)
    qchunk = jnp.right_shift(q0 + lane, CHUNK_SHIFT)

    if tq != qw:
        qh_s[...] = jnp.zeros(qh_s.shape, BF16)
        iqh_s[...] = jnp.zeros(iqh_s.shape, BF16)
        pad_s[...] = jnp.zeros(pad_s.shape, F32)
    for h in range(B_HEADS):
        qh_s[h, 0:tq, :] = q_ref[:, B_HEAD_DIM * h:B_HEAD_DIM * (h + 1)]
    for h in range(IDX_HEADS):
        iqh_s[h, 0:tq, :] = iq_ref[:, IDX_DIM * h:IDX_DIM * (h + 1)]
    pad_s[0:tq, :] = ikw_ref[...]
    w_s[...] = pad_s[...].T[IDX_DIM:IDX_DIM + IDX_HEADS, :] * (ni ** -0.5)

    slab = tk

    def scores(kb, masked):
        for sl in range(tk // slab):
            ks = pl.multiple_of(kb * tk + sl * slab, slab)
            ikb = ik_ref[0, pl.ds(ks, slab), :]
            acc = jnp.zeros((slab, qw), F32)
            for h in range(IDX_HEADS):
                acc = acc + w_s[h:h + 1, :] * jnp.maximum(_dot_t(ikb, iqh_s[h]), 0.0)
            if masked:
                kpos = kb * tk + sl * slab + lax.broadcasted_iota(I32, (slab, qw), 0)
                vis = (jnp.right_shift(kpos, CHUNK_SHIFT) <= qchunk) & (kpos < length)
                acc = jnp.where(vis, acc, -jnp.inf)
            bits = lax.bitcast_convert_type(acc, I32)
            key = jnp.where(bits < 0, bits ^ jnp.int32(0x7FFFFFFF), bits)
            rows = slice(sl * slab, (sl + 1) * slab)
            skey[kb, rows, :] = key
            skh[kb, rows, :] = jnp.right_shift(key, 16).astype(I16)
            skl[kb, rows, :] = ((key & 0xFFFF) - 32768).astype(I16)

    def score_pair(j, carry):
        scores(2 * j, False)
        scores(2 * j + 1, False)
        return carry

    lax.fori_loop(0, last // 2, score_pair, 0)

    @pl.when(last % 2 == 1)
    def _():
        scores(last - 1, False)

    scores(last, True)

    n_acc = 4
    prow = PACKED_ROWS

    @pl.when(nkb % 2 == 1)
    def _():
        skh[nkb] = jnp.full((tk, qw), -32768, I16)
        skl[nkb] = jnp.full((tk, qw), -32768, I16)

    def count16(ref, pred):
        def body(j, accs):
            accs = list(accs)
            for kb in (2 * j, 2 * j + 1):
                for g in range(tk // prow):
                    blk = ref[kb, g * prow:(g + 1) * prow, :]
                    accs[g % n_acc] = accs[g % n_acc] + jnp.where(pred(blk), jnp.int16(1), jnp.int16(0))
            return tuple(accs)
        accs = lax.fori_loop(0, (nkb + 1) // 2, body, tuple(jnp.zeros((prow, qw), I16) for _ in range(n_acc)))
        tot = (accs[0].astype(I32) + accs[1].astype(I32)) + (accs[2].astype(I32) + accs[3].astype(I32))
        return jnp.sum(tot, axis=0, keepdims=True)

    def rep16(v):
        return jnp.broadcast_to(v, (prow, qw)).astype(I16)

    def kth16(ref, kth):
        def bit_body(it, prefix):
            cand_u = prefix | jnp.left_shift(jnp.int32(1), 15 - it)
            cand = rep16(cand_u - 32768)
            cnt = count16(ref, lambda blk: blk >= cand)
            return jnp.where(cnt >= kth, cand_u, prefix)
        return lax.fori_loop(0, 16, bit_body, jnp.zeros((1, qw), I32)) - 32768

    def count(pred):
        def body(kb, accs):
            accs = list(accs)
            for g in range(tk // SUBLANES):
                blk = skey[kb, g * SUBLANES:(g + 1) * SUBLANES, :]
                accs[g % n_acc] = accs[g % n_acc] + jnp.where(pred(kb, g, blk), 1, 0)
            return tuple(accs)
        accs = lax.fori_loop(0, nkb, body, tuple(jnp.zeros((SUBLANES, qw), I32) for _ in range(n_acc)))
        tot = (accs[0] + accs[1]) + (accs[2] + accs[3])
        return jnp.sum(tot, axis=0, keepdims=True)

    def rep8(v):
        return jnp.broadcast_to(v, (SUBLANES, qw))

    p_hi = kth16(skh, ntop)
    hi16 = rep16(p_hi)
    above = count16(skh, lambda blk: blk > hi16)

    def low_body(kb, carry):
        for g in range(tk // prow):
            rows = slice(g * prow, (g + 1) * prow)
            skl[kb, rows, :] = jnp.where(skh[kb, rows, :] == hi16, skl[kb, rows, :], jnp.int16(-32768))
        return carry

    lax.fori_loop(0, nkb, low_body, 0)
    p_lo = kth16(skl, ntop - above)
    tau = p_hi * 65536 + (p_lo + 32768)
    tau8 = rep8(tau)

    cnt_ge = count(lambda kb, g, blk: blk >= tau8)
    key_ninf = jnp.int32(0x7FFFFFFF) ^ jnp.int32(-8388608)
    finite = tau != key_ninf
    tie_rows = (cnt_ge > ntop) & finite & (lane < tq)

    @pl.when(jnp.max(jnp.where(tie_rows, 1, 0)) > 0)
    def _():
        need8 = rep8(ntop - count(lambda kb, g, blk: blk > tau8))

        def xbit(it, xlim):
            cand8 = xlim | jnp.left_shift(jnp.int32(1), idx_bits - 1 - it)
            cnt = count(lambda kb, g, blk: (blk == tau8) & ((kb * tk + g * SUBLANES + sub8) < cand8))
            return jnp.where(rep8(cnt) <= need8, cand8, xlim)
        xlim8 = lax.fori_loop(0, idx_bits, xbit, jnp.zeros((SUBLANES, qw), I32))

        def demote(kb, carry):
            for g in range(tk // SUBLANES):
                rows = slice(g * SUBLANES, (g + 1) * SUBLANES)
                key = skey[kb, rows, :]
                late = (key == tau8) & ((kb * tk + g * SUBLANES + sub8) >= xlim8)
                skey[kb, rows, :] = jnp.where(late, key - 1, key)
            return carry

        lax.fori_loop(0, nkb, demote, 0)

    m_s[...] = jnp.full(m_s.shape, NEG_INF, F32)
    acc_s[...] = jnp.zeros(acc_s.shape, F32)
    vrows = B_HEAD_DIM + PACKED_ROWS
    ones_rows = jnp.ones((PACKED_ROWS, tk), BF16)

    tau_ge8 = rep8(jnp.where(finite, tau, tau + 1))

    def attend(kbs, near):
        nblk = len(kbs)
        tiles = [nblk - 1 - i for i in range(nblk)] if near else None
        alphas = {}
        for i, kb in enumerate(kbs):
            for g in range(tk // SUBLANES):
                rows = slice(g * SUBLANES, (g + 1) * SUBLANES)
                madd_s[i, rows, :] = jnp.where(skey[kb, rows, :] >= tau_ge8, 0.0, NEG_INF)

        def logits(h):
            for i, kb in enumerate(kbs):
                kn = k_ref[0, h // groups, pl.ds(pl.multiple_of(kb * tk, tk), tk), :]
                bias = madd_s[i] if tiles is None else madd_s[i] + near_ref[tiles[i], h]
                s_s[i, h] = (_dot_t(kn, qh_s[h]) + bias).astype(BF16)

        def softmax(h):
            tiles16 = [s_s[i, h, r * prow:(r + 1) * prow, :] for i in range(nblk) for r in range(tk // prow)]
            mx = tiles16[:n_acc]
            for j, t in enumerate(tiles16[n_acc:]):
                mx[j % n_acc] = jnp.maximum(mx[j % n_acc], t)
            mx = jnp.maximum(jnp.maximum(mx[0], mx[1]), jnp.maximum(mx[2], mx[3]))
            m_cur = jnp.max(mx.astype(F32), axis=0, keepdims=True)
            c_h = jnp.zeros((1, qw), F32) if near else far_ref[h, 0:1, 0:qw]
            m_prev = m_s[h:h + 1, :]
            shift = (jnp.maximum(m_prev, m_cur + c_h) - c_h).astype(BF16)
            m_new = shift.astype(F32) + c_h
            m_s[h:h + 1, :] = m_new
            alphas[h] = jnp.exp2(m_prev - m_new)
            shift = jnp.broadcast_to(shift, (prow, qw))
            for i in range(nblk):
                for r in range(tk // prow):
                    rows = slice(r * prow, (r + 1) * prow)
                    p_s[i, h, rows, :] = jnp.exp2(s_s[i, h, rows, :] - shift)

        def values(h):
            hs = slice(h * vrows, (h + 1) * vrows)
            pv = None
            for i in range(nblk):
                vt1 = jnp.concatenate([vt_ref[0, h // groups, kbs[i]], ones_rows], axis=0)
                pv = _dot(vt1, p_s[i, h]) if pv is None else pv + _dot(vt1, p_s[i, h])
            acc_s[hs, :] = acc_s[hs, :] * alphas[h] + pv

        for phase in (logits, softmax, values):
            for h in range(B_HEADS):
                phase(h)

    nfar = nkb - 2

    def far_pair(j, carry):
        attend([2 * j, 2 * j + 1], False)
        return carry

    lax.fori_loop(0, nfar // 2, far_pair, 0)

    @pl.when((nfar > 0) & (nfar % 2 == 1))
    def _():
        attend([nfar - 1], False)

    @pl.when(nkb >= 2)
    def _():
        attend([nkb - 2, last], True)

    @pl.when(nkb < 2)
    def _():
        attend([last], True)

    outs = []
    for h in range(B_HEADS):
        r0 = h * vrows
        outs.append(acc_s[r0:r0 + B_HEAD_DIM, :] / acc_s[r0 + B_HEAD_DIM:r0 + B_HEAD_DIM + 1, :])
    o = jnp.concatenate(outs, axis=0).T
    o_ref[...] = o[0:tq, :].astype(BF16)


def _dsa(q, iq, ikw, k_att, vt_att, ik_att, near, far, tq, past, length, ntop):
    tk = KEY_BLOCK
    nb, t_len, _ = q.shape
    lp = k_att.shape[2]
    vrows = B_HEAD_DIM + PACKED_ROWS
    nq = B_HEADS * B_HEAD_DIM
    ni = IDX_HEADS * IDX_DIM
    qw = max(tq, LANES)
    assert past % tk == 0 and (tq == tk or t_len == tq) and tq <= tk and lp % tk == 0
    idx_bits = lp.bit_length()

    def qblk(c):
        return pl.BlockSpec((None, tq, c), lambda b, i: (b, i, 0))

    kern = functools.partial(_dsa_kernel, tq=tq, qw=qw, tk=tk, past=past, length=length, ntop=ntop,
                             idx_bits=idx_bits)
    return pl.pallas_call(
        kern,
        grid=(nb, t_len // tq),
        in_specs=[qblk(nq), qblk(ni), qblk(LANES),
                  pl.BlockSpec((1, B_KV_HEADS, lp, B_HEAD_DIM), lambda b, i: (b, 0, 0, 0)),
                  pl.BlockSpec((1, B_KV_HEADS, lp // tk, B_HEAD_DIM, tk), lambda b, i: (b, 0, 0, 0, 0)),
                  pl.BlockSpec((1, lp, IDX_DIM), lambda b, i: (b, 0, 0)),
                  pl.BlockSpec((2, B_HEADS, tk, qw), lambda b, i: (0, 0, 0, 0)),
                  pl.BlockSpec((B_HEADS, SUBLANES, tk), lambda b, i: (0, 0, 0))],
        out_specs=qblk(nq),
        out_shape=jax.ShapeDtypeStruct((nb, t_len, nq), BF16),
        scratch_shapes=[pltpu.VMEM((lp // tk, tk, qw), I32),
                        pltpu.VMEM((lp // tk + 1, tk, qw), I16),
                        pltpu.VMEM((lp // tk + 1, tk, qw), I16),
                        pltpu.VMEM((2, tk, qw), F32),
                        pltpu.VMEM((2, B_HEADS, tk, qw), BF16),
                        pltpu.VMEM((2, B_HEADS, tk, qw), BF16),
                        pltpu.VMEM((B_HEADS, qw, B_HEAD_DIM), BF16),
                        pltpu.VMEM((IDX_HEADS, qw, IDX_DIM), BF16),
                        pltpu.VMEM((qw, LANES), F32),
                        pltpu.VMEM((IDX_HEADS, qw), F32),
                        pltpu.VMEM((B_HEADS, qw), F32),
                        pltpu.VMEM((B_HEADS * vrows, qw), F32)],
        compiler_params=_params(("arbitrary", "arbitrary")),
        name="dsa",
    )(q, iq, ikw, k_att, vt_att, ik_att, near, far)


def _xq_tail(x1, gx_ref, wxq_ref, qnx_ref, cs, qx_ref):
    xn = _rms(x1, gx_ref[...]).astype(BF16)
    qx = _head_rms(_dot(xn, wxq_ref[...]), X_HEAD_DIM, qnx_ref[...])
    _tm_to_seq(qx * (X_HEAD_DIM ** -0.5), cs, [(qx_ref, D_MODEL)])


def _even_out_kernel(x_ref, ya_ref, yb_ref, wo_ref, gx_ref, wxq_ref, qnx_ref, x1_ref, qx_ref, ys, cs):
    yb = _seq_to_tm(yb_ref, ys).astype(BF16)
    x1 = x_ref[...] + _dot(ya_ref[...], wo_ref[0:A_WIDTH, :]) + _dot(yb, wo_ref[A_WIDTH:, :])
    x1_ref[...] = x1
    _xq_tail(x1, gx_ref, wxq_ref, qnx_ref, cs, qx_ref)


def _even_out(x, ya, yb, w_out, gx, w_xq, qnx, tt, layer):
    nb, t_len, nyb = yb.shape
    tm = tt * nb
    rows = x.shape[0]
    d = D_MODEL

    def row(c):
        return pl.BlockSpec((tm, c), lambda i: (i, 0))

    def seq(c):
        return pl.BlockSpec((nb, tt, c), lambda i: (0, i, 0))

    return pl.pallas_call(
        _even_out_kernel,
        grid=(rows // tm,),
        in_specs=[row(d), row(A_WIDTH), seq(nyb), _const_spec((d, d)),
                  _const_spec((1, d)), _const_spec((d, d), layer), _const_spec((1, d))],
        out_specs=[row(d), seq(d)],
        out_shape=[jax.ShapeDtypeStruct((rows, d), F32), jax.ShapeDtypeStruct((nb, t_len, d), BF16)],
        scratch_shapes=[pltpu.VMEM((nyb // LANES, tm, LANES), F32), pltpu.VMEM((d // LANES, tm, LANES), F32)],
        compiler_params=_params(("arbitrary",)),
        name="even_out",
    )(x, ya, yb, w_out, gx, w_xq, qnx)


def _xattn_kernel(q_ref, mk_ref, mv_ref, o_ref):
    for h in range(X_HEADS):
        sl = slice(h * X_HEAD_DIM, (h + 1) * X_HEAD_DIM)
        s = _dot_t(q_ref[:, sl], mk_ref[0, :, sl])
        p = jnp.exp(s - jnp.max(s, axis=1, keepdims=True))
        o = _dot(p.astype(BF16), mv_ref[0, :, sl]) / jnp.sum(p, axis=1, keepdims=True)
        o_ref[:, sl] = o.astype(BF16)


def _xattn(qx, mk, mv, tq):
    nb, t_len, d = qx.shape
    m = mk.shape[1]
    return pl.pallas_call(
        _xattn_kernel,
        grid=(nb, t_len // tq),
        in_specs=[pl.BlockSpec((None, tq, d), lambda b, i: (b, i, 0)),
                  pl.BlockSpec((1, m, d), lambda b, i: (b, 0, 0)),
                  pl.BlockSpec((1, m, d), lambda b, i: (b, 0, 0))],
        out_specs=pl.BlockSpec((None, tq, d), lambda b, i: (b, i, 0)),
        out_shape=jax.ShapeDtypeStruct((nb, t_len, d), BF16),
        compiler_params=_params(("arbitrary", "arbitrary")),
        name="mem_attn",
    )(qx, mk, mv)


def _ffn_kernel(x_ref, o_ref, wxo_ref, g_ref, wup_ref, cw_ref, cb_ref, wdn_ref, hist_ref,
                y_ref, fh_ref, gbuf, cs, *, seq_out):
    tm = x_ref.shape[0]
    nh = (F_CONV - 1) * SUBLANES

    @pl.when(pl.program_id(0) == 0)
    def _():
        gbuf[0:nh, :] = hist_ref[...]

    x2 = x_ref[...] + _dot(_seq_to_tm(o_ref, cs).astype(BF16), wxo_ref[...])
    xn = _rms(x2, g_ref[...]).astype(BF16)
    val = _dot(xn, wup_ref[:, :D_FF])
    gate = _dot(xn, wup_ref[:, D_FF:])
    gbuf[nh:nh + tm, :] = gate
    conv = _conv_taps(gbuf, gate, cw_ref, F_CONV, tm) + cb_ref[...]
    act = (jax.nn.gelu(conv) * val).astype(BF16)
    y = x2 + _dot(act, wdn_ref[...])
    tail = gbuf[tm:tm + nh, :]
    gbuf[0:nh, :] = tail
    fh_ref[...] = tail
    if seq_out:
        _tm_to_seq(y, cs, [(y_ref, D_MODEL)])
    else:
        y_ref[...] = y


def _ffn(x, o, w_xo, g, w_up, conv_w, conv_b, w_down, hist, tt, layer, seq_out):
    nb, t_len, d = o.shape
    tm = tt * nb
    rows = x.shape[0]
    nh = (F_CONV - 1) * SUBLANES

    def row(c):
        return pl.BlockSpec((tm, c), lambda i: (i, 0))

    def seq(c):
        return pl.BlockSpec((nb, tt, c), lambda i: (0, i, 0))

    y_shape = jax.ShapeDtypeStruct((nb, t_len, d) if seq_out else (rows, d), F32)
    return pl.pallas_call(
        functools.partial(_ffn_kernel, seq_out=seq_out),
        grid=(rows // tm,),
        in_specs=[row(d), seq(d), _const_spec((d, d), layer), _const_spec((1, d)),
                  _const_spec((d, 2 * D_FF), layer), _const_spec((F_CONV, D_FF)), _const_spec((1, D_FF)),
                  _const_spec((D_FF, d), layer), _const_spec((nh, D_FF))],
        out_specs=[seq(d) if seq_out else row(d), pl.BlockSpec((nh, D_FF), lambda i: (0, 0))],
        out_shape=[y_shape, jax.ShapeDtypeStruct((nh, D_FF), F32)],
        scratch_shapes=[pltpu.VMEM((tm + nh, D_FF), F32), pltpu.VMEM((d // LANES, tm, LANES), F32)],
        compiler_params=_params(("arbitrary",)),
        name="ffn",
    )(x, o, w_xo, g, w_up, conv_w, conv_b, w_down, hist)


def _odd_kernel(x_ref, g_ref, win_ref, cw_ref, cb_ref, wai_ref, ba_ref, bi_ref, lam_ref,
                wo_ref, hist_ref, h0_ref, gx_ref, wxq_ref, qnx_ref,
                x1_ref, qx_ref, ch_ref, hl_ref, xbuf, a_s, b_s, h_s, cs, *, stream_start):
    tm = x_ref.shape[0]
    nh = (C_CONV - 1) * SUBLANES
    first = pl.program_id(0) == 0

    @pl.when(first)
    def _():
        xbuf[0:nh, :] = hist_ref[...]
        h_s[...] = h0_ref[...]

    x = x_ref[...]
    xn = _rms(x, g_ref[...]).astype(BF16)
    xr_in = _dot(xn, win_ref[:, RNN_WIDTH:])
    xbuf[nh:nh + tm, :] = xr_in
    xr = _conv_taps(xbuf, xr_in, cw_ref, C_CONV, tm) + cb_ref[...]
    tail = xbuf[tm:tm + nh, :]
    xbuf[0:nh, :] = tail
    ch_ref[...] = tail

    xrb = xr.astype(BF16)
    lam = -lam_ref[...]
    sp = jnp.maximum(lam, 0.0) + jnp.log1p(jnp.exp(-jnp.abs(lam)))
    gates = [_dot(xrb[:, n * RNN_BLOCK:(n + 1) * RNN_BLOCK], wai_ref[n]) for n in range(RNN_BLOCKS)]
    r = jax.nn.sigmoid(jnp.concatenate([g[:, :RNN_BLOCK] for g in gates], axis=-1) + ba_ref[...])
    ig = jax.nn.sigmoid(jnp.concatenate([g[:, RNN_BLOCK:] for g in gates], axis=-1) + bi_ref[...])
    log_a = -RG_C * r * sp
    a = jnp.exp(log_a)
    m2 = jnp.tanh(-log_a) * (1.0 + a * a)
    mult = jnp.where(m2 > 0.0, m2 * lax.rsqrt(m2), 0.0)
    if stream_start:
        rows = lax.broadcasted_iota(I32, (tm, RNN_WIDTH), 0)
        mult = jnp.where(first & (rows < SUBLANES), 1.0, mult)
    a_s[...] = a
    b_s[...] = mult * ig * xr

    def step(t, h):
        r0 = pl.multiple_of(t * SUBLANES, SUBLANES)
        h = a_s[pl.ds(r0, SUBLANES), :] * h + b_s[pl.ds(r0, SUBLANES), :]
        b_s[pl.ds(r0, SUBLANES), :] = h
        return h

    h = lax.fori_loop(0, tm // SUBLANES, step, h_s[...], unroll=True)
    h_s[...] = h
    hl_ref[...] = h

    gate = _dot(xn, win_ref[:, :RNN_WIDTH])
    act = (jax.nn.gelu(gate) * b_s[...]).astype(BF16)
    x1 = x + _dot(act, wo_ref[...])
    x1_ref[...] = x1
    _xq_tail(x1, gx_ref, wxq_ref, qnx_ref, cs, qx_ref)


def _odd(x, g, w_in, conv_w, conv_b, w_ai, b_a, b_i, lam, w_out, hist, h0, gx, w_xq, qnx, tt, layer,
         stream_start):
    nb = SUBLANES
    tm = tt * nb
    rows = x.shape[0]
    t_len = rows // nb
    d = D_MODEL
    r = RNN_WIDTH
    nh = (C_CONV - 1) * SUBLANES

    def row(c):
        return pl.BlockSpec((tm, c), lambda i: (i, 0))

    blk = (RNN_BLOCKS, RNN_BLOCK, 2 * RNN_BLOCK)
    return pl.pallas_call(
        functools.partial(_odd_kernel, stream_start=stream_start),
        grid=(rows // tm,),
        in_specs=[row(d), _const_spec((1, d)), _const_spec((d, 2 * r)), _const_spec((C_CONV, r)),
                  _const_spec((1, r)), _const_spec(blk), _const_spec((1, r)),
                  _const_spec((1, r)), _const_spec((1, r)), _const_spec((r, d)), _const_spec((nh, r)),
                  _const_spec((SUBLANES, r)), _const_spec((1, d)), _const_spec((d, d), layer), _const_spec((1, d))],
        out_specs=[row(d), pl.BlockSpec((nb, tt, d), lambda i: (0, i, 0)), pl.BlockSpec((nh, r), lambda i: (0, 0)),
                   pl.BlockSpec((SUBLANES, r), lambda i: (0, 0))],
        out_shape=[jax.ShapeDtypeStruct((rows, d), F32), jax.ShapeDtypeStruct((nb, t_len, d), BF16),
                   jax.ShapeDtypeStruct((nh, r), F32), jax.ShapeDtypeStruct((SUBLANES, r), F32)],
        scratch_shapes=[pltpu.VMEM((tm + nh, r), F32), pltpu.VMEM((tm, r), F32), pltpu.VMEM((tm, r), F32),
                        pltpu.VMEM((SUBLANES, r), F32), pltpu.VMEM((d // LANES, tm, LANES), F32)],
        compiler_params=_params(("arbitrary",)),
        name="odd_mixer",
    )(x, g, w_in, conv_w, conv_b, w_ai, b_a, b_i, lam, w_out, hist, h0, gx, w_xq, qnx)


def _to_tm(a):
    return jnp.transpose(a, (1, 0, 2)).reshape(a.shape[1] * a.shape[0], a.shape[2])


def _from_tm(a, w):
    return jnp.transpose(a.reshape(w, SUBLANES, a.shape[1]), (1, 0, 2))


def _tile_plan(t_len):
    return {"proj": min(128, t_len),
            "ffn": min(64, t_len),
            "dsa": min(KEY_BLOCK, t_len),
            "xattn": t_len}


def _trunk(x, st, mem_k, mem_v, p, bias):
    nb, t_len, d = x.shape
    assert nb == SUBLANES
    tiles = _tile_plan(t_len)
    past = 0 if st is None else st["b_k"].shape[2]
    length = past + t_len
    ntop = min(TOPK_MAX, length // 4)
    near, far = bias
    xt = None
    out = {}

    def hist(name, l, width, c):
        if st is None:
            return jnp.zeros(((width - 1) * nb, c), F32)
        return _to_tm(st[name][l])

    for l in range(DEPTH):
        if l % 2 == 0:
            e = l // 2
            assert l == 0, "the per-sequence input is converted by the first layer's kernel"
            xt, ya, q, iq, ikw, k, v, uh = _even_in(
                x, p["g_mix"][l], p["w_in_even"][e], p["a_conv_w"][e], p["b_q_norm"][e], p["b_k_norm"][e],
                hist("a_conv", e, A_CONV, A_WIDTH), tiles["proj"])
            k_new = k.reshape(nb, t_len, B_KV_HEADS, B_HEAD_DIM)
            v_new = v.reshape(nb, t_len, B_KV_HEADS, B_HEAD_DIM)
            ik_new = ikw[:, :, :IDX_DIM]
            k_all, v_all, ik_all = k_new, v_new, ik_new
            if st is not None:
                k_all = jnp.concatenate([st["b_k"][e], k_new], axis=1)
                v_all = jnp.concatenate([st["b_v"][e], v_new], axis=1)
                ik_all = jnp.concatenate([st["b_kidx"][e], ik_new], axis=1)
            lp = -(-length // KEY_BLOCK) * KEY_BLOCK
            padl = lp - length
            k_att = jnp.pad(jnp.transpose(k_all, (0, 2, 1, 3)).astype(BF16), ((0, 0), (0, 0), (0, padl), (0, 0)))
            v_att = jnp.pad(jnp.transpose(v_all, (0, 2, 1, 3)).astype(BF16), ((0, 0), (0, 0), (0, padl), (0, 0)))
            vt_att = jnp.transpose(v_att.reshape(nb, B_KV_HEADS, lp // KEY_BLOCK, KEY_BLOCK, B_HEAD_DIM),
                                   (0, 1, 2, 4, 3))
            ik_att = jnp.pad(ik_all.astype(BF16), ((0, 0), (0, padl), (0, 0)))
            yb = _dsa(q, iq, ikw, k_att, vt_att, ik_att, near, far, tiles["dsa"], past, length, ntop)
            x1, qx = _even_out(xt, ya, yb, p["w_out_even"][e], p["g_x"][l], p["w_xq"], p["x_q_norm"][l],
                               tiles["proj"], l)
            out.setdefault("a_conv", []).append(_from_tm(uh, A_CONV - 1))
            out.setdefault("b_k", []).append(k_new)
            out.setdefault("b_v", []).append(v_new)
            out.setdefault("b_kidx", []).append(ik_new)
        else:
            o = l // 2
            h0 = jnp.zeros((nb, RNN_WIDTH), F32) if st is None else st["c_h"][o]
            x1, qx, ch, hl = _odd(
                xt, p["g_mix"][l], p["w_in_odd"][o], p["c_conv_w"][o], p["c_conv_b"][o], p["c_w_ai"][o],
                p["c_b_a"][o], p["c_b_i"][o], p["c_lambda"][o], p["w_out_odd"][o],
                hist("c_conv", o, C_CONV, RNN_WIDTH), h0, p["g_x"][l], p["w_xq"], p["x_q_norm"][l],
                tiles["proj"], l, stream_start=(past == 0))
            out.setdefault("c_conv", []).append(_from_tm(ch, C_CONV - 1))
            out.setdefault("c_h", []).append(hl)
        xo = _xattn(qx, mem_k[l], mem_v[l], tiles["xattn"])
        xt, fh = _ffn(x1, xo, p["w_xo"], p["g_ffn"][l], p["w_up"], p["f_conv_w"][l], p["f_conv_b"][l],
                      p["w_down"], hist("f_conv", l, F_CONV, D_FF), tiles["ffn"], l, seq_out=(l == DEPTH - 1))
        out.setdefault("f_conv", []).append(_from_tm(fh, F_CONV - 1))
    return xt, {name: jnp.stack(v) for name, v in out.items()}


def kernel(x_prompt, x_sample, cache_b_k, cache_b_v, cache_b_kidx, state_a_conv, state_c_conv, state_c_h, state_ffn_conv, cache_mem_k, cache_mem_v, mem_prompt, rel_table, g_mix, w_in_even, a_conv_w, b_q_norm, b_k_norm, w_out_even, w_in_odd, c_conv_w, c_conv_b, c_w_a, c_b_a, c_w_i, c_b_i, c_lambda, w_out_odd, g_mem, g_x, w_xq, w_xk, w_xv, x_q_norm, x_k_norm, w_xo, g_ffn, w_up, f_conv_w, f_conv_b, w_down):
    d = D_MODEL
    bp, t_p, _ = x_prompt.shape
    m = mem_prompt.shape[1]

    def rowvec(a):
        return a.reshape(a.shape[0], 1, a.shape[-1])

    def mxu(a):
        return [a[l].astype(BF16) for l in range(a.shape[0])]

    p = {
        "g_mix": rowvec(g_mix), "g_x": rowvec(g_x), "g_ffn": rowvec(g_ffn),
        "w_in_even": [jnp.pad(w, ((0, 0), (0, EVEN_IN_PAD - EVEN_IN))) for w in mxu(w_in_even)],
        "a_conv_w": a_conv_w,
        "b_q_norm": rowvec(jnp.tile(b_q_norm, (1, B_HEADS))),
        "b_k_norm": rowvec(jnp.tile(b_k_norm, (1, B_KV_HEADS))),
        "w_out_even": mxu(w_out_even),
        "w_in_odd": mxu(w_in_odd), "c_conv_w": c_conv_w, "c_conv_b": rowvec(c_conv_b),
        "c_w_ai": mxu(jnp.concatenate([c_w_a, c_w_i], axis=-1)), "c_b_a": rowvec(c_b_a), "c_b_i": rowvec(c_b_i),
        "c_lambda": rowvec(c_lambda), "w_out_odd": mxu(w_out_odd),
        "w_xq": w_xq.astype(BF16), "x_q_norm": rowvec(jnp.tile(x_q_norm, (1, X_HEADS))),
        "w_xo": w_xo.astype(BF16), "w_up": w_up.astype(BF16), "f_conv_w": f_conv_w,
        "f_conv_b": rowvec(f_conv_b), "w_down": w_down.astype(BF16),
    }
    bias = _bias_tiles(rel_table, KEY_BLOCK)

    mk, mv = _mem_kv(mem_prompt.reshape(bp * m, d), g_mem, w_xk, x_k_norm, w_xv)
    p_mem_k = mk.reshape(DEPTH, bp, m, X_HEADS, X_HEAD_DIM)
    p_mem_v = mv.reshape(DEPTH, bp, m, X_HEADS, X_HEAD_DIM)
    y_prompt, new_p = _trunk(x_prompt, None, mk.reshape(DEPTH, bp, m, d).astype(BF16),
                             mv.reshape(DEPTH, bp, m, d).astype(BF16), p, bias)

    bs, t_s, _ = x_sample.shape
    st_s = {"b_k": cache_b_k, "b_v": cache_b_v, "b_kidx": cache_b_kidx, "a_conv": state_a_conv,
            "c_conv": state_c_conv, "c_h": state_c_h, "f_conv": state_ffn_conv}
    ms = cache_mem_k.shape[2]
    y_sample, new_s = _trunk(x_sample, st_s, cache_mem_k.reshape(DEPTH, bs, ms, d).astype(BF16),
                             cache_mem_v.reshape(DEPTH, bs, ms, d).astype(BF16), p, bias)
    return (y_prompt, y_sample,
            new_p["b_k"], new_p["b_v"], new_p["b_kidx"], new_p["a_conv"], new_p["c_conv"],
            new_p["c_h"], new_p["f_conv"], p_mem_k, p_mem_v,
            new_s["b_k"], new_s["b_v"], new_s["b_kidx"], new_s["a_conv"], new_s["c_conv"],
            new_s["c_h"], new_s["f_conv"])
```

```python
import functools
import math

import jax
import jax.numpy as jnp
from jax import lax
from jax.experimental import pallas as pl
from jax.experimental.pallas import tpu as pltpu

F32 = jnp.float32
BF16 = jnp.bfloat16
I32 = jnp.int32
I16 = jnp.int16

D_MODEL = 1024
DEPTH = 2
CHUNK = 64
CHUNK_SHIFT = CHUNK.bit_length() - 1
EPS = 1e-6
NEG_INF = -1e30
LOG2E = math.log2(math.e)
A_WIDTH = 512
A_CONV = 3
B_HEADS = 8
B_KV_HEADS = 2
B_HEAD_DIM = 64
IDX_HEADS = 8
IDX_DIM = 32
TOPK_MAX = 256
REL_BUCKETS = 32
REL_MAX_DIST = 128
RNN_WIDTH = 1024
RNN_BLOCKS = 8
RNN_BLOCK = 128
C_CONV = 4
RG_C = 8.0
X_HEADS = 4
X_HEAD_DIM = 256
D_FF = 2816
F_CONV = 3
EVEN_IN = 2600

SUBLANES = 8
LANES = 128
PACKED_ROWS = 16
VMEM_LIMIT = 56 * 1024 * 1024

EVEN_IN_PAD = 2688
KEY_BLOCK = 256
INT_MIN = -2147483648


def _params(sem, vmem=VMEM_LIMIT):
    return pltpu.CompilerParams(dimension_semantics=sem, vmem_limit_bytes=vmem)


def _const_spec(shape, layer=None):
    nd = len(shape)
    if layer is None:
        return pl.BlockSpec(shape, lambda *_: (0,) * nd, pipeline_mode=pl.Buffered(1))
    return pl.BlockSpec((None,) + tuple(shape), lambda *_: (layer,) + (0,) * nd, pipeline_mode=pl.Buffered(1))


def _rms(x, g):
    ms = jnp.mean(x * x, axis=-1, keepdims=True)
    return x * lax.rsqrt(ms + EPS) * g


def _head_rms(x, hd, gain):
    m, c = x.shape
    s = x * x
    parts = []
    if hd >= LANES:
        for h in range(c // hd):
            ms = jnp.mean(s[:, h * hd:(h + 1) * hd], axis=-1, keepdims=True)
            parts.append(x[:, h * hd:(h + 1) * hd] * lax.rsqrt(ms + EPS))
    else:
        lane = lax.broadcasted_iota(I32, (m, LANES), 1)
        for j in range(c // LANES):
            sj = s[:, j * LANES:(j + 1) * LANES]
            inv = jnp.zeros((m, LANES), F32)
            for k in range(LANES // hd):
                msk = (lane >= k * hd) & (lane < (k + 1) * hd)
                ms = jnp.sum(jnp.where(msk, sj, 0.0), axis=-1, keepdims=True) * (1.0 / hd)
                inv = jnp.where(msk, lax.rsqrt(ms + EPS), inv)
            parts.append(x[:, j * LANES:(j + 1) * LANES] * inv)
    y = parts[0] if len(parts) == 1 else jnp.concatenate(parts, axis=-1)
    return y * gain


def _dot(a, b):
    return jnp.dot(a, b, preferred_element_type=F32)


def _dot_t(a, b):
    return lax.dot_general(a, b, (((1,), (1,)), ((), ())), preferred_element_type=F32)


def _conv_taps(buf, cur, w_ref, width, tm):
    y = cur * w_ref[width - 1:width, :]
    for i in range(width - 1):
        y = y + buf[i * SUBLANES:i * SUBLANES + tm, :] * w_ref[i:i + 1, :]
    return y


def _seq_to_tm(src_ref, scr):
    nb, tt, c = src_ref.shape
    for b in range(nb):
        for j in range(c // LANES):
            scr[j, pl.ds(b, tt, stride=nb), :] = src_ref[b, :, j * LANES:(j + 1) * LANES].astype(F32)
    return jnp.concatenate([scr[j] for j in range(c // LANES)], axis=-1)


def _tm_to_seq(val, scr, dst_refs):
    tm, c = val.shape
    tt = tm // SUBLANES
    for j in range(c // LANES):
        scr[j] = val[:, j * LANES:(j + 1) * LANES]
    for b in range(SUBLANES):
        j0 = 0
        for ref, ci in dst_refs:
            nj = ci // LANES
            parts = [scr[j0 + j, pl.ds(b, tt, stride=SUBLANES), :] for j in range(nj)]
            ref[b] = (parts[0] if nj == 1 else jnp.concatenate(parts, axis=-1)).astype(ref.dtype)
            j0 += nj


def _memkv_kernel(mem_ref, g_ref, wk_ref, kn_ref, wv_ref, k_ref, v_ref):
    hm = _rms(mem_ref[...], g_ref[0]).astype(BF16)
    k_ref[0] = _head_rms(_dot(hm, wk_ref[0]), X_HEAD_DIM, kn_ref[0])
    v_ref[0] = _dot(hm, wv_ref[0])


def _mem_kv(mem, g_mem, w_xk, x_k_norm, w_xv):
    rows = mem.shape[0]
    tm = min(512, rows)
    d = D_MODEL
    kn = jnp.tile(x_k_norm, (1, X_HEADS)).reshape(DEPTH, 1, d)
    out = jax.ShapeDtypeStruct((DEPTH, rows, d), F32)
    return pl.pallas_call(
        _memkv_kernel,
        grid=(DEPTH, rows // tm),
        in_specs=[
            pl.BlockSpec((tm, d), lambda l, i: (i, 0)),
            pl.BlockSpec((1, 1, d), lambda l, i: (l, 0, 0)),
            pl.BlockSpec((1, d, d), lambda l, i: (l, 0, 0)),
            pl.BlockSpec((1, 1, d), lambda l, i: (l, 0, 0)),
            pl.BlockSpec((1, d, d), lambda l, i: (l, 0, 0)),
        ],
        out_specs=[pl.BlockSpec((1, tm, d), lambda l, i: (l, i, 0))] * 2,
        out_shape=[out, out],
        compiler_params=_params(("arbitrary", "arbitrary")),
        name="mem_kv",
    )(mem, g_mem.reshape(DEPTH, 1, d), w_xk.astype(BF16), kn, w_xv.astype(BF16))


def _even_in_kernel(x_ref, g_ref, w_ref, cw_ref, qn_ref, kn_ref, hist_ref,
                    xt_ref, ya_ref, q_ref, iq_ref, ikw_ref, k_ref, v_ref, uh_ref, ubuf, xs, cs):
    tm = xt_ref.shape[0]
    nh = (A_CONV - 1) * SUBLANES

    @pl.when(pl.program_id(0) == 0)
    def _():
        ubuf[0:nh, :] = hist_ref[...]

    x = _seq_to_tm(x_ref, xs)
    xt_ref[...] = x
    xn = _rms(x, g_ref[...]).astype(BF16)

    def proj(a, b):
        return _dot(xn, w_ref[:, a:b])

    zc = proj(A_WIDTH, 3 * A_WIDTH)
    u = zc[:, :A_WIDTH] * zc[:, A_WIDTH:]
    ubuf[nh:nh + tm, :] = u
    conv = _conv_taps(ubuf, u, cw_ref, A_CONV, tm)
    ya_ref[...] = (proj(0, A_WIDTH) * conv).astype(BF16)
    tail = ubuf[tm:tm + nh, :]
    ubuf[0:nh, :] = tail
    uh_ref[...] = tail

    o = 3 * A_WIDTH
    nq = B_HEADS * B_HEAD_DIM
    nkv = B_KV_HEADS * B_HEAD_DIM
    ni = IDX_HEADS * IDX_DIM
    q = _head_rms(proj(o, o + nq), B_HEAD_DIM, qn_ref[...]) * (B_HEAD_DIM ** -0.5 * LOG2E)
    zkv = proj(o + nq, o + nq + 2 * nkv)
    k = _head_rms(zkv[:, :nkv], B_HEAD_DIM, kn_ref[...])
    zi = proj(o + nq + 2 * nkv, EVEN_IN_PAD)
    seq = jnp.concatenate([q, zi, k, zkv[:, nkv:]], axis=-1)
    _tm_to_seq(seq, cs, [(q_ref, nq), (iq_ref, ni), (ikw_ref, LANES), (k_ref, nkv), (v_ref, nkv)])


def _even_in(x, g, w_pad, conv_w, qn, kn, hist, tt):
    nb, t_len, d = x.shape
    tm = tt * nb
    rows = t_len * nb
    nh = (A_CONV - 1) * SUBLANES
    nq = B_HEADS * B_HEAD_DIM
    nkv = B_KV_HEADS * B_HEAD_DIM
    ni = IDX_HEADS * IDX_DIM
    nseq = nq + ni + LANES + 2 * nkv

    def row(c):
        return pl.BlockSpec((tm, c), lambda i: (i, 0))

    def seq(c):
        return pl.BlockSpec((nb, tt, c), lambda i: (0, i, 0))

    def seq_shape(c, dt):
        return jax.ShapeDtypeStruct((nb, t_len, c), dt)

    return pl.pallas_call(
        _even_in_kernel,
        grid=(t_len // tt,),
        in_specs=[seq(d), _const_spec((1, d)), _const_spec((d, EVEN_IN_PAD)), _const_spec((A_CONV, A_WIDTH)),
                  _const_spec((1, nq)), _const_spec((1, nkv)), _const_spec((nh, A_WIDTH))],
        out_specs=[row(d), row(A_WIDTH), seq(nq), seq(ni), seq(LANES), seq(nkv), seq(nkv),
                   pl.BlockSpec((nh, A_WIDTH), lambda i: (0, 0))],
        out_shape=[jax.ShapeDtypeStruct((rows, d), F32), jax.ShapeDtypeStruct((rows, A_WIDTH), BF16),
                   seq_shape(nq, BF16), seq_shape(ni, BF16), seq_shape(LANES, F32), seq_shape(nkv, F32),
                   seq_shape(nkv, F32), jax.ShapeDtypeStruct((nh, A_WIDTH), F32)],
        scratch_shapes=[pltpu.VMEM((tm + nh, A_WIDTH), F32), pltpu.VMEM((d // LANES, tm, LANES), F32),
                        pltpu.VMEM((nseq // LANES, tm, LANES), F32)],
        compiler_params=_params(("arbitrary",)),
        name="even_in",
    )(x, g, w_pad, conv_w, qn, kn, hist)


def _rel_bucket(rel):
    half = REL_BUCKETS // 2
    max_exact = half // 2
    n = -rel
    ret = jnp.where(n < 0, half, 0)
    n = jnp.abs(n)
    nf = jnp.maximum(n, 1).astype(F32)
    large = max_exact + (jnp.log(nf / max_exact) / math.log(REL_MAX_DIST / max_exact)
                         * (half - max_exact)).astype(I32)
    large = jnp.minimum(large, half - 1)
    return ret + jnp.where(n < max_exact, n, large)


def _bias_kernel(tab_ref, near_ref, far_ref):
    tk = near_ref.shape[-1]
    r = lax.broadcasted_iota(I32, (tk, tk), 0)
    c = lax.broadcasted_iota(I32, (tk, tk), 1)

    def lookup(bucket, h):
        def body(j, acc):
            return jnp.where(bucket == j, tab_ref[j, h], acc)
        return lax.fori_loop(0, REL_BUCKETS, body, jnp.zeros(bucket.shape, F32))

    for blk in range(2):
        bucket = _rel_bucket(r - c - blk * tk)
        for h in range(B_HEADS):
            near_ref[blk, h] = lookup(bucket, h) * LOG2E
    bucket = _rel_bucket(-REL_MAX_DIST - c[0:SUBLANES, :])
    for h in range(B_HEADS):
        far_ref[h] = lookup(bucket, h) * LOG2E


def _bias_tiles(rel_table, tk):
    return pl.pallas_call(
        _bias_kernel,
        in_specs=[pl.BlockSpec(memory_space=pltpu.SMEM)],
        out_shape=[jax.ShapeDtypeStruct((2, B_HEADS, tk, tk), F32),
                   jax.ShapeDtypeStruct((B_HEADS, SUBLANES, tk), F32)],
        name="rel_bias",
    )(rel_table)


def _dsa_kernel(q_ref, iq_ref, ikw_ref, k_ref, vt_ref, ik_ref, near_ref, far_ref, o_ref,
                skey, skh, skl, madd_s, s_s, p_s, qh_s, iqh_s, pad_s, w_s, m_s, acc_s,
                *, tq, qw, tk, past, length, ntop, idx_bits):
    i = pl.program_id(1)
    q0 = past + i * tq
    nkb = (q0 + tq + tk - 1) // tk
    last = nkb - 1
    ni = IDX_HEADS * IDX_DIM
    groups = B_HEADS // B_KV_HEADS
    lane = lax.broadcasted_iota(I32, (1, qw), 1)
    sub8 = lax.broadcasted_iota(I32, (SUBLANES, qw), 0)
    qchunk = jnp.right_shift(q0 + lane, CHUNK_SHIFT)

    if tq != qw:
        qh_s[...] = jnp.zeros(qh_s.shape, BF16)
        iqh_s[...] = jnp.zeros(iqh_s.shape, BF16)
        pad_s[...] = jnp.zeros(pad_s.shape, F32)
    for h in range(B_HEADS):
        qh_s[h, 0:tq, :] = q_ref[:, B_HEAD_DIM * h:B_HEAD_DIM * (h + 1)]
    for h in range(IDX_HEADS):
        iqh_s[h, 0:tq, :] = iq_ref[:, IDX_DIM * h:IDX_DIM * (h + 1)]
    pad_s[0:tq, :] = ikw_ref[...]
    w_s[...] = pad_s[...].T[IDX_DIM:IDX_DIM + IDX_HEADS, :] * (ni ** -0.5)

    slab = tk

    def scores(kb, masked):
        for sl in range(tk // slab):
            ks = pl.multiple_of(kb * tk + sl * slab, slab)
            ikb = ik_ref[0, pl.ds(ks, slab), :]
            acc = jnp.zeros((slab, qw), F32)
            for h in range(IDX_HEADS):
                acc = acc + w_s[h:h + 1, :] * jnp.maximum(_dot_t(ikb, iqh_s[h]), 0.0)
            if masked:
                kpos = kb * tk + sl * slab + lax.broadcasted_iota(I32, (slab, qw), 0)
                vis = (jnp.right_shift(kpos, CHUNK_SHIFT) <= qchunk) & (kpos < length)
                acc = jnp.where(vis, acc, -jnp.inf)
            bits = lax.bitcast_convert_type(acc, I32)
            key = jnp.where(bits < 0, bits ^ jnp.int32(0x7FFFFFFF), bits)
            rows = slice(sl * slab, (sl + 1) * slab)
            skey[kb, rows, :] = key
            skh[kb, rows, :] = jnp.right_shift(key, 16).astype(I16)
            skl[kb, rows, :] = ((key & 0xFFFF) - 32768).astype(I16)

    def score_quad(j, carry):
        for u in range(4):
            scores(4 * j + u, False)
        return carry

    lax.fori_loop(0, last // 4, score_quad, 0)
    rem = last % 4

    @pl.when(rem >= 2)
    def _():
        scores(last - rem, False)
        scores(last - rem + 1, False)

    @pl.when(rem % 2 == 1)
    def _():
        scores(last - 1, False)

    scores(last, True)

    n_acc = 4
    prow = PACKED_ROWS

    @pl.when(nkb % 2 == 1)
    def _():
        skh[nkb] = jnp.full((tk, qw), -32768, I16)
        skl[nkb] = jnp.full((tk, qw), -32768, I16)

    def count16(ref, pred):
        def body(j, accs):
            accs = list(accs)
            for kb in (2 * j, 2 * j + 1):
                for g in range(tk // prow):
                    blk = ref[kb, g * prow:(g + 1) * prow, :]
                    accs[g % n_acc] = accs[g % n_acc] + jnp.where(pred(blk), jnp.int16(1), jnp.int16(0))
            return tuple(accs)
        accs = lax.fori_loop(0, (nkb + 1) // 2, body, tuple(jnp.zeros((prow, qw), I16) for _ in range(n_acc)))
        tot = (accs[0].astype(I32) + accs[1].astype(I32)) + (accs[2].astype(I32) + accs[3].astype(I32))
        return jnp.sum(tot, axis=0, keepdims=True)

    def rep16(v):
        return jnp.broadcast_to(v, (prow, qw)).astype(I16)

    def kth16(ref, kth):
        def bit_body(it, prefix):
            cand_u = prefix | jnp.left_shift(jnp.int32(1), 15 - it)
            cand = rep16(cand_u - 32768)
            cnt = count16(ref, lambda blk: blk >= cand)
            return jnp.where(cnt >= kth, cand_u, prefix)
        return lax.fori_loop(0, 16, bit_body, jnp.zeros((1, qw), I32)) - 32768

    def count(pred):
        def body(kb, accs):
            accs = list(accs)
            for g in range(tk // SUBLANES):
                blk = skey[kb, g * SUBLANES:(g + 1) * SUBLANES, :]
                accs[g % n_acc] = accs[g % n_acc] + jnp.where(pred(kb, g, blk), 1, 0)
            return tuple(accs)
        accs = lax.fori_loop(0, nkb, body, tuple(jnp.zeros((SUBLANES, qw), I32) for _ in range(n_acc)))
        tot = (accs[0] + accs[1]) + (accs[2] + accs[3])
        return jnp.sum(tot, axis=0, keepdims=True)

    def rep8(v):
        return jnp.broadcast_to(v, (SUBLANES, qw))

    p_hi = kth16(skh, ntop)
    hi16 = rep16(p_hi)
    above = count16(skh, lambda blk: blk > hi16)

    def low_body(kb, carry):
        for g in range(tk // prow):
            rows = slice(g * prow, (g + 1) * prow)
            skl[kb, rows, :] = jnp.where(skh[kb, rows, :] == hi16, skl[kb, rows, :], jnp.int16(-32768))
        return carry

    lax.fori_loop(0, nkb, low_body, 0)
    p_lo = kth16(skl, ntop - above)
    tau = p_hi * 65536 + (p_lo + 32768)
    tau8 = rep8(tau)

    cnt_ge = count(lambda kb, g, blk: blk >= tau8)
    key_ninf = jnp.int32(0x7FFFFFFF) ^ jnp.int32(-8388608)
    finite = tau != key_ninf
    tie_rows = (cnt_ge > ntop) & finite & (lane < tq)

    @pl.when(jnp.max(jnp.where(tie_rows, 1, 0)) > 0)
    def _():
        need8 = rep8(ntop - count(lambda kb, g, blk: blk > tau8))

        def xbit(it, xlim):
            cand8 = xlim | jnp.left_shift(jnp.int32(1), idx_bits - 1 - it)
            cnt = count(lambda kb, g, blk: (blk == tau8) & ((kb * tk + g * SUBLANES + sub8) < cand8))
            return jnp.where(rep8(cnt) <= need8, cand8, xlim)
        xlim8 = lax.fori_loop(0, idx_bits, xbit, jnp.zeros((SUBLANES, qw), I32))

        def demote(kb, carry):
            for g in range(tk // SUBLANES):
                rows = slice(g * SUBLANES, (g + 1) * SUBLANES)
                key = skey[kb, rows, :]
                late = (key == tau8) & ((kb * tk + g * SUBLANES + sub8) >= xlim8)
                skey[kb, rows, :] = jnp.where(late, key - 1, key)
            return carry

        lax.fori_loop(0, nkb, demote, 0)

    m_s[...] = jnp.full(m_s.shape, NEG_INF, F32)
    acc_s[...] = jnp.zeros(acc_s.shape, F32)
    vrows = B_HEAD_DIM + PACKED_ROWS
    ones_rows = jnp.ones((PACKED_ROWS, tk), BF16)

    tau_ge8 = rep8(jnp.where(finite, tau, tau + 1))

    def attend(kbs, near):
        nblk = len(kbs)
        tiles = [nblk - 1 - i for i in range(nblk)] if near else None
        alphas = {}
        for i, kb in enumerate(kbs):
            for g in range(tk // SUBLANES):
                rows = slice(g * SUBLANES, (g + 1) * SUBLANES)
                madd_s[i, rows, :] = jnp.where(skey[kb, rows, :] >= tau_ge8, 0.0, NEG_INF)

        def logits(h):
            for i, kb in enumerate(kbs):
                kn = k_ref[0, h // groups, pl.ds(pl.multiple_of(kb * tk, tk), tk), :]
                bias = madd_s[i] if tiles is None else madd_s[i] + near_ref[tiles[i], h]
                s_s[i, h] = (_dot_t(kn, qh_s[h]) + bias).astype(BF16)

        def softmax(h):
            tiles16 = [s_s[i, h, r * prow:(r + 1) * prow, :] for i in range(nblk) for r in range(tk // prow)]
            mx = tiles16[:n_acc]
            for j, t in enumerate(tiles16[n_acc:]):
                mx[j % n_acc] = jnp.maximum(mx[j % n_acc], t)
            mx = jnp.maximum(jnp.maximum(mx[0], mx[1]), jnp.maximum(mx[2], mx[3]))
            m_cur = jnp.max(mx.astype(F32), axis=0, keepdims=True)
            c_h = jnp.zeros((1, qw), F32) if near else far_ref[h, 0:1, 0:qw]
            m_prev = m_s[h:h + 1, :]
            shift = (jnp.maximum(m_prev, m_cur + c_h) - c_h).astype(BF16)
            m_new = shift.astype(F32) + c_h
            m_s[h:h + 1, :] = m_new
            alphas[h] = jnp.exp2(m_prev - m_new)
            shift = jnp.broadcast_to(shift, (prow, qw))
            for i in range(nblk):
                for r in range(tk // prow):
                    rows = slice(r * prow, (r + 1) * prow)
                    p_s[i, h, rows, :] = jnp.exp2(s_s[i, h, rows, :] - shift)

        def values(h):
            hs = slice(h * vrows, (h + 1) * vrows)
            pv = None
            for i in range(nblk):
                vt1 = jnp.concatenate([vt_ref[0, h // groups, kbs[i]], ones_rows], axis=0)
                pv = _dot(vt1, p_s[i, h]) if pv is None else pv + _dot(vt1, p_s[i, h])
            acc_s[hs, :] = acc_s[hs, :] * alphas[h] + pv

        for phase in (logits, softmax, values):
            for h in range(B_HEADS):
                phase(h)

    nfar = nkb - 2

    def far_pair(j, carry):
        attend([2 * j, 2 * j + 1], False)
        return carry

    lax.fori_loop(0, nfar // 2, far_pair, 0)

    @pl.when((nfar > 0) & (nfar % 2 == 1))
    def _():
        attend([nfar - 1], False)

    @pl.when(nkb >= 2)
    def _():
        attend([nkb - 2, last], True)

    @pl.when(nkb < 2)
    def _():
        attend([last], True)

    outs = []
    for h in range(B_HEADS):
        r0 = h * vrows
        outs.append(acc_s[r0:r0 + B_HEAD_DIM, :] / acc_s[r0 + B_HEAD_DIM:r0 + B_HEAD_DIM + 1, :])
    o = jnp.concatenate(outs, axis=0).T
    o_ref[...] = o[0:tq, :].astype(BF16)


def _dsa(q, iq, ikw, k_att, vt_att, ik_att, near, far, tq, past, length, ntop):
    tk = KEY_BLOCK
    nb, t_len, _ = q.shape
    lp = k_att.shape[2]
    vrows = B_HEAD_DIM + PACKED_ROWS
    nq = B_HEADS * B_HEAD_DIM
    ni = IDX_HEADS * IDX_DIM
    qw = max(tq, LANES)
    assert past % tk == 0 and (tq == tk or t_len == tq) and tq <= tk and lp % tk == 0
    idx_bits = lp.bit_length()

    def qblk(c):
        return pl.BlockSpec((None, tq, c), lambda b, i: (b, i, 0))

    kern = functools.partial(_dsa_kernel, tq=tq, qw=qw, tk=tk, past=past, length=length, ntop=ntop,
                             idx_bits=idx_bits)
    return pl.pallas_call(
        kern,
        grid=(nb, t_len // tq),
        in_specs=[qblk(nq), qblk(ni), qblk(LANES),
                  pl.BlockSpec((1, B_KV_HEADS, lp, B_HEAD_DIM), lambda b, i: (b, 0, 0, 0)),
                  pl.BlockSpec((1, B_KV_HEADS, lp // tk, B_HEAD_DIM, tk), lambda b, i: (b, 0, 0, 0, 0)),
                  pl.BlockSpec((1, lp, IDX_DIM), lambda b, i: (b, 0, 0)),
                  pl.BlockSpec((2, B_HEADS, tk, qw), lambda b, i: (0, 0, 0, 0)),
                  pl.BlockSpec((B_HEADS, SUBLANES, tk), lambda b, i: (0, 0, 0))],
        out_specs=qblk(nq),
        out_shape=jax.ShapeDtypeStruct((nb, t_len, nq), BF16),
        scratch_shapes=[pltpu.VMEM((lp // tk, tk, qw), I32),
                        pltpu.VMEM((lp // tk + 1, tk, qw), I16),
                        pltpu.VMEM((lp // tk + 1, tk, qw), I16),
                        pltpu.VMEM((2, tk, qw), F32),
                        pltpu.VMEM((2, B_HEADS, tk, qw), BF16),
                        pltpu.VMEM((2, B_HEADS, tk, qw), BF16),
                        pltpu.VMEM((B_HEADS, qw, B_HEAD_DIM), BF16),
                        pltpu.VMEM((IDX_HEADS, qw, IDX_DIM), BF16),
                        pltpu.VMEM((qw, LANES), F32),
                        pltpu.VMEM((IDX_HEADS, qw), F32),
                        pltpu.VMEM((B_HEADS, qw), F32),
                        pltpu.VMEM((B_HEADS * vrows, qw), F32)],
        compiler_params=_params(("arbitrary", "arbitrary")),
        name="dsa",
    )(q, iq, ikw, k_att, vt_att, ik_att, near, far)


def _xq_tail(x1, gx_ref, wxq_ref, qnx_ref, cs, qx_ref):
    xn = _rms(x1, gx_ref[...]).astype(BF16)
    qx = _head_rms(_dot(xn, wxq_ref[...]), X_HEAD_DIM, qnx_ref[...])
    _tm_to_seq(qx * (X_HEAD_DIM ** -0.5), cs, [(qx_ref, D_MODEL)])


def _even_out_kernel(x_ref, ya_ref, yb_ref, wo_ref, gx_ref, wxq_ref, qnx_ref, x1_ref, qx_ref, ys, cs):
    yb = _seq_to_tm(yb_ref, ys).astype(BF16)
    x1 = x_ref[...] + _dot(ya_ref[...], wo_ref[0:A_WIDTH, :]) + _dot(yb, wo_ref[A_WIDTH:, :])
    x1_ref[...] = x1
    _xq_tail(x1, gx_ref, wxq_ref, qnx_ref, cs, qx_ref)


def _even_out(x, ya, yb, w_out, gx, w_xq, qnx, tt, layer):
    nb, t_len, nyb = yb.shape
    tm = tt * nb
    rows = x.shape[0]
    d = D_MODEL

    def row(c):
        return pl.BlockSpec((tm, c), lambda i: (i, 0))

    def seq(c):
        return pl.BlockSpec((nb, tt, c), lambda i: (0, i, 0))

    return pl.pallas_call(
        _even_out_kernel,
        grid=(rows // tm,),
        in_specs=[row(d), row(A_WIDTH), seq(nyb), _const_spec((d, d)),
                  _const_spec((1, d)), _const_spec((d, d), layer), _const_spec((1, d))],
        out_specs=[row(d), seq(d)],
        out_shape=[jax.ShapeDtypeStruct((rows, d), F32), jax.ShapeDtypeStruct((nb, t_len, d), BF16)],
        scratch_shapes=[pltpu.VMEM((nyb // LANES, tm, LANES), F32), pltpu.VMEM((d // LANES, tm, LANES), F32)],
        compiler_params=_params(("arbitrary",)),
        name="even_out",
    )(x, ya, yb, w_out, gx, w_xq, qnx)


def _xattn_kernel(q_ref, mk_ref, mv_ref, o_ref):
    for h in range(X_HEADS):
        sl = slice(h * X_HEAD_DIM, (h + 1) * X_HEAD_DIM)
        s = _dot_t(q_ref[:, sl], mk_ref[0, :, sl])
        p = jnp.exp(s - jnp.max(s, axis=1, keepdims=True))
        o = _dot(p.astype(BF16), mv_ref[0, :, sl]) / jnp.sum(p, axis=1, keepdims=True)
        o_ref[:, sl] = o.astype(BF16)


def _xattn(qx, mk, mv, tq):
    nb, t_len, d = qx.shape
    m = mk.shape[1]
    return pl.pallas_call(
        _xattn_kernel,
        grid=(nb, t_len // tq),
        in_specs=[pl.BlockSpec((None, tq, d), lambda b, i: (b, i, 0)),
                  pl.BlockSpec((1, m, d), lambda b, i: (b, 0, 0)),
                  pl.BlockSpec((1, m, d), lambda b, i: (b, 0, 0))],
        out_specs=pl.BlockSpec((None, tq, d), lambda b, i: (b, i, 0)),
        out_shape=jax.ShapeDtypeStruct((nb, t_len, d), BF16),
        compiler_params=_params(("arbitrary", "arbitrary")),
        name="mem_attn",
    )(qx, mk, mv)


def _ffn_kernel(x_ref, o_ref, wxo_ref, g_ref, wup_ref, cw_ref, cb_ref, wdn_ref, hist_ref,
                y_ref, fh_ref, gbuf, cs, *, seq_out):
    tm = x_ref.shape[0]
    nh = (F_CONV - 1) * SUBLANES

    @pl.when(pl.program_id(0) == 0)
    def _():
        gbuf[0:nh, :] = hist_ref[...]

    x2 = x_ref[...] + _dot(_seq_to_tm(o_ref, cs).astype(BF16), wxo_ref[...])
    xn = _rms(x2, g_ref[...]).astype(BF16)
    val = _dot(xn, wup_ref[:, :D_FF])
    gate = _dot(xn, wup_ref[:, D_FF:])
    gbuf[nh:nh + tm, :] = gate
    conv = _conv_taps(gbuf, gate, cw_ref, F_CONV, tm) + cb_ref[...]
    act = (jax.nn.gelu(conv) * val).astype(BF16)
    y = x2 + _dot(act, wdn_ref[...])
    tail = gbuf[tm:tm + nh, :]
    gbuf[0:nh, :] = tail
    fh_ref[...] = tail
    if seq_out:
        _tm_to_seq(y, cs, [(y_ref, D_MODEL)])
    else:
        y_ref[...] = y


def _ffn(x, o, w_xo, g, w_up, conv_w, conv_b, w_down, hist, tt, layer, seq_out):
    nb, t_len, d = o.shape
    tm = tt * nb
    rows = x.shape[0]
    nh = (F_CONV - 1) * SUBLANES

    def row(c):
        return pl.BlockSpec((tm, c), lambda i: (i, 0))

    def seq(c):
        return pl.BlockSpec((nb, tt, c), lambda i: (0, i, 0))

    y_shape = jax.ShapeDtypeStruct((nb, t_len, d) if seq_out else (rows, d), F32)
    return pl.pallas_call(
        functools.partial(_ffn_kernel, seq_out=seq_out),
        grid=(rows // tm,),
        in_specs=[row(d), seq(d), _const_spec((d, d), layer), _const_spec((1, d)),
                  _const_spec((d, 2 * D_FF), layer), _const_spec((F_CONV, D_FF)), _const_spec((1, D_FF)),
                  _const_spec((D_FF, d), layer), _const_spec((nh, D_FF))],
        out_specs=[seq(d) if seq_out else row(d), pl.BlockSpec((nh, D_FF), lambda i: (0, 0))],
        out_shape=[y_shape, jax.ShapeDtypeStruct((nh, D_FF), F32)],
        scratch_shapes=[pltpu.VMEM((tm + nh, D_FF), F32), pltpu.VMEM((d // LANES, tm, LANES), F32)],
        compiler_params=_params(("arbitrary",)),
        name="ffn",
    )(x, o, w_xo, g, w_up, conv_w, conv_b, w_down, hist)


def _odd_kernel(x_ref, g_ref, win_ref, cw_ref, cb_ref, wai_ref, ba_ref, bi_ref, lam_ref,
                wo_ref, hist_ref, h0_ref, gx_ref, wxq_ref, qnx_ref,
                x1_ref, qx_ref, ch_ref, hl_ref, xbuf, a_s, b_s, h_s, cs, *, stream_start):
    tm = x_ref.shape[0]
    nh = (C_CONV - 1) * SUBLANES
    first = pl.program_id(0) == 0

    @pl.when(first)
    def _():
        xbuf[0:nh, :] = hist_ref[...]
        h_s[...] = h0_ref[...]

    x = x_ref[...]
    xn = _rms(x, g_ref[...]).astype(BF16)
    xr_in = _dot(xn, win_ref[:, RNN_WIDTH:])
    xbuf[nh:nh + tm, :] = xr_in
    xr = _conv_taps(xbuf, xr_in, cw_ref, C_CONV, tm) + cb_ref[...]
    tail = xbuf[tm:tm + nh, :]
    xbuf[0:nh, :] = tail
    ch_ref[...] = tail

    xrb = xr.astype(BF16)
    lam = -lam_ref[...]
    sp = jnp.maximum(lam, 0.0) + jnp.log1p(jnp.exp(-jnp.abs(lam)))
    gates = [_dot(xrb[:, n * RNN_BLOCK:(n + 1) * RNN_BLOCK], wai_ref[n]) for n in range(RNN_BLOCKS)]
    r = jax.nn.sigmoid(jnp.concatenate([g[:, :RNN_BLOCK] for g in gates], axis=-1) + ba_ref[...])
    ig = jax.nn.sigmoid(jnp.concatenate([g[:, RNN_BLOCK:] for g in gates], axis=-1) + bi_ref[...])
    log_a = -RG_C * r * sp
    a = jnp.exp(log_a)
    m2 = jnp.tanh(-log_a) * (1.0 + a * a)
    mult = jnp.where(m2 > 0.0, m2 * lax.rsqrt(m2), 0.0)
    if stream_start:
        rows = lax.broadcasted_iota(I32, (tm, RNN_WIDTH), 0)
        mult = jnp.where(first & (rows < SUBLANES), 1.0, mult)
    a_s[...] = a
    b_s[...] = mult * ig * xr

    def step(t, h):
        r0 = pl.multiple_of(t * SUBLANES, SUBLANES)
        h = a_s[pl.ds(r0, SUBLANES), :] * h + b_s[pl.ds(r0, SUBLANES), :]
        b_s[pl.ds(r0, SUBLANES), :] = h
        return h

    h = lax.fori_loop(0, tm // SUBLANES, step, h_s[...], unroll=True)
    h_s[...] = h
    hl_ref[...] = h

    gate = _dot(xn, win_ref[:, :RNN_WIDTH])
    act = (jax.nn.gelu(gate) * b_s[...]).astype(BF16)
    x1 = x + _dot(act, wo_ref[...])
    x1_ref[...] = x1
    _xq_tail(x1, gx_ref, wxq_ref, qnx_ref, cs, qx_ref)


def _odd(x, g, w_in, conv_w, conv_b, w_ai, b_a, b_i, lam, w_out, hist, h0, gx, w_xq, qnx, tt, layer,
         stream_start):
    nb = SUBLANES
    tm = tt * nb
    rows = x.shape[0]
    t_len = rows // nb
    d = D_MODEL
    r = RNN_WIDTH
    nh = (C_CONV - 1) * SUBLANES

    def row(c):
        return pl.BlockSpec((tm, c), lambda i: (i, 0))

    blk = (RNN_BLOCKS, RNN_BLOCK, 2 * RNN_BLOCK)
    return pl.pallas_call(
        functools.partial(_odd_kernel, stream_start=stream_start),
        grid=(rows // tm,),
        in_specs=[row(d), _const_spec((1, d)), _const_spec((d, 2 * r)), _const_spec((C_CONV, r)),
                  _const_spec((1, r)), _const_spec(blk), _const_spec((1, r)),
                  _const_spec((1, r)), _const_spec((1, r)), _const_spec((r, d)), _const_spec((nh, r)),
                  _const_spec((SUBLANES, r)), _const_spec((1, d)), _const_spec((d, d), layer), _const_spec((1, d))],
        out_specs=[row(d), pl.BlockSpec((nb, tt, d), lambda i: (0, i, 0)), pl.BlockSpec((nh, r), lambda i: (0, 0)),
                   pl.BlockSpec((SUBLANES, r), lambda i: (0, 0))],
        out_shape=[jax.ShapeDtypeStruct((rows, d), F32), jax.ShapeDtypeStruct((nb, t_len, d), BF16),
                   jax.ShapeDtypeStruct((nh, r), F32), jax.ShapeDtypeStruct((SUBLANES, r), F32)],
        scratch_shapes=[pltpu.VMEM((tm + nh, r), F32), pltpu.VMEM((tm, r), F32), pltpu.VMEM((tm, r), F32),
                        pltpu.VMEM((SUBLANES, r), F32), pltpu.VMEM((d // LANES, tm, LANES), F32)],
        compiler_params=_params(("arbitrary",)),
        name="odd_mixer",
    )(x, g, w_in, conv_w, conv_b, w_ai, b_a, b_i, lam, w_out, hist, h0, gx, w_xq, qnx)


def _to_tm(a):
    return jnp.transpose(a, (1, 0, 2)).reshape(a.shape[1] * a.shape[0], a.shape[2])


def _from_tm(a, w):
    return jnp.transpose(a.reshape(w, SUBLANES, a.shape[1]), (1, 0, 2))


def _tile_plan(t_len):
    return {"proj": min(128, t_len),
            "ffn": min(64, t_len),
            "dsa": min(KEY_BLOCK, t_len),
            "xattn": t_len}


def _trunk(x, st, mem_k, mem_v, p, bias):
    nb, t_len, d = x.shape
    assert nb == SUBLANES
    tiles = _tile_plan(t_len)
    past = 0 if st is None else st["b_k"].shape[2]
    length = past + t_len
    ntop = min(TOPK_MAX, length // 4)
    near, far = bias
    xt = None
    out = {}

    def hist(name, l, width, c):
        if st is None:
            return jnp.zeros(((width - 1) * nb, c), F32)
        return _to_tm(st[name][l])

    for l in range(DEPTH):
        if l % 2 == 0:
            e = l // 2
            assert l == 0, "the per-sequence input is converted by the first layer's kernel"
            xt, ya, q, iq, ikw, k, v, uh = _even_in(
                x, p["g_mix"][l], p["w_in_even"][e], p["a_conv_w"][e], p["b_q_norm"][e], p["b_k_norm"][e],
                hist("a_conv", e, A_CONV, A_WIDTH), tiles["proj"])
            k_new = k.reshape(nb, t_len, B_KV_HEADS, B_HEAD_DIM)
            v_new = v.reshape(nb, t_len, B_KV_HEADS, B_HEAD_DIM)
            ik_new = ikw[:, :, :IDX_DIM]
            k_all, v_all, ik_all = k_new, v_new, ik_new
            if st is not None:
                k_all = jnp.concatenate([st["b_k"][e], k_new], axis=1)
                v_all = jnp.concatenate([st["b_v"][e], v_new], axis=1)
                ik_all = jnp.concatenate([st["b_kidx"][e], ik_new], axis=1)
            lp = -(-length // KEY_BLOCK) * KEY_BLOCK
            padl = lp - length
            k_att = jnp.pad(jnp.transpose(k_all, (0, 2, 1, 3)).astype(BF16), ((0, 0), (0, 0), (0, padl), (0, 0)))
            v_att = jnp.pad(jnp.transpose(v_all, (0, 2, 1, 3)).astype(BF16), ((0, 0), (0, 0), (0, padl), (0, 0)))
            vt_att = jnp.transpose(v_att.reshape(nb, B_KV_HEADS, lp // KEY_BLOCK, KEY_BLOCK, B_HEAD_DIM),
                                   (0, 1, 2, 4, 3))
            ik_att = jnp.pad(ik_all.astype(BF16), ((0, 0), (0, padl), (0, 0)))
            yb = _dsa(q, iq, ikw, k_att, vt_att, ik_att, near, far, tiles["dsa"], past, length, ntop)
            x1, qx = _even_out(xt, ya, yb, p["w_out_even"][e], p["g_x"][l], p["w_xq"], p["x_q_norm"][l],
                               tiles["proj"], l)
            out.setdefault("a_conv", []).append(_from_tm(uh, A_CONV - 1))
            out.setdefault("b_k", []).append(k_new)
            out.setdefault("b_v", []).append(v_new)
            out.setdefault("b_kidx", []).append(ik_new)
        else:
            o = l // 2
            h0 = jnp.zeros((nb, RNN_WIDTH), F32) if st is None else st["c_h"][o]
            x1, qx, ch, hl = _odd(
                xt, p["g_mix"][l], p["w_in_odd"][o], p["c_conv_w"][o], p["c_conv_b"][o], p["c_w_ai"][o],
                p["c_b_a"][o], p["c_b_i"][o], p["c_lambda"][o], p["w_out_odd"][o],
                hist("c_conv", o, C_CONV, RNN_WIDTH), h0, p["g_x"][l], p["w_xq"], p["x_q_norm"][l],
                tiles["proj"], l, stream_start=(past == 0))
            out.setdefault("c_conv", []).append(_from_tm(ch, C_CONV - 1))
            out.setdefault("c_h", []).append(hl)
        xo = _xattn(qx, mem_k[l], mem_v[l], tiles["xattn"])
        xt, fh = _ffn(x1, xo, p["w_xo"], p["g_ffn"][l], p["w_up"], p["f_conv_w"][l], p["f_conv_b"][l],
                      p["w_down"], hist("f_conv", l, F_CONV, D_FF), tiles["ffn"], l, seq_out=(l == DEPTH - 1))
        out.setdefault("f_conv", []).append(_from_tm(fh, F_CONV - 1))
    return xt, {name: jnp.stack(v) for name, v in out.items()}


def kernel(x_prompt, x_sample, cache_b_k, cache_b_v, cache_b_kidx, state_a_conv, state_c_conv, state_c_h, state_ffn_conv, cache_mem_k, cache_mem_v, mem_prompt, rel_table, g_mix, w_in_even, a_conv_w, b_q_norm, b_k_norm, w_out_even, w_in_odd, c_conv_w, c_conv_b, c_w_a, c_b_a, c_w_i, c_b_i, c_lambda, w_out_odd, g_mem, g_x, w_xq, w_xk, w_xv, x_q_norm, x_k_norm, w_xo, g_ffn, w_up, f_conv_w, f_conv_b, w_down):
    d = D_MODEL
    bp, t_p, _ = x_prompt.shape
    m = mem_prompt.shape[1]

    def rowvec(a):
        return a.reshape(a.shape[0], 1, a.shape[-1])

    def mxu(a):
        return [a[l].astype(BF16) for l in range(a.shape[0])]

    p = {
        "g_mix": rowvec(g_mix), "g_x": rowvec(g_x), "g_ffn": rowvec(g_ffn),
        "w_in_even": [jnp.pad(w, ((0, 0), (0, EVEN_IN_PAD - EVEN_IN))) for w in mxu(w_in_even)],
        "a_conv_w": a_conv_w,
        "b_q_norm": rowvec(jnp.tile(b_q_norm, (1, B_HEADS))),
        "b_k_norm": rowvec(jnp.tile(b_k_norm, (1, B_KV_HEADS))),
        "w_out_even": mxu(w_out_even),
        "w_in_odd": mxu(w_in_odd), "c_conv_w": c_conv_w, "c_conv_b": rowvec(c_conv_b),
        "c_w_ai": mxu(jnp.concatenate([c_w_a, c_w_i], axis=-1)), "c_b_a": rowvec(c_b_a), "c_b_i": rowvec(c_b_i),
        "c_lambda": rowvec(c_lambda), "w_out_odd": mxu(w_out_odd),
        "w_xq": w_xq.astype(BF16), "x_q_norm": rowvec(jnp.tile(x_q_norm, (1, X_HEADS))),
        "w_xo": w_xo.astype(BF16), "w_up": w_up.astype(BF16), "f_conv_w": f_conv_w,
        "f_conv_b": rowvec(f_conv_b), "w_down": w_down.astype(BF16),
    }
    bias = _bias_tiles(rel_table, KEY_BLOCK)

    mk, mv = _mem_kv(mem_prompt.reshape(bp * m, d), g_mem, w_xk, x_k_norm, w_xv)
    p_mem_k = mk.reshape(DEPTH, bp, m, X_HEADS, X_HEAD_DIM)
    p_mem_v = mv.reshape(DEPTH, bp, m, X_HEADS, X_HEAD_DIM)
    y_prompt, new_p = _trunk(x_prompt, None, mk.reshape(DEPTH, bp, m, d).astype(BF16),
                             mv.reshape(DEPTH, bp, m, d).astype(BF16), p, bias)

    bs, t_s, _ = x_sample.shape
    st_s = {"b_k": cache_b_k, "b_v": cache_b_v, "b_kidx": cache_b_kidx, "a_conv": state_a_conv,
            "c_conv": state_c_conv, "c_h": state_c_h, "f_conv": state_ffn_conv}
    ms = cache_mem_k.shape[2]
    y_sample, new_s = _trunk(x_sample, st_s, cache_mem_k.reshape(DEPTH, bs, ms, d).astype(BF16),
                             cache_mem_v.reshape(DEPTH, bs, ms, d).astype(BF16), p, bias)
    return (y_prompt, y_sample,
            new_p["b_k"], new_p["b_v"], new_p["b_kidx"], new_p["a_conv"], new_p["c_conv"],
            new_p["c_h"], new_p["f_conv"], p_mem_k, p_mem_v,
            new_s["b_k"], new_s["b_v"], new_s["b_kidx"], new_s["a_conv"], new_s["c_conv"],
            new_s["c_h"], new_s["f_conv"])
```

```python
import functools
import math

import jax
import jax.numpy as jnp
from jax import lax
from jax.experimental import pallas as pl
from jax.experimental.pallas import tpu as pltpu

F32 = jnp.float32
BF16 = jnp.bfloat16
I32 = jnp.int32
I16 = jnp.int16

D_MODEL = 1024
DEPTH = 2
CHUNK = 64
CHUNK_SHIFT = CHUNK.bit_length() - 1
EPS = 1e-6
NEG_INF = -1e30
LOG2E = math.log2(math.e)
A_WIDTH = 512
A_CONV = 3
B_HEADS = 8
B_KV_HEADS = 2
B_HEAD_DIM = 64
IDX_HEADS = 8
IDX_DIM = 32
TOPK_MAX = 256
REL_BUCKETS = 32
REL_MAX_DIST = 128
RNN_WIDTH = 1024
RNN_BLOCKS = 8
RNN_BLOCK = 128
C_CONV = 4
RG_C = 8.0
X_HEADS = 4
X_HEAD_DIM = 256
D_FF = 2816
F_CONV = 3
EVEN_IN = 2600

SUBLANES = 8
LANES = 128
PACKED_ROWS = 16
VMEM_LIMIT = 56 * 1024 * 1024

EVEN_IN_PAD = 2688
KEY_BLOCK = 256
INT_MIN = -2147483648


def _params(sem, vmem=VMEM_LIMIT):
    return pltpu.CompilerParams(dimension_semantics=sem, vmem_limit_bytes=vmem)


def _const_spec(shape, layer=None):
    nd = len(shape)
    if layer is None:
        return pl.BlockSpec(shape, lambda *_: (0,) * nd, pipeline_mode=pl.Buffered(1))
    return pl.BlockSpec((None,) + tuple(shape), lambda *_: (layer,) + (0,) * nd, pipeline_mode=pl.Buffered(1))


def _rms(x, g):
    ms = jnp.mean(x * x, axis=-1, keepdims=True)
    return x * lax.rsqrt(ms + EPS) * g


def _head_rms(x, hd, gain):
    m, c = x.shape
    s = x * x
    parts = []
    if hd >= LANES:
        for h in range(c // hd):
            ms = jnp.mean(s[:, h * hd:(h + 1) * hd], axis=-1, keepdims=True)
            parts.append(x[:, h * hd:(h + 1) * hd] * lax.rsqrt(ms + EPS))
    else:
        lane = lax.broadcasted_iota(I32, (m, LANES), 1)
        for j in range(c // LANES):
            sj = s[:, j * LANES:(j + 1) * LANES]
            inv = jnp.zeros((m, LANES), F32)
            for k in range(LANES // hd):
                msk = (lane >= k * hd) & (lane < (k + 1) * hd)
                ms = jnp.sum(jnp.where(msk, sj, 0.0), axis=-1, keepdims=True) * (1.0 / hd)
                inv = jnp.where(msk, lax.rsqrt(ms + EPS), inv)
            parts.append(x[:, j * LANES:(j + 1) * LANES] * inv)
    y = parts[0] if len(parts) == 1 else jnp.concatenate(parts, axis=-1)
    return y * gain


def _dot(a, b):
    return jnp.dot(a, b, preferred_element_type=F32)


def _dot_t(a, b):
    return lax.dot_general(a, b, (((1,), (1,)), ((), ())), preferred_element_type=F32)


def _conv_taps(buf, cur, w_ref, width, tm):
    y = cur * w_ref[width - 1:width, :]
    for i in range(width - 1):
        y = y + buf[i * SUBLANES:i * SUBLANES + tm, :] * w_ref[i:i + 1, :]
    return y


def _seq_to_tm(src_ref, scr):
    nb, tt, c = src_ref.shape
    for b in range(nb):
        for j in range(c // LANES):
            scr[j, pl.ds(b, tt, stride=nb), :] = src_ref[b, :, j * LANES:(j + 1) * LANES].astype(F32)
    return jnp.concatenate([scr[j] for j in range(c // LANES)], axis=-1)


def _tm_to_seq(val, scr, dst_refs):
    tm, c = val.shape
    tt = tm // SUBLANES
    for j in range(c // LANES):
        scr[j] = val[:, j * LANES:(j + 1) * LANES]
    for b in range(SUBLANES):
        j0 = 0
        for ref, ci in dst_refs:
            nj = ci // LANES
            parts = [scr[j0 + j, pl.ds(b, tt, stride=SUBLANES), :] for j in range(nj)]
            ref[b] = (parts[0] if nj == 1 else jnp.concatenate(parts, axis=-1)).astype(ref.dtype)
            j0 += nj


def _memkv_kernel(mem_ref, g_ref, wk_ref, kn_ref, wv_ref, k_ref, v_ref):
    hm = _rms(mem_ref[...], g_ref[0]).astype(BF16)
    k_ref[0] = _head_rms(_dot(hm, wk_ref[0]), X_HEAD_DIM, kn_ref[0])
    v_ref[0] = _dot(hm, wv_ref[0])


def _mem_kv(mem, g_mem, w_xk, x_k_norm, w_xv):
    rows = mem.shape[0]
    tm = min(512, rows)
    d = D_MODEL
    kn = jnp.tile(x_k_norm, (1, X_HEADS)).reshape(DEPTH, 1, d)
    out = jax.ShapeDtypeStruct((DEPTH, rows, d), F32)
    return pl.pallas_call(
        _memkv_kernel,
        grid=(DEPTH, rows // tm),
        in_specs=[
            pl.BlockSpec((tm, d), lambda l, i: (i, 0)),
            pl.BlockSpec((1, 1, d), lambda l, i: (l, 0, 0)),
            pl.BlockSpec((1, d, d), lambda l, i: (l, 0, 0)),
            pl.BlockSpec((1, 1, d), lambda l, i: (l, 0, 0)),
            pl.BlockSpec((1, d, d), lambda l, i: (l, 0, 0)),
        ],
        out_specs=[pl.BlockSpec((1, tm, d), lambda l, i: (l, i, 0))] * 2,
        out_shape=[out, out],
        compiler_params=_params(("arbitrary", "arbitrary")),
        name="mem_kv",
    )(mem, g_mem.reshape(DEPTH, 1, d), w_xk.astype(BF16), kn, w_xv.astype(BF16))


def _even_in_kernel(x_ref, g_ref, w_ref, cw_ref, qn_ref, kn_ref, hist_ref,
                    xt_ref, ya_ref, q_ref, iq_ref, ikw_ref, k_ref, v_ref, kb_ref, ikb_ref, uh_ref, ubuf, xs, cs):
    tm = xt_ref.shape[0]
    nh = (A_CONV - 1) * SUBLANES

    @pl.when(pl.program_id(0) == 0)
    def _():
        ubuf[0:nh, :] = hist_ref[...]

    x = _seq_to_tm(x_ref, xs)
    xt_ref[...] = x
    xn = _rms(x, g_ref[...]).astype(BF16)

    def proj(a, b):
        return _dot(xn, w_ref[:, a:b])

    zc = proj(A_WIDTH, 3 * A_WIDTH)
    u = zc[:, :A_WIDTH] * zc[:, A_WIDTH:]
    ubuf[nh:nh + tm, :] = u
    conv = _conv_taps(ubuf, u, cw_ref, A_CONV, tm)
    ya_ref[...] = (proj(0, A_WIDTH) * conv).astype(BF16)
    tail = ubuf[tm:tm + nh, :]
    ubuf[0:nh, :] = tail
    uh_ref[...] = tail

    o = 3 * A_WIDTH
    nq = B_HEADS * B_HEAD_DIM
    nkv = B_KV_HEADS * B_HEAD_DIM
    ni = IDX_HEADS * IDX_DIM
    q = _head_rms(proj(o, o + nq), B_HEAD_DIM, qn_ref[...]) * (B_HEAD_DIM ** -0.5 * LOG2E)
    zkv = proj(o + nq, o + nq + 2 * nkv)
    k = _head_rms(zkv[:, :nkv], B_HEAD_DIM, kn_ref[...])
    zi = proj(o + nq + 2 * nkv, EVEN_IN_PAD)
    seq = jnp.concatenate([q, zi, k, zkv[:, nkv:], k, zi[:, ni:]], axis=-1)
    _tm_to_seq(seq, cs, [(q_ref, nq), (iq_ref, ni), (ikw_ref, LANES), (k_ref, nkv), (v_ref, nkv),
                         (kb_ref, nkv), (ikb_ref, LANES)])


def _even_in(x, g, w_pad, conv_w, qn, kn, hist, tt):
    nb, t_len, d = x.shape
    tm = tt * nb
    rows = t_len * nb
    nh = (A_CONV - 1) * SUBLANES
    nq = B_HEADS * B_HEAD_DIM
    nkv = B_KV_HEADS * B_HEAD_DIM
    ni = IDX_HEADS * IDX_DIM
    nseq = nq + ni + 2 * LANES + 3 * nkv

    def row(c):
        return pl.BlockSpec((tm, c), lambda i: (i, 0))

    def seq(c):
        return pl.BlockSpec((nb, tt, c), lambda i: (0, i, 0))

    def seq_shape(c, dt):
        return jax.ShapeDtypeStruct((nb, t_len, c), dt)

    return pl.pallas_call(
        _even_in_kernel,
        grid=(t_len // tt,),
        in_specs=[seq(d), _const_spec((1, d)), _const_spec((d, EVEN_IN_PAD)), _const_spec((A_CONV, A_WIDTH)),
                  _const_spec((1, nq)), _const_spec((1, nkv)), _const_spec((nh, A_WIDTH))],
        out_specs=[row(d), row(A_WIDTH), seq(nq), seq(ni), seq(LANES), seq(nkv), seq(nkv), seq(nkv), seq(LANES),
                   pl.BlockSpec((nh, A_WIDTH), lambda i: (0, 0))],
        out_shape=[jax.ShapeDtypeStruct((rows, d), F32), jax.ShapeDtypeStruct((rows, A_WIDTH), BF16),
                   seq_shape(nq, BF16), seq_shape(ni, BF16), seq_shape(LANES, F32), seq_shape(nkv, F32),
                   seq_shape(nkv, F32), seq_shape(nkv, BF16), seq_shape(LANES, BF16),
                   jax.ShapeDtypeStruct((nh, A_WIDTH), F32)],
        scratch_shapes=[pltpu.VMEM((tm + nh, A_WIDTH), F32), pltpu.VMEM((d // LANES, tm, LANES), F32),
                        pltpu.VMEM((nseq // LANES, tm, LANES), F32)],
        compiler_params=_params(("arbitrary",)),
        name="even_in",
    )(x, g, w_pad, conv_w, qn, kn, hist)


def _rel_bucket(rel):
    half = REL_BUCKETS // 2
    max_exact = half // 2
    n = -rel
    ret = jnp.where(n < 0, half, 0)
    n = jnp.abs(n)
    nf = jnp.maximum(n, 1).astype(F32)
    large = max_exact + (jnp.log(nf / max_exact) / math.log(REL_MAX_DIST / max_exact)
                         * (half - max_exact)).astype(I32)
    large = jnp.minimum(large, half - 1)
    return ret + jnp.where(n < max_exact, n, large)


def _bias_kernel(tab_ref, near_ref, far_ref):
    tk = near_ref.shape[-1]
    r = lax.broadcasted_iota(I32, (tk, tk), 0)
    c = lax.broadcasted_iota(I32, (tk, tk), 1)

    def lookup(bucket, h):
        def body(j, acc):
            return jnp.where(bucket == j, tab_ref[j, h], acc)
        return lax.fori_loop(0, REL_BUCKETS, body, jnp.zeros(bucket.shape, F32))

    for blk in range(2):
        bucket = _rel_bucket(r - c - blk * tk)
        for h in range(B_HEADS):
            near_ref[blk, h] = lookup(bucket, h) * LOG2E
    bucket = _rel_bucket(-REL_MAX_DIST - c[0:SUBLANES, :])
    for h in range(B_HEADS):
        far_ref[h] = lookup(bucket, h) * LOG2E


def _bias_tiles(rel_table, tk):
    return pl.pallas_call(
        _bias_kernel,
        in_specs=[pl.BlockSpec(memory_space=pltpu.SMEM)],
        out_shape=[jax.ShapeDtypeStruct((2, B_HEADS, tk, tk), F32),
                   jax.ShapeDtypeStruct((B_HEADS, SUBLANES, tk), F32)],
        name="rel_bias",
    )(rel_table)


def _dsa_kernel(q_ref, iq_ref, ikw_ref, k_ref, vt_ref, ik_ref, near_ref, far_ref, o_ref,
                skey, skh, skl, madd_s, s_s, p_s, qh_s, iqh_s, pad_s, w_s, m_s, acc_s,
                *, tq, qw, tk, past, length, ntop, idx_bits):
    i = pl.program_id(1)
    q0 = past + i * tq
    nkb = (q0 + tq + tk - 1) // tk
    last = nkb - 1
    ni = IDX_HEADS * IDX_DIM
    groups = B_HEADS // B_KV_HEADS
    lane = lax.broadcasted_iota(I32, (1, qw), 1)
    sub8 = lax.broadcasted_iota(I32, (SUBLANES, qw), 0)
    qchunk = jnp.right_shift(q0 + lane, CHUNK_SHIFT)

    if tq != qw:
        qh_s[...] = jnp.zeros(qh_s.shape, BF16)
        iqh_s[...] = jnp.zeros(iqh_s.shape, BF16)
        pad_s[...] = jnp.zeros(pad_s.shape, F32)
    for h in range(B_HEADS):
        qh_s[h, 0:tq, :] = q_ref[:, B_HEAD_DIM * h:B_HEAD_DIM * (h + 1)]
    for h in range(IDX_HEADS):
        iqh_s[h, 0:tq, :] = iq_ref[:, IDX_DIM * h:IDX_DIM * (h + 1)]
    pad_s[0:tq, :] = ikw_ref[...]
    w_s[...] = pad_s[...].T[IDX_DIM:IDX_DIM + IDX_HEADS, :] * (ni ** -0.5)

    slab = tk

    def scores(kb, masked):
        for sl in range(tk // slab):
            ks = pl.multiple_of(kb * tk + sl * slab, slab)
            ikb = ik_ref[0, pl.ds(ks, slab), 0:IDX_DIM]
            acc = jnp.zeros((slab, qw), F32)
            for h in range(IDX_HEADS):
                acc = acc + w_s[h:h + 1, :] * jnp.maximum(_dot_t(ikb, iqh_s[h]), 0.0)
            if masked:
                kpos = kb * tk + sl * slab + lax.broadcasted_iota(I32, (slab, qw), 0)
                vis = (jnp.right_shift(kpos, CHUNK_SHIFT) <= qchunk) & (kpos < length)
                acc = jnp.where(vis, acc, -jnp.inf)
            bits = lax.bitcast_convert_type(acc, I32)
            key = jnp.where(bits < 0, bits ^ jnp.int32(0x7FFFFFFF), bits)
            rows = slice(sl * slab, (sl + 1) * slab)
            skey[kb, rows, :] = key
            skh[kb, rows, :] = jnp.right_shift(key, 16).astype(I16)
            skl[kb, rows, :] = ((key & 0xFFFF) - 32768).astype(I16)

    def score_quad(j, carry):
        for u in range(4):
            scores(4 * j + u, False)
        return carry

    lax.fori_loop(0, last // 4, score_quad, 0)
    rem = last % 4

    @pl.when(rem >= 2)
    def _():
        scores(last - rem, False)
        scores(last - rem + 1, False)

    @pl.when(rem % 2 == 1)
    def _():
        scores(last - 1, False)

    scores(last, True)

    n_acc = 4
    prow = PACKED_ROWS

    @pl.when(nkb % 2 == 1)
    def _():
        skh[nkb] = jnp.full((tk, qw), -32768, I16)
        skl[nkb] = jnp.full((tk, qw), -32768, I16)

    def count16(ref, pred):
        def body(j, accs):
            accs = list(accs)
            for kb in (2 * j, 2 * j + 1):
                for g in range(tk // prow):
                    blk = ref[kb, g * prow:(g + 1) * prow, :]
                    accs[g % n_acc] = accs[g % n_acc] + jnp.where(pred(blk), jnp.int16(1), jnp.int16(0))
            return tuple(accs)
        accs = lax.fori_loop(0, (nkb + 1) // 2, body, tuple(jnp.zeros((prow, qw), I16) for _ in range(n_acc)))
        tot = (accs[0].astype(I32) + accs[1].astype(I32)) + (accs[2].astype(I32) + accs[3].astype(I32))
        return jnp.sum(tot, axis=0, keepdims=True)

    def rep16(v):
        return jnp.broadcast_to(v, (prow, qw)).astype(I16)

    def kth16(ref, kth):
        def bit_body(it, prefix):
            cand_u = prefix | jnp.left_shift(jnp.int32(1), 15 - it)
            cand = rep16(cand_u - 32768)
            cnt = count16(ref, lambda blk: blk >= cand)
            return jnp.where(cnt >= kth, cand_u, prefix)
        return lax.fori_loop(0, 16, bit_body, jnp.zeros((1, qw), I32)) - 32768

    def count(pred):
        def body(kb, accs):
            accs = list(accs)
            for g in range(tk // SUBLANES):
                blk = skey[kb, g * SUBLANES:(g + 1) * SUBLANES, :]
                accs[g % n_acc] = accs[g % n_acc] + jnp.where(pred(kb, g, blk), 1, 0)
            return tuple(accs)
        accs = lax.fori_loop(0, nkb, body, tuple(jnp.zeros((SUBLANES, qw), I32) for _ in range(n_acc)))
        tot = (accs[0] + accs[1]) + (accs[2] + accs[3])
        return jnp.sum(tot, axis=0, keepdims=True)

    def rep8(v):
        return jnp.broadcast_to(v, (SUBLANES, qw))

    p_hi = kth16(skh, ntop)
    hi16 = rep16(p_hi)
    above = count16(skh, lambda blk: blk > hi16)

    def low_body(kb, carry):
        for g in range(tk // prow):
            rows = slice(g * prow, (g + 1) * prow)
            skl[kb, rows, :] = jnp.where(skh[kb, rows, :] == hi16, skl[kb, rows, :], jnp.int16(-32768))
        return carry

    lax.fori_loop(0, nkb, low_body, 0)
    p_lo = kth16(skl, ntop - above)
    tau = p_hi * 65536 + (p_lo + 32768)
    tau8 = rep8(tau)

    cnt_ge = count(lambda kb, g, blk: blk >= tau8)
    key_ninf = jnp.int32(0x7FFFFFFF) ^ jnp.int32(-8388608)
    finite = tau != key_ninf
    tie_rows = (cnt_ge > ntop) & finite & (lane < tq)

    @pl.when(jnp.max(jnp.where(tie_rows, 1, 0)) > 0)
    def _():
        need8 = rep8(ntop - count(lambda kb, g, blk: blk > tau8))

        def xbit(it, xlim):
            cand8 = xlim | jnp.left_shift(jnp.int32(1), idx_bits - 1 - it)
            cnt = count(lambda kb, g, blk: (blk == tau8) & ((kb * tk + g * SUBLANES + sub8) < cand8))
            return jnp.where(rep8(cnt) <= need8, cand8, xlim)
        xlim8 = lax.fori_loop(0, idx_bits, xbit, jnp.zeros((SUBLANES, qw), I32))

        def demote(kb, carry):
            for g in range(tk // SUBLANES):
                rows = slice(g * SUBLANES, (g + 1) * SUBLANES)
                key = skey[kb, rows, :]
                late = (key == tau8) & ((kb * tk + g * SUBLANES + sub8) >= xlim8)
                skey[kb, rows, :] = jnp.where(late, key - 1, key)
            return carry

        lax.fori_loop(0, nkb, demote, 0)

    m_s[...] = jnp.full(m_s.shape, NEG_INF, F32)
    acc_s[...] = jnp.zeros(acc_s.shape, F32)
    vrows = B_HEAD_DIM + PACKED_ROWS
    ones_rows = jnp.ones((PACKED_ROWS, tk), BF16)

    tau_ge8 = rep8(jnp.where(finite, tau, tau + 1))

    def attend(kbs, near):
        nblk = len(kbs)
        tiles = [nblk - 1 - i for i in range(nblk)] if near else None
        alphas = {}
        for i, kb in enumerate(kbs):
            for g in range(tk // SUBLANES):
                rows = slice(g * SUBLANES, (g + 1) * SUBLANES)
                madd_s[i, rows, :] = jnp.where(skey[kb, rows, :] >= tau_ge8, 0.0, NEG_INF)

        def logits(h):
            for i, kb in enumerate(kbs):
                n = h // groups
                kn = k_ref[0, pl.ds(pl.multiple_of(kb * tk, tk), tk), n * B_HEAD_DIM:(n + 1) * B_HEAD_DIM]
                bias = madd_s[i] if tiles is None else madd_s[i] + near_ref[tiles[i], h]
                s_s[i, h] = (_dot_t(kn, qh_s[h]) + bias).astype(BF16)

        def softmax(h):
            tiles16 = [s_s[i, h, r * prow:(r + 1) * prow, :] for i in range(nblk) for r in range(tk // prow)]
            mx = tiles16[:n_acc]
            for j, t in enumerate(tiles16[n_acc:]):
                mx[j % n_acc] = jnp.maximum(mx[j % n_acc], t)
            mx = jnp.maximum(jnp.maximum(mx[0], mx[1]), jnp.maximum(mx[2], mx[3]))
            m_cur = jnp.max(mx.astype(F32), axis=0, keepdims=True)
            c_h = jnp.zeros((1, qw), F32) if near else far_ref[h, 0:1, 0:qw]
            m_prev = m_s[h:h + 1, :]
            shift = (jnp.maximum(m_prev, m_cur + c_h) - c_h).astype(BF16)
            m_new = shift.astype(F32) + c_h
            m_s[h:h + 1, :] = m_new
            alphas[h] = jnp.exp2(m_prev - m_new)
            shift = jnp.broadcast_to(shift, (prow, qw))
            for i in range(nblk):
                for r in range(tk // prow):
                    rows = slice(r * prow, (r + 1) * prow)
                    p_s[i, h, rows, :] = jnp.exp2(s_s[i, h, rows, :] - shift)

        def values(h):
            hs = slice(h * vrows, (h + 1) * vrows)
            pv = None
            for i in range(nblk):
                vt1 = jnp.concatenate([vt_ref[0, h // groups, kbs[i]], ones_rows], axis=0)
                pv = _dot(vt1, p_s[i, h]) if pv is None else pv + _dot(vt1, p_s[i, h])
            acc_s[hs, :] = acc_s[hs, :] * alphas[h] + pv

        for phase in (logits, softmax, values):
            for h in range(B_HEADS):
                phase(h)

    nfar = nkb - 2

    def far_pair(j, carry):
        attend([2 * j, 2 * j + 1], False)
        return carry

    lax.fori_loop(0, nfar // 2, far_pair, 0)

    @pl.when((nfar > 0) & (nfar % 2 == 1))
    def _():
        attend([nfar - 1], False)

    @pl.when(nkb >= 2)
    def _():
        attend([nkb - 2, last], True)

    @pl.when(nkb < 2)
    def _():
        attend([last], True)

    outs = []
    for h in range(B_HEADS):
        r0 = h * vrows
        outs.append(acc_s[r0:r0 + B_HEAD_DIM, :] / acc_s[r0 + B_HEAD_DIM:r0 + B_HEAD_DIM + 1, :])
    o = jnp.concatenate(outs, axis=0).T
    o_ref[...] = o[0:tq, :].astype(BF16)


def _dsa(q, iq, ikw, k_att, vt_att, ik_att, near, far, tq, past, length, ntop):
    tk = KEY_BLOCK
    nb, t_len, _ = q.shape
    lp = k_att.shape[1]
    vrows = B_HEAD_DIM + PACKED_ROWS
    nq = B_HEADS * B_HEAD_DIM
    ni = IDX_HEADS * IDX_DIM
    qw = max(tq, LANES)
    assert past % tk == 0 and (tq == tk or t_len == tq) and tq <= tk and lp % tk == 0
    idx_bits = lp.bit_length()

    def qblk(c):
        return pl.BlockSpec((None, tq, c), lambda b, i: (b, i, 0))

    kern = functools.partial(_dsa_kernel, tq=tq, qw=qw, tk=tk, past=past, length=length, ntop=ntop,
                             idx_bits=idx_bits)
    return pl.pallas_call(
        kern,
        grid=(nb, t_len // tq),
        in_specs=[qblk(nq), qblk(ni), qblk(LANES),
                  pl.BlockSpec((1, lp, B_KV_HEADS * B_HEAD_DIM), lambda b, i: (b, 0, 0)),
                  pl.BlockSpec((1, B_KV_HEADS, lp // tk, B_HEAD_DIM, tk), lambda b, i: (b, 0, 0, 0, 0)),
                  pl.BlockSpec((1, lp, LANES), lambda b, i: (b, 0, 0)),
                  pl.BlockSpec((2, B_HEADS, tk, qw), lambda b, i: (0, 0, 0, 0)),
                  pl.BlockSpec((B_HEADS, SUBLANES, tk), lambda b, i: (0, 0, 0))],
        out_specs=qblk(nq),
        out_shape=jax.ShapeDtypeStruct((nb, t_len, nq), BF16),
        scratch_shapes=[pltpu.VMEM((lp // tk, tk, qw), I32),
                        pltpu.VMEM((lp // tk + 1, tk, qw), I16),
                        pltpu.VMEM((lp // tk + 1, tk, qw), I16),
                        pltpu.VMEM((2, tk, qw), F32),
                        pltpu.VMEM((2, B_HEADS, tk, qw), BF16),
                        pltpu.VMEM((2, B_HEADS, tk, qw), BF16),
                        pltpu.VMEM((B_HEADS, qw, B_HEAD_DIM), BF16),
                        pltpu.VMEM((IDX_HEADS, qw, IDX_DIM), BF16),
                        pltpu.VMEM((qw, LANES), F32),
                        pltpu.VMEM((IDX_HEADS, qw), F32),
                        pltpu.VMEM((B_HEADS, qw), F32),
                        pltpu.VMEM((B_HEADS * vrows, qw), F32)],
        compiler_params=_params(("arbitrary", "arbitrary")),
        name="dsa",
    )(q, iq, ikw, k_att, vt_att, ik_att, near, far)


def _xq_tail(x1, gx_ref, wxq_ref, qnx_ref, cs, qx_ref):
    xn = _rms(x1, gx_ref[...]).astype(BF16)
    qx = _head_rms(_dot(xn, wxq_ref[...]), X_HEAD_DIM, qnx_ref[...])
    _tm_to_seq(qx * (X_HEAD_DIM ** -0.5), cs, [(qx_ref, D_MODEL)])


def _even_out_kernel(x_ref, ya_ref, yb_ref, wo_ref, gx_ref, wxq_ref, qnx_ref, x1_ref, qx_ref, ys, cs):
    yb = _seq_to_tm(yb_ref, ys).astype(BF16)
    x1 = x_ref[...] + _dot(ya_ref[...], wo_ref[0:A_WIDTH, :]) + _dot(yb, wo_ref[A_WIDTH:, :])
    x1_ref[...] = x1
    _xq_tail(x1, gx_ref, wxq_ref, qnx_ref, cs, qx_ref)


def _even_out(x, ya, yb, w_out, gx, w_xq, qnx, tt, layer):
    nb, t_len, nyb = yb.shape
    tm = tt * nb
    rows = x.shape[0]
    d = D_MODEL

    def row(c):
        return pl.BlockSpec((tm, c), lambda i: (i, 0))

    def seq(c):
        return pl.BlockSpec((nb, tt, c), lambda i: (0, i, 0))

    return pl.pallas_call(
        _even_out_kernel,
        grid=(rows // tm,),
        in_specs=[row(d), row(A_WIDTH), seq(nyb), _const_spec((d, d)),
                  _const_spec((1, d)), _const_spec((d, d), layer), _const_spec((1, d))],
        out_specs=[row(d), seq(d)],
        out_shape=[jax.ShapeDtypeStruct((rows, d), F32), jax.ShapeDtypeStruct((nb, t_len, d), BF16)],
        scratch_shapes=[pltpu.VMEM((nyb // LANES, tm, LANES), F32), pltpu.VMEM((d // LANES, tm, LANES), F32)],
        compiler_params=_params(("arbitrary",)),
        name="even_out",
    )(x, ya, yb, w_out, gx, w_xq, qnx)


def _xattn_kernel(q_ref, mk_ref, mv_ref, o_ref):
    for h in range(X_HEADS):
        sl = slice(h * X_HEAD_DIM, (h + 1) * X_HEAD_DIM)
        s = _dot_t(q_ref[:, sl], mk_ref[0, :, sl])
        p = jnp.exp(s - jnp.max(s, axis=1, keepdims=True))
        o = _dot(p.astype(BF16), mv_ref[0, :, sl]) / jnp.sum(p, axis=1, keepdims=True)
        o_ref[:, sl] = o.astype(BF16)


def _xattn(qx, mk, mv, tq):
    nb, t_len, d = qx.shape
    m = mk.shape[1]
    return pl.pallas_call(
        _xattn_kernel,
        grid=(nb, t_len // tq),
        in_specs=[pl.BlockSpec((None, tq, d), lambda b, i: (b, i, 0)),
                  pl.BlockSpec((1, m, d), lambda b, i: (b, 0, 0)),
                  pl.BlockSpec((1, m, d), lambda b, i: (b, 0, 0))],
        out_specs=pl.BlockSpec((None, tq, d), lambda b, i: (b, i, 0)),
        out_shape=jax.ShapeDtypeStruct((nb, t_len, d), BF16),
        compiler_params=_params(("arbitrary", "arbitrary")),
        name="mem_attn",
    )(qx, mk, mv)


def _ffn_kernel(x_ref, o_ref, wxo_ref, g_ref, wup_ref, cw_ref, cb_ref, wdn_ref, hist_ref,
                y_ref, fh_ref, gbuf, cs, *, seq_out):
    tm = x_ref.shape[0]
    nh = (F_CONV - 1) * SUBLANES

    @pl.when(pl.program_id(0) == 0)
    def _():
        gbuf[0:nh, :] = hist_ref[...]

    x2 = x_ref[...] + _dot(_seq_to_tm(o_ref, cs).astype(BF16), wxo_ref[...])
    xn = _rms(x2, g_ref[...]).astype(BF16)
    val = _dot(xn, wup_ref[:, :D_FF])
    gate = _dot(xn, wup_ref[:, D_FF:])
    gbuf[nh:nh + tm, :] = gate
    conv = _conv_taps(gbuf, gate, cw_ref, F_CONV, tm) + cb_ref[...]
    act = (jax.nn.gelu(conv) * val).astype(BF16)
    y = x2 + _dot(act, wdn_ref[...])
    tail = gbuf[tm:tm + nh, :]
    gbuf[0:nh, :] = tail
    fh_ref[...] = tail
    if seq_out:
        _tm_to_seq(y, cs, [(y_ref, D_MODEL)])
    else:
        y_ref[...] = y


def _ffn(x, o, w_xo, g, w_up, conv_w, conv_b, w_down, hist, tt, layer, seq_out):
    nb, t_len, d = o.shape
    tm = tt * nb
    rows = x.shape[0]
    nh = (F_CONV - 1) * SUBLANES

    def row(c):
        return pl.BlockSpec((tm, c), lambda i: (i, 0))

    def seq(c):
        return pl.BlockSpec((nb, tt, c), lambda i: (0, i, 0))

    y_shape = jax.ShapeDtypeStruct((nb, t_len, d) if seq_out else (rows, d), F32)
    return pl.pallas_call(
        functools.partial(_ffn_kernel, seq_out=seq_out),
        grid=(rows // tm,),
        in_specs=[row(d), seq(d), _const_spec((d, d), layer), _const_spec((1, d)),
                  _const_spec((d, 2 * D_FF), layer), _const_spec((F_CONV, D_FF)), _const_spec((1, D_FF)),
                  _const_spec((D_FF, d), layer), _const_spec((nh, D_FF))],
        out_specs=[seq(d) if seq_out else row(d), pl.BlockSpec((nh, D_FF), lambda i: (0, 0))],
        out_shape=[y_shape, jax.ShapeDtypeStruct((nh, D_FF), F32)],
        scratch_shapes=[pltpu.VMEM((tm + nh, D_FF), F32), pltpu.VMEM((d // LANES, tm, LANES), F32)],
        compiler_params=_params(("arbitrary",)),
        name="ffn",
    )(x, o, w_xo, g, w_up, conv_w, conv_b, w_down, hist)


def _odd_kernel(x_ref, g_ref, win_ref, cw_ref, cb_ref, wai_ref, ba_ref, bi_ref, lam_ref,
                wo_ref, hist_ref, h0_ref, gx_ref, wxq_ref, qnx_ref,
                x1_ref, qx_ref, ch_ref, hl_ref, xbuf, a_s, b_s, h_s, cs, *, stream_start):
    tm = x_ref.shape[0]
    nh = (C_CONV - 1) * SUBLANES
    first = pl.program_id(0) == 0

    @pl.when(first)
    def _():
        xbuf[0:nh, :] = hist_ref[...]
        h_s[...] = h0_ref[...]

    x = x_ref[...]
    xn = _rms(x, g_ref[...]).astype(BF16)
    xr_in = _dot(xn, win_ref[:, RNN_WIDTH:])
    xbuf[nh:nh + tm, :] = xr_in
    xr = _conv_taps(xbuf, xr_in, cw_ref, C_CONV, tm) + cb_ref[...]
    tail = xbuf[tm:tm + nh, :]
    xbuf[0:nh, :] = tail
    ch_ref[...] = tail

    xrb = xr.astype(BF16)
    lam = -lam_ref[...]
    sp = jnp.maximum(lam, 0.0) + jnp.log1p(jnp.exp(-jnp.abs(lam)))
    gates = [_dot(xrb[:, n * RNN_BLOCK:(n + 1) * RNN_BLOCK], wai_ref[n]) for n in range(RNN_BLOCKS)]
    r = jax.nn.sigmoid(jnp.concatenate([g[:, :RNN_BLOCK] for g in gates], axis=-1) + ba_ref[...])
    ig = jax.nn.sigmoid(jnp.concatenate([g[:, RNN_BLOCK:] for g in gates], axis=-1) + bi_ref[...])
    log_a = -RG_C * r * sp
    a = jnp.exp(log_a)
    m2 = jnp.tanh(-log_a) * (1.0 + a * a)
    mult = jnp.where(m2 > 0.0, m2 * lax.rsqrt(m2), 0.0)
    if stream_start:
        rows = lax.broadcasted_iota(I32, (tm, RNN_WIDTH), 0)
        mult = jnp.where(first & (rows < SUBLANES), 1.0, mult)
    a_s[...] = a
    b_s[...] = mult * ig * xr

    def step(t, h):
        r0 = pl.multiple_of(t * SUBLANES, SUBLANES)
        h = a_s[pl.ds(r0, SUBLANES), :] * h + b_s[pl.ds(r0, SUBLANES), :]
        b_s[pl.ds(r0, SUBLANES), :] = h
        return h

    h = lax.fori_loop(0, tm // SUBLANES, step, h_s[...], unroll=True)
    h_s[...] = h
    hl_ref[...] = h

    gate = _dot(xn, win_ref[:, :RNN_WIDTH])
    act = (jax.nn.gelu(gate) * b_s[...]).astype(BF16)
    x1 = x + _dot(act, wo_ref[...])
    x1_ref[...] = x1
    _xq_tail(x1, gx_ref, wxq_ref, qnx_ref, cs, qx_ref)


def _odd(x, g, w_in, conv_w, conv_b, w_ai, b_a, b_i, lam, w_out, hist, h0, gx, w_xq, qnx, tt, layer,
         stream_start):
    nb = SUBLANES
    tm = tt * nb
    rows = x.shape[0]
    t_len = rows // nb
    d = D_MODEL
    r = RNN_WIDTH
    nh = (C_CONV - 1) * SUBLANES

    def row(c):
        return pl.BlockSpec((tm, c), lambda i: (i, 0))

    blk = (RNN_BLOCKS, RNN_BLOCK, 2 * RNN_BLOCK)
    return pl.pallas_call(
        functools.partial(_odd_kernel, stream_start=stream_start),
        grid=(rows // tm,),
        in_specs=[row(d), _const_spec((1, d)), _const_spec((d, 2 * r)), _const_spec((C_CONV, r)),
                  _const_spec((1, r)), _const_spec(blk), _const_spec((1, r)),
                  _const_spec((1, r)), _const_spec((1, r)), _const_spec((r, d)), _const_spec((nh, r)),
                  _const_spec((SUBLANES, r)), _const_spec((1, d)), _const_spec((d, d), layer), _const_spec((1, d))],
        out_specs=[row(d), pl.BlockSpec((nb, tt, d), lambda i: (0, i, 0)), pl.BlockSpec((nh, r), lambda i: (0, 0)),
                   pl.BlockSpec((SUBLANES, r), lambda i: (0, 0))],
        out_shape=[jax.ShapeDtypeStruct((rows, d), F32), jax.ShapeDtypeStruct((nb, t_len, d), BF16),
                   jax.ShapeDtypeStruct((nh, r), F32), jax.ShapeDtypeStruct((SUBLANES, r), F32)],
        scratch_shapes=[pltpu.VMEM((tm + nh, r), F32), pltpu.VMEM((tm, r), F32), pltpu.VMEM((tm, r), F32),
                        pltpu.VMEM((SUBLANES, r), F32), pltpu.VMEM((d // LANES, tm, LANES), F32)],
        compiler_params=_params(("arbitrary",)),
        name="odd_mixer",
    )(x, g, w_in, conv_w, conv_b, w_ai, b_a, b_i, lam, w_out, hist, h0, gx, w_xq, qnx)


def _to_tm(a):
    return jnp.transpose(a, (1, 0, 2)).reshape(a.shape[1] * a.shape[0], a.shape[2])


def _from_tm(a, w):
    return jnp.transpose(a.reshape(w, SUBLANES, a.shape[1]), (1, 0, 2))


def _tile_plan(t_len):
    return {"proj": min(128, t_len),
            "ffn": min(64, t_len),
            "dsa": min(KEY_BLOCK, t_len),
            "xattn": t_len}


def _trunk(x, st, mem_k, mem_v, p, bias):
    nb, t_len, d = x.shape
    assert nb == SUBLANES
    tiles = _tile_plan(t_len)
    past = 0 if st is None else st["b_k"].shape[2]
    length = past + t_len
    ntop = min(TOPK_MAX, length // 4)
    near, far = bias
    xt = None
    out = {}

    def hist(name, l, width, c):
        if st is None:
            return jnp.zeros(((width - 1) * nb, c), F32)
        return _to_tm(st[name][l])

    for l in range(DEPTH):
        if l % 2 == 0:
            e = l // 2
            assert l == 0, "the per-sequence input is converted by the first layer's kernel"
            xt, ya, q, iq, ikw, k, v, k_att, ik_att, uh = _even_in(
                x, p["g_mix"][l], p["w_in_even"][e], p["a_conv_w"][e], p["b_q_norm"][e], p["b_k_norm"][e],
                hist("a_conv", e, A_CONV, A_WIDTH), tiles["proj"])
            k_new = k.reshape(nb, t_len, B_KV_HEADS, B_HEAD_DIM)
            v_new = v.reshape(nb, t_len, B_KV_HEADS, B_HEAD_DIM)
            ik_new = ikw[:, :, :IDX_DIM]
            v_all = v_new
            if st is not None:
                k_past = st["b_k"][e].reshape(nb, past, B_KV_HEADS * B_HEAD_DIM).astype(BF16)
                ik_past = jnp.pad(st["b_kidx"][e].astype(BF16), ((0, 0), (0, 0), (0, LANES - IDX_DIM)))
                k_att = jnp.concatenate([k_past, k_att], axis=1)
                ik_att = jnp.concatenate([ik_past, ik_att], axis=1)
                v_all = jnp.concatenate([st["b_v"][e], v_new], axis=1)
            lp = -(-length // KEY_BLOCK) * KEY_BLOCK
            padl = lp - length
            k_att = jnp.pad(k_att, ((0, 0), (0, padl), (0, 0)))
            ik_att = jnp.pad(ik_att, ((0, 0), (0, padl), (0, 0)))
            v_att = jnp.pad(jnp.transpose(v_all, (0, 2, 1, 3)).astype(BF16), ((0, 0), (0, 0), (0, padl), (0, 0)))
            vt_att = jnp.transpose(v_att.reshape(nb, B_KV_HEADS, lp // KEY_BLOCK, KEY_BLOCK, B_HEAD_DIM),
                                   (0, 1, 2, 4, 3))
            yb = _dsa(q, iq, ikw, k_att, vt_att, ik_att, near, far, tiles["dsa"], past, length, ntop)
            x1, qx = _even_out(xt, ya, yb, p["w_out_even"][e], p["g_x"][l], p["w_xq"], p["x_q_norm"][l],
                               tiles["proj"], l)
            out.setdefault("a_conv", []).append(_from_tm(uh, A_CONV - 1))
            out.setdefault("b_k", []).append(k_new)
            out.setdefault("b_v", []).append(v_new)
            out.setdefault("b_kidx", []).append(ik_new)
        else:
            o = l // 2
            h0 = jnp.zeros((nb, RNN_WIDTH), F32) if st is None else st["c_h"][o]
            x1, qx, ch, hl = _odd(
                xt, p["g_mix"][l], p["w_in_odd"][o], p["c_conv_w"][o], p["c_conv_b"][o], p["c_w_ai"][o],
                p["c_b_a"][o], p["c_b_i"][o], p["c_lambda"][o], p["w_out_odd"][o],
                hist("c_conv", o, C_CONV, RNN_WIDTH), h0, p["g_x"][l], p["w_xq"], p["x_q_norm"][l],
                tiles["proj"], l, stream_start=(past == 0))
            out.setdefault("c_conv", []).append(_from_tm(ch, C_CONV - 1))
            out.setdefault("c_h", []).append(hl)
        xo = _xattn(qx, mem_k[l], mem_v[l], tiles["xattn"])
        xt, fh = _ffn(x1, xo, p["w_xo"], p["g_ffn"][l], p["w_up"], p["f_conv_w"][l], p["f_conv_b"][l],
                      p["w_down"], hist("f_conv", l, F_CONV, D_FF), tiles["ffn"], l, seq_out=(l == DEPTH - 1))
        out.setdefault("f_conv", []).append(_from_tm(fh, F_CONV - 1))
    return xt, {name: jnp.stack(v) for name, v in out.items()}


def kernel(x_prompt, x_sample, cache_b_k, cache_b_v, cache_b_kidx, state_a_conv, state_c_conv, state_c_h, state_ffn_conv, cache_mem_k, cache_mem_v, mem_prompt, rel_table, g_mix, w_in_even, a_conv_w, b_q_norm, b_k_norm, w_out_even, w_in_odd, c_conv_w, c_conv_b, c_w_a, c_b_a, c_w_i, c_b_i, c_lambda, w_out_odd, g_mem, g_x, w_xq, w_xk, w_xv, x_q_norm, x_k_norm, w_xo, g_ffn, w_up, f_conv_w, f_conv_b, w_down):
    d = D_MODEL
    bp, t_p, _ = x_prompt.shape
    m = mem_prompt.shape[1]

    def rowvec(a):
        return a.reshape(a.shape[0], 1, a.shape[-1])

    def mxu(a):
        return [a[l].astype(BF16) for l in range(a.shape[0])]

    p = {
        "g_mix": rowvec(g_mix), "g_x": rowvec(g_x), "g_ffn": rowvec(g_ffn),
        "w_in_even": [jnp.pad(w, ((0, 0), (0, EVEN_IN_PAD - EVEN_IN))) for w in mxu(w_in_even)],
        "a_conv_w": a_conv_w,
        "b_q_norm": rowvec(jnp.tile(b_q_norm, (1, B_HEADS))),
        "b_k_norm": rowvec(jnp.tile(b_k_norm, (1, B_KV_HEADS))),
        "w_out_even": mxu(w_out_even),
        "w_in_odd": mxu(w_in_odd), "c_conv_w": c_conv_w, "c_conv_b": rowvec(c_conv_b),
        "c_w_ai": mxu(jnp.concatenate([c_w_a, c_w_i], axis=-1)), "c_b_a": rowvec(c_b_a), "c_b_i": rowvec(c_b_i),
        "c_lambda": rowvec(c_lambda), "w_out_odd": mxu(w_out_odd),
        "w_xq": w_xq.astype(BF16), "x_q_norm": rowvec(jnp.tile(x_q_norm, (1, X_HEADS))),
        "w_xo": w_xo.astype(BF16), "w_up": w_up.astype(BF16), "f_conv_w": f_conv_w,
        "f_conv_b": rowvec(f_conv_b), "w_down": w_down.astype(BF16),
    }
    bias = _bias_tiles(rel_table, KEY_BLOCK)

    mk, mv = _mem_kv(mem_prompt.reshape(bp * m, d), g_mem, w_xk, x_k_norm, w_xv)
    p_mem_k = mk.reshape(DEPTH, bp, m, X_HEADS, X_HEAD_DIM)
    p_mem_v = mv.reshape(DEPTH, bp, m, X_HEADS, X_HEAD_DIM)
    y_prompt, new_p = _trunk(x_prompt, None, mk.reshape(DEPTH, bp, m, d).astype(BF16),
                             mv.reshape(DEPTH, bp, m, d).astype(BF16), p, bias)

    bs, t_s, _ = x_sample.shape
    st_s = {"b_k": cache_b_k, "b_v": cache_b_v, "b_kidx": cache_b_kidx, "a_conv": state_a_conv,
            "c_conv": state_c_conv, "c_h": state_c_h, "f_conv": state_ffn_conv}
    ms = cache_mem_k.shape[2]
    y_sample, new_s = _trunk(x_sample, st_s, cache_mem_k.reshape(DEPTH, bs, ms, d).astype(BF16),
                             cache_mem_v.reshape(DEPTH, bs, ms, d).astype(BF16), p, bias)
    return (y_prompt, y_sample,
            new_p["b_k"], new_p["b_v"], new_p["b_kidx"], new_p["a_conv"], new_p["c_conv"],
            new_p["c_h"], new_p["f_conv"], p_mem_k, p_mem_v,
            new_s["b_k"], new_s["b_v"], new_s["b_kidx"], new_s["a_conv"], new_s["c_conv"],
            new_s["c_h"], new_s["f_conv"])
```

```python
import functools
import math

import jax
import jax.numpy as jnp
from jax import lax
from jax.experimental import pallas as pl
from jax.experimental.pallas import tpu as pltpu

F32 = jnp.float32
BF16 = jnp.bfloat16
I32 = jnp.int32
I16 = jnp.int16

D_MODEL = 1024
DEPTH = 2
CHUNK = 64
CHUNK_SHIFT = CHUNK.bit_length() - 1
EPS = 1e-6
NEG_INF = -1e30
LOG2E = math.log2(math.e)
A_WIDTH = 512
A_CONV = 3
B_HEADS = 8
B_KV_HEADS = 2
B_HEAD_DIM = 64
IDX_HEADS = 8
IDX_DIM = 32
TOPK_MAX = 256
REL_BUCKETS = 32
REL_MAX_DIST = 128
RNN_WIDTH = 1024
RNN_BLOCKS = 8
RNN_BLOCK = 128
C_CONV = 4
RG_C = 8.0
X_HEADS = 4
X_HEAD_DIM = 256
D_FF = 2816
F_CONV = 3
EVEN_IN = 2600

SUBLANES = 8
LANES = 128
PACKED_ROWS = 16
VMEM_LIMIT = 56 * 1024 * 1024

EVEN_IN_PAD = 2688
KEY_BLOCK = 256
INT_MIN = -2147483648


def _params(sem, vmem=VMEM_LIMIT):
    return pltpu.CompilerParams(dimension_semantics=sem, vmem_limit_bytes=vmem)


def _const_spec(shape, layer=None):
    nd = len(shape)
    if layer is None:
        return pl.BlockSpec(shape, lambda *_: (0,) * nd, pipeline_mode=pl.Buffered(1))
    return pl.BlockSpec((None,) + tuple(shape), lambda *_: (layer,) + (0,) * nd, pipeline_mode=pl.Buffered(1))


def _rms(x, g):
    ms = jnp.mean(x * x, axis=-1, keepdims=True)
    return x * lax.rsqrt(ms + EPS) * g


def _head_rms(x, hd, gain):
    m, c = x.shape
    s = x * x
    parts = []
    if hd >= LANES:
        for h in range(c // hd):
            ms = jnp.mean(s[:, h * hd:(h + 1) * hd], axis=-1, keepdims=True)
            parts.append(x[:, h * hd:(h + 1) * hd] * lax.rsqrt(ms + EPS))
    else:
        lane = lax.broadcasted_iota(I32, (m, LANES), 1)
        for j in range(c // LANES):
            sj = s[:, j * LANES:(j + 1) * LANES]
            inv = jnp.zeros((m, LANES), F32)
            for k in range(LANES // hd):
                msk = (lane >= k * hd) & (lane < (k + 1) * hd)
                ms = jnp.sum(jnp.where(msk, sj, 0.0), axis=-1, keepdims=True) * (1.0 / hd)
                inv = jnp.where(msk, lax.rsqrt(ms + EPS), inv)
            parts.append(x[:, j * LANES:(j + 1) * LANES] * inv)
    y = parts[0] if len(parts) == 1 else jnp.concatenate(parts, axis=-1)
    return y * gain


def _dot(a, b):
    return jnp.dot(a, b, preferred_element_type=F32)


def _dot_t(a, b):
    return lax.dot_general(a, b, (((1,), (1,)), ((), ())), preferred_element_type=F32)


def _conv_taps(buf, cur, w_ref, width, tm):
    y = cur * w_ref[width - 1:width, :]
    for i in range(width - 1):
        y = y + buf[i * SUBLANES:i * SUBLANES + tm, :] * w_ref[i:i + 1, :]
    return y


def _seq_to_tm(src_ref, scr):
    nb, tt, c = src_ref.shape
    for b in range(nb):
        for j in range(c // LANES):
            scr[j, pl.ds(b, tt, stride=nb), :] = src_ref[b, :, j * LANES:(j + 1) * LANES].astype(F32)
    return jnp.concatenate([scr[j] for j in range(c // LANES)], axis=-1)


def _tm_to_seq(val, scr, dst_refs):
    tm, c = val.shape
    tt = tm // SUBLANES
    for j in range(c // LANES):
        scr[j] = val[:, j * LANES:(j + 1) * LANES]
    for b in range(SUBLANES):
        j0 = 0
        for ref, ci in dst_refs:
            nj = ci // LANES
            parts = [scr[j0 + j, pl.ds(b, tt, stride=SUBLANES), :] for j in range(nj)]
            ref[b] = (parts[0] if nj == 1 else jnp.concatenate(parts, axis=-1)).astype(ref.dtype)
            j0 += nj


def _memkv_kernel(mem_ref, g_ref, wk_ref, kn_ref, wv_ref, k_ref, v_ref):
    hm = _rms(mem_ref[...], g_ref[0]).astype(BF16)
    k_ref[0] = _head_rms(_dot(hm, wk_ref[0]), X_HEAD_DIM, kn_ref[0])
    v_ref[0] = _dot(hm, wv_ref[0])


def _mem_kv(mem, g_mem, w_xk, x_k_norm, w_xv):
    rows = mem.shape[0]
    tm = min(512, rows)
    d = D_MODEL
    kn = jnp.tile(x_k_norm, (1, X_HEADS)).reshape(DEPTH, 1, d)
    out = jax.ShapeDtypeStruct((DEPTH, rows, d), F32)
    return pl.pallas_call(
        _memkv_kernel,
        grid=(DEPTH, rows // tm),
        in_specs=[
            pl.BlockSpec((tm, d), lambda l, i: (i, 0)),
            pl.BlockSpec((1, 1, d), lambda l, i: (l, 0, 0)),
            pl.BlockSpec((1, d, d), lambda l, i: (l, 0, 0)),
            pl.BlockSpec((1, 1, d), lambda l, i: (l, 0, 0)),
            pl.BlockSpec((1, d, d), lambda l, i: (l, 0, 0)),
        ],
        out_specs=[pl.BlockSpec((1, tm, d), lambda l, i: (l, i, 0))] * 2,
        out_shape=[out, out],
        compiler_params=_params(("arbitrary", "arbitrary")),
        name="mem_kv",
    )(mem, g_mem.reshape(DEPTH, 1, d), w_xk.astype(BF16), kn, w_xv.astype(BF16))


def _even_in_kernel(x_ref, g_ref, w_ref, cw_ref, qn_ref, kn_ref, hist_ref,
                    xt_ref, ya_ref, q_ref, iq_ref, ikw_ref, k_ref, v_ref, kb_ref, ikb_ref, vb_ref, uh_ref, ubuf, xs, cs):
    tm = xt_ref.shape[0]
    nh = (A_CONV - 1) * SUBLANES

    @pl.when(pl.program_id(0) == 0)
    def _():
        ubuf[0:nh, :] = hist_ref[...]

    x = _seq_to_tm(x_ref, xs)
    xt_ref[...] = x
    xn = _rms(x, g_ref[...]).astype(BF16)

    def proj(a, b):
        return _dot(xn, w_ref[:, a:b])

    zc = proj(A_WIDTH, 3 * A_WIDTH)
    u = zc[:, :A_WIDTH] * zc[:, A_WIDTH:]
    ubuf[nh:nh + tm, :] = u
    conv = _conv_taps(ubuf, u, cw_ref, A_CONV, tm)
    ya_ref[...] = (proj(0, A_WIDTH) * conv).astype(BF16)
    tail = ubuf[tm:tm + nh, :]
    ubuf[0:nh, :] = tail
    uh_ref[...] = tail

    o = 3 * A_WIDTH
    nq = B_HEADS * B_HEAD_DIM
    nkv = B_KV_HEADS * B_HEAD_DIM
    ni = IDX_HEADS * IDX_DIM
    q = _head_rms(proj(o, o + nq), B_HEAD_DIM, qn_ref[...]) * (B_HEAD_DIM ** -0.5 * LOG2E)
    zkv = proj(o + nq, o + nq + 2 * nkv)
    k = _head_rms(zkv[:, :nkv], B_HEAD_DIM, kn_ref[...])
    zi = proj(o + nq + 2 * nkv, EVEN_IN_PAD)
    seq = jnp.concatenate([q, zi, k, zkv[:, nkv:], k, zi[:, ni:], zkv[:, nkv:]], axis=-1)
    _tm_to_seq(seq, cs, [(q_ref, nq), (iq_ref, ni), (ikw_ref, LANES), (k_ref, nkv), (v_ref, nkv),
                         (kb_ref, nkv), (ikb_ref, LANES), (vb_ref, nkv)])


def _even_in(x, g, w_pad, conv_w, qn, kn, hist, tt):
    nb, t_len, d = x.shape
    tm = tt * nb
    rows = t_len * nb
    nh = (A_CONV - 1) * SUBLANES
    nq = B_HEADS * B_HEAD_DIM
    nkv = B_KV_HEADS * B_HEAD_DIM
    ni = IDX_HEADS * IDX_DIM
    nseq = nq + ni + 2 * LANES + 4 * nkv

    def row(c):
        return pl.BlockSpec((tm, c), lambda i: (i, 0))

    def seq(c):
        return pl.BlockSpec((nb, tt, c), lambda i: (0, i, 0))

    def seq_shape(c, dt):
        return jax.ShapeDtypeStruct((nb, t_len, c), dt)

    return pl.pallas_call(
        _even_in_kernel,
        grid=(t_len // tt,),
        in_specs=[seq(d), _const_spec((1, d)), _const_spec((d, EVEN_IN_PAD)), _const_spec((A_CONV, A_WIDTH)),
                  _const_spec((1, nq)), _const_spec((1, nkv)), _const_spec((nh, A_WIDTH))],
        out_specs=[row(d), row(A_WIDTH), seq(nq), seq(ni), seq(LANES), seq(nkv), seq(nkv), seq(nkv), seq(LANES),
                   seq(nkv),
                   pl.BlockSpec((nh, A_WIDTH), lambda i: (0, 0))],
        out_shape=[jax.ShapeDtypeStruct((rows, d), F32), jax.ShapeDtypeStruct((rows, A_WIDTH), BF16),
                   seq_shape(nq, BF16), seq_shape(ni, BF16), seq_shape(LANES, F32), seq_shape(nkv, F32),
                   seq_shape(nkv, F32), seq_shape(nkv, BF16), seq_shape(LANES, BF16), seq_shape(nkv, BF16),
                   jax.ShapeDtypeStruct((nh, A_WIDTH), F32)],
        scratch_shapes=[pltpu.VMEM((tm + nh, A_WIDTH), F32), pltpu.VMEM((d // LANES, tm, LANES), F32),
                        pltpu.VMEM((nseq // LANES, tm, LANES), F32)],
        compiler_params=_params(("arbitrary",)),
        name="even_in",
    )(x, g, w_pad, conv_w, qn, kn, hist)


def _rel_bucket(rel):
    half = REL_BUCKETS // 2
    max_exact = half // 2
    n = -rel
    ret = jnp.where(n < 0, half, 0)
    n = jnp.abs(n)
    nf = jnp.maximum(n, 1).astype(F32)
    large = max_exact + (jnp.log(nf / max_exact) / math.log(REL_MAX_DIST / max_exact)
                         * (half - max_exact)).astype(I32)
    large = jnp.minimum(large, half - 1)
    return ret + jnp.where(n < max_exact, n, large)


def _bias_kernel(tab_ref, near_ref, far_ref):
    tk = near_ref.shape[-1]
    r = lax.broadcasted_iota(I32, (tk, tk), 0)
    c = lax.broadcasted_iota(I32, (tk, tk), 1)

    def lookup(bucket, h):
        def body(j, acc):
            return jnp.where(bucket == j, tab_ref[j, h], acc)
        return lax.fori_loop(0, REL_BUCKETS, body, jnp.zeros(bucket.shape, F32))

    for blk in range(2):
        bucket = _rel_bucket(r - c - blk * tk)
        for h in range(B_HEADS):
            near_ref[blk, h] = lookup(bucket, h) * LOG2E
    bucket = _rel_bucket(-REL_MAX_DIST - c[0:SUBLANES, :])
    for h in range(B_HEADS):
        far_ref[h] = lookup(bucket, h) * LOG2E


def _bias_tiles(rel_table, tk):
    return pl.pallas_call(
        _bias_kernel,
        in_specs=[pl.BlockSpec(memory_space=pltpu.SMEM)],
        out_shape=[jax.ShapeDtypeStruct((2, B_HEADS, tk, tk), F32),
                   jax.ShapeDtypeStruct((B_HEADS, SUBLANES, tk), F32)],
        name="rel_bias",
    )(rel_table)


def _dsa_kernel(q_ref, iq_ref, ikw_ref, k_ref, vt_ref, ik_ref, near_ref, far_ref, o_ref,
                skey, skh, skl, madd_s, s_s, p_s, qh_s, iqh_s, pad_s, w_s, m_s, acc_s,
                *, tq, qw, tk, past, length, ntop, idx_bits):
    i = pl.program_id(1)
    q0 = past + i * tq
    nkb = (q0 + tq + tk - 1) // tk
    last = nkb - 1
    ni = IDX_HEADS * IDX_DIM
    groups = B_HEADS // B_KV_HEADS
    lane = lax.broadcasted_iota(I32, (1, qw), 1)
    sub8 = lax.broadcasted_iota(I32, (SUBLANES, qw), 0)
    qchunk = jnp.right_shift(q0 + lane, CHUNK_SHIFT)

    if tq != qw:
        qh_s[...] = jnp.zeros(qh_s.shape, BF16)
        iqh_s[...] = jnp.zeros(iqh_s.shape, BF16)
        pad_s[...] = jnp.zeros(pad_s.shape, F32)
    for h in range(B_HEADS):
        qh_s[h, 0:tq, :] = q_ref[:, B_HEAD_DIM * h:B_HEAD_DIM * (h + 1)]
    for h in range(IDX_HEADS):
        iqh_s[h, 0:tq, :] = iq_ref[:, IDX_DIM * h:IDX_DIM * (h + 1)]
    pad_s[0:tq, :] = ikw_ref[...]
    w_s[...] = pad_s[...].T[IDX_DIM:IDX_DIM + IDX_HEADS, :] * (ni ** -0.5)

    slab = tk

    def scores(kb, masked):
        for sl in range(tk // slab):
            ks = pl.multiple_of(kb * tk + sl * slab, slab)
            ikb = ik_ref[0, pl.ds(ks, slab), 0:IDX_DIM]
            acc = jnp.zeros((slab, qw), F32)
            for h in range(IDX_HEADS):
                acc = acc + w_s[h:h + 1, :] * jnp.maximum(_dot_t(ikb, iqh_s[h]), 0.0)
            if masked:
                kpos = kb * tk + sl * slab + lax.broadcasted_iota(I32, (slab, qw), 0)
                vis = (jnp.right_shift(kpos, CHUNK_SHIFT) <= qchunk) & (kpos < length)
                acc = jnp.where(vis, acc, -jnp.inf)
            bits = lax.bitcast_convert_type(acc, I32)
            key = jnp.where(bits < 0, bits ^ jnp.int32(0x7FFFFFFF), bits)
            rows = slice(sl * slab, (sl + 1) * slab)
            skey[kb, rows, :] = key
            skh[kb, rows, :] = jnp.right_shift(key, 16).astype(I16)
            skl[kb, rows, :] = ((key & 0xFFFF) - 32768).astype(I16)

    def score_quad(j, carry):
        for u in range(4):
            scores(4 * j + u, False)
        return carry

    lax.fori_loop(0, last // 4, score_quad, 0)
    rem = last % 4

    @pl.when(rem >= 2)
    def _():
        scores(last - rem, False)
        scores(last - rem + 1, False)

    @pl.when(rem % 2 == 1)
    def _():
        scores(last - 1, False)

    scores(last, True)

    n_acc = 4
    prow = PACKED_ROWS

    @pl.when(nkb % 2 == 1)
    def _():
        skh[nkb] = jnp.full((tk, qw), -32768, I16)
        skl[nkb] = jnp.full((tk, qw), -32768, I16)

    def count16(ref, pred):
        def body(j, accs):
            accs = list(accs)
            for kb in (2 * j, 2 * j + 1):
                for g in range(tk // prow):
                    blk = ref[kb, g * prow:(g + 1) * prow, :]
                    accs[g % n_acc] = accs[g % n_acc] + jnp.where(pred(blk), jnp.int16(1), jnp.int16(0))
            return tuple(accs)
        accs = lax.fori_loop(0, (nkb + 1) // 2, body, tuple(jnp.zeros((prow, qw), I16) for _ in range(n_acc)))
        tot = (accs[0].astype(I32) + accs[1].astype(I32)) + (accs[2].astype(I32) + accs[3].astype(I32))
        return jnp.sum(tot, axis=0, keepdims=True)

    def rep16(v):
        return jnp.broadcast_to(v, (prow, qw)).astype(I16)

    def kth16(ref, kth):
        def bit_body(it, prefix):
            cand_u = prefix | jnp.left_shift(jnp.int32(1), 15 - it)
            cand = rep16(cand_u - 32768)
            cnt = count16(ref, lambda blk: blk >= cand)
            return jnp.where(cnt >= kth, cand_u, prefix)
        return lax.fori_loop(0, 16, bit_body, jnp.zeros((1, qw), I32)) - 32768

    def count(pred):
        def body(kb, accs):
            accs = list(accs)
            for g in range(tk // SUBLANES):
                blk = skey[kb, g * SUBLANES:(g + 1) * SUBLANES, :]
                accs[g % n_acc] = accs[g % n_acc] + jnp.where(pred(kb, g, blk), 1, 0)
            return tuple(accs)
        accs = lax.fori_loop(0, nkb, body, tuple(jnp.zeros((SUBLANES, qw), I32) for _ in range(n_acc)))
        tot = (accs[0] + accs[1]) + (accs[2] + accs[3])
        return jnp.sum(tot, axis=0, keepdims=True)

    def rep8(v):
        return jnp.broadcast_to(v, (SUBLANES, qw))

    p_hi = kth16(skh, ntop)
    hi16 = rep16(p_hi)
    above = count16(skh, lambda blk: blk > hi16)

    def low_body(kb, carry):
        for g in range(tk // prow):
            rows = slice(g * prow, (g + 1) * prow)
            skl[kb, rows, :] = jnp.where(skh[kb, rows, :] == hi16, skl[kb, rows, :], jnp.int16(-32768))
        return carry

    lax.fori_loop(0, nkb, low_body, 0)
    p_lo = kth16(skl, ntop - above)
    tau = p_hi * 65536 + (p_lo + 32768)
    tau8 = rep8(tau)

    cnt_ge = count(lambda kb, g, blk: blk >= tau8)
    key_ninf = jnp.int32(0x7FFFFFFF) ^ jnp.int32(-8388608)
    finite = tau != key_ninf
    tie_rows = (cnt_ge > ntop) & finite & (lane < tq)

    @pl.when(jnp.max(jnp.where(tie_rows, 1, 0)) > 0)
    def _():
        need8 = rep8(ntop - count(lambda kb, g, blk: blk > tau8))

        def xbit(it, xlim):
            cand8 = xlim | jnp.left_shift(jnp.int32(1), idx_bits - 1 - it)
            cnt = count(lambda kb, g, blk: (blk == tau8) & ((kb * tk + g * SUBLANES + sub8) < cand8))
            return jnp.where(rep8(cnt) <= need8, cand8, xlim)
        xlim8 = lax.fori_loop(0, idx_bits, xbit, jnp.zeros((SUBLANES, qw), I32))

        def demote(kb, carry):
            for g in range(tk // SUBLANES):
                rows = slice(g * SUBLANES, (g + 1) * SUBLANES)
                key = skey[kb, rows, :]
                late = (key == tau8) & ((kb * tk + g * SUBLANES + sub8) >= xlim8)
                skey[kb, rows, :] = jnp.where(late, key - 1, key)
            return carry

        lax.fori_loop(0, nkb, demote, 0)

    m_s[...] = jnp.full(m_s.shape, NEG_INF, F32)
    acc_s[...] = jnp.zeros(acc_s.shape, F32)
    vrows = B_HEAD_DIM + PACKED_ROWS
    ones_rows = jnp.ones((PACKED_ROWS, tk), BF16)

    tau_ge8 = rep8(jnp.where(finite, tau, tau + 1))

    def attend(kbs, near):
        nblk = len(kbs)
        tiles = [nblk - 1 - i for i in range(nblk)] if near else None
        alphas = {}
        for i, kb in enumerate(kbs):
            for g in range(tk // SUBLANES):
                rows = slice(g * SUBLANES, (g + 1) * SUBLANES)
                madd_s[i, rows, :] = jnp.where(skey[kb, rows, :] >= tau_ge8, 0.0, NEG_INF)

        def logits(h):
            for i, kb in enumerate(kbs):
                n = h // groups
                kn = k_ref[0, pl.ds(pl.multiple_of(kb * tk, tk), tk), n * B_HEAD_DIM:(n + 1) * B_HEAD_DIM]
                bias = madd_s[i] if tiles is None else madd_s[i] + near_ref[tiles[i], h]
                s_s[i, h] = (_dot_t(kn, qh_s[h]) + bias).astype(BF16)

        def softmax(h):
            tiles16 = [s_s[i, h, r * prow:(r + 1) * prow, :] for i in range(nblk) for r in range(tk // prow)]
            mx = tiles16[:n_acc]
            for j, t in enumerate(tiles16[n_acc:]):
                mx[j % n_acc] = jnp.maximum(mx[j % n_acc], t)
            mx = jnp.maximum(jnp.maximum(mx[0], mx[1]), jnp.maximum(mx[2], mx[3]))
            m_cur = jnp.max(mx.astype(F32), axis=0, keepdims=True)
            c_h = jnp.zeros((1, qw), F32) if near else far_ref[h, 0:1, 0:qw]
            m_prev = m_s[h:h + 1, :]
            shift = (jnp.maximum(m_prev, m_cur + c_h) - c_h).astype(BF16)
            m_new = shift.astype(F32) + c_h
            m_s[h:h + 1, :] = m_new
            alphas[h] = jnp.exp2(m_prev - m_new)
            shift = jnp.broadcast_to(shift, (prow, qw))
            for i in range(nblk):
                for r in range(tk // prow):
                    rows = slice(r * prow, (r + 1) * prow)
                    p_s[i, h, rows, :] = jnp.exp2(s_s[i, h, rows, :] - shift)

        def values(h):
            hs = slice(h * vrows, (h + 1) * vrows)
            pv = None
            for i in range(nblk):
                vt1 = jnp.concatenate([vt_ref[0, h // groups, kbs[i]], ones_rows], axis=0)
                pv = _dot(vt1, p_s[i, h]) if pv is None else pv + _dot(vt1, p_s[i, h])
            acc_s[hs, :] = acc_s[hs, :] * alphas[h] + pv

        for phase in (logits, softmax, values):
            for h in range(B_HEADS):
                phase(h)

    nfar = nkb - 2

    def far_pair(j, carry):
        attend([2 * j, 2 * j + 1], False)
        return carry

    lax.fori_loop(0, nfar // 2, far_pair, 0)

    @pl.when((nfar > 0) & (nfar % 2 == 1))
    def _():
        attend([nfar - 1], False)

    @pl.when(nkb >= 2)
    def _():
        attend([nkb - 2, last], True)

    @pl.when(nkb < 2)
    def _():
        attend([last], True)

    outs = []
    for h in range(B_HEADS):
        r0 = h * vrows
        outs.append(acc_s[r0:r0 + B_HEAD_DIM, :] / acc_s[r0 + B_HEAD_DIM:r0 + B_HEAD_DIM + 1, :])
    o = jnp.concatenate(outs, axis=0).T
    o_ref[...] = o[0:tq, :].astype(BF16)


def _dsa(q, iq, ikw, k_att, vt_att, ik_att, near, far, tq, past, length, ntop):
    tk = KEY_BLOCK
    nb, t_len, _ = q.shape
    lp = k_att.shape[1]
    vrows = B_HEAD_DIM + PACKED_ROWS
    nq = B_HEADS * B_HEAD_DIM
    ni = IDX_HEADS * IDX_DIM
    qw = max(tq, LANES)
    assert past % tk == 0 and (tq == tk or t_len == tq) and tq <= tk and lp % tk == 0
    idx_bits = lp.bit_length()

    def qblk(c):
        return pl.BlockSpec((None, tq, c), lambda b, i: (b, i, 0))

    kern = functools.partial(_dsa_kernel, tq=tq, qw=qw, tk=tk, past=past, length=length, ntop=ntop,
                             idx_bits=idx_bits)
    return pl.pallas_call(
        kern,
        grid=(nb, t_len // tq),
        in_specs=[qblk(nq), qblk(ni), qblk(LANES),
                  pl.BlockSpec((1, lp, B_KV_HEADS * B_HEAD_DIM), lambda b, i: (b, 0, 0)),
                  pl.BlockSpec((1, B_KV_HEADS, lp // tk, B_HEAD_DIM, tk), lambda b, i: (b, 0, 0, 0, 0)),
                  pl.BlockSpec((1, lp, LANES), lambda b, i: (b, 0, 0)),
                  pl.BlockSpec((2, B_HEADS, tk, qw), lambda b, i: (0, 0, 0, 0)),
                  pl.BlockSpec((B_HEADS, SUBLANES, tk), lambda b, i: (0, 0, 0))],
        out_specs=qblk(nq),
        out_shape=jax.ShapeDtypeStruct((nb, t_len, nq), BF16),
        scratch_shapes=[pltpu.VMEM((lp // tk, tk, qw), I32),
                        pltpu.VMEM((lp // tk + 1, tk, qw), I16),
                        pltpu.VMEM((lp // tk + 1, tk, qw), I16),
                        pltpu.VMEM((2, tk, qw), F32),
                        pltpu.VMEM((2, B_HEADS, tk, qw), BF16),
                        pltpu.VMEM((2, B_HEADS, tk, qw), BF16),
                        pltpu.VMEM((B_HEADS, qw, B_HEAD_DIM), BF16),
                        pltpu.VMEM((IDX_HEADS, qw, IDX_DIM), BF16),
                        pltpu.VMEM((qw, LANES), F32),
                        pltpu.VMEM((IDX_HEADS, qw), F32),
                        pltpu.VMEM((B_HEADS, qw), F32),
                        pltpu.VMEM((B_HEADS * vrows, qw), F32)],
        compiler_params=_params(("arbitrary", "arbitrary")),
        name="dsa",
    )(q, iq, ikw, k_att, vt_att, ik_att, near, far)


def _xq_tail(x1, gx_ref, wxq_ref, qnx_ref, cs, qx_ref):
    xn = _rms(x1, gx_ref[...]).astype(BF16)
    qx = _head_rms(_dot(xn, wxq_ref[...]), X_HEAD_DIM, qnx_ref[...])
    _tm_to_seq(qx * (X_HEAD_DIM ** -0.5), cs, [(qx_ref, D_MODEL)])


def _even_out_kernel(x_ref, ya_ref, yb_ref, wo_ref, gx_ref, wxq_ref, qnx_ref, x1_ref, qx_ref, ys, cs):
    yb = _seq_to_tm(yb_ref, ys).astype(BF16)
    x1 = x_ref[...] + _dot(ya_ref[...], wo_ref[0:A_WIDTH, :]) + _dot(yb, wo_ref[A_WIDTH:, :])
    x1_ref[...] = x1
    _xq_tail(x1, gx_ref, wxq_ref, qnx_ref, cs, qx_ref)


def _even_out(x, ya, yb, w_out, gx, w_xq, qnx, tt, layer):
    nb, t_len, nyb = yb.shape
    tm = tt * nb
    rows = x.shape[0]
    d = D_MODEL

    def row(c):
        return pl.BlockSpec((tm, c), lambda i: (i, 0))

    def seq(c):
        return pl.BlockSpec((nb, tt, c), lambda i: (0, i, 0))

    return pl.pallas_call(
        _even_out_kernel,
        grid=(rows // tm,),
        in_specs=[row(d), row(A_WIDTH), seq(nyb), _const_spec((d, d)),
                  _const_spec((1, d)), _const_spec((d, d), layer), _const_spec((1, d))],
        out_specs=[row(d), seq(d)],
        out_shape=[jax.ShapeDtypeStruct((rows, d), F32), jax.ShapeDtypeStruct((nb, t_len, d), BF16)],
        scratch_shapes=[pltpu.VMEM((nyb // LANES, tm, LANES), F32), pltpu.VMEM((d // LANES, tm, LANES), F32)],
        compiler_params=_params(("arbitrary",)),
        name="even_out",
    )(x, ya, yb, w_out, gx, w_xq, qnx)


def _xattn_kernel(q_ref, mk_ref, mv_ref, o_ref):
    for h in range(X_HEADS):
        sl = slice(h * X_HEAD_DIM, (h + 1) * X_HEAD_DIM)
        s = _dot_t(q_ref[:, sl], mk_ref[0, :, sl])
        p = jnp.exp(s - jnp.max(s, axis=1, keepdims=True))
        o = _dot(p.astype(BF16), mv_ref[0, :, sl]) / jnp.sum(p, axis=1, keepdims=True)
        o_ref[:, sl] = o.astype(BF16)


def _xattn(qx, mk, mv, tq):
    nb, t_len, d = qx.shape
    m = mk.shape[1]
    return pl.pallas_call(
        _xattn_kernel,
        grid=(nb, t_len // tq),
        in_specs=[pl.BlockSpec((None, tq, d), lambda b, i: (b, i, 0)),
                  pl.BlockSpec((1, m, d), lambda b, i: (b, 0, 0)),
                  pl.BlockSpec((1, m, d), lambda b, i: (b, 0, 0))],
        out_specs=pl.BlockSpec((None, tq, d), lambda b, i: (b, i, 0)),
        out_shape=jax.ShapeDtypeStruct((nb, t_len, d), BF16),
        compiler_params=_params(("arbitrary", "arbitrary")),
        name="mem_attn",
    )(qx, mk, mv)


def _ffn_kernel(x_ref, o_ref, wxo_ref, g_ref, wup_ref, cw_ref, cb_ref, wdn_ref, hist_ref,
                y_ref, fh_ref, gbuf, cs, *, seq_out):
    tm = x_ref.shape[0]
    nh = (F_CONV - 1) * SUBLANES

    @pl.when(pl.program_id(0) == 0)
    def _():
        gbuf[0:nh, :] = hist_ref[...]

    x2 = x_ref[...] + _dot(_seq_to_tm(o_ref, cs).astype(BF16), wxo_ref[...])
    xn = _rms(x2, g_ref[...]).astype(BF16)
    val = _dot(xn, wup_ref[:, :D_FF])
    gate = _dot(xn, wup_ref[:, D_FF:])
    gbuf[nh:nh + tm, :] = gate
    conv = _conv_taps(gbuf, gate, cw_ref, F_CONV, tm) + cb_ref[...]
    act = (jax.nn.gelu(conv) * val).astype(BF16)
    y = x2 + _dot(act, wdn_ref[...])
    tail = gbuf[tm:tm + nh, :]
    gbuf[0:nh, :] = tail
    fh_ref[...] = tail
    if seq_out:
        _tm_to_seq(y, cs, [(y_ref, D_MODEL)])
    else:
        y_ref[...] = y


def _ffn(x, o, w_xo, g, w_up, conv_w, conv_b, w_down, hist, tt, layer, seq_out):
    nb, t_len, d = o.shape
    tm = tt * nb
    rows = x.shape[0]
    nh = (F_CONV - 1) * SUBLANES

    def row(c):
        return pl.BlockSpec((tm, c), lambda i: (i, 0))

    def seq(c):
        return pl.BlockSpec((nb, tt, c), lambda i: (0, i, 0))

    y_shape = jax.ShapeDtypeStruct((nb, t_len, d) if seq_out else (rows, d), F32)
    return pl.pallas_call(
        functools.partial(_ffn_kernel, seq_out=seq_out),
        grid=(rows // tm,),
        in_specs=[row(d), seq(d), _const_spec((d, d), layer), _const_spec((1, d)),
                  _const_spec((d, 2 * D_FF), layer), _const_spec((F_CONV, D_FF)), _const_spec((1, D_FF)),
                  _const_spec((D_FF, d), layer), _const_spec((nh, D_FF))],
        out_specs=[seq(d) if seq_out else row(d), pl.BlockSpec((nh, D_FF), lambda i: (0, 0))],
        out_shape=[y_shape, jax.ShapeDtypeStruct((nh, D_FF), F32)],
        scratch_shapes=[pltpu.VMEM((tm + nh, D_FF), F32), pltpu.VMEM((d // LANES, tm, LANES), F32)],
        compiler_params=_params(("arbitrary",)),
        name="ffn",
    )(x, o, w_xo, g, w_up, conv_w, conv_b, w_down, hist)


def _odd_kernel(x_ref, g_ref, win_ref, cw_ref, cb_ref, wai_ref, ba_ref, bi_ref, lam_ref,
                wo_ref, hist_ref, h0_ref, gx_ref, wxq_ref, qnx_ref,
                x1_ref, qx_ref, ch_ref, hl_ref, xbuf, a_s, b_s, h_s, cs, *, stream_start):
    tm = x_ref.shape[0]
    nh = (C_CONV - 1) * SUBLANES
    first = pl.program_id(0) == 0

    @pl.when(first)
    def _():
        xbuf[0:nh, :] = hist_ref[...]
        h_s[...] = h0_ref[...]

    x = x_ref[...]
    xn = _rms(x, g_ref[...]).astype(BF16)
    xr_in = _dot(xn, win_ref[:, RNN_WIDTH:])
    xbuf[nh:nh + tm, :] = xr_in
    xr = _conv_taps(xbuf, xr_in, cw_ref, C_CONV, tm) + cb_ref[...]
    tail = xbuf[tm:tm + nh, :]
    xbuf[0:nh, :] = tail
    ch_ref[...] = tail

    xrb = xr.astype(BF16)
    lam = -lam_ref[...]
    sp = jnp.maximum(lam, 0.0) + jnp.log1p(jnp.exp(-jnp.abs(lam)))
    gates = [_dot(xrb[:, n * RNN_BLOCK:(n + 1) * RNN_BLOCK], wai_ref[n]) for n in range(RNN_BLOCKS)]
    r = jax.nn.sigmoid(jnp.concatenate([g[:, :RNN_BLOCK] for g in gates], axis=-1) + ba_ref[...])
    ig = jax.nn.sigmoid(jnp.concatenate([g[:, RNN_BLOCK:] for g in gates], axis=-1) + bi_ref[...])
    log_a = -RG_C * r * sp
    a = jnp.exp(log_a)
    m2 = jnp.tanh(-log_a) * (1.0 + a * a)
    mult = jnp.where(m2 > 0.0, m2 * lax.rsqrt(m2), 0.0)
    if stream_start:
        rows = lax.broadcasted_iota(I32, (tm, RNN_WIDTH), 0)
        mult = jnp.where(first & (rows < SUBLANES), 1.0, mult)
    a_s[...] = a
    b_s[...] = mult * ig * xr

    def step(t, h):
        r0 = pl.multiple_of(t * SUBLANES, SUBLANES)
        h = a_s[pl.ds(r0, SUBLANES), :] * h + b_s[pl.ds(r0, SUBLANES), :]
        b_s[pl.ds(r0, SUBLANES), :] = h
        return h

    h = lax.fori_loop(0, tm // SUBLANES, step, h_s[...], unroll=True)
    h_s[...] = h
    hl_ref[...] = h

    gate = _dot(xn, win_ref[:, :RNN_WIDTH])
    act = (jax.nn.gelu(gate) * b_s[...]).astype(BF16)
    x1 = x + _dot(act, wo_ref[...])
    x1_ref[...] = x1
    _xq_tail(x1, gx_ref, wxq_ref, qnx_ref, cs, qx_ref)


def _odd(x, g, w_in, conv_w, conv_b, w_ai, b_a, b_i, lam, w_out, hist, h0, gx, w_xq, qnx, tt, layer,
         stream_start):
    nb = SUBLANES
    tm = tt * nb
    rows = x.shape[0]
    t_len = rows // nb
    d = D_MODEL
    r = RNN_WIDTH
    nh = (C_CONV - 1) * SUBLANES

    def row(c):
        return pl.BlockSpec((tm, c), lambda i: (i, 0))

    blk = (RNN_BLOCKS, RNN_BLOCK, 2 * RNN_BLOCK)
    return pl.pallas_call(
        functools.partial(_odd_kernel, stream_start=stream_start),
        grid=(rows // tm,),
        in_specs=[row(d), _const_spec((1, d)), _const_spec((d, 2 * r)), _const_spec((C_CONV, r)),
                  _const_spec((1, r)), _const_spec(blk), _const_spec((1, r)),
                  _const_spec((1, r)), _const_spec((1, r)), _const_spec((r, d)), _const_spec((nh, r)),
                  _const_spec((SUBLANES, r)), _const_spec((1, d)), _const_spec((d, d), layer), _const_spec((1, d))],
        out_specs=[row(d), pl.BlockSpec((nb, tt, d), lambda i: (0, i, 0)), pl.BlockSpec((nh, r), lambda i: (0, 0)),
                   pl.BlockSpec((SUBLANES, r), lambda i: (0, 0))],
        out_shape=[jax.ShapeDtypeStruct((rows, d), F32), jax.ShapeDtypeStruct((nb, t_len, d), BF16),
                   jax.ShapeDtypeStruct((nh, r), F32), jax.ShapeDtypeStruct((SUBLANES, r), F32)],
        scratch_shapes=[pltpu.VMEM((tm + nh, r), F32), pltpu.VMEM((tm, r), F32), pltpu.VMEM((tm, r), F32),
                        pltpu.VMEM((SUBLANES, r), F32), pltpu.VMEM((d // LANES, tm, LANES), F32)],
        compiler_params=_params(("arbitrary",)),
        name="odd_mixer",
    )(x, g, w_in, conv_w, conv_b, w_ai, b_a, b_i, lam, w_out, hist, h0, gx, w_xq, qnx)


def _to_tm(a):
    return jnp.transpose(a, (1, 0, 2)).reshape(a.shape[1] * a.shape[0], a.shape[2])


def _from_tm(a, w):
    return jnp.transpose(a.reshape(w, SUBLANES, a.shape[1]), (1, 0, 2))


def _tile_plan(t_len):
    return {"proj": min(128, t_len),
            "ffn": min(64, t_len),
            "dsa": min(KEY_BLOCK, t_len),
            "xattn": t_len}


def _trunk(x, st, mem_k, mem_v, p, bias):
    nb, t_len, d = x.shape
    assert nb == SUBLANES
    tiles = _tile_plan(t_len)
    past = 0 if st is None else st["b_k"].shape[2]
    length = past + t_len
    ntop = min(TOPK_MAX, length // 4)
    near, far = bias
    xt = None
    out = {}

    def hist(name, l, width, c):
        if st is None:
            return jnp.zeros(((width - 1) * nb, c), F32)
        return _to_tm(st[name][l])

    for l in range(DEPTH):
        if l % 2 == 0:
            e = l // 2
            assert l == 0, "the per-sequence input is converted by the first layer's kernel"
            xt, ya, q, iq, ikw, k, v, k_att, ik_att, v_att, uh = _even_in(
                x, p["g_mix"][l], p["w_in_even"][e], p["a_conv_w"][e], p["b_q_norm"][e], p["b_k_norm"][e],
                hist("a_conv", e, A_CONV, A_WIDTH), tiles["proj"])
            k_new = k.reshape(nb, t_len, B_KV_HEADS, B_HEAD_DIM)
            v_new = v.reshape(nb, t_len, B_KV_HEADS, B_HEAD_DIM)
            ik_new = ikw[:, :, :IDX_DIM]
            if st is not None:
                k_past = st["b_k"][e].reshape(nb, past, B_KV_HEADS * B_HEAD_DIM).astype(BF16)
                ik_past = jnp.pad(st["b_kidx"][e].astype(BF16), ((0, 0), (0, 0), (0, LANES - IDX_DIM)))
                k_att = jnp.concatenate([k_past, k_att], axis=1)
                ik_att = jnp.concatenate([ik_past, ik_att], axis=1)
                v_past = st["b_v"][e].reshape(nb, past, B_KV_HEADS * B_HEAD_DIM).astype(BF16)
                v_att = jnp.concatenate([v_past, v_att], axis=1)
            lp = -(-length // KEY_BLOCK) * KEY_BLOCK
            padl = lp - length
            k_att = jnp.pad(k_att, ((0, 0), (0, padl), (0, 0)))
            ik_att = jnp.pad(ik_att, ((0, 0), (0, padl), (0, 0)))
            v_att = jnp.pad(v_att, ((0, 0), (0, padl), (0, 0)))
            vt_att = jnp.transpose(v_att.reshape(nb, lp // KEY_BLOCK, KEY_BLOCK, B_KV_HEADS, B_HEAD_DIM),
                                   (0, 3, 1, 4, 2))
            yb = _dsa(q, iq, ikw, k_att, vt_att, ik_att, near, far, tiles["dsa"], past, length, ntop)
            x1, qx = _even_out(xt, ya, yb, p["w_out_even"][e], p["g_x"][l], p["w_xq"], p["x_q_norm"][l],
                               tiles["proj"], l)
            out.setdefault("a_conv", []).append(_from_tm(uh, A_CONV - 1))
            out.setdefault("b_k", []).append(k_new)
            out.setdefault("b_v", []).append(v_new)
            out.setdefault("b_kidx", []).append(ik_new)
        else:
            o = l // 2
            h0 = jnp.zeros((nb, RNN_WIDTH), F32) if st is None else st["c_h"][o]
            x1, qx, ch, hl = _odd(
                xt, p["g_mix"][l], p["w_in_odd"][o], p["c_conv_w"][o], p["c_conv_b"][o], p["c_w_ai"][o],
                p["c_b_a"][o], p["c_b_i"][o], p["c_lambda"][o], p["w_out_odd"][o],
                hist("c_conv", o, C_CONV, RNN_WIDTH), h0, p["g_x"][l], p["w_xq"], p["x_q_norm"][l],
                tiles["proj"], l, stream_start=(past == 0))
            out.setdefault("c_conv", []).append(_from_tm(ch, C_CONV - 1))
            out.setdefault("c_h", []).append(hl)
        xo = _xattn(qx, mem_k[l], mem_v[l], tiles["xattn"])
        xt, fh = _ffn(x1, xo, p["w_xo"], p["g_ffn"][l], p["w_up"], p["f_conv_w"][l], p["f_conv_b"][l],
                      p["w_down"], hist("f_conv", l, F_CONV, D_FF), tiles["ffn"], l, seq_out=(l == DEPTH - 1))
        out.setdefault("f_conv", []).append(_from_tm(fh, F_CONV - 1))
    return xt, {name: jnp.stack(v) for name, v in out.items()}


def kernel(x_prompt, x_sample, cache_b_k, cache_b_v, cache_b_kidx, state_a_conv, state_c_conv, state_c_h, state_ffn_conv, cache_mem_k, cache_mem_v, mem_prompt, rel_table, g_mix, w_in_even, a_conv_w, b_q_norm, b_k_norm, w_out_even, w_in_odd, c_conv_w, c_conv_b, c_w_a, c_b_a, c_w_i, c_b_i, c_lambda, w_out_odd, g_mem, g_x, w_xq, w_xk, w_xv, x_q_norm, x_k_norm, w_xo, g_ffn, w_up, f_conv_w, f_conv_b, w_down):
    d = D_MODEL
    bp, t_p, _ = x_prompt.shape
    m = mem_prompt.shape[1]

    def rowvec(a):
        return a.reshape(a.shape[0], 1, a.shape[-1])

    def mxu(a):
        return [a[l].astype(BF16) for l in range(a.shape[0])]

    p = {
        "g_mix": rowvec(g_mix), "g_x": rowvec(g_x), "g_ffn": rowvec(g_ffn),
        "w_in_even": [jnp.pad(w, ((0, 0), (0, EVEN_IN_PAD - EVEN_IN))) for w in mxu(w_in_even)],
        "a_conv_w": a_conv_w,
        "b_q_norm": rowvec(jnp.tile(b_q_norm, (1, B_HEADS))),
        "b_k_norm": rowvec(jnp.tile(b_k_norm, (1, B_KV_HEADS))),
        "w_out_even": mxu(w_out_even),
        "w_in_odd": mxu(w_in_odd), "c_conv_w": c_conv_w, "c_conv_b": rowvec(c_conv_b),
        "c_w_ai": mxu(jnp.concatenate([c_w_a, c_w_i], axis=-1)), "c_b_a": rowvec(c_b_a), "c_b_i": rowvec(c_b_i),
        "c_lambda": rowvec(c_lambda), "w_out_odd": mxu(w_out_odd),
        "w_xq": w_xq.astype(BF16), "x_q_norm": rowvec(jnp.tile(x_q_norm, (1, X_HEADS))),
        "w_xo": w_xo.astype(BF16), "w_up": w_up.astype(BF16), "f_conv_w": f_conv_w,
        "f_conv_b": rowvec(f_conv_b), "w_down": w_down.astype(BF16),
    }
    bias = _bias_tiles(rel_table, KEY_BLOCK)

    mk, mv = _mem_kv(mem_prompt.reshape(bp * m, d), g_mem, w_xk, x_k_norm, w_xv)
    p_mem_k = mk.reshape(DEPTH, bp, m, X_HEADS, X_HEAD_DIM)
    p_mem_v = mv.reshape(DEPTH, bp, m, X_HEADS, X_HEAD_DIM)
    y_prompt, new_p = _trunk(x_prompt, None, mk.reshape(DEPTH, bp, m, d).astype(BF16),
                             mv.reshape(DEPTH, bp, m, d).astype(BF16), p, bias)

    bs, t_s, _ = x_sample.shape
    st_s = {"b_k": cache_b_k, "b_v": cache_b_v, "b_kidx": cache_b_kidx, "a_conv": state_a_conv,
            "c_conv": state_c_conv, "c_h": state_c_h, "f_conv": state_ffn_conv}
    ms = cache_mem_k.shape[2]
    y_sample, new_s = _trunk(x_sample, st_s, cache_mem_k.reshape(DEPTH, bs, ms, d).astype(BF16),
                             cache_mem_v.reshape(DEPTH, bs, ms, d).astype(BF16), p, bias)
    return (y_prompt, y_sample,
            new_p["b_k"], new_p["b_v"], new_p["b_kidx"], new_p["a_conv"], new_p["c_conv"],
            new_p["c_h"], new_p["f_conv"], p_mem_k, p_mem_v,
            new_s["b_k"], new_s["b_v"], new_s["b_kidx"], new_s["a_conv"], new_s["c_conv"],
            new_s["c_h"], new_s["f_conv"])
```

```python
import functools
import math

import jax
import jax.numpy as jnp
from jax import lax
from jax.experimental import pallas as pl
from jax.experimental.pallas import tpu as pltpu

F32 = jnp.float32
BF16 = jnp.bfloat16
I32 = jnp.int32
I16 = jnp.int16

D_MODEL = 1024
DEPTH = 2
CHUNK = 64
CHUNK_SHIFT = CHUNK.bit_length() - 1
EPS = 1e-6
NEG_INF = -1e30
LOG2E = math.log2(math.e)
A_WIDTH = 512
A_CONV = 3
B_HEADS = 8
B_KV_HEADS = 2
B_HEAD_DIM = 64
IDX_HEADS = 8
IDX_DIM = 32
TOPK_MAX = 256
REL_BUCKETS = 32
REL_MAX_DIST = 128
RNN_WIDTH = 1024
RNN_BLOCKS = 8
RNN_BLOCK = 128
C_CONV = 4
RG_C = 8.0
X_HEADS = 4
X_HEAD_DIM = 256
D_FF = 2816
F_CONV = 3
EVEN_IN = 2600

SUBLANES = 8
LANES = 128
PACKED_ROWS = 16
VMEM_LIMIT = 56 * 1024 * 1024

EVEN_IN_PAD = 2688
KEY_BLOCK = 256
INT_MIN = -2147483648


def _params(sem, vmem=VMEM_LIMIT):
    return pltpu.CompilerParams(dimension_semantics=sem, vmem_limit_bytes=vmem)


def _const_spec(shape, layer=None):
    nd = len(shape)
    if layer is None:
        return pl.BlockSpec(shape, lambda *_: (0,) * nd, pipeline_mode=pl.Buffered(1))
    return pl.BlockSpec((None,) + tuple(shape), lambda *_: (layer,) + (0,) * nd, pipeline_mode=pl.Buffered(1))


def _rms(x, g):
    ms = jnp.mean(x * x, axis=-1, keepdims=True)
    return x * lax.rsqrt(ms + EPS) * g


def _head_rms(x, hd, gain):
    m, c = x.shape
    s = x * x
    parts = []
    if hd >= LANES:
        for h in range(c // hd):
            ms = jnp.mean(s[:, h * hd:(h + 1) * hd], axis=-1, keepdims=True)
            parts.append(x[:, h * hd:(h + 1) * hd] * lax.rsqrt(ms + EPS))
    else:
        lane = lax.broadcasted_iota(I32, (m, LANES), 1)
        for j in range(c // LANES):
            sj = s[:, j * LANES:(j + 1) * LANES]
            inv = jnp.zeros((m, LANES), F32)
            for k in range(LANES // hd):
                msk = (lane >= k * hd) & (lane < (k + 1) * hd)
                ms = jnp.sum(jnp.where(msk, sj, 0.0), axis=-1, keepdims=True) * (1.0 / hd)
                inv = jnp.where(msk, lax.rsqrt(ms + EPS), inv)
            parts.append(x[:, j * LANES:(j + 1) * LANES] * inv)
    y = parts[0] if len(parts) == 1 else jnp.concatenate(parts, axis=-1)
    return y * gain


def _dot(a, b):
    return jnp.dot(a, b, preferred_element_type=F32)


def _dot_t(a, b):
    return lax.dot_general(a, b, (((1,), (1,)), ((), ())), preferred_element_type=F32)


def _conv_taps(buf, cur, w_ref, width, tm):
    y = cur * w_ref[width - 1:width, :]
    for i in range(width - 1):
        y = y + buf[i * SUBLANES:i * SUBLANES + tm, :] * w_ref[i:i + 1, :]
    return y


def _seq_to_tm(src_ref, scr):
    nb, tt, c = src_ref.shape
    for b in range(nb):
        for j in range(c // LANES):
            scr[j, pl.ds(b, tt, stride=nb), :] = src_ref[b, :, j * LANES:(j + 1) * LANES].astype(F32)
    return jnp.concatenate([scr[j] for j in range(c // LANES)], axis=-1)


def _tm_to_seq(val, scr, dst_refs):
    tm, c = val.shape
    tt = tm // SUBLANES
    for j in range(c // LANES):
        scr[j] = val[:, j * LANES:(j + 1) * LANES]
    for b in range(SUBLANES):
        j0 = 0
        for ref, ci in dst_refs:
            nj = ci // LANES
            parts = [scr[j0 + j, pl.ds(b, tt, stride=SUBLANES), :] for j in range(nj)]
            ref[b] = (parts[0] if nj == 1 else jnp.concatenate(parts, axis=-1)).astype(ref.dtype)
            j0 += nj


def _memkv_kernel(mem_ref, g_ref, wk_ref, kn_ref, wv_ref, k_ref, v_ref):
    hm = _rms(mem_ref[...], g_ref[0]).astype(BF16)
    k_ref[0] = _head_rms(_dot(hm, wk_ref[0]), X_HEAD_DIM, kn_ref[0])
    v_ref[0] = _dot(hm, wv_ref[0])


def _mem_kv(mem, g_mem, w_xk, x_k_norm, w_xv):
    rows = mem.shape[0]
    tm = min(512, rows)
    d = D_MODEL
    kn = jnp.tile(x_k_norm, (1, X_HEADS)).reshape(DEPTH, 1, d)
    out = jax.ShapeDtypeStruct((DEPTH, rows, d), F32)
    return pl.pallas_call(
        _memkv_kernel,
        grid=(DEPTH, rows // tm),
        in_specs=[
            pl.BlockSpec((tm, d), lambda l, i: (i, 0)),
            pl.BlockSpec((1, 1, d), lambda l, i: (l, 0, 0)),
            pl.BlockSpec((1, d, d), lambda l, i: (l, 0, 0)),
            pl.BlockSpec((1, 1, d), lambda l, i: (l, 0, 0)),
            pl.BlockSpec((1, d, d), lambda l, i: (l, 0, 0)),
        ],
        out_specs=[pl.BlockSpec((1, tm, d), lambda l, i: (l, i, 0))] * 2,
        out_shape=[out, out],
        compiler_params=_params(("arbitrary", "arbitrary")),
        name="mem_kv",
    )(mem, g_mem.reshape(DEPTH, 1, d), w_xk.astype(BF16), kn, w_xv.astype(BF16))


def _even_in_kernel(x_ref, g_ref, w_ref, cw_ref, qn_ref, kn_ref, hist_ref,
                    xt_ref, ya_ref, q_ref, iq_ref, ikw_ref, k_ref, v_ref, kb_ref, ikb_ref, uh_ref, ubuf, xs, cs):
    tm = xt_ref.shape[0]
    nh = (A_CONV - 1) * SUBLANES

    @pl.when(pl.program_id(0) == 0)
    def _():
        ubuf[0:nh, :] = hist_ref[...]

    x = _seq_to_tm(x_ref, xs)
    xt_ref[...] = x
    xn = _rms(x, g_ref[...]).astype(BF16)

    def proj(a, b):
        return _dot(xn, w_ref[:, a:b])

    zc = proj(A_WIDTH, 3 * A_WIDTH)
    u = zc[:, :A_WIDTH] * zc[:, A_WIDTH:]
    ubuf[nh:nh + tm, :] = u
    conv = _conv_taps(ubuf, u, cw_ref, A_CONV, tm)
    ya_ref[...] = (proj(0, A_WIDTH) * conv).astype(BF16)
    tail = ubuf[tm:tm + nh, :]
    ubuf[0:nh, :] = tail
    uh_ref[...] = tail

    o = 3 * A_WIDTH
    nq = B_HEADS * B_HEAD_DIM
    nkv = B_KV_HEADS * B_HEAD_DIM
    ni = IDX_HEADS * IDX_DIM
    q = _head_rms(proj(o, o + nq), B_HEAD_DIM, qn_ref[...]) * (B_HEAD_DIM ** -0.5 * LOG2E)
    zkv = proj(o + nq, o + nq + 2 * nkv)
    k = _head_rms(zkv[:, :nkv], B_HEAD_DIM, kn_ref[...])
    zi = proj(o + nq + 2 * nkv, EVEN_IN_PAD)
    seq = jnp.concatenate([q, zi, k, zkv[:, nkv:], k, zi[:, ni:]], axis=-1)
    _tm_to_seq(seq, cs, [(q_ref, nq), (iq_ref, ni), (ikw_ref, LANES), (k_ref, nkv), (v_ref, nkv),
                         (kb_ref, nkv), (ikb_ref, LANES)])


def _even_in(x, g, w_pad, conv_w, qn, kn, hist, tt):
    nb, t_len, d = x.shape
    tm = tt * nb
    rows = t_len * nb
    nh = (A_CONV - 1) * SUBLANES
    nq = B_HEADS * B_HEAD_DIM
    nkv = B_KV_HEADS * B_HEAD_DIM
    ni = IDX_HEADS * IDX_DIM
    nseq = nq + ni + 2 * LANES + 3 * nkv

    def row(c):
        return pl.BlockSpec((tm, c), lambda i: (i, 0))

    def seq(c):
        return pl.BlockSpec((nb, tt, c), lambda i: (0, i, 0))

    def seq_shape(c, dt):
        return jax.ShapeDtypeStruct((nb, t_len, c), dt)

    return pl.pallas_call(
        _even_in_kernel,
        grid=(t_len // tt,),
        in_specs=[seq(d), _const_spec((1, d)), _const_spec((d, EVEN_IN_PAD)), _const_spec((A_CONV, A_WIDTH)),
                  _const_spec((1, nq)), _const_spec((1, nkv)), _const_spec((nh, A_WIDTH))],
        out_specs=[row(d), row(A_WIDTH), seq(nq), seq(ni), seq(LANES), seq(nkv), seq(nkv), seq(nkv), seq(LANES),
                   pl.BlockSpec((nh, A_WIDTH), lambda i: (0, 0))],
        out_shape=[jax.ShapeDtypeStruct((rows, d), F32), jax.ShapeDtypeStruct((rows, A_WIDTH), BF16),
                   seq_shape(nq, BF16), seq_shape(ni, BF16), seq_shape(LANES, F32), seq_shape(nkv, F32),
                   seq_shape(nkv, F32), seq_shape(nkv, BF16), seq_shape(LANES, BF16),
                   jax.ShapeDtypeStruct((nh, A_WIDTH), F32)],
        scratch_shapes=[pltpu.VMEM((tm + nh, A_WIDTH), F32), pltpu.VMEM((d // LANES, tm, LANES), F32),
                        pltpu.VMEM((nseq // LANES, tm, LANES), F32)],
        compiler_params=_params(("arbitrary",)),
        name="even_in",
    )(x, g, w_pad, conv_w, qn, kn, hist)


def _rel_bucket(rel):
    half = REL_BUCKETS // 2
    max_exact = half // 2
    n = -rel
    ret = jnp.where(n < 0, half, 0)
    n = jnp.abs(n)
    nf = jnp.maximum(n, 1).astype(F32)
    large = max_exact + (jnp.log(nf / max_exact) / math.log(REL_MAX_DIST / max_exact)
                         * (half - max_exact)).astype(I32)
    large = jnp.minimum(large, half - 1)
    return ret + jnp.where(n < max_exact, n, large)


def _bias_kernel(tab_ref, near_ref, far_ref):
    tk = near_ref.shape[-1]
    r = lax.broadcasted_iota(I32, (tk, tk), 0)
    c = lax.broadcasted_iota(I32, (tk, tk), 1)

    def lookup(bucket, h):
        def body(j, acc):
            return jnp.where(bucket == j, tab_ref[j, h], acc)
        return lax.fori_loop(0, REL_BUCKETS, body, jnp.zeros(bucket.shape, F32))

    for blk in range(2):
        bucket = _rel_bucket(r - c - blk * tk)
        for h in range(B_HEADS):
            near_ref[blk, h] = lookup(bucket, h) * LOG2E
    bucket = _rel_bucket(-REL_MAX_DIST - c[0:SUBLANES, :])
    for h in range(B_HEADS):
        far_ref[h] = lookup(bucket, h) * LOG2E


def _bias_tiles(rel_table, tk):
    return pl.pallas_call(
        _bias_kernel,
        in_specs=[pl.BlockSpec(memory_space=pltpu.SMEM)],
        out_shape=[jax.ShapeDtypeStruct((2, B_HEADS, tk, tk), F32),
                   jax.ShapeDtypeStruct((B_HEADS, SUBLANES, tk), F32)],
        name="rel_bias",
    )(rel_table)


def _dsa_kernel(q_ref, iq_ref, ikw_ref, k_ref, v_ref, ik_ref, near_ref, far_ref, o_ref,
                skey, skh, skl, madd_s, s_s, p_s, qh_s, iqh_s, pad_s, w_s, m_s, acc_s,
                *, tq, qw, tk, past, length, ntop, idx_bits):
    i = pl.program_id(1)
    q0 = past + i * tq
    nkb = (q0 + tq + tk - 1) // tk
    last = nkb - 1
    ni = IDX_HEADS * IDX_DIM
    groups = B_HEADS // B_KV_HEADS
    lane = lax.broadcasted_iota(I32, (1, qw), 1)
    sub8 = lax.broadcasted_iota(I32, (SUBLANES, qw), 0)
    qchunk = jnp.right_shift(q0 + lane, CHUNK_SHIFT)

    if tq != qw:
        qh_s[...] = jnp.zeros(qh_s.shape, BF16)
        iqh_s[...] = jnp.zeros(iqh_s.shape, BF16)
        pad_s[...] = jnp.zeros(pad_s.shape, F32)
    for h in range(B_HEADS):
        qh_s[h, 0:tq, :] = q_ref[:, B_HEAD_DIM * h:B_HEAD_DIM * (h + 1)]
    for h in range(IDX_HEADS):
        iqh_s[h, 0:tq, :] = iq_ref[:, IDX_DIM * h:IDX_DIM * (h + 1)]
    pad_s[0:tq, :] = ikw_ref[...]
    w_s[...] = pad_s[...].T[IDX_DIM:IDX_DIM + IDX_HEADS, :] * (ni ** -0.5)

    slab = tk

    def scores(kb, masked):
        for sl in range(tk // slab):
            ks = pl.multiple_of(kb * tk + sl * slab, slab)
            ikb = ik_ref[0, pl.ds(ks, slab), 0:IDX_DIM]
            acc = jnp.zeros((slab, qw), F32)
            for h in range(IDX_HEADS):
                acc = acc + w_s[h:h + 1, :] * jnp.maximum(_dot_t(ikb, iqh_s[h]), 0.0)
            if masked:
                kpos = kb * tk + sl * slab + lax.broadcasted_iota(I32, (slab, qw), 0)
                vis = (jnp.right_shift(kpos, CHUNK_SHIFT) <= qchunk) & (kpos < length)
                acc = jnp.where(vis, acc, -jnp.inf)
            bits = lax.bitcast_convert_type(acc, I32)
            key = jnp.where(bits < 0, bits ^ jnp.int32(0x7FFFFFFF), bits)
            rows = slice(sl * slab, (sl + 1) * slab)
            skey[kb, rows, :] = key
            skh[kb, rows, :] = jnp.right_shift(key, 16).astype(I16)
            skl[kb, rows, :] = ((key & 0xFFFF) - 32768).astype(I16)

    def score_quad(j, carry):
        for u in range(4):
            scores(4 * j + u, False)
        return carry

    lax.fori_loop(0, last // 4, score_quad, 0)
    rem = last % 4

    @pl.when(rem >= 2)
    def _():
        scores(last - rem, False)
        scores(last - rem + 1, False)

    @pl.when(rem % 2 == 1)
    def _():
        scores(last - 1, False)

    scores(last, True)

    n_acc = 4
    prow = PACKED_ROWS

    @pl.when(nkb % 2 == 1)
    def _():
        skh[nkb] = jnp.full((tk, qw), -32768, I16)
        skl[nkb] = jnp.full((tk, qw), -32768, I16)

    def count16(ref, pred):
        def body(j, accs):
            accs = list(accs)
            for kb in (2 * j, 2 * j + 1):
                for g in range(tk // prow):
                    blk = ref[kb, g * prow:(g + 1) * prow, :]
                    accs[g % n_acc] = accs[g % n_acc] + jnp.where(pred(blk), jnp.int16(1), jnp.int16(0))
            return tuple(accs)
        accs = lax.fori_loop(0, (nkb + 1) // 2, body, tuple(jnp.zeros((prow, qw), I16) for _ in range(n_acc)))
        tot = (accs[0].astype(I32) + accs[1].astype(I32)) + (accs[2].astype(I32) + accs[3].astype(I32))
        return jnp.sum(tot, axis=0, keepdims=True)

    def rep16(v):
        return jnp.broadcast_to(v, (prow, qw)).astype(I16)

    def kth16(ref, kth):
        def bit_body(it, prefix):
            cand_u = prefix | jnp.left_shift(jnp.int32(1), 15 - it)
            cand = rep16(cand_u - 32768)
            cnt = count16(ref, lambda blk: blk >= cand)
            return jnp.where(cnt >= kth, cand_u, prefix)
        return lax.fori_loop(0, 16, bit_body, jnp.zeros((1, qw), I32)) - 32768

    def count(pred):
        def body(kb, accs):
            accs = list(accs)
            for g in range(tk // SUBLANES):
                blk = skey[kb, g * SUBLANES:(g + 1) * SUBLANES, :]
                accs[g % n_acc] = accs[g % n_acc] + jnp.where(pred(kb, g, blk), 1, 0)
            return tuple(accs)
        accs = lax.fori_loop(0, nkb, body, tuple(jnp.zeros((SUBLANES, qw), I32) for _ in range(n_acc)))
        tot = (accs[0] + accs[1]) + (accs[2] + accs[3])
        return jnp.sum(tot, axis=0, keepdims=True)

    def rep8(v):
        return jnp.broadcast_to(v, (SUBLANES, qw))

    p_hi = kth16(skh, ntop)
    hi16 = rep16(p_hi)
    above = count16(skh, lambda blk: blk > hi16)

    def low_body(kb, carry):
        for g in range(tk // prow):
            rows = slice(g * prow, (g + 1) * prow)
            skl[kb, rows, :] = jnp.where(skh[kb, rows, :] == hi16, skl[kb, rows, :], jnp.int16(-32768))
        return carry

    lax.fori_loop(0, nkb, low_body, 0)
    p_lo = kth16(skl, ntop - above)
    tau = p_hi * 65536 + (p_lo + 32768)
    tau8 = rep8(tau)

    cnt_ge = count(lambda kb, g, blk: blk >= tau8)
    key_ninf = jnp.int32(0x7FFFFFFF) ^ jnp.int32(-8388608)
    finite = tau != key_ninf
    tie_rows = (cnt_ge > ntop) & finite & (lane < tq)

    @pl.when(jnp.max(jnp.where(tie_rows, 1, 0)) > 0)
    def _():
        need8 = rep8(ntop - count(lambda kb, g, blk: blk > tau8))

        def xbit(it, xlim):
            cand8 = xlim | jnp.left_shift(jnp.int32(1), idx_bits - 1 - it)
            cnt = count(lambda kb, g, blk: (blk == tau8) & ((kb * tk + g * SUBLANES + sub8) < cand8))
            return jnp.where(rep8(cnt) <= need8, cand8, xlim)
        xlim8 = lax.fori_loop(0, idx_bits, xbit, jnp.zeros((SUBLANES, qw), I32))

        def demote(kb, carry):
            for g in range(tk // SUBLANES):
                rows = slice(g * SUBLANES, (g + 1) * SUBLANES)
                key = skey[kb, rows, :]
                late = (key == tau8) & ((kb * tk + g * SUBLANES + sub8) >= xlim8)
                skey[kb, rows, :] = jnp.where(late, key - 1, key)
            return carry

        lax.fori_loop(0, nkb, demote, 0)

    m_s[...] = jnp.full(m_s.shape, NEG_INF, F32)
    acc_s[...] = jnp.zeros(acc_s.shape, F32)
    vrows = B_HEAD_DIM + PACKED_ROWS
    ones_cols = jnp.ones((tk, PACKED_ROWS), BF16)

    tau_ge8 = rep8(jnp.where(finite, tau, tau + 1))

    def attend(kbs, near):
        nblk = len(kbs)
        tiles = [nblk - 1 - i for i in range(nblk)] if near else None
        alphas = {}
        for i, kb in enumerate(kbs):
            for g in range(tk // SUBLANES):
                rows = slice(g * SUBLANES, (g + 1) * SUBLANES)
                madd_s[i, rows, :] = jnp.where(skey[kb, rows, :] >= tau_ge8, 0.0, NEG_INF)

        def logits(h):
            for i, kb in enumerate(kbs):
                n = h // groups
                kn = k_ref[0, pl.ds(pl.multiple_of(kb * tk, tk), tk), n * B_HEAD_DIM:(n + 1) * B_HEAD_DIM]
                bias = madd_s[i] if tiles is None else madd_s[i] + near_ref[tiles[i], h]
                s_s[i, h] = (_dot_t(kn, qh_s[h]) + bias).astype(BF16)

        def softmax(h):
            tiles16 = [s_s[i, h, r * prow:(r + 1) * prow, :] for i in range(nblk) for r in range(tk // prow)]
            mx = tiles16[:n_acc]
            for j, t in enumerate(tiles16[n_acc:]):
                mx[j % n_acc] = jnp.maximum(mx[j % n_acc], t)
            mx = jnp.maximum(jnp.maximum(mx[0], mx[1]), jnp.maximum(mx[2], mx[3]))
            m_cur = jnp.max(mx.astype(F32), axis=0, keepdims=True)
            c_h = jnp.zeros((1, qw), F32) if near else far_ref[h, 0:1, 0:qw]
            m_prev = m_s[h:h + 1, :]
            shift = (jnp.maximum(m_prev, m_cur + c_h) - c_h).astype(BF16)
            m_new = shift.astype(F32) + c_h
            m_s[h:h + 1, :] = m_new
            alphas[h] = jnp.exp2(m_prev - m_new)
            shift = jnp.broadcast_to(shift, (prow, qw))
            for i in range(nblk):
                for r in range(tk // prow):
                    rows = slice(r * prow, (r + 1) * prow)
                    p_s[i, h, rows, :] = jnp.exp2(s_s[i, h, rows, :] - shift)

        def values(h):
            hs = slice(h * vrows, (h + 1) * vrows)
            pv = None
            for i in range(nblk):
                n = h // groups
                vb = v_ref[0, pl.ds(pl.multiple_of(kbs[i] * tk, tk), tk), n * B_HEAD_DIM:(n + 1) * B_HEAD_DIM]
                d = lax.dot_general(jnp.concatenate([vb, ones_cols], axis=1), p_s[i, h], (((0,), (0,)), ((), ())),
                                    preferred_element_type=F32)
                pv = d if pv is None else pv + d
            acc_s[hs, :] = acc_s[hs, :] * alphas[h] + pv

        for phase in (logits, softmax, values):
            for h in range(B_HEADS):
                phase(h)

    nfar = nkb - 2

    def far_pair(j, carry):
        attend([2 * j, 2 * j + 1], False)
        return carry

    lax.fori_loop(0, nfar // 2, far_pair, 0)

    @pl.when((nfar > 0) & (nfar % 2 == 1))
    def _():
        attend([nfar - 1], False)

    @pl.when(nkb >= 2)
    def _():
        attend([nkb - 2, last], True)

    @pl.when(nkb < 2)
    def _():
        attend([last], True)

    outs = []
    for h in range(B_HEADS):
        r0 = h * vrows
        outs.append(acc_s[r0:r0 + B_HEAD_DIM, :] / acc_s[r0 + B_HEAD_DIM:r0 + B_HEAD_DIM + 1, :])
    o = jnp.concatenate(outs, axis=0).T
    o_ref[...] = o[0:tq, :].astype(BF16)


def _dsa(q, iq, ikw, k_att, v_att, ik_att, near, far, tq, past, length, ntop):
    tk = KEY_BLOCK
    nb, t_len, _ = q.shape
    lp = k_att.shape[1]
    vrows = B_HEAD_DIM + PACKED_ROWS
    nq = B_HEADS * B_HEAD_DIM
    ni = IDX_HEADS * IDX_DIM
    qw = max(tq, LANES)
    assert past % tk == 0 and (tq == tk or t_len == tq) and tq <= tk and lp % tk == 0
    idx_bits = lp.bit_length()

    def qblk(c):
        return pl.BlockSpec((None, tq, c), lambda b, i: (b, i, 0))

    kern = functools.partial(_dsa_kernel, tq=tq, qw=qw, tk=tk, past=past, length=length, ntop=ntop,
                             idx_bits=idx_bits)
    return pl.pallas_call(
        kern,
        grid=(nb, t_len // tq),
        in_specs=[qblk(nq), qblk(ni), qblk(LANES),
                  pl.BlockSpec((1, lp, B_KV_HEADS * B_HEAD_DIM), lambda b, i: (b, 0, 0)),
                  pl.BlockSpec((1, lp, B_KV_HEADS * B_HEAD_DIM), lambda b, i: (b, 0, 0)),
                  pl.BlockSpec((1, lp, LANES), lambda b, i: (b, 0, 0)),
                  pl.BlockSpec((2, B_HEADS, tk, qw), lambda b, i: (0, 0, 0, 0)),
                  pl.BlockSpec((B_HEADS, SUBLANES, tk), lambda b, i: (0, 0, 0))],
        out_specs=qblk(nq),
        out_shape=jax.ShapeDtypeStruct((nb, t_len, nq), BF16),
        scratch_shapes=[pltpu.VMEM((lp // tk, tk, qw), I32),
                        pltpu.VMEM((lp // tk + 1, tk, qw), I16),
                        pltpu.VMEM((lp // tk + 1, tk, qw), I16),
                        pltpu.VMEM((2, tk, qw), F32),
                        pltpu.VMEM((2, B_HEADS, tk, qw), BF16),
                        pltpu.VMEM((2, B_HEADS, tk, qw), BF16),
                        pltpu.VMEM((B_HEADS, qw, B_HEAD_DIM), BF16),
                        pltpu.VMEM((IDX_HEADS, qw, IDX_DIM), BF16),
                        pltpu.VMEM((qw, LANES), F32),
                        pltpu.VMEM((IDX_HEADS, qw), F32),
                        pltpu.VMEM((B_HEADS, qw), F32),
                        pltpu.VMEM((B_HEADS * vrows, qw), F32)],
        compiler_params=_params(("arbitrary", "arbitrary")),
        name="dsa",
    )(q, iq, ikw, k_att, v_att, ik_att, near, far)


def _xq_tail(x1, gx_ref, wxq_ref, qnx_ref, cs, qx_ref):
    xn = _rms(x1, gx_ref[...]).astype(BF16)
    qx = _head_rms(_dot(xn, wxq_ref[...]), X_HEAD_DIM, qnx_ref[...])
    _tm_to_seq(qx * (X_HEAD_DIM ** -0.5), cs, [(qx_ref, D_MODEL)])


def _even_out_kernel(x_ref, ya_ref, yb_ref, wo_ref, gx_ref, wxq_ref, qnx_ref, x1_ref, qx_ref, ys, cs):
    yb = _seq_to_tm(yb_ref, ys).astype(BF16)
    x1 = x_ref[...] + _dot(ya_ref[...], wo_ref[0:A_WIDTH, :]) + _dot(yb, wo_ref[A_WIDTH:, :])
    x1_ref[...] = x1
    _xq_tail(x1, gx_ref, wxq_ref, qnx_ref, cs, qx_ref)


def _even_out(x, ya, yb, w_out, gx, w_xq, qnx, tt, layer):
    nb, t_len, nyb = yb.shape
    tm = tt * nb
    rows = x.shape[0]
    d = D_MODEL

    def row(c):
        return pl.BlockSpec((tm, c), lambda i: (i, 0))

    def seq(c):
        return pl.BlockSpec((nb, tt, c), lambda i: (0, i, 0))

    return pl.pallas_call(
        _even_out_kernel,
        grid=(rows // tm,),
        in_specs=[row(d), row(A_WIDTH), seq(nyb), _const_spec((d, d)),
                  _const_spec((1, d)), _const_spec((d, d), layer), _const_spec((1, d))],
        out_specs=[row(d), seq(d)],
        out_shape=[jax.ShapeDtypeStruct((rows, d), F32), jax.ShapeDtypeStruct((nb, t_len, d), BF16)],
        scratch_shapes=[pltpu.VMEM((nyb // LANES, tm, LANES), F32), pltpu.VMEM((d // LANES, tm, LANES), F32)],
        compiler_params=_params(("arbitrary",)),
        name="even_out",
    )(x, ya, yb, w_out, gx, w_xq, qnx)


def _xattn_kernel(q_ref, mk_ref, mv_ref, o_ref):
    for h in range(X_HEADS):
        sl = slice(h * X_HEAD_DIM, (h + 1) * X_HEAD_DIM)
        s = _dot_t(q_ref[:, sl], mk_ref[0, :, sl])
        p = jnp.exp(s - jnp.max(s, axis=1, keepdims=True))
        o = _dot(p.astype(BF16), mv_ref[0, :, sl]) / jnp.sum(p, axis=1, keepdims=True)
        o_ref[:, sl] = o.astype(BF16)


def _xattn(qx, mk, mv, tq):
    nb, t_len, d = qx.shape
    m = mk.shape[1]
    return pl.pallas_call(
        _xattn_kernel,
        grid=(nb, t_len // tq),
        in_specs=[pl.BlockSpec((None, tq, d), lambda b, i: (b, i, 0)),
                  pl.BlockSpec((1, m, d), lambda b, i: (b, 0, 0)),
                  pl.BlockSpec((1, m, d), lambda b, i: (b, 0, 0))],
        out_specs=pl.BlockSpec((None, tq, d), lambda b, i: (b, i, 0)),
        out_shape=jax.ShapeDtypeStruct((nb, t_len, d), BF16),
        compiler_params=_params(("arbitrary", "arbitrary")),
        name="mem_attn",
    )(qx, mk, mv)


def _ffn_kernel(x_ref, o_ref, wxo_ref, g_ref, wup_ref, cw_ref, cb_ref, wdn_ref, hist_ref,
                y_ref, fh_ref, gbuf, cs, *, seq_out):
    tm = x_ref.shape[0]
    nh = (F_CONV - 1) * SUBLANES

    @pl.when(pl.program_id(0) == 0)
    def _():
        gbuf[0:nh, :] = hist_ref[...]

    x2 = x_ref[...] + _dot(_seq_to_tm(o_ref, cs).astype(BF16), wxo_ref[...])
    xn = _rms(x2, g_ref[...]).astype(BF16)
    val = _dot(xn, wup_ref[:, :D_FF])
    gate = _dot(xn, wup_ref[:, D_FF:])
    gbuf[nh:nh + tm, :] = gate
    conv = _conv_taps(gbuf, gate, cw_ref, F_CONV, tm) + cb_ref[...]
    act = (jax.nn.gelu(conv) * val).astype(BF16)
    y = x2 + _dot(act, wdn_ref[...])
    tail = gbuf[tm:tm + nh, :]
    gbuf[0:nh, :] = tail
    fh_ref[...] = tail
    if seq_out:
        _tm_to_seq(y, cs, [(y_ref, D_MODEL)])
    else:
        y_ref[...] = y


def _ffn(x, o, w_xo, g, w_up, conv_w, conv_b, w_down, hist, tt, layer, seq_out):
    nb, t_len, d = o.shape
    tm = tt * nb
    rows = x.shape[0]
    nh = (F_CONV - 1) * SUBLANES

    def row(c):
        return pl.BlockSpec((tm, c), lambda i: (i, 0))

    def seq(c):
        return pl.BlockSpec((nb, tt, c), lambda i: (0, i, 0))

    y_shape = jax.ShapeDtypeStruct((nb, t_len, d) if seq_out else (rows, d), F32)
    return pl.pallas_call(
        functools.partial(_ffn_kernel, seq_out=seq_out),
        grid=(rows // tm,),
        in_specs=[row(d), seq(d), _const_spec((d, d), layer), _const_spec((1, d)),
                  _const_spec((d, 2 * D_FF), layer), _const_spec((F_CONV, D_FF)), _const_spec((1, D_FF)),
                  _const_spec((D_FF, d), layer), _const_spec((nh, D_FF))],
        out_specs=[seq(d) if seq_out else row(d), pl.BlockSpec((nh, D_FF), lambda i: (0, 0))],
        out_shape=[y_shape, jax.ShapeDtypeStruct((nh, D_FF), F32)],
        scratch_shapes=[pltpu.VMEM((tm + nh, D_FF), F32), pltpu.VMEM((d // LANES, tm, LANES), F32)],
        compiler_params=_params(("arbitrary",)),
        name="ffn",
    )(x, o, w_xo, g, w_up, conv_w, conv_b, w_down, hist)


def _odd_kernel(x_ref, g_ref, win_ref, cw_ref, cb_ref, wai_ref, ba_ref, bi_ref, lam_ref,
                wo_ref, hist_ref, h0_ref, gx_ref, wxq_ref, qnx_ref,
                x1_ref, qx_ref, ch_ref, hl_ref, xbuf, a_s, b_s, h_s, cs, *, stream_start):
    tm = x_ref.shape[0]
    nh = (C_CONV - 1) * SUBLANES
    first = pl.program_id(0) == 0

    @pl.when(first)
    def _():
        xbuf[0:nh, :] = hist_ref[...]
        h_s[...] = h0_ref[...]

    x = x_ref[...]
    xn = _rms(x, g_ref[...]).astype(BF16)
    xr_in = _dot(xn, win_ref[:, RNN_WIDTH:])
    xbuf[nh:nh + tm, :] = xr_in
    xr = _conv_taps(xbuf, xr_in, cw_ref, C_CONV, tm) + cb_ref[...]
    tail = xbuf[tm:tm + nh, :]
    xbuf[0:nh, :] = tail
    ch_ref[...] = tail

    xrb = xr.astype(BF16)
    lam = -lam_ref[...]
    sp = jnp.maximum(lam, 0.0) + jnp.log1p(jnp.exp(-jnp.abs(lam)))
    gates = [_dot(xrb[:, n * RNN_BLOCK:(n + 1) * RNN_BLOCK], wai_ref[n]) for n in range(RNN_BLOCKS)]
    r = jax.nn.sigmoid(jnp.concatenate([g[:, :RNN_BLOCK] for g in gates], axis=-1) + ba_ref[...])
    ig = jax.nn.sigmoid(jnp.concatenate([g[:, RNN_BLOCK:] for g in gates], axis=-1) + bi_ref[...])
    log_a = -RG_C * r * sp
    a = jnp.exp(log_a)
    m2 = jnp.tanh(-log_a) * (1.0 + a * a)
    mult = jnp.where(m2 > 0.0, m2 * lax.rsqrt(m2), 0.0)
    if stream_start:
        rows = lax.broadcasted_iota(I32, (tm, RNN_WIDTH), 0)
        mult = jnp.where(first & (rows < SUBLANES), 1.0, mult)
    a_s[...] = a
    b_s[...] = mult * ig * xr

    def step(t, h):
        r0 = pl.multiple_of(t * SUBLANES, SUBLANES)
        h = a_s[pl.ds(r0, SUBLANES), :] * h + b_s[pl.ds(r0, SUBLANES), :]
        b_s[pl.ds(r0, SUBLANES), :] = h
        return h

    h = lax.fori_loop(0, tm // SUBLANES, step, h_s[...], unroll=True)
    h_s[...] = h
    hl_ref[...] = h

    gate = _dot(xn, win_ref[:, :RNN_WIDTH])
    act = (jax.nn.gelu(gate) * b_s[...]).astype(BF16)
    x1 = x + _dot(act, wo_ref[...])
    x1_ref[...] = x1
    _xq_tail(x1, gx_ref, wxq_ref, qnx_ref, cs, qx_ref)


def _odd(x, g, w_in, conv_w, conv_b, w_ai, b_a, b_i, lam, w_out, hist, h0, gx, w_xq, qnx, tt, layer,
         stream_start):
    nb = SUBLANES
    tm = tt * nb
    rows = x.shape[0]
    t_len = rows // nb
    d = D_MODEL
    r = RNN_WIDTH
    nh = (C_CONV - 1) * SUBLANES

    def row(c):
        return pl.BlockSpec((tm, c), lambda i: (i, 0))

    blk = (RNN_BLOCKS, RNN_BLOCK, 2 * RNN_BLOCK)
    return pl.pallas_call(
        functools.partial(_odd_kernel, stream_start=stream_start),
        grid=(rows // tm,),
        in_specs=[row(d), _const_spec((1, d)), _const_spec((d, 2 * r)), _const_spec((C_CONV, r)),
                  _const_spec((1, r)), _const_spec(blk), _const_spec((1, r)),
                  _const_spec((1, r)), _const_spec((1, r)), _const_spec((r, d)), _const_spec((nh, r)),
                  _const_spec((SUBLANES, r)), _const_spec((1, d)), _const_spec((d, d), layer), _const_spec((1, d))],
        out_specs=[row(d), pl.BlockSpec((nb, tt, d), lambda i: (0, i, 0)), pl.BlockSpec((nh, r), lambda i: (0, 0)),
                   pl.BlockSpec((SUBLANES, r), lambda i: (0, 0))],
        out_shape=[jax.ShapeDtypeStruct((rows, d), F32), jax.ShapeDtypeStruct((nb, t_len, d), BF16),
                   jax.ShapeDtypeStruct((nh, r), F32), jax.ShapeDtypeStruct((SUBLANES, r), F32)],
        scratch_shapes=[pltpu.VMEM((tm + nh, r), F32), pltpu.VMEM((tm, r), F32), pltpu.VMEM((tm, r), F32),
                        pltpu.VMEM((SUBLANES, r), F32), pltpu.VMEM((d // LANES, tm, LANES), F32)],
        compiler_params=_params(("arbitrary",)),
        name="odd_mixer",
    )(x, g, w_in, conv_w, conv_b, w_ai, b_a, b_i, lam, w_out, hist, h0, gx, w_xq, qnx)


def _to_tm(a):
    return jnp.transpose(a, (1, 0, 2)).reshape(a.shape[1] * a.shape[0], a.shape[2])


def _from_tm(a, w):
    return jnp.transpose(a.reshape(w, SUBLANES, a.shape[1]), (1, 0, 2))


def _tile_plan(t_len):
    return {"proj": min(128, t_len),
            "ffn": min(64, t_len),
            "dsa": min(KEY_BLOCK, t_len),
            "xattn": t_len}


def _trunk(x, st, mem_k, mem_v, p, bias):
    nb, t_len, d = x.shape
    assert nb == SUBLANES
    tiles = _tile_plan(t_len)
    past = 0 if st is None else st["b_k"].shape[2]
    length = past + t_len
    ntop = min(TOPK_MAX, length // 4)
    near, far = bias
    xt = None
    out = {}

    def hist(name, l, width, c):
        if st is None:
            return jnp.zeros(((width - 1) * nb, c), F32)
        return _to_tm(st[name][l])

    for l in range(DEPTH):
        if l % 2 == 0:
            e = l // 2
            assert l == 0, "the per-sequence input is converted by the first layer's kernel"
            xt, ya, q, iq, ikw, k, v, k_att, ik_att, uh = _even_in(
                x, p["g_mix"][l], p["w_in_even"][e], p["a_conv_w"][e], p["b_q_norm"][e], p["b_k_norm"][e],
                hist("a_conv", e, A_CONV, A_WIDTH), tiles["proj"])
            k_new = k.reshape(nb, t_len, B_KV_HEADS, B_HEAD_DIM)
            v_new = v.reshape(nb, t_len, B_KV_HEADS, B_HEAD_DIM)
            ik_new = ikw[:, :, :IDX_DIM]
            v_att = v.astype(BF16)
            if st is not None:
                k_past = st["b_k"][e].reshape(nb, past, B_KV_HEADS * B_HEAD_DIM).astype(BF16)
                ik_past = jnp.pad(st["b_kidx"][e].astype(BF16), ((0, 0), (0, 0), (0, LANES - IDX_DIM)))
                k_att = jnp.concatenate([k_past, k_att], axis=1)
                ik_att = jnp.concatenate([ik_past, ik_att], axis=1)
                v_past = st["b_v"][e].reshape(nb, past, B_KV_HEADS * B_HEAD_DIM).astype(BF16)
                v_att = jnp.concatenate([v_past, v_att], axis=1)
            lp = -(-length // KEY_BLOCK) * KEY_BLOCK
            padl = lp - length
            k_att = jnp.pad(k_att, ((0, 0), (0, padl), (0, 0)))
            ik_att = jnp.pad(ik_att, ((0, 0), (0, padl), (0, 0)))
            v_att = jnp.pad(v_att, ((0, 0), (0, padl), (0, 0)))
            yb = _dsa(q, iq, ikw, k_att, v_att, ik_att, near, far, tiles["dsa"], past, length, ntop)
            x1, qx = _even_out(xt, ya, yb, p["w_out_even"][e], p["g_x"][l], p["w_xq"], p["x_q_norm"][l],
                               tiles["proj"], l)
            out.setdefault("a_conv", []).append(_from_tm(uh, A_CONV - 1))
            out.setdefault("b_k", []).append(k_new)
            out.setdefault("b_v", []).append(v_new)
            out.setdefault("b_kidx", []).append(ik_new)
        else:
            o = l // 2
            h0 = jnp.zeros((nb, RNN_WIDTH), F32) if st is None else st["c_h"][o]
            x1, qx, ch, hl = _odd(
                xt, p["g_mix"][l], p["w_in_odd"][o], p["c_conv_w"][o], p["c_conv_b"][o], p["c_w_ai"][o],
                p["c_b_a"][o], p["c_b_i"][o], p["c_lambda"][o], p["w_out_odd"][o],
                hist("c_conv", o, C_CONV, RNN_WIDTH), h0, p["g_x"][l], p["w_xq"], p["x_q_norm"][l],
                tiles["proj"], l, stream_start=(past == 0))
            out.setdefault("c_conv", []).append(_from_tm(ch, C_CONV - 1))
            out.setdefault("c_h", []).append(hl)
        xo = _xattn(qx, mem_k[l], mem_v[l], tiles["xattn"])
        xt, fh = _ffn(x1, xo, p["w_xo"], p["g_ffn"][l], p["w_up"], p["f_conv_w"][l], p["f_conv_b"][l],
                      p["w_down"], hist("f_conv", l, F_CONV, D_FF), tiles["ffn"], l, seq_out=(l == DEPTH - 1))
        out.setdefault("f_conv", []).append(_from_tm(fh, F_CONV - 1))
    return xt, {name: jnp.stack(v) for name, v in out.items()}


def kernel(x_prompt, x_sample, cache_b_k, cache_b_v, cache_b_kidx, state_a_conv, state_c_conv, state_c_h, state_ffn_conv, cache_mem_k, cache_mem_v, mem_prompt, rel_table, g_mix, w_in_even, a_conv_w, b_q_norm, b_k_norm, w_out_even, w_in_odd, c_conv_w, c_conv_b, c_w_a, c_b_a, c_w_i, c_b_i, c_lambda, w_out_odd, g_mem, g_x, w_xq, w_xk, w_xv, x_q_norm, x_k_norm, w_xo, g_ffn, w_up, f_conv_w, f_conv_b, w_down):
    d = D_MODEL
    bp, t_p, _ = x_prompt.shape
    m = mem_prompt.shape[1]

    def rowvec(a):
        return a.reshape(a.shape[0], 1, a.shape[-1])

    def mxu(a):
        return [a[l].astype(BF16) for l in range(a.shape[0])]

    p = {
        "g_mix": rowvec(g_mix), "g_x": rowvec(g_x), "g_ffn": rowvec(g_ffn),
        "w_in_even": [jnp.pad(w, ((0, 0), (0, EVEN_IN_PAD - EVEN_IN))) for w in mxu(w_in_even)],
        "a_conv_w": a_conv_w,
        "b_q_norm": rowvec(jnp.tile(b_q_norm, (1, B_HEADS))),
        "b_k_norm": rowvec(jnp.tile(b_k_norm, (1, B_KV_HEADS))),
        "w_out_even": mxu(w_out_even),
        "w_in_odd": mxu(w_in_odd), "c_conv_w": c_conv_w, "c_conv_b": rowvec(c_conv_b),
        "c_w_ai": mxu(jnp.concatenate([c_w_a, c_w_i], axis=-1)), "c_b_a": rowvec(c_b_a), "c_b_i": rowvec(c_b_i),
        "c_lambda": rowvec(c_lambda), "w_out_odd": mxu(w_out_odd),
        "w_xq": w_xq.astype(BF16), "x_q_norm": rowvec(jnp.tile(x_q_norm, (1, X_HEADS))),
        "w_xo": w_xo.astype(BF16), "w_up": w_up.astype(BF16), "f_conv_w": f_conv_w,
        "f_conv_b": rowvec(f_conv_b), "w_down": w_down.astype(BF16),
    }
    bias = _bias_tiles(rel_table, KEY_BLOCK)

    mk, mv = _mem_kv(mem_prompt.reshape(bp * m, d), g_mem, w_xk, x_k_norm, w_xv)
    p_mem_k = mk.reshape(DEPTH, bp, m, X_HEADS, X_HEAD_DIM)
    p_mem_v = mv.reshape(DEPTH, bp, m, X_HEADS, X_HEAD_DIM)
    y_prompt, new_p = _trunk(x_prompt, None, mk.reshape(DEPTH, bp, m, d).astype(BF16),
                             mv.reshape(DEPTH, bp, m, d).astype(BF16), p, bias)

    bs, t_s, _ = x_sample.shape
    st_s = {"b_k": cache_b_k, "b_v": cache_b_v, "b_kidx": cache_b_kidx, "a_conv": state_a_conv,
            "c_conv": state_c_conv, "c_h": state_c_h, "f_conv": state_ffn_conv}
    ms = cache_mem_k.shape[2]
    y_sample, new_s = _trunk(x_sample, st_s, cache_mem_k.reshape(DEPTH, bs, ms, d).astype(BF16),
                             cache_mem_v.reshape(DEPTH, bs, ms, d).astype(BF16), p, bias)
    return (y_prompt, y_sample,
            new_p["b_k"], new_p["b_v"], new_p["b_kidx"], new_p["a_conv"], new_p["c_conv"],
            new_p["c_h"], new_p["f_conv"], p_mem_k, p_mem_v,
            new_s["b_k"], new_s["b_v"], new_s["b_kidx"], new_s["a_conv"], new_s["c_conv"],
            new_s["c_h"], new_s["f_conv"])
```
